```python
import math
import jax
import jax.numpy as jnp
from jax import lax
import numpy as np

D_MODEL = 1024
BATCH = 16
SEQ = 2048
DEPTH = 2

N_MIXERS = 2
HEAD_DIM = 64
N_HEADS = D_MODEL // HEAD_DIM
ATTN_WIDTH = N_HEADS * HEAD_DIM
ROPE_THETA = 10000.0
NORM_EPS = 1e-6

NSA_KV_HEADS = 2
NSA_GROUP = N_HEADS // NSA_KV_HEADS
NSA_KV_WIDTH = NSA_KV_HEADS * HEAD_DIM
NSA_BRANCHES = 3
CMP_LEN = 32
CMP_STRIDE = 16
CMP_HIDDEN = 256
SLC_LEN = 64
SLC_TOP = 16
WIN_LEN = 512
NSA_IN = ATTN_WIDTH + 6 * NSA_KV_WIDTH + ATTN_WIDTH + NSA_BRANCHES * N_HEADS
SLC_Q_BLOCK = 32
WIN_Q_BLOCK = 128

DIL_PATTERNS = ((128, 1), (512, 4), (2048, 16))
DIL_GROUP_HEADS = (N_HEADS - 2 * (N_HEADS // 3), N_HEADS // 3, N_HEADS // 3)
DIL_IN = 4 * ATTN_WIDTH
DIL_Q_BLOCK = 32

kernel_name = "hybrid_nsa_dilated_adaln_trunk"


def rms_norm(x, g):
    xf = x.astype(jnp.float32)
    y = xf * lax.rsqrt(jnp.mean(xf * xf, axis=-1, keepdims=True) + NORM_EPS)
    return (y * g.astype(jnp.float32)).astype(x.dtype)


def rope(x):
    T = x.shape[1]
    half = x.shape[-1] // 2
    inv = ROPE_THETA ** (-jnp.arange(half, dtype=jnp.float32) / half)
    ang = jnp.arange(T, dtype=jnp.float32)[:, None] * inv[None, :]
    cos = jnp.cos(ang)[None, :, None, :]
    sin = jnp.sin(ang)[None, :, None, :]
    xf = x.astype(jnp.float32)
    x1, x2 = xf[..., :half], xf[..., half:]
    return jnp.concatenate([x1 * cos - x2 * sin, x2 * cos + x1 * sin], axis=-1).astype(x.dtype)


def masked_softmax(s, mask):
    s = jnp.where(mask, s.astype(jnp.float32), -jnp.inf)
    m = jnp.max(s, axis=-1, keepdims=True)
    m = jnp.where(jnp.isfinite(m), m, 0.0)
    e = jnp.exp(s - m)
    den = jnp.sum(e, axis=-1, keepdims=True)
    return e / jnp.where(den > 0, den, 1.0)


def adaln_norm(x, c, g, w_ada, b_ada):
    mod = jax.nn.silu(c) @ w_ada + b_ada
    shift, scale, gate = jnp.split(mod, 3, axis=-1)
    h = rms_norm(x, g) * (1.0 + scale[:, None, :]) + shift[:, None, :]
    return h, gate[:, None, :]


def nsa_mixer(h, w_in, pe_k, pe_v, ck_w1, ck_w2, cv_w1, cv_w2, w_out):
    B, T, _ = h.shape
    G, HG, DH = NSA_KV_HEADS, NSA_GROUP, HEAD_DIM
    scale = DH ** -0.5
    u = h @ w_in
    cuts = np.cumsum([ATTN_WIDTH] + [NSA_KV_WIDTH] * 6 + [ATTN_WIDTH]).tolist()
    q, k_c, v_c, k_s, v_s, k_w, v_w, z, g_logit = jnp.split(u, cuts, axis=-1)
    k_c, v_c, k_s, v_s, k_w, v_w = [a.reshape(B, T, G, DH) for a in (k_c, v_c, k_s, v_s, k_w, v_w)]
    gates = jax.nn.sigmoid(g_logit.astype(jnp.float32)).reshape(B, T, G, HG, NSA_BRANCHES).astype(h.dtype)
    t = jnp.arange(T)

    q_plain = q.reshape(B, T, G, HG, DH)
    n_cmp = (T - CMP_LEN) // CMP_STRIDE + 1
    cmp_idx = CMP_STRIDE * np.arange(n_cmp)[:, None] + np.arange(CMP_LEN)[None, :]

    def compress(a, pe, w1, w2):
        blk = a[:, cmp_idx] + pe[:, None, :]
        blk = jnp.moveaxis(blk, 3, 2).reshape(B, n_cmp, G, CMP_LEN * DH)
        return jax.nn.silu(blk @ w1) @ w2

    k_cmp = compress(k_c, pe_k, ck_w1, ck_w2)
    v_cmp = compress(v_c, pe_v, cv_w1, cv_w2)
    cmp_end = CMP_STRIDE * jnp.arange(n_cmp) + CMP_LEN - 1
    cmask = cmp_end[None, :] <= t[:, None]
    s_cmp = jnp.einsum('btghd,bngd->btghn', q_plain, k_cmp) * scale
    p_cmp = masked_softmax(s_cmp, cmask[None, :, None, None, :])
    o_cmp = jnp.einsum('btghn,bngd->btghd', p_cmp.astype(v_cmp.dtype), v_cmp)

    n_slc = T // SLC_LEN
    c_start = CMP_STRIDE * np.arange(n_cmp)
    s_start = SLC_LEN * np.arange(n_slc)
    overlap = ((c_start[:, None] < s_start[None, :] + SLC_LEN)
               & (c_start[:, None] + CMP_LEN > s_start[None, :])).astype(np.float32)
    p_slc = jnp.einsum('btghn,nj->btgj', p_cmp, overlap)
    jblk = jnp.arange(n_slc)[None, :]
    cur = (t // SLC_LEN)[:, None]
    visible = jblk <= cur
    forced = (jblk == 0) | (jblk == cur) | (jblk == cur - 1)
    rank = jnp.where(forced[None, :, None, :], jnp.inf,
                     jnp.where(visible[None, :, None, :], p_slc, -jnp.inf))
    n_top = min(SLC_TOP, n_slc)
    top_val, top_idx = lax.top_k(rank, n_top)
    top_ok = top_val > -jnp.inf

    q_r = rope(q.reshape(B, T, N_HEADS, DH)).reshape(B, T, G, HG, DH)
    k_s = rope(k_s)
    k_w = rope(k_w)

    k_blk = k_s.reshape(B, n_slc, SLC_LEN, G, DH).transpose(0, 3, 1, 2, 4)
    v_blk = v_s.reshape(B, n_slc, SLC_LEN, G, DH).transpose(0, 3, 1, 2, 4)
    C = SLC_Q_BLOCK
    n_qb = T // C
    q_sb = jnp.moveaxis(q_r.reshape(B, n_qb, C, G, HG, DH), 1, 0)
    idx_sb = jnp.moveaxis(top_idx.reshape(B, n_qb, C, G, n_top), 1, 0)
    ok_sb = jnp.moveaxis(top_ok.reshape(B, n_qb, C, G, n_top), 1, 0)
    gather_blocks = jax.vmap(jax.vmap(lambda blk, ix: blk[ix]))

    def slc_block(args):
        s0, qq, ii, ok = args
        ii_g = jnp.swapaxes(ii, 1, 2)
        kg = gather_blocks(k_blk, ii_g)
        vg = gather_blocks(v_blk, ii_g)
        s = jnp.einsum('bcghd,bgcnld->bcghnl', qq, kg) * scale
        tq = s0 + jnp.arange(C)
        kpos = ii[..., None] * SLC_LEN + jnp.arange(SLC_LEN)
        mask = (kpos <= tq[None, :, None, None, None]) & ok[..., None]
        p = masked_softmax(s.reshape(B, C, G, HG, n_top * SLC_LEN),
                           mask.reshape(B, C, G, 1, n_top * SLC_LEN))
        return jnp.einsum('bcghnl,bgcnld->bcghd', p.reshape(s.shape).astype(vg.dtype), vg)

    o_slc = lax.map(slc_block, (jnp.arange(n_qb) * C, q_sb, idx_sb, ok_sb))
    o_slc = jnp.moveaxis(o_slc, 0, 1).reshape(B, T, G, HG, DH)

    span = WIN_LEN + WIN_Q_BLOCK
    n_wb = T // WIN_Q_BLOCK
    k_pad = jnp.pad(k_w, ((0, 0), (WIN_LEN, 0), (0, 0), (0, 0)))
    v_pad = jnp.pad(v_w, ((0, 0), (WIN_LEN, 0), (0, 0), (0, 0)))
    q_wb = jnp.moveaxis(q_r.reshape(B, n_wb, WIN_Q_BLOCK, G, HG, DH), 1, 0)

    def win_block(args):
        s0, qq = args
        kk = lax.dynamic_slice_in_dim(k_pad, s0, span, axis=1)
        vv = lax.dynamic_slice_in_dim(v_pad, s0, span, axis=1)
        s = jnp.einsum('bcghd,bkgd->bcghk', qq, kk) * scale
        tq = s0 + jnp.arange(WIN_Q_BLOCK)
        kpos = s0 - WIN_LEN + jnp.arange(span)
        mask = ((kpos[None, :] <= tq[:, None]) & (kpos[None, :] > tq[:, None] - WIN_LEN)
                & (kpos[None, :] >= 0))
        p = masked_softmax(s, mask[None, :, None, None, :])
        return jnp.einsum('bcghk,bkgd->bcghd', p.astype(vv.dtype), vv)

    o_win = lax.map(win_block, (jnp.arange(n_wb) * WIN_Q_BLOCK, q_wb))
    o_win = jnp.moveaxis(o_win, 0, 1).reshape(B, T, G, HG, DH)

    o = gates[..., 0:1] * o_cmp + gates[..., 1:2] * o_slc + gates[..., 2:3] * o_win
    o = o.reshape(B, T, ATTN_WIDTH)
    return (o * jax.nn.silu(z)) @ w_out


def dilated_mixer(h, w_in, w_out):
    B, T, _ = h.shape
    DH = HEAD_DIM
    scale = DH ** -0.5
    n_grp = len(DIL_PATTERNS)
    q, k, v, z = jnp.split(h @ w_in, 4, axis=-1)
    q = rope(q.reshape(B, T, N_HEADS, DH))
    k = rope(k.reshape(B, T, N_HEADS, DH))
    v = v.reshape(B, T, N_HEADS, DH)
    offs = np.cumsum((0,) + DIL_GROUP_HEADS).tolist()
    k_grp = [k[:, :, offs[i]:offs[i + 1]] for i in range(n_grp)]
    v_grp = [v[:, :, offs[i]:offs[i + 1]] for i in range(n_grp)]
    C = DIL_Q_BLOCK
    n_qb = T // C
    q_blocks = jnp.moveaxis(q.reshape(B, n_qb, C, N_HEADS, DH), 1, 0)

    def block(args):
        s0, qb = args
        tq = s0 + jnp.arange(C)
        outs, lses = [], []
        for gi, (win, dil) in enumerate(DIL_PATTERNS):
            n_k = win // dil + 1
            kpos = tq[:, None] - dil * jnp.arange(n_k)[None, :]
            ok = kpos >= 0
            kidx = jnp.maximum(kpos, 0)
            kg = k_grp[gi][:, kidx]
            vg = v_grp[gi][:, kidx]
            qg = qb[:, :, offs[gi]:offs[gi + 1]]
            s = jnp.einsum('bchd,bckhd->bchk', qg, kg).astype(jnp.float32) * scale
            s = jnp.where(ok[None, :, None, :], s, -jnp.inf)
            m = jnp.max(s, axis=-1, keepdims=True)
            e = jnp.exp(s - m)
            den = jnp.sum(e, axis=-1, keepdims=True)
            outs.append(jnp.einsum('bchk,bckhd->bchd', (e / den).astype(vg.dtype), vg))
            lse = (m + jnp.log(den))[..., 0]
            lses.append(jax.nn.logsumexp(lse, axis=-1) - math.log(DIL_GROUP_HEADS[gi]))
        alpha = jax.nn.softmax(jnp.stack(lses, axis=-1), axis=-1) * n_grp
        return jnp.concatenate(
            [outs[i] * alpha[:, :, i, None, None].astype(outs[i].dtype) for i in range(n_grp)], axis=2)

    o = lax.map(block, (jnp.arange(n_qb) * C, q_blocks))
    o = jnp.moveaxis(o, 0, 1).reshape(B, T, ATTN_WIDTH)
    return (o * jax.nn.silu(z)) @ w_out


def setup_inputs(seed: int = 0) -> dict:
    key = jax.random.key(seed)
    ks = jax.random.split(key, 16)
    n_a = len(range(0, DEPTH, N_MIXERS))
    n_b = len(range(1, DEPTH, N_MIXERS))
    D = D_MODEL

    def nrm(k, shape, s):
        return jax.random.normal(k, shape, jnp.float32) * s

    return {
        "x": nrm(ks[0], (BATCH, SEQ, D), 1.0),
        "c": nrm(ks[1], (BATCH, D), 1.0),
        "norm_g": 1.0 + nrm(ks[2], (DEPTH, D), 0.05),
        "ada_w": nrm(ks[3], (DEPTH, D, 3 * D), 0.5 * D ** -0.5),
        "ada_b": nrm(ks[4], (DEPTH, 3 * D), 0.02),
        "nsa_w_in": nrm(ks[5], (n_a, D, NSA_IN), D ** -0.5),
        "nsa_pe_k": nrm(ks[6], (n_a, CMP_LEN, HEAD_DIM), 0.1),
        "nsa_pe_v": nrm(ks[7], (n_a, CMP_LEN, HEAD_DIM), 0.1),
        "nsa_ck_w1": nrm(ks[8], (n_a, CMP_LEN * HEAD_DIM, CMP_HIDDEN), (CMP_LEN * HEAD_DIM) ** -0.5),
        "nsa_ck_w2": nrm(ks[9], (n_a, CMP_HIDDEN, HEAD_DIM), CMP_HIDDEN ** -0.5),
        "nsa_cv_w1": nrm(ks[10], (n_a, CMP_LEN * HEAD_DIM, CMP_HIDDEN), (CMP_LEN * HEAD_DIM) ** -0.5),
        "nsa_cv_w2": nrm(ks[11], (n_a, CMP_HIDDEN, HEAD_DIM), CMP_HIDDEN ** -0.5),
        "nsa_w_out": nrm(ks[12], (n_a, ATTN_WIDTH, D), ATTN_WIDTH ** -0.5),
        "dil_w_in": nrm(ks[13], (n_b, D, DIL_IN), D ** -0.5),
        "dil_w_out": nrm(ks[14], (n_b, ATTN_WIDTH, D), ATTN_WIDTH ** -0.5),
        "final_g": 1.0 + nrm(ks[15], (D,), 0.05),
    }


def reference(x, c, norm_g, ada_w, ada_b, nsa_w_in, nsa_pe_k, nsa_pe_v, nsa_ck_w1, nsa_ck_w2,
              nsa_cv_w1, nsa_cv_w2, nsa_w_out, dil_w_in, dil_w_out, final_g):
    for i in range(DEPTH):
        h, gate = adaln_norm(x, c, norm_g[i], ada_w[i], ada_b[i])
        j = i // N_MIXERS
        if i % N_MIXERS == 0:
            y = nsa_mixer(h, nsa_w_in[j], nsa_pe_k[j], nsa_pe_v[j], nsa_ck_w1[j], nsa_ck_w2[j],
                          nsa_cv_w1[j], nsa_cv_w2[j], nsa_w_out[j])
        else:
            y = dilated_mixer(h, dil_w_in[j], dil_w_out[j])
        x = x + gate * y
    return rms_norm(x, final_g)
```

```python
import functools
import math

import numpy as np
import jax
import jax.numpy as jnp
from jax import lax
from jax.experimental import pallas as pl
from jax.experimental.pallas import tpu as pltpu

F32 = jnp.float32
BF16 = jnp.bfloat16
HIGHEST = lax.Precision.HIGHEST

HEAD_DIM = 64
HALF = HEAD_DIM // 2
N_HEADS = 16
ROPE_THETA = 10000.0
NORM_EPS = 1e-6
NSA_KV_HEADS = 2
NSA_GROUP = N_HEADS // NSA_KV_HEADS
NSA_BRANCHES = 3
CMP_LEN = 32
CMP_STRIDE = 16
CMP_HIDDEN = 256
SLC_LEN = 64
SLC_TOP = 16
WIN_LEN = 512
DIL_PATTERNS = ((128, 1), (512, 4), (2048, 16))
DIL_GROUP_HEADS = (6, 5, 5)
DIL_WIN = 128
DIL_TILES_PER_GROUP = 3

LANES = 128
VMEM_LIMIT_BYTES = 48 * 1024 * 1024

NEG = -1e30
TQ = 128
SLC_CK = 512
TM = 256


def _cparams(*sem):
    return pltpu.CompilerParams(dimension_semantics=sem, vmem_limit_bytes=VMEM_LIMIT_BYTES)


def _dot(a, b):
    return jnp.dot(a, b, preferred_element_type=F32)


def _dot_t(a, b):
    return lax.dot_general(a, b, (((1,), (1,)), ((), ())), preferred_element_type=F32)


def _silu(v):
    return v * jax.nn.sigmoid(v)


def _mod_kernel(c_ref, w_ref, b_ref, o_ref):
    s = _silu(c_ref[...])
    o_ref[...] = jnp.dot(s, w_ref[...], precision=HIGHEST, preferred_element_type=F32) + b_ref[...]


def _adaln_mod(c, ada_w, ada_b):
    depth, d, n3 = ada_w.shape
    b = c.shape[0]
    tn = 1024
    return pl.pallas_call(
        _mod_kernel,
        grid=(depth, n3 // tn),
        in_specs=[
            pl.BlockSpec((b, d), lambda i, j: (0, 0)),
            pl.BlockSpec((None, d, tn), lambda i, j: (i, 0, j)),
            pl.BlockSpec((None, 1, tn), lambda i, j: (i, 0, j)),
        ],
        out_specs=pl.BlockSpec((None, b, tn), lambda i, j: (i, 0, j)),
        out_shape=jax.ShapeDtypeStruct((depth, b, n3), F32),
        compiler_params=_cparams("arbitrary", "arbitrary"),
        name="adaln_mod",
    )(c, ada_w, ada_b.reshape(depth, 1, n3))


def _modulated_norm(x, mod_ref, g_ref):
    ms = jnp.mean(x * x, axis=-1, keepdims=True)
    y = x * lax.rsqrt(ms + NORM_EPS) * g_ref[...]
    return y * (1.0 + mod_ref[1:2, :]) + mod_ref[0:1, :]


def _rope_tile(v, cos, sin):
    return v * cos + pltpu.roll(v, 2 * HALF, axis=1) * sin


def _inproj0_kernel(x_ref, mod_ref, g_ref, cos_ref, sin_ref, wq_ref, wkv_ref, wz_ref, wg_ref,
                    qp_ref, qr_ref, kc_ref, vc_ref, ks_ref, vs_ref, kw_ref, vw_ref, z_ref, gl_ref):
    h = _modulated_norm(x_ref[...], mod_ref, g_ref).astype(BF16)
    cos = cos_ref[...]
    sin = sin_ref[...]
    q = _dot(h, wq_ref[...])
    for j in range(q.shape[1] // LANES):
        sl = slice(j * LANES, (j + 1) * LANES)
        qp_ref[:, sl] = q[:, sl].astype(BF16)
        qr_ref[:, sl] = _rope_tile(q[:, sl], cos, sin).astype(BF16)
    kv = _dot(h, wkv_ref[...])
    tiles = [kv[:, i * LANES:(i + 1) * LANES] for i in range(6)]
    kc_ref[...] = tiles[0].astype(BF16)
    vc_ref[...] = tiles[1].astype(BF16)
    ks_ref[...] = _rope_tile(tiles[2], cos, sin).astype(BF16)
    vs_ref[...] = tiles[3].astype(BF16)
    kw_ref[...] = _rope_tile(tiles[4], cos, sin).astype(BF16)
    vw_ref[...] = tiles[5].astype(BF16)
    z_ref[...] = _dot(h, wz_ref[...]).astype(BF16)
    gl_ref[...] = _dot(h, wg_ref[...])


def _inproj0(x, mod, g, cos, sin, wq, wkv, wz, wg):
    b, t, d = x.shape
    nq, nz = wq.shape[1], wz.shape[1]
    tok = lambda n: pl.BlockSpec((None, TM, n), lambda i, j: (i, j, 0))
    full = lambda a: pl.BlockSpec(a.shape, lambda i, j: (0,) * a.ndim)
    tab = pl.BlockSpec((TM, LANES), lambda i, j: (j, 0))
    shp = lambda n, dt: jax.ShapeDtypeStruct((b, t, n), dt)
    return pl.pallas_call(
        _inproj0_kernel,
        grid=(b, t // TM),
        in_specs=[tok(d), pl.BlockSpec((None, 3, d), lambda i, j: (i, 0, 0)), full(g), tab, tab,
                  full(wq), full(wkv), full(wz), full(wg)],
        out_specs=[tok(nq), tok(nq)] + [tok(LANES)] * 6 + [tok(nz), tok(LANES)],
        out_shape=[shp(nq, BF16), shp(nq, BF16)] + [shp(LANES, BF16)] * 6 + [shp(nz, BF16), shp(LANES, F32)],
        compiler_params=_cparams("parallel", "parallel"),
        name="inproj_nsa",
    )(x, mod, g, cos, sin, wq, wkv, wz, wg)


def _compress_kernel(ak_ref, av_ref, pekt_ref, pekb_ref, pevt_ref, pevb_ref,
                     w1kt_ref, w1kb_ref, w1vt_ref, w1vb_ref, w2k_ref, w2v_ref, kcmp_ref, vcmp_ref):
    def mlp(a_ref, pet_ref, peb_ref, w1t_ref, w1b_ref, w2_ref):
        a = a_ref[...].astype(F32)
        top = _dot((a + pet_ref[...]).astype(BF16), w1t_ref[...])
        bot = _dot((a + peb_ref[...]).astype(BF16), w1b_ref[...])
        n = bot.shape[0]
        hid = top + pltpu.roll(bot, n - 1, axis=0)
        return _dot(_silu(hid).astype(BF16), w2_ref[...])

    kcmp_ref[...] = mlp(ak_ref, pekt_ref, pekb_ref, w1kt_ref, w1kb_ref, w2k_ref).astype(BF16)
    vcmp_ref[...] = mlp(av_ref, pevt_ref, pevb_ref, w1vt_ref, w1vb_ref, w2v_ref).astype(BF16)


def _compress(ak, av, consts):
    b, n, w = ak.shape
    full = lambda a: pl.BlockSpec(a.shape, lambda i: (0,) * a.ndim)
    blk = pl.BlockSpec((None, n, w), lambda i: (i, 0, 0))
    out = pl.BlockSpec((None, n, LANES), lambda i: (i, 0, 0))
    return pl.pallas_call(
        _compress_kernel,
        grid=(b,),
        in_specs=[blk, blk] + [full(a) for a in consts],
        out_specs=[out, out],
        out_shape=[jax.ShapeDtypeStruct((b, n, LANES), BF16)] * 2,
        compiler_params=_cparams("parallel"),
        name="compress",
    )(ak, av, *consts)


def _stack_heads(q_ref, lane_sets):
    n_tiles = q_ref.shape[1] // LANES
    parts = []
    for j in range(n_tiles):
        qj = q_ref[:, j * LANES:(j + 1) * LANES]
        parts.append(jnp.where(lane_sets, qj, jnp.zeros_like(qj)))
    return jnp.concatenate(parts, axis=0)


def _cmp_attn_kernel(qp_ref, kcmp_ref, vcmp_ref, ovl_ref, ocmp_ref, msel_ref, *, n_slc):
    tq = qp_ref.shape[0]
    n_cmp = kcmp_ref.shape[0]
    hg = qp_ref.shape[1] // LANES
    t0 = pl.program_id(1) * tq
    lane = lax.broadcasted_iota(jnp.int32, (tq, LANES), 1)
    kc = kcmp_ref[...]
    vc = vcmp_ref[...]
    row = lax.broadcasted_iota(jnp.int32, (hg * tq, n_cmp), 0)
    col = lax.broadcasted_iota(jnp.int32, (hg * tq, n_cmp), 1)
    tpos = t0 + (row & (tq - 1))
    valid = (CMP_STRIDE * col + (CMP_LEN - 1)) <= tpos

    jb = lax.broadcasted_iota(jnp.int32, (n_slc, tq), 0)
    cur = (t0 + lax.broadcasted_iota(jnp.int32, (n_slc, tq), 1)) // SLC_LEN
    visible = jb <= cur
    forced = (jb == 0) | (jb == cur) | (jb == cur - 1)

    o_groups = []
    sel_groups = []
    for g in range(NSA_KV_HEADS):
        in_group = ((lane % HEAD_DIM) // HALF) == g
        qs = _stack_heads(qp_ref, in_group)
        s = jnp.where(valid, _dot_t(qs, kc), NEG)
        m = jnp.max(s, axis=-1, keepdims=True)
        e = jnp.where(valid, jnp.exp(s - m), 0.0)
        den = jnp.sum(e, axis=-1, keepdims=True)
        p = e / jnp.where(den > 0, den, 1.0)
        o_groups.append(_dot(p.astype(BF16), vc))
        p_sum = p[0:tq]
        for j in range(1, hg):
            p_sum = p_sum + p[j * tq:(j + 1) * tq]
        imp = lax.dot_general(ovl_ref[...], p_sum, (((1,), (1,)), ((), ())),
                              precision=HIGHEST, preferred_element_type=F32)
        rank = jnp.where(forced, -NEG, jnp.where(visible, imp, NEG))
        cnt = jnp.zeros((n_slc, tq), F32)
        for jp in range(n_slc):
            rj = rank[jp:jp + 1, :]
            tie = jnp.where(jb > jp, 1.0, 0.0)
            cnt = cnt + jnp.where(rj > rank, 1.0, jnp.where(rj == rank, tie, 0.0))
        sel_groups.append(jnp.where(visible, jnp.where(cnt < SLC_TOP, 1.0, 0.0), 0.0))

    for j in range(hg):
        rows = slice(j * tq, (j + 1) * tq)
        ocmp_ref[:, j * LANES:(j + 1) * LANES] = jnp.where(
            lane < HEAD_DIM, o_groups[0][rows], o_groups[1][rows]).astype(BF16)
    pad = jnp.zeros((LANES - NSA_KV_HEADS * n_slc, tq), F32)
    sel_t = jnp.concatenate(sel_groups + [pad], axis=0)
    msel_ref[...] = sel_t.T.astype(BF16)


def _cmp_attn(qp, kcmp, vcmp, ovl_t):
    b, t, nq = qp.shape
    n_cmp = kcmp.shape[1]
    n_slc = t // SLC_LEN
    tok = lambda n: pl.BlockSpec((None, TQ, n), lambda i, j: (i, j, 0))
    per_b = pl.BlockSpec((None, n_cmp, LANES), lambda i, j: (i, 0, 0))
    return pl.pallas_call(
        functools.partial(_cmp_attn_kernel, n_slc=n_slc),
        grid=(b, t // TQ),
        in_specs=[tok(nq), per_b, per_b, pl.BlockSpec(ovl_t.shape, lambda i, j: (0, 0))],
        out_specs=[tok(nq), tok(LANES)],
        out_shape=[jax.ShapeDtypeStruct((b, t, nq), BF16), jax.ShapeDtypeStruct((b, t, LANES), BF16)],
        compiler_params=_cparams("parallel", "parallel"),
        name="cmp_attn_select",
    )(qp, kcmp, vcmp, ovl_t)


def _nsa_attn_kernel(qr_ref, ks_ref, vs_ref, kw_ref, vw_ref, msel_ref, eexp_ref, ocmp_ref, z_ref, gl_ref,
                     og_ref, s_scr, sw_scr, mx_scr, l_scr, acc_scr):
    tq = qr_ref.shape[0]
    hg = qr_ref.shape[1] // LANES
    rows = hg * tq
    ck = s_scr.shape[2]
    n_sub = ck // LANES
    n_wc = sw_scr.shape[0]
    t_len = ks_ref.shape[0]
    qi = pl.program_id(1)
    t0 = qi * tq
    nk = (t0 + tq + ck - 1) // ck
    lane = lax.broadcasted_iota(jnp.int32, (tq, LANES), 1)
    a_idx = lax.broadcasted_iota(jnp.int32, (tq, LANES), 0)
    msel = msel_ref[...]
    gates = jax.nn.sigmoid(gl_ref[...])

    def lane_max(ref):
        return jnp.max(ref[...], axis=-1, keepdims=True)

    results = []
    for g in range(NSA_KV_HEADS):
        in_group = ((lane % HEAD_DIM) // HALF) == g
        qs = _stack_heads(qr_ref, in_group)

        mx_scr[...] = jnp.full((rows, LANES), NEG, F32)

        def score_chunk(kc, carry):
            k0 = pl.multiple_of(kc * ck, ck)
            s = _dot_t(qs, ks_ref[pl.ds(k0, ck), :])
            picked = _dot(msel, eexp_ref[g, kc])
            kpos = k0 + lax.broadcasted_iota(jnp.int32, (tq, ck), 1)
            tpos = t0 + lax.broadcasted_iota(jnp.int32, (tq, ck), 0)
            bias = jnp.where((picked > 0.5) & (kpos <= tpos), 0.0, NEG)
            s = (s.reshape(hg, tq, ck) + bias[None]).reshape(rows, ck)
            s_scr[kc] = s
            mx = mx_scr[...]
            for i in range(n_sub):
                mx = jnp.maximum(mx, s[:, i * LANES:(i + 1) * LANES])
            mx_scr[...] = mx
            return carry

        lax.fori_loop(0, nk, score_chunk, 0)
        m_b = jnp.broadcast_to(lane_max(mx_scr), (rows, LANES))
        l_scr[...] = jnp.zeros((rows, LANES), F32)
        acc_scr[...] = jnp.zeros((rows, LANES), F32)

        def value_chunk(kc, carry):
            k0 = pl.multiple_of(kc * ck, ck)
            s = s_scr[kc]
            parts = [jnp.exp(s[:, i * LANES:(i + 1) * LANES] - m_b) for i in range(n_sub)]
            lsum = parts[0]
            for i in range(1, n_sub):
                lsum = lsum + parts[i]
            l_scr[...] += lsum
            p = jnp.concatenate(parts, axis=1).astype(BF16)
            acc_scr[...] += _dot(p, vs_ref[pl.ds(k0, ck), :])
            return carry

        lax.fori_loop(0, nk, value_chunk, 0)
        o_slc = acc_scr[...] / jnp.sum(l_scr[...], axis=-1, keepdims=True)

        mx = jnp.full((rows, LANES), NEG, F32)
        for c in range(n_wc):
            k0 = t0 - (n_wc - 1 - c) * tq
            ksafe = pl.multiple_of(jnp.maximum(k0, 0), tq)
            s = _dot_t(qs, kw_ref[pl.ds(ksafe, tq), :])
            if c == 0:
                keep = lane > a_idx
            elif c == n_wc - 1:
                keep = lane <= a_idx
            else:
                keep = lane >= 0
            bias = jnp.where(keep & (k0 >= 0), 0.0, NEG)
            s = (s.reshape(hg, tq, tq) + bias[None]).reshape(rows, tq)
            sw_scr[c] = s
            mx = jnp.maximum(mx, s)
        m_b = jnp.broadcast_to(jnp.max(mx, axis=-1, keepdims=True), (rows, LANES))
        lsum = jnp.zeros((rows, LANES), F32)
        acc = jnp.zeros((rows, LANES), F32)
        for c in range(n_wc):
            k0 = t0 - (n_wc - 1 - c) * tq
            ksafe = pl.multiple_of(jnp.maximum(k0, 0), tq)
            p = jnp.exp(sw_scr[c] - m_b)
            lsum = lsum + p
            acc = acc + _dot(p.astype(BF16), vw_ref[pl.ds(ksafe, tq), :])
        o_win = acc / jnp.sum(lsum, axis=-1, keepdims=True)
        results.append((o_slc, o_win))

    for j in range(hg):
        rs = slice(j * tq, (j + 1) * tq)
        cols = slice(j * LANES, (j + 1) * LANES)

        def gate(g, br):
            c = g * NSA_GROUP * NSA_BRANCHES + j * NSA_BRANCHES + br
            return gates[:, c:c + 1]

        mixed = [gate(g, 1) * results[g][0][rs] + gate(g, 2) * results[g][1][rs] for g in range(NSA_KV_HEADS)]
        g_cmp = jnp.where(lane < HEAD_DIM, gate(0, 0), gate(1, 0))
        o = g_cmp * ocmp_ref[:, cols].astype(F32) + jnp.where(lane < HEAD_DIM, mixed[0], mixed[1])
        og_ref[:, cols] = (o * _silu(z_ref[:, cols].astype(F32))).astype(BF16)


def _nsa_attn(qr, ks, vs, kw, vw, msel, eexp, ocmp, z, gl):
    b, t, nq = qr.shape
    hg = nq // LANES
    rows = hg * TQ
    n_wc = WIN_LEN // TQ + 1
    tok = lambda n: pl.BlockSpec((None, TQ, n), lambda i, j: (i, j, 0))
    per_b = pl.BlockSpec((None, t, LANES), lambda i, j: (i, 0, 0))
    return pl.pallas_call(
        _nsa_attn_kernel,
        grid=(b, t // TQ),
        in_specs=[tok(nq), per_b, per_b, per_b, per_b, tok(LANES),
                  pl.BlockSpec(eexp.shape, lambda i, j: (0, 0, 0, 0)), tok(nq), tok(nq), tok(LANES)],
        out_specs=tok(nq),
        out_shape=jax.ShapeDtypeStruct((b, t, nq), BF16),
        scratch_shapes=[
            pltpu.VMEM((t // SLC_CK, rows, SLC_CK), F32),
            pltpu.VMEM((n_wc, rows, TQ), F32),
            pltpu.VMEM((rows, LANES), F32),
            pltpu.VMEM((rows, LANES), F32),
            pltpu.VMEM((rows, LANES), F32),
        ],
        compiler_params=_cparams("parallel", "arbitrary"),
        name="nsa_slc_win_attn",
    )(qr, ks, vs, kw, vw, msel, eexp, ocmp, z, gl)


def _outproj0_kernel(og_ref, w_ref, x_ref, mod_ref, o_ref):
    y = _dot(og_ref[...], w_ref[...])
    o_ref[...] = x_ref[...] + mod_ref[2:3, :] * y


def _outproj0(og, w, x, mod):
    b, t, d = x.shape
    n = og.shape[2]
    tok = lambda k: pl.BlockSpec((None, TM, k), lambda i, j: (i, j, 0))
    return pl.pallas_call(
        _outproj0_kernel,
        grid=(b, t // TM),
        in_specs=[tok(n), pl.BlockSpec(w.shape, lambda i, j: (0, 0)), tok(d),
                  pl.BlockSpec((None, 3, d), lambda i, j: (i, 0, 0))],
        out_specs=tok(d),
        out_shape=jax.ShapeDtypeStruct((b, t, d), F32),
        compiler_params=_cparams("parallel", "parallel"),
        name="outproj_nsa",
    )(og, w, x, mod)


def _inproj1_kernel(x_ref, mod_ref, g_ref, cos_ref, sin_ref, wq_ref, wk_ref, wv_ref, wz_ref,
                    q_ref, k_ref, v_ref, z_ref):
    h = _modulated_norm(x_ref[...], mod_ref, g_ref).astype(BF16)
    cos = cos_ref[...]
    sin = sin_ref[...]
    for w_ref, o_ref in ((wq_ref, q_ref), (wk_ref, k_ref)):
        u = _dot(h, w_ref[...])
        for j in range(u.shape[1] // LANES):
            sl = slice(j * LANES, (j + 1) * LANES)
            o_ref[:, sl] = _rope_tile(u[:, sl], cos, sin).astype(BF16)
    v_ref[...] = _dot(h, wv_ref[...]).astype(BF16)
    z_ref[...] = _dot(h, wz_ref[...]).astype(BF16)


def _inproj1(x, mod, g, cos, sin, wq, wk, wv, wz):
    b, t, d = x.shape
    n = wq.shape[1]
    tok = lambda k: pl.BlockSpec((None, TM, k), lambda i, j: (i, j, 0))
    full = lambda a: pl.BlockSpec(a.shape, lambda i, j: (0,) * a.ndim)
    tab = pl.BlockSpec((TM, LANES), lambda i, j: (j, 0))
    return pl.pallas_call(
        _inproj1_kernel,
        grid=(b, t // TM),
        in_specs=[tok(d), pl.BlockSpec((None, 3, d), lambda i, j: (i, 0, 0)), full(g), tab, tab,
                  full(wq), full(wk), full(wv), full(wz)],
        out_specs=[tok(n)] * 4,
        out_shape=[jax.ShapeDtypeStruct((b, t, n), BF16)] * 4,
        compiler_params=_cparams("parallel", "parallel"),
        name="inproj_dil",
    )(x, mod, g, cos, sin, wq, wk, wv, wz)


def _dil_attn_kernel(q_ref, k_ref, v_ref, o_ref, lse_ref, *, n_heads):
    tq = q_ref.shape[0]
    n_tiles = q_ref.shape[1] // LANES
    i = pl.program_id(1)
    t0 = i * tq
    kprev = pl.multiple_of(jnp.maximum(t0 - tq, 0), tq)
    kdiag = pl.multiple_of(t0, tq)
    lane = lax.broadcasted_iota(jnp.int32, (tq, LANES), 1)
    a_idx = lax.broadcasted_iota(jnp.int32, (tq, LANES), 0)
    bias_d = jnp.where(lane <= a_idx, 0.0, NEG)
    bias_p = jnp.where((lane >= a_idx) & (i > 0), 0.0, NEG)
    lses = []
    for jt in range(n_tiles):
        cols = slice(jt * LANES, (jt + 1) * LANES)
        qj = q_ref[:, cols]
        zero = jnp.zeros_like(qj)
        first = ((lane % HEAD_DIM) // HALF) == 0
        qs = jnp.concatenate([jnp.where(first, qj, zero), jnp.where(first, zero, qj)], axis=0)
        s_d = (_dot_t(qs, k_ref[pl.ds(kdiag, tq), cols]).reshape(2, tq, tq) + bias_d[None]).reshape(2 * tq, tq)
        s_p = (_dot_t(qs, k_ref[pl.ds(kprev, tq), cols]).reshape(2, tq, tq) + bias_p[None]).reshape(2 * tq, tq)
        m = jnp.max(jnp.maximum(s_d, s_p), axis=-1, keepdims=True)
        p_d = jnp.exp(s_d - m)
        p_p = jnp.exp(s_p - m)
        l = jnp.sum(p_d + p_p, axis=-1, keepdims=True)
        acc = _dot(p_d.astype(BF16), v_ref[pl.ds(kdiag, tq), cols]) + _dot(p_p.astype(BF16), v_ref[pl.ds(kprev, tq), cols])
        o = acc / l
        o_ref[:, cols] = jnp.where(lane < HEAD_DIM, o[0:tq], o[tq:2 * tq]).astype(BF16)
        lse = m + jnp.log(l)
        lses.append(lse[0:tq])
        if 2 * jt + 1 < n_heads:
            lses.append(lse[tq:2 * tq])
    top = lses[0]
    for v in lses[1:]:
        top = jnp.maximum(top, v)
    tot = jnp.zeros_like(top)
    for v in lses:
        tot = tot + jnp.exp(v - top)
    group_lse = top + jnp.log(tot) - math.log(n_heads)
    lse_ref[...] = jnp.broadcast_to(group_lse, (tq, LANES))


def _dil_attn(q, k, v, n_heads):
    nb, length, n = q.shape
    tq = min(TQ, length)
    tok = lambda w: pl.BlockSpec((None, tq, w), lambda i, j: (i, j, 0))
    per_b = pl.BlockSpec((None, length, n), lambda i, j: (i, 0, 0))
    return pl.pallas_call(
        functools.partial(_dil_attn_kernel, n_heads=n_heads),
        grid=(nb, length // tq),
        in_specs=[tok(n), per_b, per_b],
        out_specs=[tok(n), tok(LANES)],
        out_shape=[jax.ShapeDtypeStruct((nb, length, n), BF16), jax.ShapeDtypeStruct((nb, length, LANES), F32)],
        compiler_params=_cparams("parallel", "parallel"),
        name="dil_attn",
    )(q, k, v)


def _outproj1_kernel(o0_ref, o1_ref, o2_ref, l0_ref, l1_ref, l2_ref, z_ref, w_ref, x_ref, mod_ref, fg_ref, out_ref):
    ls = [l0_ref[...], l1_ref[...], l2_ref[...]]
    top = jnp.maximum(jnp.maximum(ls[0], ls[1]), ls[2])
    es = [jnp.exp(v - top) for v in ls]
    den = es[0] + es[1] + es[2]
    n_grp = len(ls)
    parts = []
    for gi, o_ref in enumerate((o0_ref, o1_ref, o2_ref)):
        alpha = es[gi] / den * float(n_grp)
        for jt in range(o_ref.shape[1] // LANES):
            c0 = (gi * DIL_TILES_PER_GROUP + jt) * LANES
            zt = z_ref[:, c0:c0 + LANES].astype(F32)
            ot = o_ref[:, jt * LANES:(jt + 1) * LANES].astype(F32)
            parts.append((ot * alpha * _silu(zt)).astype(BF16))
    og = jnp.concatenate(parts, axis=1)
    y = _dot(og, w_ref[...])
    x2 = x_ref[...] + mod_ref[2:3, :] * y
    ms = jnp.mean(x2 * x2, axis=-1, keepdims=True)
    out_ref[...] = x2 * lax.rsqrt(ms + NORM_EPS) * fg_ref[...]


def _outproj1(o_groups, lse_groups, z, w, x, mod, final_g):
    b, t, d = x.shape
    tok = lambda k: pl.BlockSpec((None, TM, k), lambda i, j: (i, j, 0))
    full = lambda a: pl.BlockSpec(a.shape, lambda i, j: (0,) * a.ndim)
    ng = o_groups[0].shape[2]
    return pl.pallas_call(
        _outproj1_kernel,
        grid=(b, t // TM),
        in_specs=[tok(ng)] * 3 + [tok(LANES)] * 3 + [tok(z.shape[2]), full(w), tok(d),
                  pl.BlockSpec((None, 3, d), lambda i, j: (i, 0, 0)), full(final_g)],
        out_specs=tok(d),
        out_shape=jax.ShapeDtypeStruct((b, t, d), F32),
        compiler_params=_cparams("parallel", "parallel"),
        name="outproj_dil_final",
    )(*o_groups, *lse_groups, z, w, x, mod, final_g)


def _rope_tables(t):
    inv = ROPE_THETA ** (-jnp.arange(HALF, dtype=F32) / HALF)
    ang = jnp.arange(t, dtype=F32)[:, None] * inv[None, :]
    cos = jnp.tile(jnp.cos(ang), (1, 4))
    sin = jnp.tile(jnp.sin(ang), (1, 4))
    sign = jnp.where(jnp.arange(LANES) < 2 * HALF, -1.0, 1.0).astype(F32)
    return cos, sin * sign[None, :]


def _pair_rope_layout(w, n_pairs):
    d = w.shape[0]
    return w.reshape(d, n_pairs, 2, 2, HALF).transpose(0, 1, 3, 2, 4).reshape(d, n_pairs * LANES)


def _nsa_weights(w_in, pe_k, pe_v, ck_w1, ck_w2, cv_w1, cv_w2, w_out):
    d = w_in.shape[0]
    aw = N_HEADS * HEAD_DIM
    kvw = NSA_KV_HEADS * HEAD_DIM
    cuts = np.cumsum([aw] + [kvw] * 6 + [aw]).tolist()
    q, k_c, v_c, k_s, v_s, k_w, v_w, z, gl = jnp.split(w_in, cuts, axis=1)
    scale = HEAD_DIM ** -0.5
    wq = (q * scale).reshape(d, NSA_KV_HEADS, NSA_GROUP, 2, HALF).transpose(0, 2, 3, 1, 4).reshape(d, aw)
    k_lay = lambda w: _pair_rope_layout(w, 1)
    wkv = jnp.concatenate([k_lay(k_c), v_c, k_lay(k_s), v_s, k_lay(k_w), v_w], axis=1)
    wz = z.reshape(d, NSA_KV_HEADS, NSA_GROUP, HEAD_DIM).transpose(0, 2, 1, 3).reshape(d, aw)
    wo = w_out.reshape(NSA_KV_HEADS, NSA_GROUP, HEAD_DIM, -1).transpose(1, 0, 2, 3).reshape(aw, -1)
    wg = jnp.pad(gl, ((0, 0), (0, LANES - gl.shape[1])))

    def grouped(w1_half, rope_lanes):
        out = []
        for g in range(NSA_KV_HEADS):
            if rope_lanes:
                src = w1_half.reshape(CMP_STRIDE, 2, 1, HALF, CMP_HIDDEN)
                pads = ((0, 0), (0, 0), (g, NSA_KV_HEADS - 1 - g), (0, 0), (0, 0))
            else:
                src = w1_half.reshape(CMP_STRIDE, 1, HEAD_DIM, CMP_HIDDEN)
                pads = ((0, 0), (g, NSA_KV_HEADS - 1 - g), (0, 0), (0, 0))
            out.append(jnp.pad(src, pads).reshape(CMP_STRIDE * LANES, CMP_HIDDEN))
        return jnp.concatenate(out, axis=1)

    def pe_rows(pe_half, rope_lanes):
        if rope_lanes:
            v = jnp.broadcast_to(pe_half.reshape(CMP_STRIDE, 2, 1, HALF), (CMP_STRIDE, 2, NSA_KV_HEADS, HALF))
        else:
            v = jnp.broadcast_to(pe_half.reshape(CMP_STRIDE, 1, HEAD_DIM), (CMP_STRIDE, NSA_KV_HEADS, HEAD_DIM))
        return v.reshape(1, CMP_STRIDE * LANES)

    def w2_padded(w2, rope_lanes):
        out = []
        for g in range(NSA_KV_HEADS):
            if rope_lanes:
                src = w2.reshape(CMP_HIDDEN, 2, 1, HALF)
                pads = ((0, 0), (0, 0), (g, NSA_KV_HEADS - 1 - g), (0, 0))
            else:
                src = w2.reshape(CMP_HIDDEN, 1, HEAD_DIM)
                pads = ((0, 0), (g, NSA_KV_HEADS - 1 - g), (0, 0))
            out.append(jnp.pad(src, pads).reshape(CMP_HIDDEN, LANES))
        return jnp.concatenate(out, axis=0)

    k1 = ck_w1.reshape(2, CMP_STRIDE, HEAD_DIM, CMP_HIDDEN)
    v1 = cv_w1.reshape(2, CMP_STRIDE, HEAD_DIM, CMP_HIDDEN)
    compress_consts = (
        pe_rows(pe_k[:CMP_STRIDE], True), pe_rows(pe_k[CMP_STRIDE:], True),
        pe_rows(pe_v[:CMP_STRIDE], False), pe_rows(pe_v[CMP_STRIDE:], False),
        grouped(k1[0], True).astype(BF16), grouped(k1[1], True).astype(BF16),
        grouped(v1[0], False).astype(BF16), grouped(v1[1], False).astype(BF16),
        w2_padded(ck_w2, True).astype(BF16), w2_padded(cv_w2, False).astype(BF16),
    )
    return wq.astype(BF16), wkv.astype(BF16), wz.astype(BF16), wg.astype(BF16), wo.astype(BF16), compress_consts


def _dil_weights(w_in, w_out):
    d = w_in.shape[0]
    q, k, v, z = jnp.split(w_in, 4, axis=1)
    n_pairs = len(DIL_GROUP_HEADS) * DIL_TILES_PER_GROUP
    offs = np.cumsum((0,) + DIL_GROUP_HEADS).tolist()

    def pad_heads(w, axis):
        parts = []
        for gi, hn in enumerate(DIL_GROUP_HEADS):
            sl = [slice(None)] * w.ndim
            sl[axis] = slice(offs[gi] * HEAD_DIM, offs[gi + 1] * HEAD_DIM)
            pads = [(0, 0)] * w.ndim
            pads[axis] = (0, (2 * DIL_TILES_PER_GROUP - hn) * HEAD_DIM)
            parts.append(jnp.pad(w[tuple(sl)], pads))
        return jnp.concatenate(parts, axis=axis)

    scale = HEAD_DIM ** -0.5
    wq = _pair_rope_layout(pad_heads(q * scale, 1), n_pairs)
    wk = _pair_rope_layout(pad_heads(k, 1), n_pairs)
    return (wq.astype(BF16), wk.astype(BF16), pad_heads(v, 1).astype(BF16), pad_heads(z, 1).astype(BF16),
            pad_heads(w_out, 0).astype(BF16))


def _selection_constants(t):
    n_cmp_rows = t // CMP_STRIDE
    n_slc = t // SLC_LEN
    c_start = CMP_STRIDE * np.arange(n_cmp_rows)
    s_start = SLC_LEN * np.arange(n_slc)
    ovl_t = ((c_start[None, :] < s_start[:, None] + SLC_LEN)
             & (c_start[None, :] + CMP_LEN > s_start[:, None])).astype(np.float32)
    ovl_t[:, (t - CMP_LEN) // CMP_STRIDE + 1:] = 0.0
    eexp = np.zeros((NSA_KV_HEADS, t // SLC_CK, LANES, SLC_CK), np.float32)
    for g in range(NSA_KV_HEADS):
        for kc in range(t // SLC_CK):
            for kk in range(SLC_CK):
                eexp[g, kc, g * n_slc + (kc * SLC_CK + kk) // SLC_LEN, kk] = 1.0
    return jnp.asarray(ovl_t), jnp.asarray(eexp, dtype=BF16)


def _deinterleave(a, dil):
    b, t, n = a.shape
    if dil == 1:
        return a
    return a.reshape(b, t // dil, dil, n).transpose(0, 2, 1, 3).reshape(b * dil, t // dil, n)


def _interleave(a, dil, b):
    if dil == 1:
        return a
    nb, length, n = a.shape
    return a.reshape(b, dil, length, n).transpose(0, 2, 1, 3).reshape(b, length * dil, n)


@jax.jit
def kernel(x, c, norm_g, ada_w, ada_b, nsa_w_in, nsa_pe_k, nsa_pe_v, nsa_ck_w1, nsa_ck_w2,
           nsa_cv_w1, nsa_cv_w2, nsa_w_out, dil_w_in, dil_w_out, final_g):
    b, t, d = x.shape
    assert t % SLC_CK == 0 and t % TM == 0 and d % LANES == 0
    assert NSA_KV_HEADS * (t // SLC_LEN) <= LANES and t // CMP_STRIDE == LANES
    mod = _adaln_mod(c, ada_w, ada_b).reshape(ada_w.shape[0], b, 3, d)
    cos, sin = _rope_tables(t)

    wq, wkv, wz, wg, wo, compress_consts = _nsa_weights(
        nsa_w_in[0], nsa_pe_k[0], nsa_pe_v[0], nsa_ck_w1[0], nsa_ck_w2[0], nsa_cv_w1[0], nsa_cv_w2[0], nsa_w_out[0])
    qp, qr, kc, vc, ks, vs, kw, vw, z, gl = _inproj0(x, mod[0], norm_g[0:1], cos, sin, wq, wkv, wz, wg)
    rows16 = lambda a: a.reshape(b, t // CMP_STRIDE, CMP_STRIDE * LANES)
    kcmp, vcmp = _compress(rows16(kc), rows16(vc), compress_consts)
    ovl_t, eexp = _selection_constants(t)
    ocmp, msel = _cmp_attn(qp, kcmp, vcmp, ovl_t)
    og = _nsa_attn(qr, ks, vs, kw, vw, msel, eexp, ocmp, z, gl)
    x1 = _outproj0(og, wo, x, mod[0])

    dq, dk, dv, dz, dwo = _dil_weights(dil_w_in[0], dil_w_out[0])
    q1, k1, v1, z1 = _inproj1(x1, mod[1], norm_g[1:2], cos, sin, dq, dk, dv, dz)
    gw = DIL_TILES_PER_GROUP * LANES
    o_groups, lse_groups = [], []
    for gi, (win, dil) in enumerate(DIL_PATTERNS):
        assert win // dil == DIL_WIN
        part = lambda a: _deinterleave(a[:, :, gi * gw:(gi + 1) * gw], dil)
        o, lse = _dil_attn(part(q1), part(k1), part(v1), DIL_GROUP_HEADS[gi])
        o_groups.append(_interleave(o, dil, b))
        lse_groups.append(_interleave(lse, dil, b))
    return _outproj1(o_groups, lse_groups, z1, dwo, x1, mod[1], final_g.reshape(1, d))
```

```python
import functools
import math

import numpy as np
import jax
import jax.numpy as jnp
from jax import lax
from jax.experimental import pallas as pl
from jax.experimental.pallas import tpu as pltpu

F32 = jnp.float32
BF16 = jnp.bfloat16
HIGHEST = lax.Precision.HIGHEST

HEAD_DIM = 64
HALF = HEAD_DIM // 2
N_HEADS = 16
ROPE_THETA = 10000.0
NORM_EPS = 1e-6
NSA_KV_HEADS = 2
NSA_GROUP = N_HEADS // NSA_KV_HEADS
NSA_BRANCHES = 3
CMP_LEN = 32
CMP_STRIDE = 16
CMP_HIDDEN = 256
SLC_LEN = 64
SLC_TOP = 16
WIN_LEN = 512
DIL_PATTERNS = ((128, 1), (512, 4), (2048, 16))
DIL_GROUP_HEADS = (6, 5, 5)
DIL_WIN = 128
DIL_TILES_PER_GROUP = 3

LANES = 128
VMEM_LIMIT_BYTES = 48 * 1024 * 1024

NEG = -1e30
TQ = 128
KEY_CHUNK = 128
SLC_CK = 512
TM = 256
DIL_STEP_TOKENS = 512


def _cparams(*sem):
    return pltpu.CompilerParams(dimension_semantics=sem, vmem_limit_bytes=VMEM_LIMIT_BYTES)


def _dot(a, b):
    return jnp.dot(a, b, preferred_element_type=F32)


def _dot_t(a, b):
    return lax.dot_general(a, b, (((1,), (1,)), ((), ())), preferred_element_type=F32)


def _silu(v):
    return v * jax.nn.sigmoid(v)


def _mod_kernel(c_ref, w_ref, b_ref, o_ref):
    s = _silu(c_ref[...])
    o_ref[...] = jnp.dot(s, w_ref[...], precision=HIGHEST, preferred_element_type=F32) + b_ref[...]


def _adaln_mod(c, ada_w, ada_b):
    depth, d, n3 = ada_w.shape
    b = c.shape[0]
    tn = 1024
    return pl.pallas_call(
        _mod_kernel,
        grid=(depth, n3 // tn),
        in_specs=[
            pl.BlockSpec((b, d), lambda i, j: (0, 0)),
            pl.BlockSpec((None, d, tn), lambda i, j: (i, 0, j)),
            pl.BlockSpec((None, 1, tn), lambda i, j: (i, 0, j)),
        ],
        out_specs=pl.BlockSpec((None, b, tn), lambda i, j: (i, 0, j)),
        out_shape=jax.ShapeDtypeStruct((depth, b, n3), F32),
        compiler_params=_cparams("arbitrary", "arbitrary"),
        name="adaln_mod",
    )(c, ada_w, ada_b.reshape(depth, 1, n3))


def _modulated_norm(x, mod_ref, g_ref):
    ms = jnp.mean(x * x, axis=-1, keepdims=True)
    y = x * lax.rsqrt(ms + NORM_EPS) * g_ref[...]
    return y * (1.0 + mod_ref[1:2, :]) + mod_ref[0:1, :]


def _rope_tile(v, cos, sin):
    return v * cos + pltpu.roll(v, 2 * HALF, axis=1) * sin


def _store_key_chunks_t(kt_ref, k):
    for c in range(kt_ref.shape[0]):
        kt_ref[c] = k[c * KEY_CHUNK:(c + 1) * KEY_CHUNK, :].T.astype(BF16)


def _inproj0_kernel(x_ref, mod_ref, g_ref, cos_ref, sin_ref, wq_ref, wkv_ref, wz_ref, wg_ref,
                    qp_ref, qr_ref, kc_ref, vc_ref, ks_ref, vs_ref, kw_ref, vw_ref, z_ref, gl_ref):
    h = _modulated_norm(x_ref[...], mod_ref, g_ref).astype(BF16)
    cos = cos_ref[...]
    sin = sin_ref[...]
    q = _dot(h, wq_ref[...])
    for j in range(q.shape[1] // LANES):
        sl = slice(j * LANES, (j + 1) * LANES)
        qp_ref[:, sl] = q[:, sl].astype(BF16)
        qr_ref[:, sl] = _rope_tile(q[:, sl], cos, sin).astype(BF16)
    kv = _dot(h, wkv_ref[...])
    tiles = [kv[:, i * LANES:(i + 1) * LANES] for i in range(6)]
    kc_ref[...] = tiles[0].astype(BF16)
    vc_ref[...] = tiles[1].astype(BF16)
    _store_key_chunks_t(ks_ref, _rope_tile(tiles[2], cos, sin))
    vs_ref[...] = tiles[3].astype(BF16)
    _store_key_chunks_t(kw_ref, _rope_tile(tiles[4], cos, sin))
    vw_ref[...] = tiles[5].astype(BF16)
    z_ref[...] = _dot(h, wz_ref[...]).astype(BF16)
    gl_ref[...] = _dot(h, wg_ref[...])


def _inproj0(x, mod, g, cos, sin, wq, wkv, wz, wg):
    b, t, d = x.shape
    nq, nz = wq.shape[1], wz.shape[1]
    tok = lambda n: pl.BlockSpec((None, TM, n), lambda i, j: (i, j, 0))
    full = lambda a: pl.BlockSpec(a.shape, lambda i, j: (0,) * a.ndim)
    tab = pl.BlockSpec((TM, LANES), lambda i, j: (j, 0))
    shp = lambda n, dt: jax.ShapeDtypeStruct((b, t, n), dt)
    kt = pl.BlockSpec((None, TM // KEY_CHUNK, LANES, KEY_CHUNK), lambda i, j: (i, j, 0, 0))
    kt_shp = jax.ShapeDtypeStruct((b, t // KEY_CHUNK, LANES, KEY_CHUNK), BF16)
    kv = tok(LANES)
    kv_shp = shp(LANES, BF16)
    return pl.pallas_call(
        _inproj0_kernel,
        grid=(b, t // TM),
        in_specs=[tok(d), pl.BlockSpec((None, 3, d), lambda i, j: (i, 0, 0)), full(g), tab, tab,
                  full(wq), full(wkv), full(wz), full(wg)],
        out_specs=[tok(nq), tok(nq), kv, kv, kt, kv, kt, kv, tok(nz), tok(LANES)],
        out_shape=[shp(nq, BF16), shp(nq, BF16), kv_shp, kv_shp, kt_shp, kv_shp, kt_shp, kv_shp,
                   shp(nz, BF16), shp(LANES, F32)],
        compiler_params=_cparams("parallel", "parallel"),
        name="inproj_nsa",
    )(x, mod, g, cos, sin, wq, wkv, wz, wg)


def _compress_kernel(ak_ref, av_ref, pekt_ref, pekb_ref, pevt_ref, pevb_ref,
                     w1kt_ref, w1kb_ref, w1vt_ref, w1vb_ref, w2k_ref, w2v_ref, kcmp_ref, vcmp_ref):
    def mlp(a_ref, pet_ref, peb_ref, w1t_ref, w1b_ref, w2_ref):
        a = a_ref[...].astype(F32)
        top = _dot((a + pet_ref[...]).astype(BF16), w1t_ref[...])
        bot = _dot((a + peb_ref[...]).astype(BF16), w1b_ref[...])
        n = bot.shape[0]
        hid = top + pltpu.roll(bot, n - 1, axis=0)
        return _dot(_silu(hid).astype(BF16), w2_ref[...])

    kcmp_ref[...] = mlp(ak_ref, pekt_ref, pekb_ref, w1kt_ref, w1kb_ref, w2k_ref).astype(BF16)
    vcmp_ref[...] = mlp(av_ref, pevt_ref, pevb_ref, w1vt_ref, w1vb_ref, w2v_ref).astype(BF16)


def _compress(ak, av, consts):
    b, n, w = ak.shape
    full = lambda a: pl.BlockSpec(a.shape, lambda i: (0,) * a.ndim)
    blk = pl.BlockSpec((None, n, w), lambda i: (i, 0, 0))
    out = pl.BlockSpec((None, n, LANES), lambda i: (i, 0, 0))
    return pl.pallas_call(
        _compress_kernel,
        grid=(b,),
        in_specs=[blk, blk] + [full(a) for a in consts],
        out_specs=[out, out],
        out_shape=[jax.ShapeDtypeStruct((b, n, LANES), BF16)] * 2,
        compiler_params=_cparams("parallel"),
        name="compress",
    )(ak, av, *consts)


def _stack_heads(q_ref, lane_sets):
    n_tiles = q_ref.shape[1] // LANES
    parts = []
    for j in range(n_tiles):
        qj = q_ref[:, j * LANES:(j + 1) * LANES]
        parts.append(jnp.where(lane_sets, qj, jnp.zeros_like(qj)))
    return jnp.concatenate(parts, axis=0)


def _cmp_attn_kernel(qp_ref, kcmp_ref, vcmp_ref, ovl_ref, ocmp_ref, msel_ref, *, n_slc):
    tq = qp_ref.shape[0]
    n_cmp = vcmp_ref.shape[0]
    hg = qp_ref.shape[1] // LANES
    t0 = pl.program_id(1) * tq
    lane = lax.broadcasted_iota(jnp.int32, (tq, LANES), 1)
    kc = kcmp_ref[...]
    vc = vcmp_ref[...]
    row = lax.broadcasted_iota(jnp.int32, (hg * tq, n_cmp), 0)
    col = lax.broadcasted_iota(jnp.int32, (hg * tq, n_cmp), 1)
    tpos = t0 + (row & (tq - 1))
    valid = (CMP_STRIDE * col + (CMP_LEN - 1)) <= tpos

    jb = lax.broadcasted_iota(jnp.int32, (n_slc, tq), 0)
    cur = (t0 + lax.broadcasted_iota(jnp.int32, (n_slc, tq), 1)) // SLC_LEN
    visible = jb <= cur
    forced = (jb == 0) | (jb == cur) | (jb == cur - 1)

    o_groups = []
    sel_groups = []
    for g in range(NSA_KV_HEADS):
        in_group = ((lane % HEAD_DIM) // HALF) == g
        qs = _stack_heads(qp_ref, in_group)
        s = jnp.where(valid, _dot_t(qs, kc), NEG)
        m = jnp.max(s, axis=-1, keepdims=True)
        e = jnp.where(valid, jnp.exp(s - m), 0.0)
        den = jnp.sum(e, axis=-1, keepdims=True)
        p = e / jnp.where(den > 0, den, 1.0)
        o_groups.append(_dot(p.astype(BF16), vc))
        p_sum = p[0:tq]
        for j in range(1, hg):
            p_sum = p_sum + p[j * tq:(j + 1) * tq]
        imp = lax.dot_general(ovl_ref[...], p_sum, (((1,), (1,)), ((), ())),
                              precision=HIGHEST, preferred_element_type=F32)
        rank = jnp.where(forced, -NEG, jnp.where(visible, imp, NEG))
        cnt = jnp.zeros((n_slc, tq), F32)
        for jp in range(n_slc):
            rj = rank[jp:jp + 1, :]
            tie = jnp.where(jb > jp, 1.0, 0.0)
            cnt = cnt + jnp.where(rj > rank, 1.0, jnp.where(rj == rank, tie, 0.0))
        sel_groups.append(jnp.where(visible, jnp.where(cnt < SLC_TOP, 1.0, 0.0), 0.0))

    for j in range(hg):
        rows = slice(j * tq, (j + 1) * tq)
        ocmp_ref[:, j * LANES:(j + 1) * LANES] = jnp.where(
            lane < HEAD_DIM, o_groups[0][rows], o_groups[1][rows]).astype(BF16)
    pad = jnp.zeros((LANES - NSA_KV_HEADS * n_slc, tq), F32)
    sel_t = jnp.concatenate(sel_groups + [pad], axis=0)
    msel_ref[...] = sel_t.T.astype(BF16)


def _cmp_attn(qp, kcmp, vcmp, ovl_t):
    b, t, nq = qp.shape
    n_cmp = kcmp.shape[1]
    n_slc = t // SLC_LEN
    tok = lambda n: pl.BlockSpec((None, TQ, n), lambda i, j: (i, j, 0))
    per_b = pl.BlockSpec((None, n_cmp, LANES), lambda i, j: (i, 0, 0))
    return pl.pallas_call(
        functools.partial(_cmp_attn_kernel, n_slc=n_slc),
        grid=(b, t // TQ),
        in_specs=[tok(nq), per_b, per_b, pl.BlockSpec(ovl_t.shape, lambda i, j: (0, 0))],
        out_specs=[tok(nq), tok(LANES)],
        out_shape=[jax.ShapeDtypeStruct((b, t, nq), BF16), jax.ShapeDtypeStruct((b, t, LANES), BF16)],
        compiler_params=_cparams("parallel", "parallel"),
        name="cmp_attn_select",
    )(qp, kcmp, vcmp, ovl_t)


def _nsa_attn_kernel(qr_ref, ks_ref, vs_ref, kw_ref, vw_ref, msel_ref, e_ref, ge_ref, ocmp_ref, z_ref, gl_ref,
                     og_ref, qa_scr, qe_scr, m_scr, acc_scr, *, n_slc, ck):
    tq = qr_ref.shape[0]
    hg = qr_ref.shape[1] // LANES
    rows = hg * tq
    n_wc = WIN_LEN // tq + 1
    per_ck = ck // KEY_CHUNK
    qi = pl.program_id(1)
    t0 = qi * tq
    lane = lax.broadcasted_iota(jnp.int32, (tq, LANES), 1)
    a_idx = lax.broadcasted_iota(jnp.int32, (tq, LANES), 0)

    eye = jnp.where(lane == a_idx, 1.0, 0.0).astype(BF16)
    earlier = jnp.where((lane % n_slc) < t0 // SLC_LEN, 0.0, NEG)
    mneg = jnp.where(msel_ref[...].astype(F32) > 0.5, earlier, NEG).astype(BF16)
    for g in range(NSA_KV_HEADS):
        in_group = ((lane % HEAD_DIM) // HALF) == g
        for j in range(hg):
            qj = qr_ref[:, j * LANES:(j + 1) * LANES]
            qj = jnp.where(in_group, qj, jnp.zeros_like(qj))
            rs = slice(j * tq, (j + 1) * tq)
            qa_scr[g, rs, 0:LANES] = qj
            qa_scr[g, rs, LANES:2 * LANES] = mneg
            qe_scr[g, rs, 0:LANES] = qj
            qe_scr[g, rs, LANES:2 * LANES] = eye
    zero_b = jnp.zeros((tq, LANES), BF16)
    neg_b = jnp.full((tq, LANES), NEG, BF16)
    causal_b = jnp.where(lane <= a_idx, 0.0, NEG).astype(BF16)
    far_b = jnp.where(lane > a_idx, 0.0, NEG).astype(BF16)
    own_half = [lane < HEAD_DIM, lane >= HEAD_DIM]

    def values_with_ones(v, g):
        in_own_half = (lax.broadcasted_iota(jnp.int32, v.shape, 1) // HEAD_DIM) == g
        return jnp.where(in_own_half, v, jnp.ones_like(v))

    def lane_tiles(v):
        return [v[:, i * LANES:(i + 1) * LANES] for i in range(v.shape[1] // LANES)]

    def tile_max(tiles):
        mx = tiles[0]
        for v in tiles[1:]:
            mx = jnp.maximum(mx, v)
        return jnp.max(mx, axis=-1, keepdims=True)

    def probs(tiles, m):
        return jnp.concatenate([jnp.exp(v - m) for v in tiles], axis=1).astype(BF16)

    m_scr[...] = jnp.full(m_scr.shape, NEG, F32)
    acc_scr[...] = jnp.zeros(acc_scr.shape, F32)

    def earlier_keys(kc, carry):
        k0 = pl.multiple_of(kc * ck, ck)
        for g in range(NSA_KV_HEADS):
            kt = jnp.concatenate([ks_ref[kc * per_ck + i] for i in range(per_ck)], axis=1)
            eb = jnp.concatenate([e_ref[g, kc * per_ck + i] for i in range(per_ck)], axis=1)
            tiles = lane_tiles(_dot(qa_scr[g], jnp.concatenate([kt, eb], axis=0)))
            m_old = m_scr[g]
            m_new = jnp.maximum(m_old, tile_max(tiles))
            pv = _dot(probs(tiles, m_new), values_with_ones(vs_ref[pl.ds(k0, ck), :], g))
            acc_scr[g] = jnp.exp(m_old - m_new) * acc_scr[g] + pv
            m_scr[g] = m_new
        return carry

    lax.fori_loop(0, (t0 + ck - 1) // ck, earlier_keys, 0)

    gates = jax.nn.sigmoid(gl_ref[...])
    g_hi = gates.astype(BF16)
    g_lo = (gates - g_hi.astype(F32)).astype(BF16)
    g_two = jnp.concatenate([g_hi, g_lo], axis=1)
    gexp = [_dot(g_two, jnp.concatenate([ge_ref[br], ge_ref[br]], axis=0)) for br in range(NSA_BRANCHES)]

    kd = pl.multiple_of(t0, tq)
    zeros_v = jnp.zeros((tq, LANES), BF16)
    acc_slc, acc_win = [], []
    for g in range(NSA_KV_HEADS):
        own_k = jnp.concatenate([jnp.concatenate([kw_ref[qi], causal_b], axis=0),
                                 jnp.concatenate([ks_ref[qi], causal_b], axis=0)], axis=1)
        s_own = _dot(qe_scr[g], own_k)
        s_win_own, s_slc_own = s_own[:, 0:tq], s_own[:, tq:2 * tq]
        kbs, vbs = [], []
        for c in range(n_wc - 1):
            cidx = qi - (n_wc - 1 - c)
            csafe = jnp.maximum(cidx, 0)
            bias = far_b if c == 0 else zero_b
            kbs.append(jnp.concatenate([kw_ref[csafe], jnp.where(cidx >= 0, bias, neg_b)], axis=0))
            vbs.append(values_with_ones(vw_ref[pl.ds(pl.multiple_of(csafe * tq, tq), tq), :], g))
        pairs = [(c, c + 1) for c in range(0, n_wc - 1, 2)]
        scores = [_dot(qe_scr[g], jnp.concatenate([kbs[c] for c in pr], axis=1)) for pr in pairs]
        m_win = tile_max([v for s in scores for v in lane_tiles(s)] + [s_win_own])
        m_old = m_scr[g]
        m_slc = jnp.maximum(m_old, jnp.max(s_slc_own, axis=-1, keepdims=True))
        win_v = [jnp.concatenate([v, zeros_v], axis=1) for v in vbs + [values_with_ones(vw_ref[pl.ds(kd, tq), :], g)]]
        slc_v = jnp.concatenate([zeros_v, values_with_ones(vs_ref[pl.ds(kd, tq), :], g)], axis=1)
        p_all = jnp.concatenate([probs(lane_tiles(s), m_win) for s in scores]
                                + [probs([s_win_own], m_win), probs([s_slc_own], m_slc)], axis=1)
        pv = _dot(p_all, jnp.concatenate(win_v + [slc_v], axis=0))
        acc_win.append(pv[:, 0:LANES])
        acc_slc.append(jnp.exp(m_old - m_slc) * acc_scr[g] + pv[:, LANES:2 * LANES])

    def normalised(acc0, acc1):
        both = jnp.where(own_half[0], acc0, acc1)
        sums = pltpu.roll(jnp.where(own_half[0], acc1, acc0), HEAD_DIM, axis=1)
        return both / sums

    for j in range(hg):
        rs = slice(j * tq, (j + 1) * tq)
        cols = slice(j * LANES, (j + 1) * LANES)
        o = (gexp[0][:, cols] * ocmp_ref[:, cols].astype(F32)
             + gexp[1][:, cols] * normalised(acc_slc[0][rs], acc_slc[1][rs])
             + gexp[2][:, cols] * normalised(acc_win[0][rs], acc_win[1][rs]))
        og_ref[:, cols] = (o * _silu(z_ref[:, cols].astype(F32))).astype(BF16)


def _nsa_attn(qr, ks, vs, kw, vw, msel, e, ge, ocmp, z, gl):
    b, t, nq = qr.shape
    hg = nq // LANES
    rows = hg * TQ
    assert TQ == KEY_CHUNK and WIN_LEN % TQ == 0
    tok = lambda n: pl.BlockSpec((None, TQ, n), lambda i, j: (i, j, 0))
    per_b = pl.BlockSpec((None, t, LANES), lambda i, j: (i, 0, 0))
    per_b_t = pl.BlockSpec((None, t // KEY_CHUNK, LANES, KEY_CHUNK), lambda i, j: (i, 0, 0, 0))
    full = lambda a: pl.BlockSpec(a.shape, lambda i, j: (0,) * a.ndim)
    return pl.pallas_call(
        functools.partial(_nsa_attn_kernel, n_slc=t // SLC_LEN, ck=SLC_CK),
        grid=(b, t // TQ),
        in_specs=[tok(nq), per_b_t, per_b, per_b_t, per_b, tok(LANES), full(e), full(ge), tok(nq), tok(nq), tok(LANES)],
        out_specs=tok(nq),
        out_shape=jax.ShapeDtypeStruct((b, t, nq), BF16),
        scratch_shapes=[
            pltpu.VMEM((NSA_KV_HEADS, rows, 2 * LANES), BF16),
            pltpu.VMEM((NSA_KV_HEADS, rows, 2 * LANES), BF16),
            pltpu.VMEM((NSA_KV_HEADS, rows, LANES), F32),
            pltpu.VMEM((NSA_KV_HEADS, rows, LANES), F32),
        ],
        compiler_params=_cparams("parallel", "arbitrary"),
        name="nsa_slc_win_attn",
    )(qr, ks, vs, kw, vw, msel, e, ge, ocmp, z, gl)


def _outproj0_kernel(og_ref, w_ref, x_ref, mod_ref, o_ref):
    y = _dot(og_ref[...], w_ref[...])
    o_ref[...] = x_ref[...] + mod_ref[2:3, :] * y


def _outproj0(og, w, x, mod):
    b, t, d = x.shape
    n = og.shape[2]
    tok = lambda k: pl.BlockSpec((None, TM, k), lambda i, j: (i, j, 0))
    return pl.pallas_call(
        _outproj0_kernel,
        grid=(b, t // TM),
        in_specs=[tok(n), pl.BlockSpec(w.shape, lambda i, j: (0, 0)), tok(d),
                  pl.BlockSpec((None, 3, d), lambda i, j: (i, 0, 0))],
        out_specs=tok(d),
        out_shape=jax.ShapeDtypeStruct((b, t, d), F32),
        compiler_params=_cparams("parallel", "parallel"),
        name="outproj_nsa",
    )(og, w, x, mod)


def _inproj1_kernel(x_ref, mod_ref, g_ref, cos_ref, sin_ref, wq_ref, wk_ref, wv_ref, wz_ref,
                    q_ref, k_ref, v_ref, z_ref):
    h = _modulated_norm(x_ref[...], mod_ref, g_ref).astype(BF16)
    cos = cos_ref[...]
    sin = sin_ref[...]
    for w_ref, o_ref in ((wq_ref, q_ref), (wk_ref, k_ref)):
        u = _dot(h, w_ref[...])
        for j in range(u.shape[1] // LANES):
            sl = slice(j * LANES, (j + 1) * LANES)
            o_ref[:, sl] = _rope_tile(u[:, sl], cos, sin).astype(BF16)
    v_ref[...] = _dot(h, wv_ref[...]).astype(BF16)
    z_ref[...] = _dot(h, wz_ref[...]).astype(BF16)


def _inproj1(x, mod, g, cos, sin, wq, wk, wv, wz):
    b, t, d = x.shape
    n = wq.shape[1]
    tok = lambda k: pl.BlockSpec((None, TM, k), lambda i, j: (i, j, 0))
    full = lambda a: pl.BlockSpec(a.shape, lambda i, j: (0,) * a.ndim)
    tab = pl.BlockSpec((TM, LANES), lambda i, j: (j, 0))
    return pl.pallas_call(
        _inproj1_kernel,
        grid=(b, t // TM),
        in_specs=[tok(d), pl.BlockSpec((None, 3, d), lambda i, j: (i, 0, 0)), full(g), tab, tab,
                  full(wq), full(wk), full(wv), full(wz)],
        out_specs=[tok(n)] * 4,
        out_shape=[jax.ShapeDtypeStruct((b, t, n), BF16)] * 4,
        compiler_params=_cparams("parallel", "parallel"),
        name="inproj_dil",
    )(x, mod, g, cos, sin, wq, wk, wv, wz)


def _dil_attn_kernel(q_ref, kt_ref, v_ref, o_ref, lse_ref, *, n_heads, tq):
    n_seq, tq_blk = q_ref.shape[0], q_ref.shape[1]
    n_tiles = q_ref.shape[2] // LANES
    i = pl.program_id(1)
    lane = lax.broadcasted_iota(jnp.int32, (tq, LANES), 1)
    a_idx = lax.broadcasted_iota(jnp.int32, (tq, LANES), 0)
    first = ((lane % HEAD_DIM) // HALF) == 0
    eye = jnp.where(lane == a_idx, 1.0, 0.0).astype(BF16)
    diag_b = jnp.where(lane <= a_idx, 0.0, NEG).astype(BF16)
    prev_b = jnp.where(lane >= a_idx, 0.0, NEG).astype(BF16)
    neg_b = jnp.full((tq, LANES), NEG, BF16)
    for sq in range(n_seq):
        for sub in range(tq_blk // tq):
            cdiag = i * (tq_blk // tq) + sub
            cprev = jnp.maximum(cdiag - 1, 0)
            kdiag = pl.multiple_of(cdiag * tq, tq)
            kprev = pl.multiple_of(cprev * tq, tq)
            prev_bias = prev_b if sub > 0 else jnp.where(i > 0, prev_b, neg_b)
            rows = slice(sub * tq, (sub + 1) * tq)
            lses = []
            for jt in range(n_tiles):
                cols = slice(jt * LANES, (jt + 1) * LANES)
                qj = q_ref[sq, rows, cols]
                zero = jnp.zeros_like(qj)
                qs = jnp.concatenate([jnp.concatenate([jnp.where(first, qj, zero), eye], axis=1),
                                      jnp.concatenate([jnp.where(first, zero, qj), eye], axis=1)], axis=0)
                kb = jnp.concatenate([jnp.concatenate([kt_ref[sq, cprev, jt], prev_bias], axis=0),
                                      jnp.concatenate([kt_ref[sq, cdiag, jt], diag_b], axis=0)], axis=1)
                vb = jnp.concatenate([v_ref[sq, pl.ds(kprev, tq), cols], v_ref[sq, pl.ds(kdiag, tq), cols]], axis=0)
                s = _dot(qs, kb)
                m = jnp.max(s, axis=-1, keepdims=True)
                p = jnp.exp(s - m)
                l = jnp.sum(p, axis=-1, keepdims=True)
                o = _dot(p.astype(BF16), vb) / l
                o_ref[sq, rows, cols] = jnp.where(lane < HEAD_DIM, o[0:tq], o[tq:2 * tq]).astype(BF16)
                lse = m + jnp.log(l)
                lses.append(lse[0:tq])
                if 2 * jt + 1 < n_heads:
                    lses.append(lse[tq:2 * tq])
            top = lses[0]
            for v in lses[1:]:
                top = jnp.maximum(top, v)
            tot = jnp.zeros_like(top)
            for v in lses:
                tot = tot + jnp.exp(v - top)
            group_lse = top + jnp.log(tot) - math.log(n_heads)
            lse_ref[sq, rows, :] = jnp.broadcast_to(group_lse, (tq, LANES))


def _dil_attn(q, k, v, n_heads):
    nb, length, n = q.shape
    tq = min(TQ, length)
    assert tq == KEY_CHUNK
    tq_blk = min(DIL_STEP_TOKENS, length)
    n_seq = DIL_STEP_TOKENS // tq_blk
    kt = k.reshape(nb, length // KEY_CHUNK, KEY_CHUNK, n // LANES, LANES).transpose(0, 1, 3, 4, 2)
    tok = lambda w: pl.BlockSpec((n_seq, tq_blk, w), lambda i, j: (i, j, 0))
    per_b = pl.BlockSpec((n_seq, length, n), lambda i, j: (i, 0, 0))
    per_b_t = pl.BlockSpec((n_seq,) + kt.shape[1:], lambda i, j: (i, 0, 0, 0, 0))
    return pl.pallas_call(
        functools.partial(_dil_attn_kernel, n_heads=n_heads, tq=tq),
        grid=(nb // n_seq, length // tq_blk),
        in_specs=[tok(n), per_b_t, per_b],
        out_specs=[tok(n), tok(LANES)],
        out_shape=[jax.ShapeDtypeStruct((nb, length, n), BF16), jax.ShapeDtypeStruct((nb, length, LANES), F32)],
        compiler_params=_cparams("parallel", "parallel"),
        name="dil_attn",
    )(q, kt, v)


def _outproj1_kernel(o0_ref, o1_ref, o2_ref, l0_ref, l1_ref, l2_ref, z_ref, w_ref, x_ref, mod_ref, fg_ref, out_ref):
    ls = [l0_ref[...], l1_ref[...], l2_ref[...]]
    top = jnp.maximum(jnp.maximum(ls[0], ls[1]), ls[2])
    es = [jnp.exp(v - top) for v in ls]
    den = es[0] + es[1] + es[2]
    n_grp = len(ls)
    parts = []
    for gi, o_ref in enumerate((o0_ref, o1_ref, o2_ref)):
        alpha = es[gi] / den * float(n_grp)
        for jt in range(o_ref.shape[1] // LANES):
            c0 = (gi * DIL_TILES_PER_GROUP + jt) * LANES
            zt = z_ref[:, c0:c0 + LANES].astype(F32)
            ot = o_ref[:, jt * LANES:(jt + 1) * LANES].astype(F32)
            parts.append((ot * alpha * _silu(zt)).astype(BF16))
    og = jnp.concatenate(parts, axis=1)
    y = _dot(og, w_ref[...])
    x2 = x_ref[...] + mod_ref[2:3, :] * y
    ms = jnp.mean(x2 * x2, axis=-1, keepdims=True)
    out_ref[...] = x2 * lax.rsqrt(ms + NORM_EPS) * fg_ref[...]


def _outproj1(o_groups, lse_groups, z, w, x, mod, final_g):
    b, t, d = x.shape
    tok = lambda k: pl.BlockSpec((None, TM, k), lambda i, j: (i, j, 0))
    full = lambda a: pl.BlockSpec(a.shape, lambda i, j: (0,) * a.ndim)
    ng = o_groups[0].shape[2]
    return pl.pallas_call(
        _outproj1_kernel,
        grid=(b, t // TM),
        in_specs=[tok(ng)] * 3 + [tok(LANES)] * 3 + [tok(z.shape[2]), full(w), tok(d),
                  pl.BlockSpec((None, 3, d), lambda i, j: (i, 0, 0)), full(final_g)],
        out_specs=tok(d),
        out_shape=jax.ShapeDtypeStruct((b, t, d), F32),
        compiler_params=_cparams("parallel", "parallel"),
        name="outproj_dil_final",
    )(*o_groups, *lse_groups, z, w, x, mod, final_g)


def _rope_tables(t):
    inv = ROPE_THETA ** (-jnp.arange(HALF, dtype=F32) / HALF)
    ang = jnp.arange(t, dtype=F32)[:, None] * inv[None, :]
    cos = jnp.tile(jnp.cos(ang), (1, 4))
    sin = jnp.tile(jnp.sin(ang), (1, 4))
    sign = jnp.where(jnp.arange(LANES) < 2 * HALF, -1.0, 1.0).astype(F32)
    return cos, sin * sign[None, :]


def _pair_rope_layout(w, n_pairs):
    d = w.shape[0]
    return w.reshape(d, n_pairs, 2, 2, HALF).transpose(0, 1, 3, 2, 4).reshape(d, n_pairs * LANES)


def _nsa_weights(w_in, pe_k, pe_v, ck_w1, ck_w2, cv_w1, cv_w2, w_out):
    d = w_in.shape[0]
    aw = N_HEADS * HEAD_DIM
    kvw = NSA_KV_HEADS * HEAD_DIM
    cuts = np.cumsum([aw] + [kvw] * 6 + [aw]).tolist()
    q, k_c, v_c, k_s, v_s, k_w, v_w, z, gl = jnp.split(w_in, cuts, axis=1)
    scale = HEAD_DIM ** -0.5
    wq = (q * scale).reshape(d, NSA_KV_HEADS, NSA_GROUP, 2, HALF).transpose(0, 2, 3, 1, 4).reshape(d, aw)
    k_lay = lambda w: _pair_rope_layout(w, 1)
    wkv = jnp.concatenate([k_lay(k_c), v_c, k_lay(k_s), v_s, k_lay(k_w), v_w], axis=1)
    wz = z.reshape(d, NSA_KV_HEADS, NSA_GROUP, HEAD_DIM).transpose(0, 2, 1, 3).reshape(d, aw)
    wo = w_out.reshape(NSA_KV_HEADS, NSA_GROUP, HEAD_DIM, -1).transpose(1, 0, 2, 3).reshape(aw, -1)
    wg = jnp.pad(gl, ((0, 0), (0, LANES - gl.shape[1])))

    def grouped(w1_half, rope_lanes):
        out = []
        for g in range(NSA_KV_HEADS):
            if rope_lanes:
                src = w1_half.reshape(CMP_STRIDE, 2, 1, HALF, CMP_HIDDEN)
                pads = ((0, 0), (0, 0), (g, NSA_KV_HEADS - 1 - g), (0, 0), (0, 0))
            else:
                src = w1_half.reshape(CMP_STRIDE, 1, HEAD_DIM, CMP_HIDDEN)
                pads = ((0, 0), (g, NSA_KV_HEADS - 1 - g), (0, 0), (0, 0))
            out.append(jnp.pad(src, pads).reshape(CMP_STRIDE * LANES, CMP_HIDDEN))
        return jnp.concatenate(out, axis=1)

    def pe_rows(pe_half, rope_lanes):
        if rope_lanes:
            v = jnp.broadcast_to(pe_half.reshape(CMP_STRIDE, 2, 1, HALF), (CMP_STRIDE, 2, NSA_KV_HEADS, HALF))
        else:
            v = jnp.broadcast_to(pe_half.reshape(CMP_STRIDE, 1, HEAD_DIM), (CMP_STRIDE, NSA_KV_HEADS, HEAD_DIM))
        return v.reshape(1, CMP_STRIDE * LANES)

    def w2_padded(w2, rope_lanes):
        out = []
        for g in range(NSA_KV_HEADS):
            if rope_lanes:
                src = w2.reshape(CMP_HIDDEN, 2, 1, HALF)
                pads = ((0, 0), (0, 0), (g, NSA_KV_HEADS - 1 - g), (0, 0))
            else:
                src = w2.reshape(CMP_HIDDEN, 1, HEAD_DIM)
                pads = ((0, 0), (g, NSA_KV_HEADS - 1 - g), (0, 0))
            out.append(jnp.pad(src, pads).reshape(CMP_HIDDEN, LANES))
        return jnp.concatenate(out, axis=0)

    k1 = ck_w1.reshape(2, CMP_STRIDE, HEAD_DIM, CMP_HIDDEN)
    v1 = cv_w1.reshape(2, CMP_STRIDE, HEAD_DIM, CMP_HIDDEN)
    compress_consts = (
        pe_rows(pe_k[:CMP_STRIDE], True), pe_rows(pe_k[CMP_STRIDE:], True),
        pe_rows(pe_v[:CMP_STRIDE], False), pe_rows(pe_v[CMP_STRIDE:], False),
        grouped(k1[0], True).astype(BF16), grouped(k1[1], True).astype(BF16),
        grouped(v1[0], False).astype(BF16), grouped(v1[1], False).astype(BF16),
        w2_padded(ck_w2, True).astype(BF16), w2_padded(cv_w2, False).astype(BF16),
    )
    return wq.astype(BF16), wkv.astype(BF16), wz.astype(BF16), wg.astype(BF16), wo.astype(BF16), compress_consts


def _dil_weights(w_in, w_out):
    d = w_in.shape[0]
    q, k, v, z = jnp.split(w_in, 4, axis=1)
    n_pairs = len(DIL_GROUP_HEADS) * DIL_TILES_PER_GROUP
    offs = np.cumsum((0,) + DIL_GROUP_HEADS).tolist()

    def pad_heads(w, axis):
        parts = []
        for gi, hn in enumerate(DIL_GROUP_HEADS):
            sl = [slice(None)] * w.ndim
            sl[axis] = slice(offs[gi] * HEAD_DIM, offs[gi + 1] * HEAD_DIM)
            pads = [(0, 0)] * w.ndim
            pads[axis] = (0, (2 * DIL_TILES_PER_GROUP - hn) * HEAD_DIM)
            parts.append(jnp.pad(w[tuple(sl)], pads))
        return jnp.concatenate(parts, axis=axis)

    scale = HEAD_DIM ** -0.5
    wq = _pair_rope_layout(pad_heads(q * scale, 1), n_pairs)
    wk = _pair_rope_layout(pad_heads(k, 1), n_pairs)
    return (wq.astype(BF16), wk.astype(BF16), pad_heads(v, 1).astype(BF16), pad_heads(z, 1).astype(BF16),
            pad_heads(w_out, 0).astype(BF16))


def _selection_constants(t):
    n_cmp_rows = t // CMP_STRIDE
    n_slc = t // SLC_LEN
    c_start = CMP_STRIDE * np.arange(n_cmp_rows)
    s_start = SLC_LEN * np.arange(n_slc)
    ovl_t = ((c_start[None, :] < s_start[:, None] + SLC_LEN)
             & (c_start[None, :] + CMP_LEN > s_start[:, None])).astype(np.float32)
    ovl_t[:, (t - CMP_LEN) // CMP_STRIDE + 1:] = 0.0
    e = np.zeros((NSA_KV_HEADS, t // KEY_CHUNK, LANES, KEY_CHUNK), np.float32)
    keys = np.arange(t)
    for g in range(NSA_KV_HEADS):
        e[g, keys // KEY_CHUNK, g * n_slc + keys // SLC_LEN, keys % KEY_CHUNK] = 1.0
    ge = np.zeros((NSA_BRANCHES, LANES, NSA_GROUP * LANES), np.float32)
    out_lane = np.arange(NSA_GROUP * LANES)
    head_j, grp = out_lane // LANES, (out_lane % LANES) // HEAD_DIM
    for br in range(NSA_BRANCHES):
        ge[br, (grp * NSA_GROUP + head_j) * NSA_BRANCHES + br, out_lane] = 1.0
    return jnp.asarray(ovl_t), jnp.asarray(e, dtype=BF16), jnp.asarray(ge, dtype=BF16)


def _deinterleave(a, dil):
    b, t, n = a.shape
    if dil == 1:
        return a
    return a.reshape(b, t // dil, dil, n).transpose(0, 2, 1, 3).reshape(b * dil, t // dil, n)


def _interleave(a, dil, b):
    if dil == 1:
        return a
    nb, length, n = a.shape
    return a.reshape(b, dil, length, n).transpose(0, 2, 1, 3).reshape(b, length * dil, n)


@jax.jit
def kernel(x, c, norm_g, ada_w, ada_b, nsa_w_in, nsa_pe_k, nsa_pe_v, nsa_ck_w1, nsa_ck_w2,
           nsa_cv_w1, nsa_cv_w2, nsa_w_out, dil_w_in, dil_w_out, final_g):
    b, t, d = x.shape
    assert t % SLC_CK == 0 and t % TM == 0 and d % LANES == 0
    assert NSA_KV_HEADS * (t // SLC_LEN) <= LANES and t // CMP_STRIDE == LANES
    mod = _adaln_mod(c, ada_w, ada_b).reshape(ada_w.shape[0], b, 3, d)
    cos, sin = _rope_tables(t)

    wq, wkv, wz, wg, wo, compress_consts = _nsa_weights(
        nsa_w_in[0], nsa_pe_k[0], nsa_pe_v[0], nsa_ck_w1[0], nsa_ck_w2[0], nsa_cv_w1[0], nsa_cv_w2[0], nsa_w_out[0])
    qp, qr, kc, vc, ks, vs, kw, vw, z, gl = _inproj0(x, mod[0], norm_g[0:1], cos, sin, wq, wkv, wz, wg)
    rows16 = lambda a: a.reshape(b, t // CMP_STRIDE, CMP_STRIDE * LANES)
    kcmp, vcmp = _compress(rows16(kc), rows16(vc), compress_consts)
    ovl_t, e_sel, ge = _selection_constants(t)
    ocmp, msel = _cmp_attn(qp, kcmp, vcmp, ovl_t)
    og = _nsa_attn(qr, ks, vs, kw, vw, msel, e_sel, ge, ocmp, z, gl)
    x1 = _outproj0(og, wo, x, mod[0])

    dq, dk, dv, dz, dwo = _dil_weights(dil_w_in[0], dil_w_out[0])
    q1, k1, v1, z1 = _inproj1(x1, mod[1], norm_g[1:2], cos, sin, dq, dk, dv, dz)
    gw = DIL_TILES_PER_GROUP * LANES
    o_groups, lse_groups = [], []
    for gi, (win, dil) in enumerate(DIL_PATTERNS):
        assert win // dil == DIL_WIN
        part = lambda a: _deinterleave(a[:, :, gi * gw:(gi + 1) * gw], dil)
        o, lse = _dil_attn(part(q1), part(k1), part(v1), DIL_GROUP_HEADS[gi])
        o_groups.append(_interleave(o, dil, b))
        lse_groups.append(_interleave(lse, dil, b))
    return _outproj1(o_groups, lse_groups, z1, dwo, x1, mod[1], final_g.reshape(1, d))
```

```python
import functools
import math

import numpy as np
import jax
import jax.numpy as jnp
from jax import lax
from jax.experimental import pallas as pl
from jax.experimental.pallas import tpu as pltpu

F32 = jnp.float32
BF16 = jnp.bfloat16
HIGHEST = lax.Precision.HIGHEST

HEAD_DIM = 64
HALF = HEAD_DIM // 2
N_HEADS = 16
ROPE_THETA = 10000.0
NORM_EPS = 1e-6
NSA_KV_HEADS = 2
NSA_GROUP = N_HEADS // NSA_KV_HEADS
NSA_BRANCHES = 3
CMP_LEN = 32
CMP_STRIDE = 16
CMP_HIDDEN = 256
SLC_LEN = 64
SLC_TOP = 16
WIN_LEN = 512
DIL_PATTERNS = ((128, 1), (512, 4), (2048, 16))
DIL_GROUP_HEADS = (6, 5, 5)
DIL_WIN = 128
DIL_TILES_PER_GROUP = 3

LANES = 128
VMEM_LIMIT_BYTES = 48 * 1024 * 1024

NEG = -1e30
TQ = 128
KEY_CHUNK = 128
SLC_CK = 512
TM = 256
DIL_STEP_TOKENS = 512


def _cparams(*sem):
    return pltpu.CompilerParams(dimension_semantics=sem, vmem_limit_bytes=VMEM_LIMIT_BYTES)


def _dot(a, b):
    return jnp.dot(a, b, preferred_element_type=F32)


def _dot_t(a, b):
    return lax.dot_general(a, b, (((1,), (1,)), ((), ())), preferred_element_type=F32)


def _silu(v):
    return v * jax.nn.sigmoid(v)


def _mod_kernel(c_ref, w_ref, b_ref, o_ref):
    s = _silu(c_ref[...])
    o_ref[...] = jnp.dot(s, w_ref[...], precision=HIGHEST, preferred_element_type=F32) + b_ref[...]


def _adaln_mod(c, ada_w, ada_b):
    depth, d, n3 = ada_w.shape
    b = c.shape[0]
    tn = 1024
    return pl.pallas_call(
        _mod_kernel,
        grid=(depth, n3 // tn),
        in_specs=[
            pl.BlockSpec((b, d), lambda i, j: (0, 0)),
            pl.BlockSpec((None, d, tn), lambda i, j: (i, 0, j)),
            pl.BlockSpec((None, 1, tn), lambda i, j: (i, 0, j)),
        ],
        out_specs=pl.BlockSpec((None, b, tn), lambda i, j: (i, 0, j)),
        out_shape=jax.ShapeDtypeStruct((depth, b, n3), F32),
        compiler_params=_cparams("arbitrary", "arbitrary"),
        name="adaln_mod",
    )(c, ada_w, ada_b.reshape(depth, 1, n3))


def _modulated_norm(x, mod_ref, g_ref):
    ms = jnp.mean(x * x, axis=-1, keepdims=True)
    y = x * lax.rsqrt(ms + NORM_EPS) * g_ref[...]
    return y * (1.0 + mod_ref[1:2, :]) + mod_ref[0:1, :]


def _rope_tile(v, cos, sin):
    return v * cos + pltpu.roll(v, 2 * HALF, axis=1) * sin


def _store_key_chunks_t(kt_ref, k):
    for c in range(kt_ref.shape[0]):
        kt_ref[c] = k[c * KEY_CHUNK:(c + 1) * KEY_CHUNK, :].T.astype(BF16)


def _inproj0_kernel(x_ref, mod_ref, g_ref, cos_ref, sin_ref, wq_ref, wkv_ref, wz_ref, wg_ref,
                    qp_ref, qr_ref, kc_ref, vc_ref, ks_ref, vs_ref, kw_ref, vw_ref, z_ref, gl_ref):
    h = _modulated_norm(x_ref[...], mod_ref, g_ref).astype(BF16)
    cos = cos_ref[...]
    sin = sin_ref[...]
    q = _dot(h, wq_ref[...])
    for j in range(q.shape[1] // LANES):
        sl = slice(j * LANES, (j + 1) * LANES)
        qp_ref[:, sl] = q[:, sl].astype(BF16)
        qr_ref[:, sl] = _rope_tile(q[:, sl], cos, sin).astype(BF16)
    kv = _dot(h, wkv_ref[...])
    tiles = [kv[:, i * LANES:(i + 1) * LANES] for i in range(6)]
    kc_ref[...] = tiles[0].astype(BF16)
    vc_ref[...] = tiles[1].astype(BF16)
    _store_key_chunks_t(ks_ref, _rope_tile(tiles[2], cos, sin))
    vs_ref[...] = tiles[3].astype(BF16)
    _store_key_chunks_t(kw_ref, _rope_tile(tiles[4], cos, sin))
    vw_ref[...] = tiles[5].astype(BF16)
    z_ref[...] = _dot(h, wz_ref[...]).astype(BF16)
    gl_ref[...] = _dot(h, wg_ref[...])


def _inproj0(x, mod, g, cos, sin, wq, wkv, wz, wg):
    b, t, d = x.shape
    nq, nz = wq.shape[1], wz.shape[1]
    tok = lambda n: pl.BlockSpec((None, TM, n), lambda i, j: (i, j, 0))
    full = lambda a: pl.BlockSpec(a.shape, lambda i, j: (0,) * a.ndim)
    tab = pl.BlockSpec((TM, LANES), lambda i, j: (j, 0))
    shp = lambda n, dt: jax.ShapeDtypeStruct((b, t, n), dt)
    kt = pl.BlockSpec((None, TM // KEY_CHUNK, LANES, KEY_CHUNK), lambda i, j: (i, j, 0, 0))
    kt_shp = jax.ShapeDtypeStruct((b, t // KEY_CHUNK, LANES, KEY_CHUNK), BF16)
    kv = tok(LANES)
    kv_shp = shp(LANES, BF16)
    return pl.pallas_call(
        _inproj0_kernel,
        grid=(b, t // TM),
        in_specs=[tok(d), pl.BlockSpec((None, 3, d), lambda i, j: (i, 0, 0)), full(g), tab, tab,
                  full(wq), full(wkv), full(wz), full(wg)],
        out_specs=[tok(nq), tok(nq), kv, kv, kt, kv, kt, kv, tok(nz), tok(LANES)],
        out_shape=[shp(nq, BF16), shp(nq, BF16), kv_shp, kv_shp, kt_shp, kv_shp, kt_shp, kv_shp,
                   shp(nz, BF16), shp(LANES, F32)],
        compiler_params=_cparams("parallel", "parallel"),
        name="inproj_nsa",
    )(x, mod, g, cos, sin, wq, wkv, wz, wg)


def _compress_kernel(ak_ref, av_ref, pekt_ref, pekb_ref, pevt_ref, pevb_ref,
                     w1kt_ref, w1kb_ref, w1vt_ref, w1vb_ref, w2k_ref, w2v_ref, kcmp_ref, vcmp_ref):
    def mlp(a_ref, pet_ref, peb_ref, w1t_ref, w1b_ref, w2_ref):
        a = a_ref[...].astype(F32)
        top = _dot((a + pet_ref[...]).astype(BF16), w1t_ref[...])
        bot = _dot((a + peb_ref[...]).astype(BF16), w1b_ref[...])
        n = bot.shape[0]
        hid = top + pltpu.roll(bot, n - 1, axis=0)
        return _dot(_silu(hid).astype(BF16), w2_ref[...])

    kcmp_ref[...] = mlp(ak_ref, pekt_ref, pekb_ref, w1kt_ref, w1kb_ref, w2k_ref).astype(BF16)
    vcmp_ref[...] = mlp(av_ref, pevt_ref, pevb_ref, w1vt_ref, w1vb_ref, w2v_ref).astype(BF16)


def _compress(ak, av, consts):
    b, n, w = ak.shape
    full = lambda a: pl.BlockSpec(a.shape, lambda i: (0,) * a.ndim)
    blk = pl.BlockSpec((None, n, w), lambda i: (i, 0, 0))
    out = pl.BlockSpec((None, n, LANES), lambda i: (i, 0, 0))
    return pl.pallas_call(
        _compress_kernel,
        grid=(b,),
        in_specs=[blk, blk] + [full(a) for a in consts],
        out_specs=[out, out],
        out_shape=[jax.ShapeDtypeStruct((b, n, LANES), BF16)] * 2,
        compiler_params=_cparams("parallel"),
        name="compress",
    )(ak, av, *consts)


def _stack_heads(q_ref, lane_sets):
    n_tiles = q_ref.shape[1] // LANES
    parts = []
    for j in range(n_tiles):
        qj = q_ref[:, j * LANES:(j + 1) * LANES]
        parts.append(jnp.where(lane_sets, qj, jnp.zeros_like(qj)))
    return jnp.concatenate(parts, axis=0)


def _cmp_attn_kernel(qp_ref, kcmp_ref, vcmp_ref, ovl_ref, ocmp_ref, msel_ref, *, n_slc):
    tq = qp_ref.shape[0]
    n_cmp = vcmp_ref.shape[0]
    hg = qp_ref.shape[1] // LANES
    t0 = pl.program_id(1) * tq
    lane = lax.broadcasted_iota(jnp.int32, (tq, LANES), 1)
    kc = kcmp_ref[...]
    vc = vcmp_ref[...]
    row = lax.broadcasted_iota(jnp.int32, (hg * tq, n_cmp), 0)
    col = lax.broadcasted_iota(jnp.int32, (hg * tq, n_cmp), 1)
    tpos = t0 + (row & (tq - 1))
    valid = (CMP_STRIDE * col + (CMP_LEN - 1)) <= tpos

    jb = lax.broadcasted_iota(jnp.int32, (n_slc, tq), 0)
    cur = (t0 + lax.broadcasted_iota(jnp.int32, (n_slc, tq), 1)) // SLC_LEN
    visible = jb <= cur
    forced = (jb == 0) | (jb == cur) | (jb == cur - 1)

    o_groups = []
    sel_groups = []
    for g in range(NSA_KV_HEADS):
        in_group = ((lane % HEAD_DIM) // HALF) == g
        qs = _stack_heads(qp_ref, in_group)
        s = jnp.where(valid, _dot_t(qs, kc), NEG)
        m = jnp.max(s, axis=-1, keepdims=True)
        e = jnp.where(valid, jnp.exp(s - m), 0.0)
        den = jnp.sum(e, axis=-1, keepdims=True)
        p = e / jnp.where(den > 0, den, 1.0)
        o_groups.append(_dot(p.astype(BF16), vc))
        p_sum = p[0:tq]
        for j in range(1, hg):
            p_sum = p_sum + p[j * tq:(j + 1) * tq]
        imp = lax.dot_general(ovl_ref[...], p_sum, (((1,), (1,)), ((), ())),
                              precision=HIGHEST, preferred_element_type=F32)
        rank = jnp.where(forced, -NEG, jnp.where(visible, imp, NEG))
        cnt = jnp.zeros((n_slc, tq), F32)
        for jp in range(n_slc):
            rj = rank[jp:jp + 1, :]
            tie = jnp.where(jb > jp, 1.0, 0.0)
            cnt = cnt + jnp.where(rj > rank, 1.0, jnp.where(rj == rank, tie, 0.0))
        sel_groups.append(jnp.where(visible, jnp.where(cnt < SLC_TOP, 1.0, 0.0), 0.0))

    for j in range(hg):
        rows = slice(j * tq, (j + 1) * tq)
        ocmp_ref[:, j * LANES:(j + 1) * LANES] = jnp.where(
            lane < HEAD_DIM, o_groups[0][rows], o_groups[1][rows]).astype(BF16)
    pad = jnp.zeros((LANES - NSA_KV_HEADS * n_slc, tq), F32)
    sel_t = jnp.concatenate(sel_groups + [pad], axis=0)
    msel_ref[...] = sel_t.T.astype(BF16)


def _cmp_attn(qp, kcmp, vcmp, ovl_t):
    b, t, nq = qp.shape
    n_cmp = kcmp.shape[1]
    n_slc = t // SLC_LEN
    tok = lambda n: pl.BlockSpec((None, TQ, n), lambda i, j: (i, j, 0))
    per_b = pl.BlockSpec((None, n_cmp, LANES), lambda i, j: (i, 0, 0))
    return pl.pallas_call(
        functools.partial(_cmp_attn_kernel, n_slc=n_slc),
        grid=(b, t // TQ),
        in_specs=[tok(nq), per_b, per_b, pl.BlockSpec(ovl_t.shape, lambda i, j: (0, 0))],
        out_specs=[tok(nq), tok(LANES)],
        out_shape=[jax.ShapeDtypeStruct((b, t, nq), BF16), jax.ShapeDtypeStruct((b, t, LANES), BF16)],
        compiler_params=_cparams("parallel", "parallel"),
        name="cmp_attn_select",
    )(qp, kcmp, vcmp, ovl_t)


def _nsa_attn_kernel(qr_ref, ks_ref, vs_ref, kw_ref, vw_ref, msel_ref, e_ref, ge_ref, ocmp_ref, z_ref, gl_ref,
                     og_ref, qa_scr, qe_scr, m_scr, acc_scr, *, n_slc, ck):
    tq = qr_ref.shape[0]
    hg = qr_ref.shape[1] // LANES
    rows = hg * tq
    n_wc = WIN_LEN // tq + 1
    per_ck = ck // KEY_CHUNK
    qi = pl.program_id(1)
    t0 = qi * tq
    lane = lax.broadcasted_iota(jnp.int32, (tq, LANES), 1)
    a_idx = lax.broadcasted_iota(jnp.int32, (tq, LANES), 0)

    eye = jnp.where(lane == a_idx, 1.0, 0.0).astype(BF16)
    earlier = jnp.where((lane % n_slc) < t0 // SLC_LEN, 0.0, NEG)
    mneg = jnp.where(msel_ref[...].astype(F32) > 0.5, earlier, NEG).astype(BF16)
    for g in range(NSA_KV_HEADS):
        in_group = ((lane % HEAD_DIM) // HALF) == g
        for j in range(hg):
            qj = qr_ref[:, j * LANES:(j + 1) * LANES]
            qj = jnp.where(in_group, qj, jnp.zeros_like(qj))
            rs = slice(j * tq, (j + 1) * tq)
            qa_scr[g, rs, 0:LANES] = qj
            qa_scr[g, rs, LANES:2 * LANES] = mneg
            qe_scr[g, rs, 0:LANES] = qj
            qe_scr[g, rs, LANES:2 * LANES] = eye
    zero_b = jnp.zeros((tq, LANES), BF16)
    neg_b = jnp.full((tq, LANES), NEG, BF16)
    causal_b = jnp.where(lane <= a_idx, 0.0, NEG).astype(BF16)
    far_b = jnp.where(lane > a_idx, 0.0, NEG).astype(BF16)
    own_half = [lane < HEAD_DIM, lane >= HEAD_DIM]

    def values_with_ones(v, g):
        in_own_half = (lax.broadcasted_iota(jnp.int32, v.shape, 1) // HEAD_DIM) == g
        return jnp.where(in_own_half, v, jnp.ones_like(v))

    def lane_tiles(v):
        return [v[:, i * LANES:(i + 1) * LANES] for i in range(v.shape[1] // LANES)]

    def tile_max(tiles):
        mx = tiles[0]
        for v in tiles[1:]:
            mx = jnp.maximum(mx, v)
        return jnp.max(mx, axis=-1, keepdims=True)

    def probs(tiles, m):
        return jnp.concatenate([jnp.exp(v - m) for v in tiles], axis=1).astype(BF16)

    m_scr[...] = jnp.full(m_scr.shape, NEG, F32)
    acc_scr[...] = jnp.zeros(acc_scr.shape, F32)

    def earlier_keys(kc, carry):
        k0 = pl.multiple_of(kc * ck, ck)
        for g in range(NSA_KV_HEADS):
            kt = jnp.concatenate([ks_ref[kc * per_ck + i] for i in range(per_ck)], axis=1)
            eb = jnp.concatenate([e_ref[g, kc * per_ck + i] for i in range(per_ck)], axis=1)
            tiles = lane_tiles(_dot(qa_scr[g], jnp.concatenate([kt, eb], axis=0)))
            m_old = m_scr[g]
            m_new = jnp.maximum(m_old, tile_max(tiles))
            pv = _dot(probs(tiles, m_new), values_with_ones(vs_ref[pl.ds(k0, ck), :], g))
            acc_scr[g] = jnp.exp(m_old - m_new) * acc_scr[g] + pv
            m_scr[g] = m_new
        return carry

    lax.fori_loop(0, (t0 + ck - 1) // ck, earlier_keys, 0)

    gates = jax.nn.sigmoid(gl_ref[...])
    g_hi = gates.astype(BF16)
    g_lo = (gates - g_hi.astype(F32)).astype(BF16)
    g_two = jnp.concatenate([g_hi, g_lo], axis=1)
    gexp = [_dot(g_two, jnp.concatenate([ge_ref[br], ge_ref[br]], axis=0)) for br in range(NSA_BRANCHES)]

    kd = pl.multiple_of(t0, tq)
    zeros_v = jnp.zeros((tq, LANES), BF16)
    acc_slc, acc_win = [], []
    for g in range(NSA_KV_HEADS):
        own_k = jnp.concatenate([jnp.concatenate([kw_ref[qi], causal_b], axis=0),
                                 jnp.concatenate([ks_ref[qi], causal_b], axis=0)], axis=1)
        s_own = _dot(qe_scr[g], own_k)
        s_win_own, s_slc_own = s_own[:, 0:tq], s_own[:, tq:2 * tq]
        kbs, vbs = [], []
        for c in range(n_wc - 1):
            cidx = qi - (n_wc - 1 - c)
            csafe = jnp.maximum(cidx, 0)
            bias = far_b if c == 0 else zero_b
            kbs.append(jnp.concatenate([kw_ref[csafe], jnp.where(cidx >= 0, bias, neg_b)], axis=0))
            vbs.append(values_with_ones(vw_ref[pl.ds(pl.multiple_of(csafe * tq, tq), tq), :], g))
        pairs = [(c, c + 1) for c in range(0, n_wc - 1, 2)]
        scores = [_dot(qe_scr[g], jnp.concatenate([kbs[c] for c in pr], axis=1)) for pr in pairs]
        m_win = tile_max([v for s in scores for v in lane_tiles(s)] + [s_win_own])
        m_old = m_scr[g]
        m_slc = jnp.maximum(m_old, jnp.max(s_slc_own, axis=-1, keepdims=True))
        win_v = [jnp.concatenate([v, zeros_v], axis=1) for v in vbs + [values_with_ones(vw_ref[pl.ds(kd, tq), :], g)]]
        slc_v = jnp.concatenate([zeros_v, values_with_ones(vs_ref[pl.ds(kd, tq), :], g)], axis=1)
        p_all = jnp.concatenate([probs(lane_tiles(s), m_win) for s in scores]
                                + [probs([s_win_own], m_win), probs([s_slc_own], m_slc)], axis=1)
        pv = _dot(p_all, jnp.concatenate(win_v + [slc_v], axis=0))
        acc_win.append(pv[:, 0:LANES])
        acc_slc.append(jnp.exp(m_old - m_slc) * acc_scr[g] + pv[:, LANES:2 * LANES])

    def normalised(acc0, acc1):
        both = jnp.where(own_half[0], acc0, acc1)
        sums = pltpu.roll(jnp.where(own_half[0], acc1, acc0), HEAD_DIM, axis=1)
        return both / sums

    for j in range(hg):
        rs = slice(j * tq, (j + 1) * tq)
        cols = slice(j * LANES, (j + 1) * LANES)
        o = (gexp[0][:, cols] * ocmp_ref[:, cols].astype(F32)
             + gexp[1][:, cols] * normalised(acc_slc[0][rs], acc_slc[1][rs])
             + gexp[2][:, cols] * normalised(acc_win[0][rs], acc_win[1][rs]))
        og_ref[:, cols] = (o * _silu(z_ref[:, cols].astype(F32))).astype(BF16)


def _nsa_attn(qr, ks, vs, kw, vw, msel, e, ge, ocmp, z, gl):
    b, t, nq = qr.shape
    hg = nq // LANES
    rows = hg * TQ
    assert TQ == KEY_CHUNK and WIN_LEN % TQ == 0
    tok = lambda n: pl.BlockSpec((None, TQ, n), lambda i, j: (i, j, 0))
    per_b = pl.BlockSpec((None, t, LANES), lambda i, j: (i, 0, 0))
    per_b_t = pl.BlockSpec((None, t // KEY_CHUNK, LANES, KEY_CHUNK), lambda i, j: (i, 0, 0, 0))
    full = lambda a: pl.BlockSpec(a.shape, lambda i, j: (0,) * a.ndim)
    return pl.pallas_call(
        functools.partial(_nsa_attn_kernel, n_slc=t // SLC_LEN, ck=SLC_CK),
        grid=(b, t // TQ),
        in_specs=[tok(nq), per_b_t, per_b, per_b_t, per_b, tok(LANES), full(e), full(ge), tok(nq), tok(nq), tok(LANES)],
        out_specs=tok(nq),
        out_shape=jax.ShapeDtypeStruct((b, t, nq), BF16),
        scratch_shapes=[
            pltpu.VMEM((NSA_KV_HEADS, rows, 2 * LANES), BF16),
            pltpu.VMEM((NSA_KV_HEADS, rows, 2 * LANES), BF16),
            pltpu.VMEM((NSA_KV_HEADS, rows, LANES), F32),
            pltpu.VMEM((NSA_KV_HEADS, rows, LANES), F32),
        ],
        compiler_params=_cparams("parallel", "arbitrary"),
        name="nsa_slc_win_attn",
    )(qr, ks, vs, kw, vw, msel, e, ge, ocmp, z, gl)


def _outproj0_kernel(og_ref, w_ref, x_ref, mod_ref, o_ref):
    y = _dot(og_ref[...], w_ref[...])
    o_ref[...] = x_ref[...] + mod_ref[2:3, :] * y


def _outproj0(og, w, x, mod):
    b, t, d = x.shape
    n = og.shape[2]
    tok = lambda k: pl.BlockSpec((None, TM, k), lambda i, j: (i, j, 0))
    return pl.pallas_call(
        _outproj0_kernel,
        grid=(b, t // TM),
        in_specs=[tok(n), pl.BlockSpec(w.shape, lambda i, j: (0, 0)), tok(d),
                  pl.BlockSpec((None, 3, d), lambda i, j: (i, 0, 0))],
        out_specs=tok(d),
        out_shape=jax.ShapeDtypeStruct((b, t, d), F32),
        compiler_params=_cparams("parallel", "parallel"),
        name="outproj_nsa",
    )(og, w, x, mod)


def _inproj1_kernel(x_ref, mod_ref, g_ref, cos_ref, sin_ref, wq_ref, wk_ref, wv_ref, wz_ref, *refs):
    n_grp = len(DIL_PATTERNS)
    outs = [refs[a * n_grp:(a + 1) * n_grp] for a in range(3)]
    z_ref, stage = refs[3 * n_grp], refs[3 * n_grp + 1]
    tm = x_ref.shape[0]
    gw = DIL_TILES_PER_GROUP * LANES
    h = _modulated_norm(x_ref[...], mod_ref, g_ref).astype(BF16)
    cos = cos_ref[...]
    sin = sin_ref[...]
    for a, w_ref in enumerate((wq_ref, wk_ref, wv_ref)):
        u = _dot(h, w_ref[...])
        for gi, (_, dil) in enumerate(DIL_PATTERNS):
            for jt in range(DIL_TILES_PER_GROUP):
                tile = u[:, gi * gw + jt * LANES:gi * gw + (jt + 1) * LANES]
                if a < 2:
                    tile = _rope_tile(tile, cos, sin)
                if dil == 1:
                    outs[a][gi][:, jt * LANES:(jt + 1) * LANES] = tile.astype(BF16)
                    continue
                slab = (a * n_grp + gi) * DIL_TILES_PER_GROUP + jt
                stage[slab] = tile
                for r in range(dil):
                    rows = stage[slab, pl.ds(r, tm // dil, stride=dil), :]
                    outs[a][gi][:, r * gw + jt * LANES:r * gw + (jt + 1) * LANES] = rows.astype(BF16)
    z_ref[...] = _dot(h, wz_ref[...]).astype(BF16)


def _inproj1(x, mod, g, cos, sin, wq, wk, wv, wz):
    b, t, d = x.shape
    n = wq.shape[1]
    gw = DIL_TILES_PER_GROUP * LANES
    tok = lambda k: pl.BlockSpec((None, TM, k), lambda i, j: (i, j, 0))
    full = lambda a: pl.BlockSpec(a.shape, lambda i, j: (0,) * a.ndim)
    tab = pl.BlockSpec((TM, LANES), lambda i, j: (j, 0))
    grp_specs = [pl.BlockSpec((None, TM // dil, dil * gw), lambda i, j: (i, j, 0)) for _, dil in DIL_PATTERNS]
    grp_shapes = [jax.ShapeDtypeStruct((b, t // dil, dil * gw), BF16) for _, dil in DIL_PATTERNS]
    n_slabs = 3 * len(DIL_PATTERNS) * DIL_TILES_PER_GROUP
    return pl.pallas_call(
        _inproj1_kernel,
        grid=(b, t // TM),
        in_specs=[tok(d), pl.BlockSpec((None, 3, d), lambda i, j: (i, 0, 0)), full(g), tab, tab,
                  full(wq), full(wk), full(wv), full(wz)],
        out_specs=grp_specs * 3 + [tok(n)],
        out_shape=grp_shapes * 3 + [jax.ShapeDtypeStruct((b, t, n), BF16)],
        scratch_shapes=[pltpu.VMEM((n_slabs, TM, LANES), F32)],
        compiler_params=_cparams("parallel", "parallel"),
        name="inproj_dil",
    )(x, mod, g, cos, sin, wq, wk, wv, wz)


def _dil_attn_kernel(q_ref, k_ref, v_ref, o_ref, lse_ref, *, n_heads, tq, n_seq):
    tq_blk = q_ref.shape[0]
    gw = q_ref.shape[1] // n_seq
    n_tiles = gw // LANES
    i = pl.program_id(2)
    lane = lax.broadcasted_iota(jnp.int32, (tq, LANES), 1)
    a_idx = lax.broadcasted_iota(jnp.int32, (tq, LANES), 0)
    first = ((lane % HEAD_DIM) // HALF) == 0
    eye = jnp.where(lane == a_idx, 1.0, 0.0).astype(BF16)
    diag_b = jnp.where(lane <= a_idx, 0.0, NEG).astype(BF16)
    prev_b = jnp.where(lane >= a_idx, 0.0, NEG).astype(BF16)
    neg_b = jnp.full((tq, LANES), NEG, BF16)
    for sq in range(n_seq):
        kt_prev = [None] * n_tiles
        for sub in range(tq_blk // tq):
            cdiag = i * (tq_blk // tq) + sub
            kdiag = pl.multiple_of(cdiag * tq, tq)
            kprev = pl.multiple_of(jnp.maximum(cdiag - 1, 0) * tq, tq)
            prev_bias = prev_b if sub > 0 else jnp.where(i > 0, prev_b, neg_b)
            rows = slice(sub * tq, (sub + 1) * tq)
            lses = []
            for jt in range(n_tiles):
                cols = slice(sq * gw + jt * LANES, sq * gw + (jt + 1) * LANES)
                qj = q_ref[rows, cols]
                zero = jnp.zeros_like(qj)
                qs = jnp.concatenate([jnp.concatenate([jnp.where(first, qj, zero), eye], axis=1),
                                      jnp.concatenate([jnp.where(first, zero, qj), eye], axis=1)], axis=0)
                kt_diag = k_ref[pl.ds(kdiag, tq), cols].T
                kt_before = kt_prev[jt] if sub > 0 else k_ref[pl.ds(kprev, tq), cols].T
                kt_prev[jt] = kt_diag
                kb = jnp.concatenate([jnp.concatenate([kt_before, prev_bias], axis=0),
                                      jnp.concatenate([kt_diag, diag_b], axis=0)], axis=1)
                vb = jnp.concatenate([v_ref[pl.ds(kprev, tq), cols], v_ref[pl.ds(kdiag, tq), cols]], axis=0)
                s = _dot(qs, kb)
                m = jnp.max(s, axis=-1, keepdims=True)
                p = jnp.exp(s - m)
                l = jnp.sum(p, axis=-1, keepdims=True)
                o = _dot(p.astype(BF16), vb) / l
                o_ref[rows, cols] = jnp.where(lane < HEAD_DIM, o[0:tq], o[tq:2 * tq]).astype(BF16)
                lse = m + jnp.log(l)
                lses.append(lse[0:tq])
                if 2 * jt + 1 < n_heads:
                    lses.append(lse[tq:2 * tq])
            top = lses[0]
            for v in lses[1:]:
                top = jnp.maximum(top, v)
            tot = jnp.zeros_like(top)
            for v in lses:
                tot = tot + jnp.exp(v - top)
            group_lse = top + jnp.log(tot) - math.log(n_heads)
            lse_ref[rows, sq * LANES:(sq + 1) * LANES] = jnp.broadcast_to(group_lse, (tq, LANES))


def _dil_attn(q, k, v, dil, n_heads):
    b, length, n = q.shape
    gw = n // dil
    tq = min(TQ, length)
    tq_blk = min(DIL_STEP_TOKENS, length)
    n_seq = min(DIL_STEP_TOKENS // tq_blk, dil)
    tok = lambda w: pl.BlockSpec((None, tq_blk, n_seq * w), lambda bi, ri, i: (bi, i, ri))
    seq = pl.BlockSpec((None, length, n_seq * gw), lambda bi, ri, i: (bi, 0, ri))
    return pl.pallas_call(
        functools.partial(_dil_attn_kernel, n_heads=n_heads, tq=tq, n_seq=n_seq),
        grid=(b, dil // n_seq, length // tq_blk),
        in_specs=[tok(gw), seq, seq],
        out_specs=[tok(gw), tok(LANES)],
        out_shape=[jax.ShapeDtypeStruct((b, length, n), BF16), jax.ShapeDtypeStruct((b, length, dil * LANES), F32)],
        compiler_params=_cparams("parallel", "parallel", "parallel"),
        name="dil_attn",
    )(q, k, v)


def _outproj1_kernel(*refs):
    n_grp = len(DIL_PATTERNS)
    o_refs, l_refs = refs[0:n_grp], refs[n_grp:2 * n_grp]
    z_ref, w_ref, x_ref, mod_ref, fg_ref, out_ref, stage = refs[2 * n_grp:]
    tm = x_ref.shape[0]
    gw = DIL_TILES_PER_GROUP * LANES

    def token_order(ref, gi, width, col0, slab):
        dil = DIL_PATTERNS[gi][1]
        if dil == 1:
            return ref[:, col0:col0 + LANES].astype(F32)
        for r in range(dil):
            stage[slab, pl.ds(r, tm // dil, stride=dil), :] = ref[:, r * width + col0:r * width + col0 + LANES].astype(F32)
        return stage[slab]

    ls = [token_order(l_refs[gi], gi, LANES, 0, gi) for gi in range(n_grp)]
    top = jnp.maximum(jnp.maximum(ls[0], ls[1]), ls[2])
    es = [jnp.exp(v - top) for v in ls]
    den = es[0] + es[1] + es[2]
    parts = []
    for gi in range(n_grp):
        alpha = es[gi] / den * float(n_grp)
        for jt in range(DIL_TILES_PER_GROUP):
            c0 = (gi * DIL_TILES_PER_GROUP + jt) * LANES
            zt = z_ref[:, c0:c0 + LANES].astype(F32)
            ot = token_order(o_refs[gi], gi, gw, jt * LANES, n_grp + gi * DIL_TILES_PER_GROUP + jt)
            parts.append((ot * alpha * _silu(zt)).astype(BF16))
    og = jnp.concatenate(parts, axis=1)
    y = _dot(og, w_ref[...])
    x2 = x_ref[...] + mod_ref[2:3, :] * y
    ms = jnp.mean(x2 * x2, axis=-1, keepdims=True)
    out_ref[...] = x2 * lax.rsqrt(ms + NORM_EPS) * fg_ref[...]


def _outproj1(o_groups, lse_groups, z, w, x, mod, final_g):
    b, t, d = x.shape
    tok = lambda k: pl.BlockSpec((None, TM, k), lambda i, j: (i, j, 0))
    full = lambda a: pl.BlockSpec(a.shape, lambda i, j: (0,) * a.ndim)
    grp = lambda a, dil: pl.BlockSpec((None, TM // dil, a.shape[2]), lambda i, j: (i, j, 0))
    dils = [dil for _, dil in DIL_PATTERNS]
    n_slabs = len(dils) * (1 + DIL_TILES_PER_GROUP)
    return pl.pallas_call(
        _outproj1_kernel,
        grid=(b, t // TM),
        in_specs=[grp(a, dl) for a, dl in zip(o_groups, dils)] + [grp(a, dl) for a, dl in zip(lse_groups, dils)]
                 + [tok(z.shape[2]), full(w), tok(d), pl.BlockSpec((None, 3, d), lambda i, j: (i, 0, 0)), full(final_g)],
        out_specs=tok(d),
        out_shape=jax.ShapeDtypeStruct((b, t, d), F32),
        scratch_shapes=[pltpu.VMEM((n_slabs, TM, LANES), F32)],
        compiler_params=_cparams("parallel", "parallel"),
        name="outproj_dil_final",
    )(*o_groups, *lse_groups, z, w, x, mod, final_g)


def _rope_tables(t):
    inv = ROPE_THETA ** (-jnp.arange(HALF, dtype=F32) / HALF)
    ang = jnp.arange(t, dtype=F32)[:, None] * inv[None, :]
    cos = jnp.tile(jnp.cos(ang), (1, 4))
    sin = jnp.tile(jnp.sin(ang), (1, 4))
    sign = jnp.where(jnp.arange(LANES) < 2 * HALF, -1.0, 1.0).astype(F32)
    return cos, sin * sign[None, :]


def _pair_rope_layout(w, n_pairs):
    d = w.shape[0]
    return w.reshape(d, n_pairs, 2, 2, HALF).transpose(0, 1, 3, 2, 4).reshape(d, n_pairs * LANES)


def _nsa_weights(w_in, pe_k, pe_v, ck_w1, ck_w2, cv_w1, cv_w2, w_out):
    d = w_in.shape[0]
    aw = N_HEADS * HEAD_DIM
    kvw = NSA_KV_HEADS * HEAD_DIM
    cuts = np.cumsum([aw] + [kvw] * 6 + [aw]).tolist()
    q, k_c, v_c, k_s, v_s, k_w, v_w, z, gl = jnp.split(w_in, cuts, axis=1)
    scale = HEAD_DIM ** -0.5
    wq = (q * scale).reshape(d, NSA_KV_HEADS, NSA_GROUP, 2, HALF).transpose(0, 2, 3, 1, 4).reshape(d, aw)
    k_lay = lambda w: _pair_rope_layout(w, 1)
    wkv = jnp.concatenate([k_lay(k_c), v_c, k_lay(k_s), v_s, k_lay(k_w), v_w], axis=1)
    wz = z.reshape(d, NSA_KV_HEADS, NSA_GROUP, HEAD_DIM).transpose(0, 2, 1, 3).reshape(d, aw)
    wo = w_out.reshape(NSA_KV_HEADS, NSA_GROUP, HEAD_DIM, -1).transpose(1, 0, 2, 3).reshape(aw, -1)
    wg = jnp.pad(gl, ((0, 0), (0, LANES - gl.shape[1])))

    def grouped(w1_half, rope_lanes):
        out = []
        for g in range(NSA_KV_HEADS):
            if rope_lanes:
                src = w1_half.reshape(CMP_STRIDE, 2, 1, HALF, CMP_HIDDEN)
                pads = ((0, 0), (0, 0), (g, NSA_KV_HEADS - 1 - g), (0, 0), (0, 0))
            else:
                src = w1_half.reshape(CMP_STRIDE, 1, HEAD_DIM, CMP_HIDDEN)
                pads = ((0, 0), (g, NSA_KV_HEADS - 1 - g), (0, 0), (0, 0))
            out.append(jnp.pad(src, pads).reshape(CMP_STRIDE * LANES, CMP_HIDDEN))
        return jnp.concatenate(out, axis=1)

    def pe_rows(pe_half, rope_lanes):
        if rope_lanes:
            v = jnp.broadcast_to(pe_half.reshape(CMP_STRIDE, 2, 1, HALF), (CMP_STRIDE, 2, NSA_KV_HEADS, HALF))
        else:
            v = jnp.broadcast_to(pe_half.reshape(CMP_STRIDE, 1, HEAD_DIM), (CMP_STRIDE, NSA_KV_HEADS, HEAD_DIM))
        return v.reshape(1, CMP_STRIDE * LANES)

    def w2_padded(w2, rope_lanes):
        out = []
        for g in range(NSA_KV_HEADS):
            if rope_lanes:
                src = w2.reshape(CMP_HIDDEN, 2, 1, HALF)
                pads = ((0, 0), (0, 0), (g, NSA_KV_HEADS - 1 - g), (0, 0))
            else:
                src = w2.reshape(CMP_HIDDEN, 1, HEAD_DIM)
                pads = ((0, 0), (g, NSA_KV_HEADS - 1 - g), (0, 0))
            out.append(jnp.pad(src, pads).reshape(CMP_HIDDEN, LANES))
        return jnp.concatenate(out, axis=0)

    k1 = ck_w1.reshape(2, CMP_STRIDE, HEAD_DIM, CMP_HIDDEN)
    v1 = cv_w1.reshape(2, CMP_STRIDE, HEAD_DIM, CMP_HIDDEN)
    compress_consts = (
        pe_rows(pe_k[:CMP_STRIDE], True), pe_rows(pe_k[CMP_STRIDE:], True),
        pe_rows(pe_v[:CMP_STRIDE], False), pe_rows(pe_v[CMP_STRIDE:], False),
        grouped(k1[0], True).astype(BF16), grouped(k1[1], True).astype(BF16),
        grouped(v1[0], False).astype(BF16), grouped(v1[1], False).astype(BF16),
        w2_padded(ck_w2, True).astype(BF16), w2_padded(cv_w2, False).astype(BF16),
    )
    return wq.astype(BF16), wkv.astype(BF16), wz.astype(BF16), wg.astype(BF16), wo.astype(BF16), compress_consts


def _dil_weights(w_in, w_out):
    d = w_in.shape[0]
    q, k, v, z = jnp.split(w_in, 4, axis=1)
    n_pairs = len(DIL_GROUP_HEADS) * DIL_TILES_PER_GROUP
    offs = np.cumsum((0,) + DIL_GROUP_HEADS).tolist()

    def pad_heads(w, axis):
        parts = []
        for gi, hn in enumerate(DIL_GROUP_HEADS):
            sl = [slice(None)] * w.ndim
            sl[axis] = slice(offs[gi] * HEAD_DIM, offs[gi + 1] * HEAD_DIM)
            pads = [(0, 0)] * w.ndim
            pads[axis] = (0, (2 * DIL_TILES_PER_GROUP - hn) * HEAD_DIM)
            parts.append(jnp.pad(w[tuple(sl)], pads))
        return jnp.concatenate(parts, axis=axis)

    scale = HEAD_DIM ** -0.5
    wq = _pair_rope_layout(pad_heads(q * scale, 1), n_pairs)
    wk = _pair_rope_layout(pad_heads(k, 1), n_pairs)
    return (wq.astype(BF16), wk.astype(BF16), pad_heads(v, 1).astype(BF16), pad_heads(z, 1).astype(BF16),
            pad_heads(w_out, 0).astype(BF16))


def _selection_constants(t):
    n_cmp_rows = t // CMP_STRIDE
    n_slc = t // SLC_LEN
    c_start = CMP_STRIDE * np.arange(n_cmp_rows)
    s_start = SLC_LEN * np.arange(n_slc)
    ovl_t = ((c_start[None, :] < s_start[:, None] + SLC_LEN)
             & (c_start[None, :] + CMP_LEN > s_start[:, None])).astype(np.float32)
    ovl_t[:, (t - CMP_LEN) // CMP_STRIDE + 1:] = 0.0
    e = np.zeros((NSA_KV_HEADS, t // KEY_CHUNK, LANES, KEY_CHUNK), np.float32)
    keys = np.arange(t)
    for g in range(NSA_KV_HEADS):
        e[g, keys // KEY_CHUNK, g * n_slc + keys // SLC_LEN, keys % KEY_CHUNK] = 1.0
    ge = np.zeros((NSA_BRANCHES, LANES, NSA_GROUP * LANES), np.float32)
    out_lane = np.arange(NSA_GROUP * LANES)
    head_j, grp = out_lane // LANES, (out_lane % LANES) // HEAD_DIM
    for br in range(NSA_BRANCHES):
        ge[br, (grp * NSA_GROUP + head_j) * NSA_BRANCHES + br, out_lane] = 1.0
    return jnp.asarray(ovl_t), jnp.asarray(e, dtype=BF16), jnp.asarray(ge, dtype=BF16)


@jax.jit
def kernel(x, c, norm_g, ada_w, ada_b, nsa_w_in, nsa_pe_k, nsa_pe_v, nsa_ck_w1, nsa_ck_w2,
           nsa_cv_w1, nsa_cv_w2, nsa_w_out, dil_w_in, dil_w_out, final_g):
    b, t, d = x.shape
    assert t % SLC_CK == 0 and t % TM == 0 and d % LANES == 0
    assert NSA_KV_HEADS * (t // SLC_LEN) <= LANES and t // CMP_STRIDE == LANES
    mod = _adaln_mod(c, ada_w, ada_b).reshape(ada_w.shape[0], b, 3, d)
    cos, sin = _rope_tables(t)

    wq, wkv, wz, wg, wo, compress_consts = _nsa_weights(
        nsa_w_in[0], nsa_pe_k[0], nsa_pe_v[0], nsa_ck_w1[0], nsa_ck_w2[0], nsa_cv_w1[0], nsa_cv_w2[0], nsa_w_out[0])
    qp, qr, kc, vc, ks, vs, kw, vw, z, gl = _inproj0(x, mod[0], norm_g[0:1], cos, sin, wq, wkv, wz, wg)
    rows16 = lambda a: a.reshape(b, t // CMP_STRIDE, CMP_STRIDE * LANES)
    kcmp, vcmp = _compress(rows16(kc), rows16(vc), compress_consts)
    ovl_t, e_sel, ge = _selection_constants(t)
    ocmp, msel = _cmp_attn(qp, kcmp, vcmp, ovl_t)
    og = _nsa_attn(qr, ks, vs, kw, vw, msel, e_sel, ge, ocmp, z, gl)
    x1 = _outproj0(og, wo, x, mod[0])

    dq, dk, dv, dz, dwo = _dil_weights(dil_w_in[0], dil_w_out[0])
    n_grp = len(DIL_PATTERNS)
    *qkv, z1 = _inproj1(x1, mod[1], norm_g[1:2], cos, sin, dq, dk, dv, dz)
    o_groups, lse_groups = [], []
    for gi, (win, dil) in enumerate(DIL_PATTERNS):
        assert win // dil == DIL_WIN and TM % (dil * 16) == 0
        o, lse = _dil_attn(qkv[gi], qkv[n_grp + gi], qkv[2 * n_grp + gi], dil, DIL_GROUP_HEADS[gi])
        o_groups.append(o)
        lse_groups.append(lse)
    return _outproj1(o_groups, lse_groups, z1, dwo, x1, mod[1], final_g.reshape(1, d))
```

```python
import functools
import math

import numpy as np
import jax
import jax.numpy as jnp
from jax import lax
from jax.experimental import pallas as pl
from jax.experimental.pallas import tpu as pltpu

F32 = jnp.float32
BF16 = jnp.bfloat16
HIGHEST = lax.Precision.HIGHEST

HEAD_DIM = 64
HALF = HEAD_DIM // 2
N_HEADS = 16
ROPE_THETA = 10000.0
NORM_EPS = 1e-6
NSA_KV_HEADS = 2
NSA_GROUP = N_HEADS // NSA_KV_HEADS
NSA_BRANCHES = 3
CMP_LEN = 32
CMP_STRIDE = 16
CMP_HIDDEN = 256
SLC_LEN = 64
SLC_TOP = 16
WIN_LEN = 512
DIL_PATTERNS = ((128, 1), (512, 4), (2048, 16))
DIL_GROUP_HEADS = (6, 5, 5)
DIL_WIN = 128
DIL_TILES_PER_GROUP = 3

LANES = 128
VMEM_LIMIT_BYTES = 48 * 1024 * 1024

NEG = -1e30
TQ = 128
KEY_CHUNK = 128
SLC_CK = 512
TM = 512
DIL_STEP_TOKENS = 512


def _cparams(*sem):
    return pltpu.CompilerParams(dimension_semantics=sem, vmem_limit_bytes=VMEM_LIMIT_BYTES)


def _dot(a, b):
    return jnp.dot(a, b, preferred_element_type=F32)


def _dot_t(a, b):
    return lax.dot_general(a, b, (((1,), (1,)), ((), ())), preferred_element_type=F32)


def _silu(v):
    return v * jax.nn.sigmoid(v)


def _mod_kernel(c_ref, w_ref, b_ref, o_ref):
    s = _silu(c_ref[...])
    o_ref[...] = jnp.dot(s, w_ref[...], precision=HIGHEST, preferred_element_type=F32) + b_ref[...]


def _adaln_mod(c, ada_w, ada_b):
    depth, d, n3 = ada_w.shape
    b = c.shape[0]
    tn = 1024
    return pl.pallas_call(
        _mod_kernel,
        grid=(depth, n3 // tn),
        in_specs=[
            pl.BlockSpec((b, d), lambda i, j: (0, 0)),
            pl.BlockSpec((None, d, tn), lambda i, j: (i, 0, j)),
            pl.BlockSpec((None, 1, tn), lambda i, j: (i, 0, j)),
        ],
        out_specs=pl.BlockSpec((None, b, tn), lambda i, j: (i, 0, j)),
        out_shape=jax.ShapeDtypeStruct((depth, b, n3), F32),
        compiler_params=_cparams("arbitrary", "arbitrary"),
        name="adaln_mod",
    )(c, ada_w, ada_b.reshape(depth, 1, n3))


def _modulated_norm(x, mod_ref, g_ref):
    ms = jnp.mean(x * x, axis=-1, keepdims=True)
    y = x * lax.rsqrt(ms + NORM_EPS) * g_ref[...]
    return y * (1.0 + mod_ref[1:2, :]) + mod_ref[0:1, :]


def _rope_tile(v, cos, sin):
    return v * cos + pltpu.roll(v, 2 * HALF, axis=1) * sin


def _store_key_chunks_t(kt_ref, k):
    for c in range(kt_ref.shape[0]):
        kt_ref[c] = k[c * KEY_CHUNK:(c + 1) * KEY_CHUNK, :].T.astype(BF16)


def _inproj0_kernel(x_ref, mod_ref, g_ref, cos_ref, sin_ref, wq_ref, wkv_ref, wz_ref, wg_ref,
                    qp_ref, qr_ref, kc_ref, vc_ref, ks_ref, vs_ref, kw_ref, vw_ref, z_ref, gl_ref):
    h = _modulated_norm(x_ref[...], mod_ref, g_ref).astype(BF16)
    cos = cos_ref[...]
    sin = sin_ref[...]
    q = _dot(h, wq_ref[...])
    for j in range(q.shape[1] // LANES):
        sl = slice(j * LANES, (j + 1) * LANES)
        qp_ref[:, sl] = q[:, sl].astype(BF16)
        qr_ref[:, sl] = _rope_tile(q[:, sl], cos, sin).astype(BF16)
    kv = _dot(h, wkv_ref[...])
    tiles = [kv[:, i * LANES:(i + 1) * LANES] for i in range(6)]
    kc_ref[...] = tiles[0].astype(BF16)
    vc_ref[...] = tiles[1].astype(BF16)
    _store_key_chunks_t(ks_ref, _rope_tile(tiles[2], cos, sin))
    vs_ref[...] = tiles[3].astype(BF16)
    _store_key_chunks_t(kw_ref, _rope_tile(tiles[4], cos, sin))
    vw_ref[...] = tiles[5].astype(BF16)
    z_ref[...] = _silu(_dot(h, wz_ref[...])).astype(BF16)
    gl_ref[...] = _dot(h, wg_ref[...])


def _inproj0(x, mod, g, cos, sin, wq, wkv, wz, wg):
    b, t, d = x.shape
    nq, nz = wq.shape[1], wz.shape[1]
    tok = lambda n: pl.BlockSpec((None, TM, n), lambda i, j: (i, j, 0))
    full = lambda a: pl.BlockSpec(a.shape, lambda i, j: (0,) * a.ndim)
    tab = pl.BlockSpec((TM, LANES), lambda i, j: (j, 0))
    shp = lambda n, dt: jax.ShapeDtypeStruct((b, t, n), dt)
    kt = pl.BlockSpec((None, TM // KEY_CHUNK, LANES, KEY_CHUNK), lambda i, j: (i, j, 0, 0))
    kt_shp = jax.ShapeDtypeStruct((b, t // KEY_CHUNK, LANES, KEY_CHUNK), BF16)
    kv = tok(LANES)
    kv_shp = shp(LANES, BF16)
    return pl.pallas_call(
        _inproj0_kernel,
        grid=(b, t // TM),
        in_specs=[tok(d), pl.BlockSpec((None, 3, d), lambda i, j: (i, 0, 0)), full(g), tab, tab,
                  full(wq), full(wkv), full(wz), full(wg)],
        out_specs=[tok(nq), tok(nq), kv, kv, kt, kv, kt, kv, tok(nz), tok(LANES)],
        out_shape=[shp(nq, BF16), shp(nq, BF16), kv_shp, kv_shp, kt_shp, kv_shp, kt_shp, kv_shp,
                   shp(nz, BF16), shp(LANES, F32)],
        compiler_params=_cparams("parallel", "parallel"),
        name="inproj_nsa",
    )(x, mod, g, cos, sin, wq, wkv, wz, wg)


def _compress_kernel(ak_ref, av_ref, pekt_ref, pekb_ref, pevt_ref, pevb_ref,
                     w1kt_ref, w1kb_ref, w1vt_ref, w1vb_ref, w2k_ref, w2v_ref, kcmp_ref, vcmp_ref):
    def mlp(a_ref, pet_ref, peb_ref, w1t_ref, w1b_ref, w2_ref):
        a = a_ref[...].astype(F32)
        top = _dot((a + pet_ref[...]).astype(BF16), w1t_ref[...])
        bot = _dot((a + peb_ref[...]).astype(BF16), w1b_ref[...])
        n = bot.shape[0]
        hid = top + pltpu.roll(bot, n - 1, axis=0)
        return _dot(_silu(hid).astype(BF16), w2_ref[...])

    kcmp_ref[...] = mlp(ak_ref, pekt_ref, pekb_ref, w1kt_ref, w1kb_ref, w2k_ref).astype(BF16)
    vcmp_ref[...] = mlp(av_ref, pevt_ref, pevb_ref, w1vt_ref, w1vb_ref, w2v_ref).astype(BF16)


def _compress(ak, av, consts):
    b, n, w = ak.shape
    full = lambda a: pl.BlockSpec(a.shape, lambda i: (0,) * a.ndim)
    blk = pl.BlockSpec((None, n, w), lambda i: (i, 0, 0))
    out = pl.BlockSpec((None, n, LANES), lambda i: (i, 0, 0))
    return pl.pallas_call(
        _compress_kernel,
        grid=(b,),
        in_specs=[blk, blk] + [full(a) for a in consts],
        out_specs=[out, out],
        out_shape=[jax.ShapeDtypeStruct((b, n, LANES), BF16)] * 2,
        compiler_params=_cparams("parallel"),
        name="compress",
    )(ak, av, *consts)


def _stack_heads(q_ref, lane_sets):
    n_tiles = q_ref.shape[1] // LANES
    parts = []
    for j in range(n_tiles):
        qj = q_ref[:, j * LANES:(j + 1) * LANES]
        parts.append(jnp.where(lane_sets, qj, jnp.zeros_like(qj)))
    return jnp.concatenate(parts, axis=0)


def _head_gate_rows(gates, g, branches, lane):
    blocks = []
    for j in range(NSA_GROUP):
        base = (g * NSA_GROUP + j) * NSA_BRANCHES
        kept = jnp.zeros_like(gates)
        for br in branches:
            kept = jnp.where(lane == base + br, gates, kept)
        hi = kept.astype(BF16)
        blocks.append(jnp.concatenate([hi, (kept - hi.astype(F32)).astype(BF16)], axis=1))
    return jnp.concatenate(blocks, axis=0)


def _gate_spread(branches):
    col = np.arange(2 * LANES) % LANES
    spread = np.zeros((2 * LANES, LANES * len(branches)), np.float32)
    for i, br in enumerate(branches):
        spread[col % NSA_BRANCHES == br, i * LANES:(i + 1) * LANES] = 1.0
    return jnp.asarray(spread, dtype=BF16)


def _cmp_attn_kernel(qp_ref, kcmp_ref, vcmp_ref, ovl_ref, gl_ref, gsp_ref, ocmp_ref, msel_ref, *, n_slc):
    tq = qp_ref.shape[0]
    n_cmp = vcmp_ref.shape[0]
    hg = qp_ref.shape[1] // LANES
    t0 = pl.program_id(1) * tq
    lane = lax.broadcasted_iota(jnp.int32, (tq, LANES), 1)
    kc = kcmp_ref[...]
    vc = vcmp_ref[...]
    row = lax.broadcasted_iota(jnp.int32, (hg * tq, n_cmp), 0)
    col = lax.broadcasted_iota(jnp.int32, (hg * tq, n_cmp), 1)
    tpos = t0 + (row & (tq - 1))
    valid = (CMP_STRIDE * col + (CMP_LEN - 1)) <= tpos

    jb = lax.broadcasted_iota(jnp.int32, (n_slc, tq), 0)
    cur = (t0 + lax.broadcasted_iota(jnp.int32, (n_slc, tq), 1)) // SLC_LEN
    visible = jb <= cur
    forced = (jb == 0) | (jb == cur) | (jb == cur - 1)

    gates = jax.nn.sigmoid(gl_ref[...])
    o_groups = []
    sel_groups = []
    for g in range(NSA_KV_HEADS):
        in_group = ((lane % HEAD_DIM) // HALF) == g
        qs = _stack_heads(qp_ref, in_group)
        s = jnp.where(valid, _dot_t(qs, kc), NEG)
        m = jnp.max(s, axis=-1, keepdims=True)
        e = jnp.where(valid, jnp.exp(s - m), 0.0)
        den = jnp.sum(e, axis=-1, keepdims=True)
        p = e / jnp.where(den > 0, den, 1.0)
        gate = _dot(_head_gate_rows(gates, g, (0,), lane), gsp_ref[...])
        o_groups.append(gate * _dot(p.astype(BF16), vc))
        p_sum = p[0:tq]
        for j in range(1, hg):
            p_sum = p_sum + p[j * tq:(j + 1) * tq]
        imp = lax.dot_general(ovl_ref[...], p_sum, (((1,), (1,)), ((), ())),
                              precision=HIGHEST, preferred_element_type=F32)
        rank = jnp.where(forced, -NEG, jnp.where(visible, imp, NEG))
        cnt = jnp.zeros((n_slc, tq), F32)
        for jp in range(n_slc):
            rj = rank[jp:jp + 1, :]
            tie = jnp.where(jb > jp, 1.0, 0.0)
            cnt = cnt + jnp.where(rj > rank, 1.0, jnp.where(rj == rank, tie, 0.0))
        sel_groups.append(jnp.where(visible, jnp.where(cnt < SLC_TOP, 1.0, 0.0), 0.0))

    for j in range(hg):
        rows = slice(j * tq, (j + 1) * tq)
        ocmp_ref[:, j * LANES:(j + 1) * LANES] = jnp.where(
            lane < HEAD_DIM, o_groups[0][rows], o_groups[1][rows]).astype(BF16)
    pad = jnp.zeros((LANES - NSA_KV_HEADS * n_slc, tq), F32)
    sel_t = jnp.concatenate(sel_groups + [pad], axis=0)
    msel_ref[...] = sel_t.T.astype(BF16)


def _cmp_attn(qp, kcmp, vcmp, ovl_t, gl):
    b, t, nq = qp.shape
    n_cmp = kcmp.shape[1]
    n_slc = t // SLC_LEN
    gsp = _gate_spread((0,))
    tok = lambda n: pl.BlockSpec((None, TQ, n), lambda i, j: (i, j, 0))
    per_b = pl.BlockSpec((None, n_cmp, LANES), lambda i, j: (i, 0, 0))
    full = lambda a: pl.BlockSpec(a.shape, lambda i, j: (0,) * a.ndim)
    return pl.pallas_call(
        functools.partial(_cmp_attn_kernel, n_slc=n_slc),
        grid=(b, t // TQ),
        in_specs=[tok(nq), per_b, per_b, full(ovl_t), tok(LANES), full(gsp)],
        out_specs=[tok(nq), tok(LANES)],
        out_shape=[jax.ShapeDtypeStruct((b, t, nq), BF16), jax.ShapeDtypeStruct((b, t, LANES), BF16)],
        compiler_params=_cparams("parallel", "parallel"),
        name="cmp_attn_select",
    )(qp, kcmp, vcmp, ovl_t, gl, gsp)


def _nsa_attn_kernel(qr_ref, ks_ref, vs_ref, kw_ref, vw_ref, msel_ref, e_ref, gsp_ref, ocmp_ref, z_ref, gl_ref,
                     og_ref, q_scr, m_scr, acc_scr, *, n_slc, ck):
    tq = qr_ref.shape[0]
    hg = qr_ref.shape[1] // LANES
    rows = hg * tq
    n_wc = WIN_LEN // tq + 1
    per_ck = ck // KEY_CHUNK
    qi = pl.program_id(1)
    t0 = qi * tq
    lane = lax.broadcasted_iota(jnp.int32, (tq, LANES), 1)
    a_idx = lax.broadcasted_iota(jnp.int32, (tq, LANES), 0)

    eye = jnp.where(lane == a_idx, 1.0, 0.0).astype(BF16)
    earlier = jnp.where((lane % n_slc) < t0 // SLC_LEN, 0.0, NEG)
    mneg = jnp.where(msel_ref[...].astype(F32) > 0.5, earlier, NEG).astype(BF16)
    for g in range(NSA_KV_HEADS):
        in_group = ((lane % HEAD_DIM) // HALF) == g
        for j in range(hg):
            qj = qr_ref[:, j * LANES:(j + 1) * LANES]
            qj = jnp.where(in_group, qj, jnp.zeros_like(qj))
            q_scr[g, j * tq:(j + 1) * tq, :] = qj
    mneg_rows = jnp.concatenate([mneg] * hg, axis=0)
    eye_rows = jnp.concatenate([eye] * hg, axis=0)

    def with_selector(g, sel_rows):
        return jnp.concatenate([q_scr[g], sel_rows], axis=1)

    zero_b = jnp.zeros((tq, LANES), BF16)
    neg_b = jnp.full((tq, LANES), NEG, BF16)
    causal_b = jnp.where(lane <= a_idx, 0.0, NEG).astype(BF16)
    far_b = jnp.where(lane > a_idx, 0.0, NEG).astype(BF16)
    own_half = [lane < HEAD_DIM, lane >= HEAD_DIM]

    def values_with_ones(v, g):
        in_own_half = (lax.broadcasted_iota(jnp.int32, v.shape, 1) // HEAD_DIM) == g
        return jnp.where(in_own_half, v, jnp.ones_like(v))

    def lane_tiles(v):
        return [v[:, i * LANES:(i + 1) * LANES] for i in range(v.shape[1] // LANES)]

    def tile_max(tiles):
        mx = tiles[0]
        for v in tiles[1:]:
            mx = jnp.maximum(mx, v)
        return jnp.max(mx, axis=-1, keepdims=True)

    def probs(tiles, m):
        return jnp.concatenate([jnp.exp(v - m) for v in tiles], axis=1).astype(BF16)

    m_scr[...] = jnp.full(m_scr.shape, NEG, F32)
    acc_scr[...] = jnp.zeros(acc_scr.shape, F32)

    def earlier_keys(kc, carry):
        k0 = pl.multiple_of(kc * ck, ck)
        kt = jnp.concatenate([ks_ref[kc * per_ck + i] for i in range(per_ck)], axis=1)
        scores = []
        for g in range(NSA_KV_HEADS):
            eb = jnp.concatenate([e_ref[g, kc * per_ck + i] for i in range(per_ck)], axis=1)
            scores.append(_dot(with_selector(g, mneg_rows), jnp.concatenate([kt, eb], axis=0)))
        for g in range(NSA_KV_HEADS):
            tiles = lane_tiles(scores[g])
            m_old = m_scr[g]
            m_new = jnp.maximum(m_old, tile_max(tiles))
            pv = _dot(probs(tiles, m_new), values_with_ones(vs_ref[pl.ds(k0, ck), :], g))
            acc_scr[g] = jnp.exp(m_old - m_new) * acc_scr[g] + pv
            m_scr[g] = m_new
        return carry

    lax.fori_loop(0, (t0 + ck - 1) // ck, earlier_keys, 0)

    kd = pl.multiple_of(t0, tq)
    zeros_v = jnp.zeros((tq, LANES), BF16)
    own_k = jnp.concatenate([jnp.concatenate([kw_ref[qi], causal_b], axis=0),
                             jnp.concatenate([ks_ref[qi], causal_b], axis=0)], axis=1)
    win_chunks = [qi - (n_wc - 1 - c) for c in range(n_wc - 1)]
    kbs = [jnp.concatenate([kw_ref[jnp.maximum(cidx, 0)], jnp.where(cidx >= 0, far_b if c == 0 else zero_b, neg_b)],
                           axis=0) for c, cidx in enumerate(win_chunks)]
    pairs = [(c, c + 1) for c in range(0, n_wc - 1, 2)]
    pair_k = [jnp.concatenate([kbs[c] for c in pr], axis=1) for pr in pairs]
    s_own, scores = [], []
    for g in range(NSA_KV_HEADS):
        q_eye = with_selector(g, eye_rows)
        s_own.append(_dot(q_eye, own_k))
        scores.append([_dot(q_eye, kb) for kb in pair_k])

    acc_slc, acc_win = [], []
    for g in range(NSA_KV_HEADS):
        s_win_own, s_slc_own = s_own[g][:, 0:tq], s_own[g][:, tq:2 * tq]
        m_win = tile_max([v for s in scores[g] for v in lane_tiles(s)] + [s_win_own])
        m_old = m_scr[g]
        m_slc = jnp.maximum(m_old, jnp.max(s_slc_own, axis=-1, keepdims=True))
        vbs = [values_with_ones(vw_ref[pl.ds(pl.multiple_of(jnp.maximum(cidx, 0) * tq, tq), tq), :], g)
               for cidx in win_chunks] + [values_with_ones(vw_ref[pl.ds(kd, tq), :], g)]
        win_v = [jnp.concatenate([v, zeros_v], axis=1) for v in vbs]
        slc_v = jnp.concatenate([zeros_v, values_with_ones(vs_ref[pl.ds(kd, tq), :], g)], axis=1)
        p_all = jnp.concatenate([probs(lane_tiles(s), m_win) for s in scores[g]]
                                + [probs([s_win_own], m_win), probs([s_slc_own], m_slc)], axis=1)
        pv = _dot(p_all, jnp.concatenate(win_v + [slc_v], axis=0))
        acc_win.append(pv[:, 0:LANES])
        acc_slc.append(jnp.exp(m_old - m_slc) * acc_scr[g] + pv[:, LANES:2 * LANES])

    gates = jax.nn.sigmoid(gl_ref[...])
    row_gates = [_dot(_head_gate_rows(gates, g, (1, 2), lane), gsp_ref[...]) for g in range(NSA_KV_HEADS)]

    def gated(accs, rs, branch_lanes):
        num = [accs[g][rs] * row_gates[g][rs, branch_lanes] for g in range(NSA_KV_HEADS)]
        sums = pltpu.roll(jnp.where(own_half[0], accs[1][rs], accs[0][rs]), HEAD_DIM, axis=1)
        return jnp.where(own_half[0], num[0], num[1]) / sums

    for j in range(hg):
        rs = slice(j * tq, (j + 1) * tq)
        cols = slice(j * LANES, (j + 1) * LANES)
        o = (ocmp_ref[:, cols].astype(F32) + gated(acc_slc, rs, slice(0, LANES))
             + gated(acc_win, rs, slice(LANES, 2 * LANES)))
        og_ref[:, cols] = (o * z_ref[:, cols].astype(F32)).astype(BF16)


def _nsa_attn(qr, ks, vs, kw, vw, msel, e, ocmp, z, gl):
    b, t, nq = qr.shape
    gsp = _gate_spread((1, 2))
    hg = nq // LANES
    rows = hg * TQ
    assert TQ == KEY_CHUNK and WIN_LEN % TQ == 0
    tok = lambda n: pl.BlockSpec((None, TQ, n), lambda i, j: (i, j, 0))
    per_b = pl.BlockSpec((None, t, LANES), lambda i, j: (i, 0, 0))
    per_b_t = pl.BlockSpec((None, t // KEY_CHUNK, LANES, KEY_CHUNK), lambda i, j: (i, 0, 0, 0))
    full = lambda a: pl.BlockSpec(a.shape, lambda i, j: (0,) * a.ndim)
    return pl.pallas_call(
        functools.partial(_nsa_attn_kernel, n_slc=t // SLC_LEN, ck=SLC_CK),
        grid=(b, t // TQ),
        in_specs=[tok(nq), per_b_t, per_b, per_b_t, per_b, tok(LANES), full(e), full(gsp), tok(nq), tok(nq), tok(LANES)],
        out_specs=tok(nq),
        out_shape=jax.ShapeDtypeStruct((b, t, nq), BF16),
        scratch_shapes=[
            pltpu.VMEM((NSA_KV_HEADS, rows, LANES), BF16),
            pltpu.VMEM((NSA_KV_HEADS, rows, LANES), F32),
            pltpu.VMEM((NSA_KV_HEADS, rows, LANES), F32),
        ],
        compiler_params=_cparams("parallel", "arbitrary"),
        name="nsa_slc_win_attn",
    )(qr, ks, vs, kw, vw, msel, e, gsp, ocmp, z, gl)


def _outproj0_kernel(og_ref, w_ref, x_ref, mod_ref, o_ref):
    y = _dot(og_ref[...], w_ref[...])
    o_ref[...] = x_ref[...] + mod_ref[2:3, :] * y


def _outproj0(og, w, x, mod):
    b, t, d = x.shape
    n = og.shape[2]
    tok = lambda k: pl.BlockSpec((None, TM, k), lambda i, j: (i, j, 0))
    return pl.pallas_call(
        _outproj0_kernel,
        grid=(b, t // TM),
        in_specs=[tok(n), pl.BlockSpec(w.shape, lambda i, j: (0, 0)), tok(d),
                  pl.BlockSpec((None, 3, d), lambda i, j: (i, 0, 0))],
        out_specs=tok(d),
        out_shape=jax.ShapeDtypeStruct((b, t, d), F32),
        compiler_params=_cparams("parallel", "parallel"),
        name="outproj_nsa",
    )(og, w, x, mod)


def _inproj1_kernel(x_ref, mod_ref, g_ref, cos_ref, sin_ref, wq_ref, wk_ref, wv_ref, wz_ref, *refs):
    n_grp = len(DIL_PATTERNS)
    outs = [refs[a * n_grp:(a + 1) * n_grp] for a in range(3)]
    z_ref, stage = refs[3 * n_grp], refs[3 * n_grp + 1]
    tm = x_ref.shape[0]
    gw = DIL_TILES_PER_GROUP * LANES
    h = _modulated_norm(x_ref[...], mod_ref, g_ref).astype(BF16)
    cos = cos_ref[...]
    sin = sin_ref[...]
    for a, w_ref in enumerate((wq_ref, wk_ref, wv_ref)):
        u = _dot(h, w_ref[...])
        for gi, (_, dil) in enumerate(DIL_PATTERNS):
            for jt in range(DIL_TILES_PER_GROUP):
                tile = u[:, gi * gw + jt * LANES:gi * gw + (jt + 1) * LANES]
                if a < 2:
                    tile = _rope_tile(tile, cos, sin)
                if dil == 1:
                    outs[a][gi][:, jt * LANES:(jt + 1) * LANES] = tile.astype(BF16)
                    continue
                slab = (a * n_grp + gi) * DIL_TILES_PER_GROUP + jt
                stage[slab] = tile
                for r in range(dil):
                    rows = stage[slab, pl.ds(r, tm // dil, stride=dil), :]
                    outs[a][gi][:, r * gw + jt * LANES:r * gw + (jt + 1) * LANES] = rows.astype(BF16)
    z_ref[...] = _silu(_dot(h, wz_ref[...])).astype(BF16)


def _inproj1(x, mod, g, cos, sin, wq, wk, wv, wz):
    b, t, d = x.shape
    n = wq.shape[1]
    gw = DIL_TILES_PER_GROUP * LANES
    tok = lambda k: pl.BlockSpec((None, TM, k), lambda i, j: (i, j, 0))
    full = lambda a: pl.BlockSpec(a.shape, lambda i, j: (0,) * a.ndim)
    tab = pl.BlockSpec((TM, LANES), lambda i, j: (j, 0))
    grp_specs = [pl.BlockSpec((None, TM // dil, dil * gw), lambda i, j: (i, j, 0)) for _, dil in DIL_PATTERNS]
    grp_shapes = [jax.ShapeDtypeStruct((b, t // dil, dil * gw), BF16) for _, dil in DIL_PATTERNS]
    n_slabs = 3 * len(DIL_PATTERNS) * DIL_TILES_PER_GROUP
    return pl.pallas_call(
        _inproj1_kernel,
        grid=(b, t // TM),
        in_specs=[tok(d), pl.BlockSpec((None, 3, d), lambda i, j: (i, 0, 0)), full(g), tab, tab,
                  full(wq), full(wk), full(wv), full(wz)],
        out_specs=grp_specs * 3 + [tok(n)],
        out_shape=grp_shapes * 3 + [jax.ShapeDtypeStruct((b, t, n), BF16)],
        scratch_shapes=[pltpu.VMEM((n_slabs, TM, LANES), F32)],
        compiler_params=_cparams("parallel", "parallel"),
        name="inproj_dil",
    )(x, mod, g, cos, sin, wq, wk, wv, wz)


def _dil_attn_kernel(q_ref, k_ref, v_ref, o_ref, lse_ref, *, n_heads, tq, n_seq):
    tq_blk = q_ref.shape[0]
    gw = q_ref.shape[1] // n_seq
    n_tiles = gw // LANES
    i = pl.program_id(2)
    lane = lax.broadcasted_iota(jnp.int32, (tq, LANES), 1)
    a_idx = lax.broadcasted_iota(jnp.int32, (tq, LANES), 0)
    first = ((lane % HEAD_DIM) // HALF) == 0
    eye = jnp.where(lane == a_idx, 1.0, 0.0).astype(BF16)
    diag_b = jnp.where(lane <= a_idx, 0.0, NEG).astype(BF16)
    prev_b = jnp.where(lane >= a_idx, 0.0, NEG).astype(BF16)
    neg_b = jnp.full((tq, LANES), NEG, BF16)
    for sq in range(n_seq):
        kt_prev = [None] * n_tiles
        for sub in range(tq_blk // tq):
            cdiag = i * (tq_blk // tq) + sub
            kdiag = pl.multiple_of(cdiag * tq, tq)
            kprev = pl.multiple_of(jnp.maximum(cdiag - 1, 0) * tq, tq)
            prev_bias = prev_b if sub > 0 else jnp.where(i > 0, prev_b, neg_b)
            rows = slice(sub * tq, (sub + 1) * tq)
            lses = []
            for jt in range(n_tiles):
                cols = slice(sq * gw + jt * LANES, sq * gw + (jt + 1) * LANES)
                qj = q_ref[rows, cols]
                zero = jnp.zeros_like(qj)
                qs = jnp.concatenate([jnp.concatenate([jnp.where(first, qj, zero), eye], axis=1),
                                      jnp.concatenate([jnp.where(first, zero, qj), eye], axis=1)], axis=0)
                kt_diag = k_ref[pl.ds(kdiag, tq), cols].T
                kt_before = kt_prev[jt] if sub > 0 else k_ref[pl.ds(kprev, tq), cols].T
                kt_prev[jt] = kt_diag
                kb = jnp.concatenate([jnp.concatenate([kt_before, prev_bias], axis=0),
                                      jnp.concatenate([kt_diag, diag_b], axis=0)], axis=1)
                vb = jnp.concatenate([v_ref[pl.ds(kprev, tq), cols], v_ref[pl.ds(kdiag, tq), cols]], axis=0)
                s = _dot(qs, kb)
                m = jnp.max(s, axis=-1, keepdims=True)
                p = jnp.exp(s - m)
                l = jnp.sum(p, axis=-1, keepdims=True)
                o = _dot(p.astype(BF16), vb) / l
                o_ref[rows, cols] = jnp.where(lane < HEAD_DIM, o[0:tq], o[tq:2 * tq]).astype(BF16)
                lse = m + jnp.log(l)
                lses.append(lse[0:tq])
                if 2 * jt + 1 < n_heads:
                    lses.append(lse[tq:2 * tq])
            top = lses[0]
            for v in lses[1:]:
                top = jnp.maximum(top, v)
            tot = jnp.zeros_like(top)
            for v in lses:
                tot = tot + jnp.exp(v - top)
            group_lse = top + jnp.log(tot) - math.log(n_heads)
            lse_ref[rows, sq * LANES:(sq + 1) * LANES] = jnp.broadcast_to(group_lse, (tq, LANES))


def _dil_attn(q, k, v, dil, n_heads):
    b, length, n = q.shape
    gw = n // dil
    tq = min(TQ, length)
    tq_blk = min(DIL_STEP_TOKENS, length)
    n_seq = min(DIL_STEP_TOKENS // tq_blk, dil)
    tok = lambda w: pl.BlockSpec((None, tq_blk, n_seq * w), lambda bi, ri, i: (bi, i, ri))
    seq = pl.BlockSpec((None, length, n_seq * gw), lambda bi, ri, i: (bi, 0, ri))
    return pl.pallas_call(
        functools.partial(_dil_attn_kernel, n_heads=n_heads, tq=tq, n_seq=n_seq),
        grid=(b, dil // n_seq, length // tq_blk),
        in_specs=[tok(gw), seq, seq],
        out_specs=[tok(gw), tok(LANES)],
        out_shape=[jax.ShapeDtypeStruct((b, length, n), BF16), jax.ShapeDtypeStruct((b, length, dil * LANES), F32)],
        compiler_params=_cparams("parallel", "parallel", "parallel"),
        name="dil_attn",
    )(q, k, v)


def _outproj1_kernel(*refs):
    n_grp = len(DIL_PATTERNS)
    o_refs, l_refs = refs[0:n_grp], refs[n_grp:2 * n_grp]
    z_ref, w_ref, x_ref, mod_ref, fg_ref, out_ref, stage = refs[2 * n_grp:]
    tm = x_ref.shape[0]
    gw = DIL_TILES_PER_GROUP * LANES

    def token_order(ref, gi, width, col0, slab):
        dil = DIL_PATTERNS[gi][1]
        if dil == 1:
            return ref[:, col0:col0 + LANES].astype(F32)
        for r in range(dil):
            stage[slab, pl.ds(r, tm // dil, stride=dil), :] = ref[:, r * width + col0:r * width + col0 + LANES].astype(F32)
        return stage[slab]

    ls = [token_order(l_refs[gi], gi, LANES, 0, gi) for gi in range(n_grp)]
    top = jnp.maximum(jnp.maximum(ls[0], ls[1]), ls[2])
    es = [jnp.exp(v - top) for v in ls]
    den = es[0] + es[1] + es[2]
    parts = []
    for gi in range(n_grp):
        alpha = es[gi] / den * float(n_grp)
        for jt in range(DIL_TILES_PER_GROUP):
            c0 = (gi * DIL_TILES_PER_GROUP + jt) * LANES
            zt = z_ref[:, c0:c0 + LANES].astype(F32)
            ot = token_order(o_refs[gi], gi, gw, jt * LANES, n_grp + gi * DIL_TILES_PER_GROUP + jt)
            parts.append((ot * alpha * zt).astype(BF16))
    og = jnp.concatenate(parts, axis=1)
    y = _dot(og, w_ref[...])
    x2 = x_ref[...] + mod_ref[2:3, :] * y
    ms = jnp.mean(x2 * x2, axis=-1, keepdims=True)
    out_ref[...] = x2 * lax.rsqrt(ms + NORM_EPS) * fg_ref[...]


def _outproj1(o_groups, lse_groups, z, w, x, mod, final_g):
    b, t, d = x.shape
    tok = lambda k: pl.BlockSpec((None, TM, k), lambda i, j: (i, j, 0))
    full = lambda a: pl.BlockSpec(a.shape, lambda i, j: (0,) * a.ndim)
    grp = lambda a, dil: pl.BlockSpec((None, TM // dil, a.shape[2]), lambda i, j: (i, j, 0))
    dils = [dil for _, dil in DIL_PATTERNS]
    n_slabs = len(dils) * (1 + DIL_TILES_PER_GROUP)
    return pl.pallas_call(
        _outproj1_kernel,
        grid=(b, t // TM),
        in_specs=[grp(a, dl) for a, dl in zip(o_groups, dils)] + [grp(a, dl) for a, dl in zip(lse_groups, dils)]
                 + [tok(z.shape[2]), full(w), tok(d), pl.BlockSpec((None, 3, d), lambda i, j: (i, 0, 0)), full(final_g)],
        out_specs=tok(d),
        out_shape=jax.ShapeDtypeStruct((b, t, d), F32),
        scratch_shapes=[pltpu.VMEM((n_slabs, TM, LANES), F32)],
        compiler_params=_cparams("parallel", "parallel"),
        name="outproj_dil_final",
    )(*o_groups, *lse_groups, z, w, x, mod, final_g)


def _rope_tables(t):
    inv = ROPE_THETA ** (-jnp.arange(HALF, dtype=F32) / HALF)
    ang = jnp.arange(t, dtype=F32)[:, None] * inv[None, :]
    cos = jnp.tile(jnp.cos(ang), (1, 4))
    sin = jnp.tile(jnp.sin(ang), (1, 4))
    sign = jnp.where(jnp.arange(LANES) < 2 * HALF, -1.0, 1.0).astype(F32)
    return cos, sin * sign[None, :]


def _pair_rope_layout(w, n_pairs):
    d = w.shape[0]
    return w.reshape(d, n_pairs, 2, 2, HALF).transpose(0, 1, 3, 2, 4).reshape(d, n_pairs * LANES)


def _nsa_weights(w_in, pe_k, pe_v, ck_w1, ck_w2, cv_w1, cv_w2, w_out):
    d = w_in.shape[0]
    aw = N_HEADS * HEAD_DIM
    kvw = NSA_KV_HEADS * HEAD_DIM
    cuts = np.cumsum([aw] + [kvw] * 6 + [aw]).tolist()
    q, k_c, v_c, k_s, v_s, k_w, v_w, z, gl = jnp.split(w_in, cuts, axis=1)
    scale = HEAD_DIM ** -0.5
    wq = (q * scale).reshape(d, NSA_KV_HEADS, NSA_GROUP, 2, HALF).transpose(0, 2, 3, 1, 4).reshape(d, aw)
    k_lay = lambda w: _pair_rope_layout(w, 1)
    wkv = jnp.concatenate([k_lay(k_c), v_c, k_lay(k_s), v_s, k_lay(k_w), v_w], axis=1)
    wz = z.reshape(d, NSA_KV_HEADS, NSA_GROUP, HEAD_DIM).transpose(0, 2, 1, 3).reshape(d, aw)
    wo = w_out.reshape(NSA_KV_HEADS, NSA_GROUP, HEAD_DIM, -1).transpose(1, 0, 2, 3).reshape(aw, -1)
    wg = jnp.pad(gl, ((0, 0), (0, LANES - gl.shape[1])))

    def grouped(w1_half, rope_lanes):
        out = []
        for g in range(NSA_KV_HEADS):
            if rope_lanes:
                src = w1_half.reshape(CMP_STRIDE, 2, 1, HALF, CMP_HIDDEN)
                pads = ((0, 0), (0, 0), (g, NSA_KV_HEADS - 1 - g), (0, 0), (0, 0))
            else:
                src = w1_half.reshape(CMP_STRIDE, 1, HEAD_DIM, CMP_HIDDEN)
                pads = ((0, 0), (g, NSA_KV_HEADS - 1 - g), (0, 0), (0, 0))
            out.append(jnp.pad(src, pads).reshape(CMP_STRIDE * LANES, CMP_HIDDEN))
        return jnp.concatenate(out, axis=1)

    def pe_rows(pe_half, rope_lanes):
        if rope_lanes:
            v = jnp.broadcast_to(pe_half.reshape(CMP_STRIDE, 2, 1, HALF), (CMP_STRIDE, 2, NSA_KV_HEADS, HALF))
        else:
            v = jnp.broadcast_to(pe_half.reshape(CMP_STRIDE, 1, HEAD_DIM), (CMP_STRIDE, NSA_KV_HEADS, HEAD_DIM))
        return v.reshape(1, CMP_STRIDE * LANES)

    def w2_padded(w2, rope_lanes):
        out = []
        for g in range(NSA_KV_HEADS):
            if rope_lanes:
                src = w2.reshape(CMP_HIDDEN, 2, 1, HALF)
                pads = ((0, 0), (0, 0), (g, NSA_KV_HEADS - 1 - g), (0, 0))
            else:
                src = w2.reshape(CMP_HIDDEN, 1, HEAD_DIM)
                pads = ((0, 0), (g, NSA_KV_HEADS - 1 - g), (0, 0))
            out.append(jnp.pad(src, pads).reshape(CMP_HIDDEN, LANES))
        return jnp.concatenate(out, axis=0)

    k1 = ck_w1.reshape(2, CMP_STRIDE, HEAD_DIM, CMP_HIDDEN)
    v1 = cv_w1.reshape(2, CMP_STRIDE, HEAD_DIM, CMP_HIDDEN)
    compress_consts = (
        pe_rows(pe_k[:CMP_STRIDE], True), pe_rows(pe_k[CMP_STRIDE:], True),
        pe_rows(pe_v[:CMP_STRIDE], False), pe_rows(pe_v[CMP_STRIDE:], False),
        grouped(k1[0], True).astype(BF16), grouped(k1[1], True).astype(BF16),
        grouped(v1[0], False).astype(BF16), grouped(v1[1], False).astype(BF16),
        w2_padded(ck_w2, True).astype(BF16), w2_padded(cv_w2, False).astype(BF16),
    )
    return wq.astype(BF16), wkv.astype(BF16), wz.astype(BF16), wg.astype(BF16), wo.astype(BF16), compress_consts


def _dil_weights(w_in, w_out):
    d = w_in.shape[0]
    q, k, v, z = jnp.split(w_in, 4, axis=1)
    n_pairs = len(DIL_GROUP_HEADS) * DIL_TILES_PER_GROUP
    offs = np.cumsum((0,) + DIL_GROUP_HEADS).tolist()

    def pad_heads(w, axis):
        parts = []
        for gi, hn in enumerate(DIL_GROUP_HEADS):
            sl = [slice(None)] * w.ndim
            sl[axis] = slice(offs[gi] * HEAD_DIM, offs[gi + 1] * HEAD_DIM)
            pads = [(0, 0)] * w.ndim
            pads[axis] = (0, (2 * DIL_TILES_PER_GROUP - hn) * HEAD_DIM)
            parts.append(jnp.pad(w[tuple(sl)], pads))
        return jnp.concatenate(parts, axis=axis)

    scale = HEAD_DIM ** -0.5
    wq = _pair_rope_layout(pad_heads(q * scale, 1), n_pairs)
    wk = _pair_rope_layout(pad_heads(k, 1), n_pairs)
    return (wq.astype(BF16), wk.astype(BF16), pad_heads(v, 1).astype(BF16), pad_heads(z, 1).astype(BF16),
            pad_heads(w_out, 0).astype(BF16))


def _selection_constants(t):
    n_cmp_rows = t // CMP_STRIDE
    n_slc = t // SLC_LEN
    c_start = CMP_STRIDE * np.arange(n_cmp_rows)
    s_start = SLC_LEN * np.arange(n_slc)
    ovl_t = ((c_start[None, :] < s_start[:, None] + SLC_LEN)
             & (c_start[None, :] + CMP_LEN > s_start[:, None])).astype(np.float32)
    ovl_t[:, (t - CMP_LEN) // CMP_STRIDE + 1:] = 0.0
    e = np.zeros((NSA_KV_HEADS, t // KEY_CHUNK, LANES, KEY_CHUNK), np.float32)
    keys = np.arange(t)
    for g in range(NSA_KV_HEADS):
        e[g, keys // KEY_CHUNK, g * n_slc + keys // SLC_LEN, keys % KEY_CHUNK] = 1.0
    return jnp.asarray(ovl_t), jnp.asarray(e, dtype=BF16)


@jax.jit
def kernel(x, c, norm_g, ada_w, ada_b, nsa_w_in, nsa_pe_k, nsa_pe_v, nsa_ck_w1, nsa_ck_w2,
           nsa_cv_w1, nsa_cv_w2, nsa_w_out, dil_w_in, dil_w_out, final_g):
    b, t, d = x.shape
    assert t % SLC_CK == 0 and t % TM == 0 and d % LANES == 0
    assert NSA_KV_HEADS * (t // SLC_LEN) <= LANES and t // CMP_STRIDE == LANES
    mod = _adaln_mod(c, ada_w, ada_b).reshape(ada_w.shape[0], b, 3, d)
    cos, sin = _rope_tables(t)

    wq, wkv, wz, wg, wo, compress_consts = _nsa_weights(
        nsa_w_in[0], nsa_pe_k[0], nsa_pe_v[0], nsa_ck_w1[0], nsa_ck_w2[0], nsa_cv_w1[0], nsa_cv_w2[0], nsa_w_out[0])
    qp, qr, kc, vc, ks, vs, kw, vw, z, gl = _inproj0(x, mod[0], norm_g[0:1], cos, sin, wq, wkv, wz, wg)
    rows16 = lambda a: a.reshape(b, t // CMP_STRIDE, CMP_STRIDE * LANES)
    kcmp, vcmp = _compress(rows16(kc), rows16(vc), compress_consts)
    ovl_t, e_sel = _selection_constants(t)
    ocmp, msel = _cmp_attn(qp, kcmp, vcmp, ovl_t, gl)
    og = _nsa_attn(qr, ks, vs, kw, vw, msel, e_sel, ocmp, z, gl)
    x1 = _outproj0(og, wo, x, mod[0])

    dq, dk, dv, dz, dwo = _dil_weights(dil_w_in[0], dil_w_out[0])
    n_grp = len(DIL_PATTERNS)
    *qkv, z1 = _inproj1(x1, mod[1], norm_g[1:2], cos, sin, dq, dk, dv, dz)
    o_groups, lse_groups = [], []
    for gi, (win, dil) in enumerate(DIL_PATTERNS):
        assert win // dil == DIL_WIN and TM % (dil * 16) == 0
        o, lse = _dil_attn(qkv[gi], qkv[n_grp + gi], qkv[2 * n_grp + gi], dil, DIL_GROUP_HEADS[gi])
        o_groups.append(o)
        lse_groups.append(lse)
    return _outproj1(o_groups, lse_groups, z1, dwo, x1, mod[1], final_g.reshape(1, d))
```

```python
import functools
import math

import numpy as np
import jax
import jax.numpy as jnp
from jax import lax
from jax.experimental import pallas as pl
from jax.experimental.pallas import tpu as pltpu

F32 = jnp.float32
BF16 = jnp.bfloat16
HIGHEST = lax.Precision.HIGHEST

HEAD_DIM = 64
HALF = HEAD_DIM // 2
N_HEADS = 16
ROPE_THETA = 10000.0
NORM_EPS = 1e-6
NSA_KV_HEADS = 2
NSA_GROUP = N_HEADS // NSA_KV_HEADS
NSA_BRANCHES = 3
CMP_LEN = 32
CMP_STRIDE = 16
CMP_HIDDEN = 256
SLC_LEN = 64
SLC_TOP = 16
WIN_LEN = 512
DIL_PATTERNS = ((128, 1), (512, 4), (2048, 16))
DIL_GROUP_HEADS = (6, 5, 5)
DIL_WIN = 128


def _dil_tile_plan():
    offs = np.cumsum((0,) + DIL_GROUP_HEADS)
    pairs, group_tiles, leftovers = [], [], []
    for gi, hn in enumerate(DIL_GROUP_HEADS):
        heads = list(range(offs[gi], offs[gi + 1]))
        group_tiles.append([])
        for i in range(0, hn - hn % 2, 2):
            group_tiles[gi].append((len(pairs), (0, 1)))
            pairs.append((heads[i], heads[i + 1]))
        if hn % 2:
            leftovers.append((gi, heads[-1]))
    assert len(leftovers) % 2 == 0
    for (ga, ha), (gb, hb) in zip(leftovers[0::2], leftovers[1::2]):
        group_tiles[ga].append((len(pairs), (0,)))
        group_tiles[gb].append((len(pairs), (1,)))
        pairs.append((ha, hb))
    return tuple(pairs), tuple(tuple(tiles) for tiles in group_tiles)


DIL_HEAD_PAIRS, DIL_GROUP_TILES = _dil_tile_plan()
DIL_TILES_PER_GROUP = len(DIL_GROUP_TILES[0])
assert all(len(tiles) == DIL_TILES_PER_GROUP for tiles in DIL_GROUP_TILES)

LANES = 128
VMEM_LIMIT_BYTES = 48 * 1024 * 1024

NEG = -1e30
TQ = 128
KEY_CHUNK = 128
SLC_CK = 512
TM = 512
DIL_STEP_TOKENS = 512
CMP_STEP_TOKENS = 512


def _cparams(*sem):
    return pltpu.CompilerParams(dimension_semantics=sem, vmem_limit_bytes=VMEM_LIMIT_BYTES)


def _dot(a, b):
    return jnp.dot(a, b, preferred_element_type=F32)


def _dot_t(a, b):
    return lax.dot_general(a, b, (((1,), (1,)), ((), ())), preferred_element_type=F32)


def _silu(v):
    return v * jax.nn.sigmoid(v)


def _mod_kernel(c_ref, w_ref, b_ref, o_ref):
    s = _silu(c_ref[...])
    o_ref[...] = jnp.dot(s, w_ref[...], precision=HIGHEST, preferred_element_type=F32) + b_ref[...]


def _adaln_mod(c, ada_w, ada_b):
    depth, d, n3 = ada_w.shape
    b = c.shape[0]
    tn = 1024
    return pl.pallas_call(
        _mod_kernel,
        grid=(depth, n3 // tn),
        in_specs=[
            pl.BlockSpec((b, d), lambda i, j: (0, 0)),
            pl.BlockSpec((None, d, tn), lambda i, j: (i, 0, j)),
            pl.BlockSpec((None, 1, tn), lambda i, j: (i, 0, j)),
        ],
        out_specs=pl.BlockSpec((None, b, tn), lambda i, j: (i, 0, j)),
        out_shape=jax.ShapeDtypeStruct((depth, b, n3), F32),
        compiler_params=_cparams("arbitrary", "arbitrary"),
        name="adaln_mod",
    )(c, ada_w, ada_b.reshape(depth, 1, n3))


def _modulated_norm(x, mod_ref, g_ref):
    ms = jnp.mean(x * x, axis=-1, keepdims=True)
    y = x * lax.rsqrt(ms + NORM_EPS) * g_ref[...]
    return y * (1.0 + mod_ref[1:2, :]) + mod_ref[0:1, :]


def _rope_tile(v, cos, sin):
    return v * cos + pltpu.roll(v, 2 * HALF, axis=1) * sin


def _store_key_chunks_t(kt_ref, k):
    for c in range(kt_ref.shape[0]):
        kt_ref[c] = k[c * KEY_CHUNK:(c + 1) * KEY_CHUNK, :].T.astype(BF16)


def _inproj0_kernel(x_ref, mod_ref, g_ref, cos_ref, sin_ref, wq_ref, wkv_ref, wz_ref, wg_ref,
                    qp_ref, qr_ref, kc_ref, vc_ref, ks_ref, vs_ref, kw_ref, vw_ref, z_ref, gl_ref):
    h = _modulated_norm(x_ref[...], mod_ref, g_ref).astype(BF16)
    cos = cos_ref[...]
    sin = sin_ref[...]
    q = _dot(h, wq_ref[...])
    for j in range(q.shape[1] // LANES):
        sl = slice(j * LANES, (j + 1) * LANES)
        qp_ref[:, sl] = q[:, sl].astype(BF16)
        qr_ref[:, sl] = _rope_tile(q[:, sl], cos, sin).astype(BF16)
    kv = _dot(h, wkv_ref[...])
    tiles = [kv[:, i * LANES:(i + 1) * LANES] for i in range(6)]
    kc_ref[...] = tiles[0].astype(BF16)
    vc_ref[...] = tiles[1].astype(BF16)
    _store_key_chunks_t(ks_ref, _rope_tile(tiles[2], cos, sin))
    vs_ref[...] = tiles[3].astype(BF16)
    _store_key_chunks_t(kw_ref, _rope_tile(tiles[4], cos, sin))
    vw_ref[...] = tiles[5].astype(BF16)
    z_ref[...] = _silu(_dot(h, wz_ref[...])).astype(BF16)
    gl_ref[...] = _dot(h, wg_ref[...])


def _inproj0(x, mod, g, cos, sin, wq, wkv, wz, wg):
    b, t, d = x.shape
    nq, nz = wq.shape[1], wz.shape[1]
    tok = lambda n: pl.BlockSpec((None, TM, n), lambda i, j: (i, j, 0))
    full = lambda a: pl.BlockSpec(a.shape, lambda i, j: (0,) * a.ndim)
    tab = pl.BlockSpec((TM, LANES), lambda i, j: (j, 0))
    shp = lambda n, dt: jax.ShapeDtypeStruct((b, t, n), dt)
    kt = pl.BlockSpec((None, TM // KEY_CHUNK, LANES, KEY_CHUNK), lambda i, j: (i, j, 0, 0))
    kt_shp = jax.ShapeDtypeStruct((b, t // KEY_CHUNK, LANES, KEY_CHUNK), BF16)
    kv = tok(LANES)
    kv_shp = shp(LANES, BF16)
    return pl.pallas_call(
        _inproj0_kernel,
        grid=(b, t // TM),
        in_specs=[tok(d), pl.BlockSpec((None, 3, d), lambda i, j: (i, 0, 0)), full(g), tab, tab,
                  full(wq), full(wkv), full(wz), full(wg)],
        out_specs=[tok(nq), tok(nq), kv, kv, kt, kv, kt, kv, tok(nz), tok(LANES)],
        out_shape=[shp(nq, BF16), shp(nq, BF16), kv_shp, kv_shp, kt_shp, kv_shp, kt_shp, kv_shp,
                   shp(nz, BF16), shp(LANES, F32)],
        compiler_params=_cparams("parallel", "parallel"),
        name="inproj_nsa",
    )(x, mod, g, cos, sin, wq, wkv, wz, wg)


def _compress_kernel(ak_ref, av_ref, pekt_ref, pekb_ref, pevt_ref, pevb_ref,
                     w1kt_ref, w1kb_ref, w1vt_ref, w1vb_ref, w2k_ref, w2v_ref, kcmp_ref, vcmp_ref):
    def mlp(a_ref, pet_ref, peb_ref, w1t_ref, w1b_ref, w2_ref):
        a = a_ref[...].astype(F32)
        top = _dot((a + pet_ref[...]).astype(BF16), w1t_ref[...])
        bot = _dot((a + peb_ref[...]).astype(BF16), w1b_ref[...])
        n = bot.shape[0]
        hid = top + pltpu.roll(bot, n - 1, axis=0)
        return _dot(_silu(hid).astype(BF16), w2_ref[...])

    kcmp_ref[...] = mlp(ak_ref, pekt_ref, pekb_ref, w1kt_ref, w1kb_ref, w2k_ref).astype(BF16)
    vcmp_ref[...] = mlp(av_ref, pevt_ref, pevb_ref, w1vt_ref, w1vb_ref, w2v_ref).astype(BF16)


def _compress(ak, av, consts):
    b, n, w = ak.shape
    full = lambda a: pl.BlockSpec(a.shape, lambda i: (0,) * a.ndim)
    blk = pl.BlockSpec((None, n, w), lambda i: (i, 0, 0))
    out = pl.BlockSpec((None, n, LANES), lambda i: (i, 0, 0))
    return pl.pallas_call(
        _compress_kernel,
        grid=(b,),
        in_specs=[blk, blk] + [full(a) for a in consts],
        out_specs=[out, out],
        out_shape=[jax.ShapeDtypeStruct((b, n, LANES), BF16)] * 2,
        compiler_params=_cparams("parallel"),
        name="compress",
    )(ak, av, *consts)


def _stack_heads(q_ref, rows, lane_sets):
    n_tiles = q_ref.shape[1] // LANES
    parts = []
    for j in range(n_tiles):
        qj = q_ref[rows, j * LANES:(j + 1) * LANES]
        parts.append(jnp.where(lane_sets, qj, jnp.zeros_like(qj)))
    return jnp.concatenate(parts, axis=0)


def _head_gate_rows(gates, g, branches, lane):
    blocks = []
    for j in range(NSA_GROUP):
        base = (g * NSA_GROUP + j) * NSA_BRANCHES
        kept = jnp.zeros_like(gates)
        for br in branches:
            kept = jnp.where(lane == base + br, gates, kept)
        hi = kept.astype(BF16)
        blocks.append(jnp.concatenate([hi, (kept - hi.astype(F32)).astype(BF16)], axis=1))
    return jnp.concatenate(blocks, axis=0)


def _gate_spread(branches):
    col = np.arange(2 * LANES) % LANES
    spread = np.zeros((2 * LANES, LANES * len(branches)), np.float32)
    for i, br in enumerate(branches):
        spread[col % NSA_BRANCHES == br, i * LANES:(i + 1) * LANES] = 1.0
    return jnp.asarray(spread, dtype=BF16)


def _cmp_attn_kernel(qp_ref, kcmp_ref, vcmp_ref, ovl_ref, gl_ref, gsp_ref, ocmp_ref, msel_ref, *, n_slc, tq):
    for sub in range(qp_ref.shape[0] // tq):
        rows = slice(sub * tq, (sub + 1) * tq)
        t0 = pl.program_id(1) * qp_ref.shape[0] + sub * tq
        _cmp_attn_tile(qp_ref, kcmp_ref, vcmp_ref, ovl_ref, gl_ref, gsp_ref, ocmp_ref, msel_ref, rows, t0, n_slc)


def _cmp_attn_tile(qp_ref, kcmp_ref, vcmp_ref, ovl_ref, gl_ref, gsp_ref, ocmp_ref, msel_ref, tok_rows, t0, n_slc):
    tq = tok_rows.stop - tok_rows.start
    n_cmp = vcmp_ref.shape[0]
    hg = qp_ref.shape[1] // LANES
    assert n_cmp == LANES
    lane = lax.broadcasted_iota(jnp.int32, (tq, LANES), 1)
    a_idx = lax.broadcasted_iota(jnp.int32, (tq, LANES), 0)
    kc = kcmp_ref[...]
    vc = vcmp_ref[...]
    seen = jnp.where(CMP_STRIDE * lane + (CMP_LEN - 1) <= t0 + a_idx, 0.0, NEG)
    seen_rows = jnp.concatenate([seen] * hg, axis=0)

    jb = lax.broadcasted_iota(jnp.int32, (n_slc, tq), 0)
    cur = (t0 + lax.broadcasted_iota(jnp.int32, (n_slc, tq), 1)) // SLC_LEN
    visible = jb <= cur
    forced = (jb == 0) | (jb == cur) | (jb == cur - 1)

    gates = jax.nn.sigmoid(gl_ref[tok_rows, :])
    o_groups = []
    sel_groups = []
    for g in range(NSA_KV_HEADS):
        in_group = ((lane % HEAD_DIM) // HALF) == g
        qs = _stack_heads(qp_ref, tok_rows, in_group)
        s = _dot_t(qs, kc) + seen_rows
        m = jnp.max(s, axis=-1, keepdims=True)
        m = jnp.where(m > 0.5 * NEG, m, 0.0)
        e = jnp.exp(s - m)
        den = jnp.sum(e, axis=-1, keepdims=True)
        p = e / jnp.where(den > 0, den, 1.0)
        gate = _dot(_head_gate_rows(gates, g, (0,), lane), gsp_ref[...])
        o_groups.append(gate * _dot(p.astype(BF16), vc))
        p_sum = p[0:tq]
        for j in range(1, hg):
            p_sum = p_sum + p[j * tq:(j + 1) * tq]
        imp = lax.dot_general(ovl_ref[...], p_sum, (((1,), (1,)), ((), ())),
                              precision=HIGHEST, preferred_element_type=F32)
        rank = jnp.where(forced, -NEG, jnp.where(visible, imp, NEG))
        cnt = jnp.zeros((n_slc, tq), F32)
        for jp in range(n_slc):
            rj = rank[jp:jp + 1, :]
            tie = jnp.where(jb > jp, 1.0, 0.0)
            cnt = cnt + jnp.where(rj > rank, 1.0, jnp.where(rj == rank, tie, 0.0))
        sel_groups.append(jnp.where(visible, jnp.where(cnt < SLC_TOP, 1.0, 0.0), 0.0))

    for j in range(hg):
        rows = slice(j * tq, (j + 1) * tq)
        ocmp_ref[tok_rows, j * LANES:(j + 1) * LANES] = jnp.where(
            lane < HEAD_DIM, o_groups[0][rows], o_groups[1][rows]).astype(BF16)
    pad = jnp.zeros((LANES - NSA_KV_HEADS * n_slc, tq), F32)
    sel_t = jnp.concatenate(sel_groups + [pad], axis=0)
    msel_ref[tok_rows, :] = sel_t.T.astype(BF16)


def _cmp_attn(qp, kcmp, vcmp, ovl_t, gl):
    b, t, nq = qp.shape
    n_cmp = kcmp.shape[1]
    n_slc = t // SLC_LEN
    gsp = _gate_spread((0,))
    tok = lambda n: pl.BlockSpec((None, CMP_STEP_TOKENS, n), lambda i, j: (i, j, 0))
    per_b = pl.BlockSpec((None, n_cmp, LANES), lambda i, j: (i, 0, 0))
    full = lambda a: pl.BlockSpec(a.shape, lambda i, j: (0,) * a.ndim)
    return pl.pallas_call(
        functools.partial(_cmp_attn_kernel, n_slc=n_slc, tq=TQ),
        grid=(b, t // CMP_STEP_TOKENS),
        in_specs=[tok(nq), per_b, per_b, full(ovl_t), tok(LANES), full(gsp)],
        out_specs=[tok(nq), tok(LANES)],
        out_shape=[jax.ShapeDtypeStruct((b, t, nq), BF16), jax.ShapeDtypeStruct((b, t, LANES), BF16)],
        compiler_params=_cparams("parallel", "parallel"),
        name="cmp_attn_select",
    )(qp, kcmp, vcmp, ovl_t, gl, gsp)


def _nsa_attn_kernel(qr_ref, ks_ref, vs_ref, kw_ref, vw_ref, msel_ref, e_ref, gsp_ref, ocmp_ref, z_ref, gl_ref,
                     og_ref, q_scr, m_scr, acc_scr, *, n_slc, ck):
    tq = qr_ref.shape[0]
    hg = qr_ref.shape[1] // LANES
    rows = hg * tq
    n_wc = WIN_LEN // tq + 1
    per_ck = ck // KEY_CHUNK
    qi = pl.program_id(1)
    t0 = qi * tq
    lane = lax.broadcasted_iota(jnp.int32, (tq, LANES), 1)
    a_idx = lax.broadcasted_iota(jnp.int32, (tq, LANES), 0)

    eye = jnp.where(lane == a_idx, 1.0, 0.0).astype(BF16)
    earlier = jnp.where((lane % n_slc) < t0 // SLC_LEN, 0.0, NEG)
    mneg = jnp.where(msel_ref[...].astype(F32) > 0.5, earlier, NEG).astype(BF16)
    for g in range(NSA_KV_HEADS):
        in_group = ((lane % HEAD_DIM) // HALF) == g
        for j in range(hg):
            qj = qr_ref[:, j * LANES:(j + 1) * LANES]
            qj = jnp.where(in_group, qj, jnp.zeros_like(qj))
            q_scr[g, j * tq:(j + 1) * tq, :] = qj
    mneg_rows = jnp.concatenate([mneg] * hg, axis=0)
    eye_rows = jnp.concatenate([eye] * hg, axis=0)

    def with_selector(g, sel_rows):
        return jnp.concatenate([q_scr[g], sel_rows], axis=1)

    zero_b = jnp.zeros((tq, LANES), BF16)
    neg_b = jnp.full((tq, LANES), NEG, BF16)
    causal_b = jnp.where(lane <= a_idx, 0.0, NEG).astype(BF16)
    far_b = jnp.where(lane > a_idx, 0.0, NEG).astype(BF16)
    own_half = [lane < HEAD_DIM, lane >= HEAD_DIM]

    def values_with_ones(v, g):
        in_own_half = (lax.broadcasted_iota(jnp.int32, v.shape, 1) // HEAD_DIM) == g
        return jnp.where(in_own_half, v, jnp.ones_like(v))

    def lane_tiles(v):
        return [v[:, i * LANES:(i + 1) * LANES] for i in range(v.shape[1] // LANES)]

    def tile_max(tiles):
        mx = tiles[0]
        for v in tiles[1:]:
            mx = jnp.maximum(mx, v)
        return jnp.max(mx, axis=-1, keepdims=True)

    def probs(tiles, m):
        return jnp.concatenate([jnp.exp(v - m) for v in tiles], axis=1).astype(BF16)

    m_scr[...] = jnp.full(m_scr.shape, NEG, F32)
    acc_scr[...] = jnp.zeros(acc_scr.shape, F32)

    def earlier_keys(kc, carry):
        k0 = pl.multiple_of(kc * ck, ck)
        kt = jnp.concatenate([ks_ref[kc * per_ck + i] for i in range(per_ck)], axis=1)
        scores = []
        for g in range(NSA_KV_HEADS):
            eb = jnp.concatenate([e_ref[g, kc * per_ck + i] for i in range(per_ck)], axis=1)
            scores.append(_dot(with_selector(g, mneg_rows), jnp.concatenate([kt, eb], axis=0)))
        for g in range(NSA_KV_HEADS):
            tiles = lane_tiles(scores[g])
            m_old = m_scr[g]
            m_new = jnp.maximum(m_old, tile_max(tiles))
            pv = _dot(probs(tiles, m_new), values_with_ones(vs_ref[pl.ds(k0, ck), :], g))
            acc_scr[g] = jnp.exp(m_old - m_new) * acc_scr[g] + pv
            m_scr[g] = m_new
        return carry

    lax.fori_loop(0, (t0 + ck - 1) // ck, earlier_keys, 0)

    kd = pl.multiple_of(t0, tq)
    zeros_v = jnp.zeros((tq, LANES), BF16)
    own_k = jnp.concatenate([jnp.concatenate([kw_ref[qi], causal_b], axis=0),
                             jnp.concatenate([ks_ref[qi], causal_b], axis=0)], axis=1)
    win_chunks = [qi - (n_wc - 1 - c) for c in range(n_wc - 1)]
    kbs = [jnp.concatenate([kw_ref[jnp.maximum(cidx, 0)], jnp.where(cidx >= 0, far_b if c == 0 else zero_b, neg_b)],
                           axis=0) for c, cidx in enumerate(win_chunks)]
    pairs = [(c, c + 1) for c in range(0, n_wc - 1, 2)]
    pair_k = [jnp.concatenate([kbs[c] for c in pr], axis=1) for pr in pairs]
    s_own, scores = [], []
    for g in range(NSA_KV_HEADS):
        q_eye = with_selector(g, eye_rows)
        s_own.append(_dot(q_eye, own_k))
        scores.append([_dot(q_eye, kb) for kb in pair_k])

    acc_slc, acc_win = [], []
    for g in range(NSA_KV_HEADS):
        s_win_own, s_slc_own = s_own[g][:, 0:tq], s_own[g][:, tq:2 * tq]
        m_win = tile_max([v for s in scores[g] for v in lane_tiles(s)] + [s_win_own])
        m_old = m_scr[g]
        m_slc = jnp.maximum(m_old, jnp.max(s_slc_own, axis=-1, keepdims=True))
        vbs = [values_with_ones(vw_ref[pl.ds(pl.multiple_of(jnp.maximum(cidx, 0) * tq, tq), tq), :], g)
               for cidx in win_chunks] + [values_with_ones(vw_ref[pl.ds(kd, tq), :], g)]
        win_v = [jnp.concatenate([v, zeros_v], axis=1) for v in vbs]
        slc_v = jnp.concatenate([zeros_v, values_with_ones(vs_ref[pl.ds(kd, tq), :], g)], axis=1)
        p_all = jnp.concatenate([probs(lane_tiles(s), m_win) for s in scores[g]]
                                + [probs([s_win_own], m_win), probs([s_slc_own], m_slc)], axis=1)
        pv = _dot(p_all, jnp.concatenate(win_v + [slc_v], axis=0))
        acc_win.append(pv[:, 0:LANES])
        acc_slc.append(jnp.exp(m_old - m_slc) * acc_scr[g] + pv[:, LANES:2 * LANES])

    gates = jax.nn.sigmoid(gl_ref[...])
    row_gates = [_dot(_head_gate_rows(gates, g, (1, 2), lane), gsp_ref[...]) for g in range(NSA_KV_HEADS)]

    def gated(accs, rs, branch_lanes):
        num = [accs[g][rs] * row_gates[g][rs, branch_lanes] for g in range(NSA_KV_HEADS)]
        sums = pltpu.roll(jnp.where(own_half[0], accs[1][rs], accs[0][rs]), HEAD_DIM, axis=1)
        return jnp.where(own_half[0], num[0], num[1]) / sums

    for j in range(hg):
        rs = slice(j * tq, (j + 1) * tq)
        cols = slice(j * LANES, (j + 1) * LANES)
        o = (ocmp_ref[:, cols].astype(F32) + gated(acc_slc, rs, slice(0, LANES))
             + gated(acc_win, rs, slice(LANES, 2 * LANES)))
        og_ref[:, cols] = (o * z_ref[:, cols].astype(F32)).astype(BF16)


def _nsa_attn(qr, ks, vs, kw, vw, msel, e, ocmp, z, gl):
    b, t, nq = qr.shape
    gsp = _gate_spread((1, 2))
    hg = nq // LANES
    rows = hg * TQ
    assert TQ == KEY_CHUNK and WIN_LEN % TQ == 0
    tok = lambda n: pl.BlockSpec((None, TQ, n), lambda i, j: (i, j, 0))
    per_b = pl.BlockSpec((None, t, LANES), lambda i, j: (i, 0, 0))
    per_b_t = pl.BlockSpec((None, t // KEY_CHUNK, LANES, KEY_CHUNK), lambda i, j: (i, 0, 0, 0))
    full = lambda a: pl.BlockSpec(a.shape, lambda i, j: (0,) * a.ndim)
    return pl.pallas_call(
        functools.partial(_nsa_attn_kernel, n_slc=t // SLC_LEN, ck=SLC_CK),
        grid=(b, t // TQ),
        in_specs=[tok(nq), per_b_t, per_b, per_b_t, per_b, tok(LANES), full(e), full(gsp), tok(nq), tok(nq), tok(LANES)],
        out_specs=tok(nq),
        out_shape=jax.ShapeDtypeStruct((b, t, nq), BF16),
        scratch_shapes=[
            pltpu.VMEM((NSA_KV_HEADS, rows, LANES), BF16),
            pltpu.VMEM((NSA_KV_HEADS, rows, LANES), F32),
            pltpu.VMEM((NSA_KV_HEADS, rows, LANES), F32),
        ],
        compiler_params=_cparams("parallel", "arbitrary"),
        name="nsa_slc_win_attn",
    )(qr, ks, vs, kw, vw, msel, e, gsp, ocmp, z, gl)


def _outproj0_kernel(og_ref, w_ref, x_ref, mod_ref, o_ref):
    y = _dot(og_ref[...], w_ref[...])
    o_ref[...] = x_ref[...] + mod_ref[2:3, :] * y


def _outproj0(og, w, x, mod):
    b, t, d = x.shape
    n = og.shape[2]
    tok = lambda k: pl.BlockSpec((None, TM, k), lambda i, j: (i, j, 0))
    return pl.pallas_call(
        _outproj0_kernel,
        grid=(b, t // TM),
        in_specs=[tok(n), pl.BlockSpec(w.shape, lambda i, j: (0, 0)), tok(d),
                  pl.BlockSpec((None, 3, d), lambda i, j: (i, 0, 0))],
        out_specs=tok(d),
        out_shape=jax.ShapeDtypeStruct((b, t, d), F32),
        compiler_params=_cparams("parallel", "parallel"),
        name="outproj_nsa",
    )(og, w, x, mod)


def _inproj1_kernel(x_ref, mod_ref, g_ref, cos_ref, sin_ref, wq_ref, wk_ref, wv_ref, wz_ref, *refs):
    n_grp = len(DIL_PATTERNS)
    outs = [refs[a * n_grp:(a + 1) * n_grp] for a in range(3)]
    z_ref, stage = refs[3 * n_grp], refs[3 * n_grp + 1]
    tm = x_ref.shape[0]
    gw = DIL_TILES_PER_GROUP * LANES
    h = _modulated_norm(x_ref[...], mod_ref, g_ref).astype(BF16)
    cos = cos_ref[...]
    sin = sin_ref[...]
    for a, w_ref in enumerate((wq_ref, wk_ref, wv_ref)):
        u = _dot(h, w_ref[...])
        tiles = [u[:, pt * LANES:(pt + 1) * LANES] for pt in range(len(DIL_HEAD_PAIRS))]
        if a < 2:
            tiles = [_rope_tile(v, cos, sin) for v in tiles]
        for gi, (_, dil) in enumerate(DIL_PATTERNS):
            for jt, (pt, _) in enumerate(DIL_GROUP_TILES[gi]):
                tile = tiles[pt]
                if dil == 1:
                    outs[a][gi][:, jt * LANES:(jt + 1) * LANES] = tile.astype(BF16)
                    continue
                slab = (a * n_grp + gi) * DIL_TILES_PER_GROUP + jt
                stage[slab] = tile
                for r in range(dil):
                    rows = stage[slab, pl.ds(r, tm // dil, stride=dil), :]
                    outs[a][gi][:, r * gw + jt * LANES:r * gw + (jt + 1) * LANES] = rows.astype(BF16)
    z_ref[...] = _silu(_dot(h, wz_ref[...])).astype(BF16)


def _inproj1(x, mod, g, cos, sin, wq, wk, wv, wz):
    b, t, d = x.shape
    n = wq.shape[1]
    gw = DIL_TILES_PER_GROUP * LANES
    tok = lambda k: pl.BlockSpec((None, TM, k), lambda i, j: (i, j, 0))
    full = lambda a: pl.BlockSpec(a.shape, lambda i, j: (0,) * a.ndim)
    tab = pl.BlockSpec((TM, LANES), lambda i, j: (j, 0))
    grp_specs = [pl.BlockSpec((None, TM // dil, dil * gw), lambda i, j: (i, j, 0)) for _, dil in DIL_PATTERNS]
    grp_shapes = [jax.ShapeDtypeStruct((b, t // dil, dil * gw), BF16) for _, dil in DIL_PATTERNS]
    n_slabs = 3 * len(DIL_PATTERNS) * DIL_TILES_PER_GROUP
    return pl.pallas_call(
        _inproj1_kernel,
        grid=(b, t // TM),
        in_specs=[tok(d), pl.BlockSpec((None, 3, d), lambda i, j: (i, 0, 0)), full(g), tab, tab,
                  full(wq), full(wk), full(wv), full(wz)],
        out_specs=grp_specs * 3 + [tok(n)],
        out_shape=grp_shapes * 3 + [jax.ShapeDtypeStruct((b, t, n), BF16)],
        scratch_shapes=[pltpu.VMEM((n_slabs, TM, LANES), F32)],
        compiler_params=_cparams("parallel", "parallel"),
        name="inproj_dil",
    )(x, mod, g, cos, sin, wq, wk, wv, wz)


def _dil_attn_kernel(*refs, plans):
    n_grp = len(plans)
    step = pl.program_id(1)
    for gi, (members, tq, n_seq, blocks_per_seq) in enumerate(plans):
        q_ref, k_ref, v_ref = refs[3 * gi:3 * gi + 3]
        o_ref, lse_ref = refs[3 * n_grp + 2 * gi:3 * n_grp + 2 * gi + 2]
        _dil_attn_group(q_ref, k_ref, v_ref, o_ref, lse_ref, step % blocks_per_seq, members, tq, n_seq)


def _dil_attn_group(q_ref, k_ref, v_ref, o_ref, lse_ref, i, members, tq, n_seq):
    n_heads = sum(len(m) for m in members)
    tq_blk = q_ref.shape[0]
    gw = q_ref.shape[1] // n_seq
    n_tiles = gw // LANES
    lane = lax.broadcasted_iota(jnp.int32, (tq, LANES), 1)
    a_idx = lax.broadcasted_iota(jnp.int32, (tq, LANES), 0)
    first = ((lane % HEAD_DIM) // HALF) == 0
    eye = jnp.where(lane == a_idx, 1.0, 0.0).astype(BF16)
    diag_b = jnp.where(lane <= a_idx, 0.0, NEG).astype(BF16)
    prev_b = jnp.where(lane >= a_idx, 0.0, NEG).astype(BF16)
    neg_b = jnp.full((tq, LANES), NEG, BF16)
    for sq in range(n_seq):
        kt_prev = [None] * n_tiles
        for sub in range(tq_blk // tq):
            cdiag = i * (tq_blk // tq) + sub
            kdiag = pl.multiple_of(cdiag * tq, tq)
            kprev = pl.multiple_of(jnp.maximum(cdiag - 1, 0) * tq, tq)
            prev_bias = prev_b if sub > 0 else jnp.where(i > 0, prev_b, neg_b)
            rows = slice(sub * tq, (sub + 1) * tq)
            lses = []
            for jt in range(n_tiles):
                cols = slice(sq * gw + jt * LANES, sq * gw + (jt + 1) * LANES)
                qj = q_ref[rows, cols]
                zero = jnp.zeros_like(qj)
                qs = jnp.concatenate([jnp.concatenate([jnp.where(first, qj, zero), eye], axis=1),
                                      jnp.concatenate([jnp.where(first, zero, qj), eye], axis=1)], axis=0)
                kt_diag = k_ref[pl.ds(kdiag, tq), cols].T
                kt_before = kt_prev[jt] if sub > 0 else k_ref[pl.ds(kprev, tq), cols].T
                kt_prev[jt] = kt_diag
                kb = jnp.concatenate([jnp.concatenate([kt_before, prev_bias], axis=0),
                                      jnp.concatenate([kt_diag, diag_b], axis=0)], axis=1)
                vb = jnp.concatenate([v_ref[pl.ds(kprev, tq), cols], v_ref[pl.ds(kdiag, tq), cols]], axis=0)
                s = _dot(qs, kb)
                m = jnp.max(s, axis=-1, keepdims=True)
                p = jnp.exp(s - m)
                l = jnp.sum(p, axis=-1, keepdims=True)
                o = _dot(p.astype(BF16), vb) / l
                o_ref[rows, cols] = jnp.where(lane < HEAD_DIM, o[0:tq], o[tq:2 * tq]).astype(BF16)
                lse = m + jnp.log(l)
                lses += [lse[mb * tq:(mb + 1) * tq] for mb in members[jt]]
            top = lses[0]
            for v in lses[1:]:
                top = jnp.maximum(top, v)
            tot = jnp.zeros_like(top)
            for v in lses:
                tot = tot + jnp.exp(v - top)
            group_lse = top + jnp.log(tot) - math.log(n_heads)
            lse_ref[rows, sq * LANES:(sq + 1) * LANES] = jnp.broadcast_to(group_lse, (tq, LANES))


def _dil_attn(qs, ks, vs, t):
    b = qs[0].shape[0]
    steps = t // DIL_STEP_TOKENS
    plans, in_specs, out_specs, out_shapes, operands = [], [], [], [], []
    for gi, (_, dil) in enumerate(DIL_PATTERNS):
        length = t // dil
        gw = qs[gi].shape[2] // dil
        tq_blk = min(DIL_STEP_TOKENS, length)
        blocks_per_seq = length // tq_blk
        n_seq = DIL_STEP_TOKENS // tq_blk
        assert dil * blocks_per_seq == steps * n_seq
        tok = lambda w, bps=blocks_per_seq, ns=n_seq, tb=tq_blk: pl.BlockSpec(
            (None, tb, ns * w), lambda bi, s: (bi, s % bps, s // bps))
        seq = pl.BlockSpec((None, length, n_seq * gw), lambda bi, s, bps=blocks_per_seq: (bi, 0, s // bps))
        plans.append((tuple(m for _, m in DIL_GROUP_TILES[gi]), min(TQ, length), n_seq, blocks_per_seq))
        in_specs += [tok(gw), seq, seq]
        operands += [qs[gi], ks[gi], vs[gi]]
        out_specs += [tok(gw), tok(LANES)]
        out_shapes += [jax.ShapeDtypeStruct(qs[gi].shape, BF16), jax.ShapeDtypeStruct((b, length, dil * LANES), F32)]
    outs = pl.pallas_call(
        functools.partial(_dil_attn_kernel, plans=tuple(plans)),
        grid=(b, steps),
        in_specs=in_specs,
        out_specs=out_specs,
        out_shape=out_shapes,
        compiler_params=_cparams("parallel", "parallel"),
        name="dil_attn",
    )(*operands)
    return outs[0::2], outs[1::2]


def _outproj1_kernel(*refs):
    n_grp = len(DIL_PATTERNS)
    o_refs, l_refs = refs[0:n_grp], refs[n_grp:2 * n_grp]
    z_ref, w_ref, x_ref, mod_ref, fg_ref, out_ref, stage = refs[2 * n_grp:]
    tm = x_ref.shape[0]
    gw = DIL_TILES_PER_GROUP * LANES

    def token_order(ref, gi, width, col0, slab):
        dil = DIL_PATTERNS[gi][1]
        if dil == 1:
            return ref[:, col0:col0 + LANES].astype(F32)
        for r in range(dil):
            stage[slab, pl.ds(r, tm // dil, stride=dil), :] = ref[:, r * width + col0:r * width + col0 + LANES].astype(F32)
        return stage[slab]

    ls = [token_order(l_refs[gi], gi, LANES, 0, gi) for gi in range(n_grp)]
    top = jnp.maximum(jnp.maximum(ls[0], ls[1]), ls[2])
    es = [jnp.exp(v - top) for v in ls]
    den = es[0] + es[1] + es[2]
    alphas = [e / den * float(n_grp) for e in es]
    lane = lax.broadcasted_iota(jnp.int32, (tm, LANES), 1)
    weighted = [None] * len(DIL_HEAD_PAIRS)
    for gi in range(n_grp):
        for jt, (pt, members) in enumerate(DIL_GROUP_TILES[gi]):
            ot = token_order(o_refs[gi], gi, gw, jt * LANES, n_grp + gi * DIL_TILES_PER_GROUP + jt) * alphas[gi]
            if len(members) == 2:
                weighted[pt] = ot
            else:
                mine = (lane // HEAD_DIM) == members[0]
                weighted[pt] = jnp.where(mine, ot, 0.0 if weighted[pt] is None else weighted[pt])
    parts = [(weighted[pt] * z_ref[:, pt * LANES:(pt + 1) * LANES].astype(F32)).astype(BF16)
             for pt in range(len(DIL_HEAD_PAIRS))]
    og = jnp.concatenate(parts, axis=1)
    y = _dot(og, w_ref[...])
    x2 = x_ref[...] + mod_ref[2:3, :] * y
    ms = jnp.mean(x2 * x2, axis=-1, keepdims=True)
    out_ref[...] = x2 * lax.rsqrt(ms + NORM_EPS) * fg_ref[...]


def _outproj1(o_groups, lse_groups, z, w, x, mod, final_g):
    b, t, d = x.shape
    tok = lambda k: pl.BlockSpec((None, TM, k), lambda i, j: (i, j, 0))
    full = lambda a: pl.BlockSpec(a.shape, lambda i, j: (0,) * a.ndim)
    grp = lambda a, dil: pl.BlockSpec((None, TM // dil, a.shape[2]), lambda i, j: (i, j, 0))
    dils = [dil for _, dil in DIL_PATTERNS]
    n_slabs = len(dils) * (1 + DIL_TILES_PER_GROUP)
    return pl.pallas_call(
        _outproj1_kernel,
        grid=(b, t // TM),
        in_specs=[grp(a, dl) for a, dl in zip(o_groups, dils)] + [grp(a, dl) for a, dl in zip(lse_groups, dils)]
                 + [tok(z.shape[2]), full(w), tok(d), pl.BlockSpec((None, 3, d), lambda i, j: (i, 0, 0)), full(final_g)],
        out_specs=tok(d),
        out_shape=jax.ShapeDtypeStruct((b, t, d), F32),
        scratch_shapes=[pltpu.VMEM((n_slabs, TM, LANES), F32)],
        compiler_params=_cparams("parallel", "parallel"),
        name="outproj_dil_final",
    )(*o_groups, *lse_groups, z, w, x, mod, final_g)


def _rope_tables(t):
    inv = ROPE_THETA ** (-jnp.arange(HALF, dtype=F32) / HALF)
    ang = jnp.arange(t, dtype=F32)[:, None] * inv[None, :]
    cos = jnp.tile(jnp.cos(ang), (1, 4))
    sin = jnp.tile(jnp.sin(ang), (1, 4))
    sign = jnp.where(jnp.arange(LANES) < 2 * HALF, -1.0, 1.0).astype(F32)
    return cos, sin * sign[None, :]


def _pair_rope_layout(w, n_pairs):
    d = w.shape[0]
    return w.reshape(d, n_pairs, 2, 2, HALF).transpose(0, 1, 3, 2, 4).reshape(d, n_pairs * LANES)


def _nsa_weights(w_in, pe_k, pe_v, ck_w1, ck_w2, cv_w1, cv_w2, w_out):
    d = w_in.shape[0]
    aw = N_HEADS * HEAD_DIM
    kvw = NSA_KV_HEADS * HEAD_DIM
    cuts = np.cumsum([aw] + [kvw] * 6 + [aw]).tolist()
    q, k_c, v_c, k_s, v_s, k_w, v_w, z, gl = jnp.split(w_in, cuts, axis=1)
    scale = HEAD_DIM ** -0.5
    wq = (q * scale).reshape(d, NSA_KV_HEADS, NSA_GROUP, 2, HALF).transpose(0, 2, 3, 1, 4).reshape(d, aw)
    k_lay = lambda w: _pair_rope_layout(w, 1)
    wkv = jnp.concatenate([k_lay(k_c), v_c, k_lay(k_s), v_s, k_lay(k_w), v_w], axis=1)
    wz = z.reshape(d, NSA_KV_HEADS, NSA_GROUP, HEAD_DIM).transpose(0, 2, 1, 3).reshape(d, aw)
    wo = w_out.reshape(NSA_KV_HEADS, NSA_GROUP, HEAD_DIM, -1).transpose(1, 0, 2, 3).reshape(aw, -1)
    wg = jnp.pad(gl, ((0, 0), (0, LANES - gl.shape[1])))

    def grouped(w1_half, rope_lanes):
        out = []
        for g in range(NSA_KV_HEADS):
            if rope_lanes:
                src = w1_half.reshape(CMP_STRIDE, 2, 1, HALF, CMP_HIDDEN)
                pads = ((0, 0), (0, 0), (g, NSA_KV_HEADS - 1 - g), (0, 0), (0, 0))
            else:
                src = w1_half.reshape(CMP_STRIDE, 1, HEAD_DIM, CMP_HIDDEN)
                pads = ((0, 0), (g, NSA_KV_HEADS - 1 - g), (0, 0), (0, 0))
            out.append(jnp.pad(src, pads).reshape(CMP_STRIDE * LANES, CMP_HIDDEN))
        return jnp.concatenate(out, axis=1)

    def pe_rows(pe_half, rope_lanes):
        if rope_lanes:
            v = jnp.broadcast_to(pe_half.reshape(CMP_STRIDE, 2, 1, HALF), (CMP_STRIDE, 2, NSA_KV_HEADS, HALF))
        else:
            v = jnp.broadcast_to(pe_half.reshape(CMP_STRIDE, 1, HEAD_DIM), (CMP_STRIDE, NSA_KV_HEADS, HEAD_DIM))
        return v.reshape(1, CMP_STRIDE * LANES)

    def w2_padded(w2, rope_lanes):
        out = []
        for g in range(NSA_KV_HEADS):
            if rope_lanes:
                src = w2.reshape(CMP_HIDDEN, 2, 1, HALF)
                pads = ((0, 0), (0, 0), (g, NSA_KV_HEADS - 1 - g), (0, 0))
            else:
                src = w2.reshape(CMP_HIDDEN, 1, HEAD_DIM)
                pads = ((0, 0), (g, NSA_KV_HEADS - 1 - g), (0, 0))
            out.append(jnp.pad(src, pads).reshape(CMP_HIDDEN, LANES))
        return jnp.concatenate(out, axis=0)

    k1 = ck_w1.reshape(2, CMP_STRIDE, HEAD_DIM, CMP_HIDDEN)
    v1 = cv_w1.reshape(2, CMP_STRIDE, HEAD_DIM, CMP_HIDDEN)
    compress_consts = (
        pe_rows(pe_k[:CMP_STRIDE], True), pe_rows(pe_k[CMP_STRIDE:], True),
        pe_rows(pe_v[:CMP_STRIDE], False), pe_rows(pe_v[CMP_STRIDE:], False),
        grouped(k1[0], True).astype(BF16), grouped(k1[1], True).astype(BF16),
        grouped(v1[0], False).astype(BF16), grouped(v1[1], False).astype(BF16),
        w2_padded(ck_w2, True).astype(BF16), w2_padded(cv_w2, False).astype(BF16),
    )
    return wq.astype(BF16), wkv.astype(BF16), wz.astype(BF16), wg.astype(BF16), wo.astype(BF16), compress_consts


def _dil_weights(w_in, w_out):
    d = w_in.shape[0]
    q, k, v, z = jnp.split(w_in, 4, axis=1)
    n_pairs = len(DIL_HEAD_PAIRS)
    order = [hd for pair in DIL_HEAD_PAIRS for hd in pair]

    def tile_order(w, axis):
        parts = []
        for hd in order:
            sl = [slice(None)] * w.ndim
            sl[axis] = slice(hd * HEAD_DIM, (hd + 1) * HEAD_DIM)
            parts.append(w[tuple(sl)])
        return jnp.concatenate(parts, axis=axis)

    scale = HEAD_DIM ** -0.5
    wq = _pair_rope_layout(tile_order(q * scale, 1), n_pairs)
    wk = _pair_rope_layout(tile_order(k, 1), n_pairs)
    return (wq.astype(BF16), wk.astype(BF16), tile_order(v, 1).astype(BF16), tile_order(z, 1).astype(BF16),
            tile_order(w_out, 0).astype(BF16))


def _selection_constants(t):
    n_cmp_rows = t // CMP_STRIDE
    n_slc = t // SLC_LEN
    c_start = CMP_STRIDE * np.arange(n_cmp_rows)
    s_start = SLC_LEN * np.arange(n_slc)
    ovl_t = ((c_start[None, :] < s_start[:, None] + SLC_LEN)
             & (c_start[None, :] + CMP_LEN > s_start[:, None])).astype(np.float32)
    ovl_t[:, (t - CMP_LEN) // CMP_STRIDE + 1:] = 0.0
    e = np.zeros((NSA_KV_HEADS, t // KEY_CHUNK, LANES, KEY_CHUNK), np.float32)
    keys = np.arange(t)
    for g in range(NSA_KV_HEADS):
        e[g, keys // KEY_CHUNK, g * n_slc + keys // SLC_LEN, keys % KEY_CHUNK] = 1.0
    return jnp.asarray(ovl_t), jnp.asarray(e, dtype=BF16)


@jax.jit
def kernel(x, c, norm_g, ada_w, ada_b, nsa_w_in, nsa_pe_k, nsa_pe_v, nsa_ck_w1, nsa_ck_w2,
           nsa_cv_w1, nsa_cv_w2, nsa_w_out, dil_w_in, dil_w_out, final_g):
    b, t, d = x.shape
    assert t % SLC_CK == 0 and t % TM == 0 and d % LANES == 0
    assert NSA_KV_HEADS * (t // SLC_LEN) <= LANES and t // CMP_STRIDE == LANES
    mod = _adaln_mod(c, ada_w, ada_b).reshape(ada_w.shape[0], b, 3, d)
    cos, sin = _rope_tables(t)

    wq, wkv, wz, wg, wo, compress_consts = _nsa_weights(
        nsa_w_in[0], nsa_pe_k[0], nsa_pe_v[0], nsa_ck_w1[0], nsa_ck_w2[0], nsa_cv_w1[0], nsa_cv_w2[0], nsa_w_out[0])
    qp, qr, kc, vc, ks, vs, kw, vw, z, gl = _inproj0(x, mod[0], norm_g[0:1], cos, sin, wq, wkv, wz, wg)
    rows16 = lambda a: a.reshape(b, t // CMP_STRIDE, CMP_STRIDE * LANES)
    kcmp, vcmp = _compress(rows16(kc), rows16(vc), compress_consts)
    ovl_t, e_sel = _selection_constants(t)
    ocmp, msel = _cmp_attn(qp, kcmp, vcmp, ovl_t, gl)
    og = _nsa_attn(qr, ks, vs, kw, vw, msel, e_sel, ocmp, z, gl)
    x1 = _outproj0(og, wo, x, mod[0])

    dq, dk, dv, dz, dwo = _dil_weights(dil_w_in[0], dil_w_out[0])
    n_grp = len(DIL_PATTERNS)
    *qkv, z1 = _inproj1(x1, mod[1], norm_g[1:2], cos, sin, dq, dk, dv, dz)
    assert all(win // dil == DIL_WIN and TM % (dil * 16) == 0 for win, dil in DIL_PATTERNS)
    o_groups, lse_groups = _dil_attn(qkv[0:n_grp], qkv[n_grp:2 * n_grp], qkv[2 * n_grp:3 * n_grp], t)
    return _outproj1(o_groups, lse_groups, z1, dwo, x1, mod[1], final_g.reshape(1, d))
```

```python
import functools
import math

import numpy as np
import jax
import jax.numpy as jnp
from jax import lax
from jax.experimental import pallas as pl
from jax.experimental.pallas import tpu as pltpu

F32 = jnp.float32
BF16 = jnp.bfloat16
HIGHEST = lax.Precision.HIGHEST

HEAD_DIM = 64
HALF = HEAD_DIM // 2
N_HEADS = 16
ROPE_THETA = 10000.0
NORM_EPS = 1e-6
NSA_KV_HEADS = 2
NSA_GROUP = N_HEADS // NSA_KV_HEADS
NSA_BRANCHES = 3
CMP_LEN = 32
CMP_STRIDE = 16
CMP_HIDDEN = 256
SLC_LEN = 64
SLC_TOP = 16
WIN_LEN = 512
DIL_PATTERNS = ((128, 1), (512, 4), (2048, 16))
DIL_GROUP_HEADS = (6, 5, 5)
DIL_WIN = 128


def _dil_tile_plan():
    offs = np.cumsum((0,) + DIL_GROUP_HEADS)
    pairs, group_tiles, leftovers = [], [], []
    for gi, hn in enumerate(DIL_GROUP_HEADS):
        heads = list(range(offs[gi], offs[gi + 1]))
        group_tiles.append([])
        for i in range(0, hn - hn % 2, 2):
            group_tiles[gi].append((len(pairs), (0, 1)))
            pairs.append((heads[i], heads[i + 1]))
        if hn % 2:
            leftovers.append((gi, heads[-1]))
    assert len(leftovers) % 2 == 0
    for (ga, ha), (gb, hb) in zip(leftovers[0::2], leftovers[1::2]):
        group_tiles[ga].append((len(pairs), (0,)))
        group_tiles[gb].append((len(pairs), (1,)))
        pairs.append((ha, hb))
    return tuple(pairs), tuple(tuple(tiles) for tiles in group_tiles)


DIL_HEAD_PAIRS, DIL_GROUP_TILES = _dil_tile_plan()
DIL_TILES_PER_GROUP = len(DIL_GROUP_TILES[0])
assert all(len(tiles) == DIL_TILES_PER_GROUP for tiles in DIL_GROUP_TILES)

LANES = 128
VMEM_LIMIT_BYTES = 48 * 1024 * 1024

NEG = -1e30
TQ = 128
KEY_CHUNK = 128
SLC_CK = 512
TM = 512
DIL_STEP_TOKENS = 512
CMP_STEP_TOKENS = 512


def _cparams(*sem):
    return pltpu.CompilerParams(dimension_semantics=sem, vmem_limit_bytes=VMEM_LIMIT_BYTES)


def _dot(a, b):
    return jnp.dot(a, b, preferred_element_type=F32)


def _dot_t(a, b):
    return lax.dot_general(a, b, (((1,), (1,)), ((), ())), preferred_element_type=F32)


def _silu(v):
    return v * jax.nn.sigmoid(v)


def _mod_kernel(c_ref, w_ref, b_ref, o_ref):
    s = _silu(c_ref[...])
    o_ref[...] = jnp.dot(s, w_ref[...], precision=HIGHEST, preferred_element_type=F32) + b_ref[...]


def _adaln_mod(c, ada_w, ada_b):
    depth, d, n3 = ada_w.shape
    b = c.shape[0]
    tn = 1024
    return pl.pallas_call(
        _mod_kernel,
        grid=(depth, n3 // tn),
        in_specs=[
            pl.BlockSpec((b, d), lambda i, j: (0, 0)),
            pl.BlockSpec((None, d, tn), lambda i, j: (i, 0, j)),
            pl.BlockSpec((None, 1, tn), lambda i, j: (i, 0, j)),
        ],
        out_specs=pl.BlockSpec((None, b, tn), lambda i, j: (i, 0, j)),
        out_shape=jax.ShapeDtypeStruct((depth, b, n3), F32),
        compiler_params=_cparams("arbitrary", "arbitrary"),
        name="adaln_mod",
    )(c, ada_w, ada_b.reshape(depth, 1, n3))


def _modulated_norm(x, mod_ref, g_ref):
    ms = jnp.mean(x * x, axis=-1, keepdims=True)
    y = x * lax.rsqrt(ms + NORM_EPS) * g_ref[...]
    return y * (1.0 + mod_ref[1:2, :]) + mod_ref[0:1, :]


def _rope_tile(v, cos, sin):
    return v * cos + pltpu.roll(v, 2 * HALF, axis=1) * sin


def _store_key_chunks_t(kt_ref, k):
    for c in range(kt_ref.shape[0]):
        kt_ref[c] = k[c * KEY_CHUNK:(c + 1) * KEY_CHUNK, :].T.astype(BF16)


def _inproj0_kernel(x_ref, mod_ref, g_ref, cos_ref, sin_ref, wq_ref, wkv_ref, wz_ref, wg_ref,
                    qp_ref, qr_ref, kc_ref, vc_ref, ks_ref, vs_ref, kw_ref, vw_ref, z_ref, gl_ref, stage):
    tm = x_ref.shape[0]
    h = _modulated_norm(x_ref[...], mod_ref, g_ref).astype(BF16)
    cos = cos_ref[...]
    sin = sin_ref[...]
    q = _dot(h, wq_ref[...])
    for j in range(q.shape[1] // LANES):
        sl = slice(j * LANES, (j + 1) * LANES)
        qp_ref[:, sl] = q[:, sl].astype(BF16)
        qr_ref[:, sl] = _rope_tile(q[:, sl], cos, sin).astype(BF16)
    kv = _dot(h, wkv_ref[...])
    tiles = [kv[:, i * LANES:(i + 1) * LANES] for i in range(6)]
    for slab, (tile, o_ref) in enumerate(((tiles[0], kc_ref), (tiles[1], vc_ref))):
        stage[slab] = tile
        for r in range(CMP_STRIDE):
            rows = stage[slab, pl.ds(r, tm // CMP_STRIDE, stride=CMP_STRIDE), :]
            o_ref[:, r * LANES:(r + 1) * LANES] = rows.astype(BF16)
    _store_key_chunks_t(ks_ref, _rope_tile(tiles[2], cos, sin))
    vs_ref[...] = tiles[3].astype(BF16)
    _store_key_chunks_t(kw_ref, _rope_tile(tiles[4], cos, sin))
    vw_ref[...] = tiles[5].astype(BF16)
    z_ref[...] = _silu(_dot(h, wz_ref[...])).astype(BF16)
    gl_ref[...] = _dot(h, wg_ref[...])


def _inproj0(x, mod, g, cos, sin, wq, wkv, wz, wg):
    b, t, d = x.shape
    nq, nz = wq.shape[1], wz.shape[1]
    tok = lambda n: pl.BlockSpec((None, TM, n), lambda i, j: (i, j, 0))
    full = lambda a: pl.BlockSpec(a.shape, lambda i, j: (0,) * a.ndim)
    tab = pl.BlockSpec((TM, LANES), lambda i, j: (j, 0))
    shp = lambda n, dt: jax.ShapeDtypeStruct((b, t, n), dt)
    kt = pl.BlockSpec((None, TM // KEY_CHUNK, LANES, KEY_CHUNK), lambda i, j: (i, j, 0, 0))
    kt_shp = jax.ShapeDtypeStruct((b, t // KEY_CHUNK, LANES, KEY_CHUNK), BF16)
    kv = tok(LANES)
    kv_shp = shp(LANES, BF16)
    cm = pl.BlockSpec((None, TM // CMP_STRIDE, CMP_STRIDE * LANES), lambda i, j: (i, j, 0))
    cm_shp = jax.ShapeDtypeStruct((b, t // CMP_STRIDE, CMP_STRIDE * LANES), BF16)
    return pl.pallas_call(
        _inproj0_kernel,
        grid=(b, t // TM),
        in_specs=[tok(d), pl.BlockSpec((None, 3, d), lambda i, j: (i, 0, 0)), full(g), tab, tab,
                  full(wq), full(wkv), full(wz), full(wg)],
        out_specs=[tok(nq), tok(nq), cm, cm, kt, kv, kt, kv, tok(nz), tok(LANES)],
        out_shape=[shp(nq, BF16), shp(nq, BF16), cm_shp, cm_shp, kt_shp, kv_shp, kt_shp, kv_shp,
                   shp(nz, BF16), shp(LANES, F32)],
        scratch_shapes=[pltpu.VMEM((2, TM, LANES), F32)],
        compiler_params=_cparams("parallel", "parallel"),
        name="inproj_nsa",
    )(x, mod, g, cos, sin, wq, wkv, wz, wg)


def _compress_kernel(ak_ref, av_ref, pekt_ref, pekb_ref, pevt_ref, pevb_ref,
                     w1kt_ref, w1kb_ref, w1vt_ref, w1vb_ref, w2k_ref, w2v_ref, kcmp_ref, vcmp_ref):
    def mlp(a_ref, pet_ref, peb_ref, w1t_ref, w1b_ref, w2_ref):
        a = a_ref[...].astype(F32)
        top = _dot((a + pet_ref[...]).astype(BF16), w1t_ref[...])
        bot = _dot((a + peb_ref[...]).astype(BF16), w1b_ref[...])
        n = bot.shape[0]
        hid = top + pltpu.roll(bot, n - 1, axis=0)
        return _dot(_silu(hid).astype(BF16), w2_ref[...])

    kcmp_ref[...] = mlp(ak_ref, pekt_ref, pekb_ref, w1kt_ref, w1kb_ref, w2k_ref).astype(BF16)
    vcmp_ref[...] = mlp(av_ref, pevt_ref, pevb_ref, w1vt_ref, w1vb_ref, w2v_ref).astype(BF16)


def _compress(ak, av, consts):
    b, n, w = ak.shape
    full = lambda a: pl.BlockSpec(a.shape, lambda i: (0,) * a.ndim)
    blk = pl.BlockSpec((None, n, w), lambda i: (i, 0, 0))
    out = pl.BlockSpec((None, n, LANES), lambda i: (i, 0, 0))
    return pl.pallas_call(
        _compress_kernel,
        grid=(b,),
        in_specs=[blk, blk] + [full(a) for a in consts],
        out_specs=[out, out],
        out_shape=[jax.ShapeDtypeStruct((b, n, LANES), BF16)] * 2,
        compiler_params=_cparams("parallel"),
        name="compress",
    )(ak, av, *consts)


def _stack_heads(q_ref, rows, lane_sets):
    n_tiles = q_ref.shape[1] // LANES
    parts = []
    for j in range(n_tiles):
        qj = q_ref[rows, j * LANES:(j + 1) * LANES]
        parts.append(jnp.where(lane_sets, qj, jnp.zeros_like(qj)))
    return jnp.concatenate(parts, axis=0)


def _head_gate_rows(gates, g, branches, lane):
    blocks = []
    for j in range(NSA_GROUP):
        base = (g * NSA_GROUP + j) * NSA_BRANCHES
        kept = jnp.zeros_like(gates)
        for br in branches:
            kept = jnp.where(lane == base + br, gates, kept)
        hi = kept.astype(BF16)
        blocks.append(jnp.concatenate([hi, (kept - hi.astype(F32)).astype(BF16)], axis=1))
    return jnp.concatenate(blocks, axis=0)


def _gate_spread(branches):
    col = np.arange(2 * LANES) % LANES
    spread = np.zeros((2 * LANES, LANES * len(branches)), np.float32)
    for i, br in enumerate(branches):
        spread[col % NSA_BRANCHES == br, i * LANES:(i + 1) * LANES] = 1.0
    return jnp.asarray(spread, dtype=BF16)


def _cmp_attn_kernel(qp_ref, kcmp_ref, vcmp_ref, ovl_ref, gl_ref, gsp_ref, ocmp_ref, msel_ref, *, n_slc, tq):
    for sub in range(qp_ref.shape[0] // tq):
        rows = slice(sub * tq, (sub + 1) * tq)
        t0 = pl.program_id(1) * qp_ref.shape[0] + sub * tq
        _cmp_attn_tile(qp_ref, kcmp_ref, vcmp_ref, ovl_ref, gl_ref, gsp_ref, ocmp_ref, msel_ref, rows, t0, n_slc)


def _cmp_attn_tile(qp_ref, kcmp_ref, vcmp_ref, ovl_ref, gl_ref, gsp_ref, ocmp_ref, msel_ref, tok_rows, t0, n_slc):
    tq = tok_rows.stop - tok_rows.start
    n_cmp = vcmp_ref.shape[0]
    hg = qp_ref.shape[1] // LANES
    assert n_cmp == LANES
    lane = lax.broadcasted_iota(jnp.int32, (tq, LANES), 1)
    a_idx = lax.broadcasted_iota(jnp.int32, (tq, LANES), 0)
    kc = kcmp_ref[...]
    vc = vcmp_ref[...]
    seen = jnp.where(CMP_STRIDE * lane + (CMP_LEN - 1) <= t0 + a_idx, 0.0, NEG)
    seen_rows = jnp.concatenate([seen] * hg, axis=0)

    jb = lax.broadcasted_iota(jnp.int32, (n_slc, tq), 0)
    cur = (t0 + lax.broadcasted_iota(jnp.int32, (n_slc, tq), 1)) // SLC_LEN
    visible = jb <= cur
    forced = (jb == 0) | (jb == cur) | (jb == cur - 1)

    gates = jax.nn.sigmoid(gl_ref[tok_rows, :])
    o_groups = []
    sel_groups = []
    for g in range(NSA_KV_HEADS):
        in_group = ((lane % HEAD_DIM) // HALF) == g
        qs = _stack_heads(qp_ref, tok_rows, in_group)
        s = _dot_t(qs, kc) + seen_rows
        m = jnp.max(s, axis=-1, keepdims=True)
        m = jnp.where(m > 0.5 * NEG, m, 0.0)
        e = jnp.exp(s - m)
        den = jnp.sum(e, axis=-1, keepdims=True)
        p = e / jnp.where(den > 0, den, 1.0)
        gate = _dot(_head_gate_rows(gates, g, (0,), lane), gsp_ref[...])
        o_groups.append(gate * _dot(p.astype(BF16), vc))
        p_sum = p[0:tq]
        for j in range(1, hg):
            p_sum = p_sum + p[j * tq:(j + 1) * tq]
        imp = lax.dot_general(ovl_ref[...], p_sum, (((1,), (1,)), ((), ())),
                              precision=HIGHEST, preferred_element_type=F32)
        rank = jnp.where(forced, -NEG, jnp.where(visible, imp, NEG))
        cnt = jnp.zeros((n_slc, tq), F32)
        for jp in range(n_slc):
            rj = rank[jp:jp + 1, :]
            tie = jnp.where(jb > jp, 1.0, 0.0)
            cnt = cnt + jnp.where(rj > rank, 1.0, jnp.where(rj == rank, tie, 0.0))
        sel_groups.append(jnp.where(visible, jnp.where(cnt < SLC_TOP, 1.0, 0.0), 0.0))

    for j in range(hg):
        rows = slice(j * tq, (j + 1) * tq)
        ocmp_ref[tok_rows, j * LANES:(j + 1) * LANES] = jnp.where(
            lane < HEAD_DIM, o_groups[0][rows], o_groups[1][rows]).astype(BF16)
    pad = jnp.zeros((LANES - NSA_KV_HEADS * n_slc, tq), F32)
    sel_t = jnp.concatenate(sel_groups + [pad], axis=0)
    msel_ref[tok_rows, :] = sel_t.T.astype(BF16)


def _cmp_attn(qp, kcmp, vcmp, ovl_t, gl):
    b, t, nq = qp.shape
    n_cmp = kcmp.shape[1]
    n_slc = t // SLC_LEN
    gsp = _gate_spread((0,))
    tok = lambda n: pl.BlockSpec((None, CMP_STEP_TOKENS, n), lambda i, j: (i, j, 0))
    per_b = pl.BlockSpec((None, n_cmp, LANES), lambda i, j: (i, 0, 0))
    full = lambda a: pl.BlockSpec(a.shape, lambda i, j: (0,) * a.ndim)
    return pl.pallas_call(
        functools.partial(_cmp_attn_kernel, n_slc=n_slc, tq=TQ),
        grid=(b, t // CMP_STEP_TOKENS),
        in_specs=[tok(nq), per_b, per_b, full(ovl_t), tok(LANES), full(gsp)],
        out_specs=[tok(nq), tok(LANES)],
        out_shape=[jax.ShapeDtypeStruct((b, t, nq), BF16), jax.ShapeDtypeStruct((b, t, LANES), BF16)],
        compiler_params=_cparams("parallel", "parallel"),
        name="cmp_attn_select",
    )(qp, kcmp, vcmp, ovl_t, gl, gsp)


def _nsa_attn_kernel(qr_ref, ks_ref, vs_ref, kw_ref, vw_ref, msel_ref, e_ref, gsp_ref, ocmp_ref, z_ref, gl_ref,
                     og_ref, q_scr, m_scr, acc_scr, *, n_slc, ck):
    tq = qr_ref.shape[0]
    hg = qr_ref.shape[1] // LANES
    rows = hg * tq
    n_wc = WIN_LEN // tq + 1
    per_ck = ck // KEY_CHUNK
    qi = pl.program_id(1)
    t0 = qi * tq
    lane = lax.broadcasted_iota(jnp.int32, (tq, LANES), 1)
    a_idx = lax.broadcasted_iota(jnp.int32, (tq, LANES), 0)

    eye = jnp.where(lane == a_idx, 1.0, 0.0).astype(BF16)
    earlier = jnp.where((lane % n_slc) < t0 // SLC_LEN, 0.0, NEG)
    mneg = jnp.where(msel_ref[...].astype(F32) > 0.5, earlier, NEG).astype(BF16)
    for g in range(NSA_KV_HEADS):
        in_group = ((lane % HEAD_DIM) // HALF) == g
        for j in range(hg):
            qj = qr_ref[:, j * LANES:(j + 1) * LANES]
            qj = jnp.where(in_group, qj, jnp.zeros_like(qj))
            q_scr[g, j * tq:(j + 1) * tq, :] = qj
    mneg_rows = jnp.concatenate([mneg] * hg, axis=0)
    eye_rows = jnp.concatenate([eye] * hg, axis=0)

    def with_selector(g, sel_rows):
        return jnp.concatenate([q_scr[g], sel_rows], axis=1)

    zero_b = jnp.zeros((tq, LANES), BF16)
    neg_b = jnp.full((tq, LANES), NEG, BF16)
    causal_b = jnp.where(lane <= a_idx, 0.0, NEG).astype(BF16)
    far_b = jnp.where(lane > a_idx, 0.0, NEG).astype(BF16)
    own_half = [lane < HEAD_DIM, lane >= HEAD_DIM]

    def values_with_ones(v, g):
        in_own_half = (lax.broadcasted_iota(jnp.int32, v.shape, 1) // HEAD_DIM) == g
        return jnp.where(in_own_half, v, jnp.ones_like(v))

    def lane_tiles(v):
        return [v[:, i * LANES:(i + 1) * LANES] for i in range(v.shape[1] // LANES)]

    def tile_max(tiles):
        mx = tiles[0]
        for v in tiles[1:]:
            mx = jnp.maximum(mx, v)
        return jnp.max(mx, axis=-1, keepdims=True)

    def probs(tiles, m):
        return jnp.concatenate([jnp.exp(v - m) for v in tiles], axis=1).astype(BF16)

    m_scr[...] = jnp.full(m_scr.shape, NEG, F32)
    acc_scr[...] = jnp.zeros(acc_scr.shape, F32)

    def attend_chunks(c0, n):
        k0 = pl.multiple_of(c0 * KEY_CHUNK, KEY_CHUNK)
        kt = jnp.concatenate([ks_ref[c0 + i] for i in range(n)], axis=1)
        scores = []
        for g in range(NSA_KV_HEADS):
            eb = jnp.concatenate([e_ref[g, c0 + i] for i in range(n)], axis=1)
            scores.append(_dot(with_selector(g, mneg_rows), jnp.concatenate([kt, eb], axis=0)))
        for g in range(NSA_KV_HEADS):
            tiles = lane_tiles(scores[g])
            m_old = m_scr[g]
            m_new = jnp.maximum(m_old, tile_max(tiles))
            pv = _dot(probs(tiles, m_new), values_with_ones(vs_ref[pl.ds(k0, n * KEY_CHUNK), :], g))
            acc_scr[g] = jnp.exp(m_old - m_new) * acc_scr[g] + pv
            m_scr[g] = m_new

    def earlier_keys(kc, carry):
        attend_chunks(kc * per_ck, per_ck)
        return carry

    lax.fori_loop(0, qi // per_ck, earlier_keys, 0)
    for rem in range(1, per_ck):
        @pl.when(qi % per_ck == rem)
        def _():
            attend_chunks((qi // per_ck) * per_ck, rem)

    kd = pl.multiple_of(t0, tq)
    zeros_v = jnp.zeros((tq, LANES), BF16)
    own_k = jnp.concatenate([jnp.concatenate([kw_ref[qi], causal_b], axis=0),
                             jnp.concatenate([ks_ref[qi], causal_b], axis=0)], axis=1)
    win_chunks = [qi - (n_wc - 1 - c) for c in range(n_wc - 1)]
    kbs = [jnp.concatenate([kw_ref[jnp.maximum(cidx, 0)], jnp.where(cidx >= 0, far_b if c == 0 else zero_b, neg_b)],
                           axis=0) for c, cidx in enumerate(win_chunks)]
    pairs = [(c, c + 1) for c in range(0, n_wc - 1, 2)]
    pair_k = [jnp.concatenate([kbs[c] for c in pr], axis=1) for pr in pairs]
    s_own, scores = [], []
    for g in range(NSA_KV_HEADS):
        q_eye = with_selector(g, eye_rows)
        s_own.append(_dot(q_eye, own_k))
        scores.append([_dot(q_eye, kb) for kb in pair_k])

    acc_slc, acc_win = [], []
    for g in range(NSA_KV_HEADS):
        s_win_own, s_slc_own = s_own[g][:, 0:tq], s_own[g][:, tq:2 * tq]
        m_win = tile_max([v for s in scores[g] for v in lane_tiles(s)] + [s_win_own])
        m_old = m_scr[g]
        m_slc = jnp.maximum(m_old, jnp.max(s_slc_own, axis=-1, keepdims=True))
        vbs = [values_with_ones(vw_ref[pl.ds(pl.multiple_of(jnp.maximum(cidx, 0) * tq, tq), tq), :], g)
               for cidx in win_chunks] + [values_with_ones(vw_ref[pl.ds(kd, tq), :], g)]
        win_v = [jnp.concatenate([v, zeros_v], axis=1) for v in vbs]
        slc_v = jnp.concatenate([zeros_v, values_with_ones(vs_ref[pl.ds(kd, tq), :], g)], axis=1)
        p_all = jnp.concatenate([probs(lane_tiles(s), m_win) for s in scores[g]]
                                + [probs([s_win_own], m_win), probs([s_slc_own], m_slc)], axis=1)
        pv = _dot(p_all, jnp.concatenate(win_v + [slc_v], axis=0))
        acc_win.append(pv[:, 0:LANES])
        acc_slc.append(jnp.exp(m_old - m_slc) * acc_scr[g] + pv[:, LANES:2 * LANES])

    gates = jax.nn.sigmoid(gl_ref[...])
    row_gates = [_dot(_head_gate_rows(gates, g, (1, 2), lane), gsp_ref[...]) for g in range(NSA_KV_HEADS)]

    def gated(accs, rs, branch_lanes):
        num = [accs[g][rs] * row_gates[g][rs, branch_lanes] for g in range(NSA_KV_HEADS)]
        sums = pltpu.roll(jnp.where(own_half[0], accs[1][rs], accs[0][rs]), HEAD_DIM, axis=1)
        return jnp.where(own_half[0], num[0], num[1]) / sums

    for j in range(hg):
        rs = slice(j * tq, (j + 1) * tq)
        cols = slice(j * LANES, (j + 1) * LANES)
        o = (ocmp_ref[:, cols].astype(F32) + gated(acc_slc, rs, slice(0, LANES))
             + gated(acc_win, rs, slice(LANES, 2 * LANES)))
        og_ref[:, cols] = (o * z_ref[:, cols].astype(F32)).astype(BF16)


def _nsa_attn(qr, ks, vs, kw, vw, msel, e, ocmp, z, gl):
    b, t, nq = qr.shape
    gsp = _gate_spread((1, 2))
    hg = nq // LANES
    rows = hg * TQ
    assert TQ == KEY_CHUNK and WIN_LEN % TQ == 0
    tok = lambda n: pl.BlockSpec((None, TQ, n), lambda i, j: (i, j, 0))
    per_b = pl.BlockSpec((None, t, LANES), lambda i, j: (i, 0, 0))
    per_b_t = pl.BlockSpec((None, t // KEY_CHUNK, LANES, KEY_CHUNK), lambda i, j: (i, 0, 0, 0))
    full = lambda a: pl.BlockSpec(a.shape, lambda i, j: (0,) * a.ndim)
    return pl.pallas_call(
        functools.partial(_nsa_attn_kernel, n_slc=t // SLC_LEN, ck=SLC_CK),
        grid=(b, t // TQ),
        in_specs=[tok(nq), per_b_t, per_b, per_b_t, per_b, tok(LANES), full(e), full(gsp), tok(nq), tok(nq), tok(LANES)],
        out_specs=tok(nq),
        out_shape=jax.ShapeDtypeStruct((b, t, nq), BF16),
        scratch_shapes=[
            pltpu.VMEM((NSA_KV_HEADS, rows, LANES), BF16),
            pltpu.VMEM((NSA_KV_HEADS, rows, LANES), F32),
            pltpu.VMEM((NSA_KV_HEADS, rows, LANES), F32),
        ],
        compiler_params=_cparams("parallel", "arbitrary"),
        name="nsa_slc_win_attn",
    )(qr, ks, vs, kw, vw, msel, e, gsp, ocmp, z, gl)


def _inproj1_kernel(og_ref, wo_ref, x_ref, mod0_ref, mod_ref, g_ref, cos_ref, sin_ref, wq_ref, wk_ref, wv_ref, wz_ref,
                    x1_ref, *refs):
    n_grp = len(DIL_PATTERNS)
    outs = [refs[a * n_grp:(a + 1) * n_grp] for a in range(3)]
    z_ref, stage = refs[3 * n_grp], refs[3 * n_grp + 1]
    tm = x_ref.shape[0]
    gw = DIL_TILES_PER_GROUP * LANES
    x1 = x_ref[...] + mod0_ref[2:3, :] * _dot(og_ref[...], wo_ref[...])
    x1_ref[...] = x1
    h = _modulated_norm(x1, mod_ref, g_ref).astype(BF16)
    cos = cos_ref[...]
    sin = sin_ref[...]
    for a, w_ref in enumerate((wq_ref, wk_ref, wv_ref)):
        u = _dot(h, w_ref[...])
        tiles = [u[:, pt * LANES:(pt + 1) * LANES] for pt in range(len(DIL_HEAD_PAIRS))]
        if a < 2:
            tiles = [_rope_tile(v, cos, sin) for v in tiles]
        for gi, (_, dil) in enumerate(DIL_PATTERNS):
            for jt, (pt, _) in enumerate(DIL_GROUP_TILES[gi]):
                tile = tiles[pt]
                if dil == 1:
                    outs[a][gi][:, jt * LANES:(jt + 1) * LANES] = tile.astype(BF16)
                    continue
                slab = (a * n_grp + gi) * DIL_TILES_PER_GROUP + jt
                stage[slab] = tile
                for r in range(dil):
                    rows = stage[slab, pl.ds(r, tm // dil, stride=dil), :]
                    outs[a][gi][:, r * gw + jt * LANES:r * gw + (jt + 1) * LANES] = rows.astype(BF16)
    z_ref[...] = _silu(_dot(h, wz_ref[...])).astype(BF16)


def _outproj0_inproj1(og, wo, x, mod0, mod, g, cos, sin, wq, wk, wv, wz):
    b, t, d = x.shape
    n = wq.shape[1]
    gw = DIL_TILES_PER_GROUP * LANES
    tok = lambda k: pl.BlockSpec((None, TM, k), lambda i, j: (i, j, 0))
    full = lambda a: pl.BlockSpec(a.shape, lambda i, j: (0,) * a.ndim)
    tab = pl.BlockSpec((TM, LANES), lambda i, j: (j, 0))
    mods = pl.BlockSpec((None, 3, d), lambda i, j: (i, 0, 0))
    grp_specs = [pl.BlockSpec((None, TM // dil, dil * gw), lambda i, j: (i, j, 0)) for _, dil in DIL_PATTERNS]
    grp_shapes = [jax.ShapeDtypeStruct((b, t // dil, dil * gw), BF16) for _, dil in DIL_PATTERNS]
    n_slabs = 3 * len(DIL_PATTERNS) * DIL_TILES_PER_GROUP
    return pl.pallas_call(
        _inproj1_kernel,
        grid=(b, t // TM),
        in_specs=[tok(og.shape[2]), full(wo), tok(d), mods, mods, full(g), tab, tab,
                  full(wq), full(wk), full(wv), full(wz)],
        out_specs=[tok(d)] + grp_specs * 3 + [tok(n)],
        out_shape=[jax.ShapeDtypeStruct((b, t, d), F32)] + grp_shapes * 3 + [jax.ShapeDtypeStruct((b, t, n), BF16)],
        scratch_shapes=[pltpu.VMEM((n_slabs, TM, LANES), F32)],
        compiler_params=_cparams("parallel", "parallel"),
        name="outproj_nsa_inproj_dil",
    )(og, wo, x, mod0, mod, g, cos, sin, wq, wk, wv, wz)


def _dil_attn_kernel(*refs, plans):
    n_grp = len(plans)
    step = pl.program_id(1)
    for gi, (members, tq, n_seq, blocks_per_seq) in enumerate(plans):
        q_ref, k_ref, v_ref = refs[3 * gi:3 * gi + 3]
        o_ref, lse_ref = refs[3 * n_grp + 2 * gi:3 * n_grp + 2 * gi + 2]
        _dil_attn_group(q_ref, k_ref, v_ref, o_ref, lse_ref, step % blocks_per_seq, members, tq, n_seq)


def _dil_attn_group(q_ref, k_ref, v_ref, o_ref, lse_ref, i, members, tq, n_seq):
    n_heads = sum(len(m) for m in members)
    tq_blk = q_ref.shape[0]
    gw = q_ref.shape[1] // n_seq
    n_tiles = gw // LANES
    lane = lax.broadcasted_iota(jnp.int32, (tq, LANES), 1)
    a_idx = lax.broadcasted_iota(jnp.int32, (tq, LANES), 0)
    first = ((lane % HEAD_DIM) // HALF) == 0
    eye = jnp.where(lane == a_idx, 1.0, 0.0).astype(BF16)
    diag_b = jnp.where(lane <= a_idx, 0.0, NEG).astype(BF16)
    prev_b = jnp.where(lane >= a_idx, 0.0, NEG).astype(BF16)
    neg_b = jnp.full((tq, LANES), NEG, BF16)
    for sq in range(n_seq):
        kt_prev = [None] * n_tiles
        for sub in range(tq_blk // tq):
            cdiag = i * (tq_blk // tq) + sub
            kdiag = pl.multiple_of(cdiag * tq, tq)
            kprev = pl.multiple_of(jnp.maximum(cdiag - 1, 0) * tq, tq)
            prev_bias = prev_b if sub > 0 else jnp.where(i > 0, prev_b, neg_b)
            rows = slice(sub * tq, (sub + 1) * tq)
            lses = []
            for jt in range(n_tiles):
                cols = slice(sq * gw + jt * LANES, sq * gw + (jt + 1) * LANES)
                qj = q_ref[rows, cols]
                zero = jnp.zeros_like(qj)
                qs = jnp.concatenate([jnp.concatenate([jnp.where(first, qj, zero), eye], axis=1),
                                      jnp.concatenate([jnp.where(first, zero, qj), eye], axis=1)], axis=0)
                kt_diag = k_ref[pl.ds(kdiag, tq), cols].T
                kt_before = kt_prev[jt] if sub > 0 else k_ref[pl.ds(kprev, tq), cols].T
                kt_prev[jt] = kt_diag
                kb = jnp.concatenate([jnp.concatenate([kt_before, prev_bias], axis=0),
                                      jnp.concatenate([kt_diag, diag_b], axis=0)], axis=1)
                vb = jnp.concatenate([v_ref[pl.ds(kprev, tq), cols], v_ref[pl.ds(kdiag, tq), cols]], axis=0)
                s = _dot(qs, kb)
                m = jnp.max(s, axis=-1, keepdims=True)
                p = jnp.exp(s - m)
                l = jnp.sum(p, axis=-1, keepdims=True)
                o = _dot(p.astype(BF16), vb) / l
                o_ref[rows, cols] = jnp.where(lane < HEAD_DIM, o[0:tq], o[tq:2 * tq]).astype(BF16)
                lse = m + jnp.log(l)
                lses += [lse[mb * tq:(mb + 1) * tq] for mb in members[jt]]
            top = lses[0]
            for v in lses[1:]:
                top = jnp.maximum(top, v)
            tot = jnp.zeros_like(top)
            for v in lses:
                tot = tot + jnp.exp(v - top)
            group_lse = top + jnp.log(tot) - math.log(n_heads)
            lse_ref[rows, sq * LANES:(sq + 1) * LANES] = jnp.broadcast_to(group_lse, (tq, LANES))


def _dil_attn(qs, ks, vs, t):
    b = qs[0].shape[0]
    steps = t // DIL_STEP_TOKENS
    plans, in_specs, out_specs, out_shapes, operands = [], [], [], [], []
    for gi, (_, dil) in enumerate(DIL_PATTERNS):
        length = t // dil
        gw = qs[gi].shape[2] // dil
        tq_blk = min(DIL_STEP_TOKENS, length)
        blocks_per_seq = length // tq_blk
        n_seq = DIL_STEP_TOKENS // tq_blk
        assert dil * blocks_per_seq == steps * n_seq
        tok = lambda w, bps=blocks_per_seq, ns=n_seq, tb=tq_blk: pl.BlockSpec(
            (None, tb, ns * w), lambda bi, s: (bi, s % bps, s // bps))
        seq = pl.BlockSpec((None, length, n_seq * gw), lambda bi, s, bps=blocks_per_seq: (bi, 0, s // bps))
        plans.append((tuple(m for _, m in DIL_GROUP_TILES[gi]), min(TQ, length), n_seq, blocks_per_seq))
        in_specs += [tok(gw), seq, seq]
        operands += [qs[gi], ks[gi], vs[gi]]
        out_specs += [tok(gw), tok(LANES)]
        out_shapes += [jax.ShapeDtypeStruct(qs[gi].shape, BF16), jax.ShapeDtypeStruct((b, length, dil * LANES), F32)]
    outs = pl.pallas_call(
        functools.partial(_dil_attn_kernel, plans=tuple(plans)),
        grid=(b, steps),
        in_specs=in_specs,
        out_specs=out_specs,
        out_shape=out_shapes,
        compiler_params=_cparams("parallel", "parallel"),
        name="dil_attn",
    )(*operands)
    return outs[0::2], outs[1::2]


def _outproj1_kernel(*refs):
    n_grp = len(DIL_PATTERNS)
    o_refs, l_refs = refs[0:n_grp], refs[n_grp:2 * n_grp]
    z_ref, w_ref, x_ref, mod_ref, fg_ref, out_ref, stage = refs[2 * n_grp:]
    tm = x_ref.shape[0]
    gw = DIL_TILES_PER_GROUP * LANES

    def token_order(ref, gi, width, col0, slab):
        dil = DIL_PATTERNS[gi][1]
        if dil == 1:
            return ref[:, col0:col0 + LANES].astype(F32)
        for r in range(dil):
            stage[slab, pl.ds(r, tm // dil, stride=dil), :] = ref[:, r * width + col0:r * width + col0 + LANES].astype(F32)
        return stage[slab]

    ls = [token_order(l_refs[gi], gi, LANES, 0, gi) for gi in range(n_grp)]
    top = jnp.maximum(jnp.maximum(ls[0], ls[1]), ls[2])
    es = [jnp.exp(v - top) for v in ls]
    den = es[0] + es[1] + es[2]
    alphas = [e / den * float(n_grp) for e in es]
    lane = lax.broadcasted_iota(jnp.int32, (tm, LANES), 1)
    weighted = [None] * len(DIL_HEAD_PAIRS)
    for gi in range(n_grp):
        for jt, (pt, members) in enumerate(DIL_GROUP_TILES[gi]):
            ot = token_order(o_refs[gi], gi, gw, jt * LANES, n_grp + gi * DIL_TILES_PER_GROUP + jt) * alphas[gi]
            if len(members) == 2:
                weighted[pt] = ot
            else:
                mine = (lane // HEAD_DIM) == members[0]
                weighted[pt] = jnp.where(mine, ot, 0.0 if weighted[pt] is None else weighted[pt])
    parts = [(weighted[pt] * z_ref[:, pt * LANES:(pt + 1) * LANES].astype(F32)).astype(BF16)
             for pt in range(len(DIL_HEAD_PAIRS))]
    og = jnp.concatenate(parts, axis=1)
    y = _dot(og, w_ref[...])
    x2 = x_ref[...] + mod_ref[2:3, :] * y
    ms = jnp.mean(x2 * x2, axis=-1, keepdims=True)
    out_ref[...] = x2 * lax.rsqrt(ms + NORM_EPS) * fg_ref[...]


def _outproj1(o_groups, lse_groups, z, w, x, mod, final_g):
    b, t, d = x.shape
    tok = lambda k: pl.BlockSpec((None, TM, k), lambda i, j: (i, j, 0))
    full = lambda a: pl.BlockSpec(a.shape, lambda i, j: (0,) * a.ndim)
    grp = lambda a, dil: pl.BlockSpec((None, TM // dil, a.shape[2]), lambda i, j: (i, j, 0))
    dils = [dil for _, dil in DIL_PATTERNS]
    n_slabs = len(dils) * (1 + DIL_TILES_PER_GROUP)
    return pl.pallas_call(
        _outproj1_kernel,
        grid=(b, t // TM),
        in_specs=[grp(a, dl) for a, dl in zip(o_groups, dils)] + [grp(a, dl) for a, dl in zip(lse_groups, dils)]
                 + [tok(z.shape[2]), full(w), tok(d), pl.BlockSpec((None, 3, d), lambda i, j: (i, 0, 0)), full(final_g)],
        out_specs=tok(d),
        out_shape=jax.ShapeDtypeStruct((b, t, d), F32),
        scratch_shapes=[pltpu.VMEM((n_slabs, TM, LANES), F32)],
        compiler_params=_cparams("parallel", "parallel"),
        name="outproj_dil_final",
    )(*o_groups, *lse_groups, z, w, x, mod, final_g)


def _rope_tables(t):
    inv = ROPE_THETA ** (-jnp.arange(HALF, dtype=F32) / HALF)
    ang = jnp.arange(t, dtype=F32)[:, None] * inv[None, :]
    cos = jnp.tile(jnp.cos(ang), (1, 4))
    sin = jnp.tile(jnp.sin(ang), (1, 4))
    sign = jnp.where(jnp.arange(LANES) < 2 * HALF, -1.0, 1.0).astype(F32)
    return cos, sin * sign[None, :]


def _pair_rope_layout(w, n_pairs):
    d = w.shape[0]
    return w.reshape(d, n_pairs, 2, 2, HALF).transpose(0, 1, 3, 2, 4).reshape(d, n_pairs * LANES)


def _nsa_weights(w_in, pe_k, pe_v, ck_w1, ck_w2, cv_w1, cv_w2, w_out):
    d = w_in.shape[0]
    aw = N_HEADS * HEAD_DIM
    kvw = NSA_KV_HEADS * HEAD_DIM
    cuts = np.cumsum([aw] + [kvw] * 6 + [aw]).tolist()
    q, k_c, v_c, k_s, v_s, k_w, v_w, z, gl = jnp.split(w_in, cuts, axis=1)
    scale = HEAD_DIM ** -0.5
    wq = (q * scale).reshape(d, NSA_KV_HEADS, NSA_GROUP, 2, HALF).transpose(0, 2, 3, 1, 4).reshape(d, aw)
    k_lay = lambda w: _pair_rope_layout(w, 1)
    wkv = jnp.concatenate([k_lay(k_c), v_c, k_lay(k_s), v_s, k_lay(k_w), v_w], axis=1)
    wz = z.reshape(d, NSA_KV_HEADS, NSA_GROUP, HEAD_DIM).transpose(0, 2, 1, 3).reshape(d, aw)
    wo = w_out.reshape(NSA_KV_HEADS, NSA_GROUP, HEAD_DIM, -1).transpose(1, 0, 2, 3).reshape(aw, -1)
    wg = jnp.pad(gl, ((0, 0), (0, LANES - gl.shape[1])))

    def grouped(w1_half, rope_lanes):
        out = []
        for g in range(NSA_KV_HEADS):
            if rope_lanes:
                src = w1_half.reshape(CMP_STRIDE, 2, 1, HALF, CMP_HIDDEN)
                pads = ((0, 0), (0, 0), (g, NSA_KV_HEADS - 1 - g), (0, 0), (0, 0))
            else:
                src = w1_half.reshape(CMP_STRIDE, 1, HEAD_DIM, CMP_HIDDEN)
                pads = ((0, 0), (g, NSA_KV_HEADS - 1 - g), (0, 0), (0, 0))
            out.append(jnp.pad(src, pads).reshape(CMP_STRIDE * LANES, CMP_HIDDEN))
        return jnp.concatenate(out, axis=1)

    def pe_rows(pe_half, rope_lanes):
        if rope_lanes:
            v = jnp.broadcast_to(pe_half.reshape(CMP_STRIDE, 2, 1, HALF), (CMP_STRIDE, 2, NSA_KV_HEADS, HALF))
        else:
            v = jnp.broadcast_to(pe_half.reshape(CMP_STRIDE, 1, HEAD_DIM), (CMP_STRIDE, NSA_KV_HEADS, HEAD_DIM))
        return v.reshape(1, CMP_STRIDE * LANES)

    def w2_padded(w2, rope_lanes):
        out = []
        for g in range(NSA_KV_HEADS):
            if rope_lanes:
                src = w2.reshape(CMP_HIDDEN, 2, 1, HALF)
                pads = ((0, 0), (0, 0), (g, NSA_KV_HEADS - 1 - g), (0, 0))
            else:
                src = w2.reshape(CMP_HIDDEN, 1, HEAD_DIM)
                pads = ((0, 0), (g, NSA_KV_HEADS - 1 - g), (0, 0))
            out.append(jnp.pad(src, pads).reshape(CMP_HIDDEN, LANES))
        return jnp.concatenate(out, axis=0)

    k1 = ck_w1.reshape(2, CMP_STRIDE, HEAD_DIM, CMP_HIDDEN)
    v1 = cv_w1.reshape(2, CMP_STRIDE, HEAD_DIM, CMP_HIDDEN)
    compress_consts = (
        pe_rows(pe_k[:CMP_STRIDE], True), pe_rows(pe_k[CMP_STRIDE:], True),
        pe_rows(pe_v[:CMP_STRIDE], False), pe_rows(pe_v[CMP_STRIDE:], False),
        grouped(k1[0], True).astype(BF16), grouped(k1[1], True).astype(BF16),
        grouped(v1[0], False).astype(BF16), grouped(v1[1], False).astype(BF16),
        w2_padded(ck_w2, True).astype(BF16), w2_padded(cv_w2, False).astype(BF16),
    )
    return wq.astype(BF16), wkv.astype(BF16), wz.astype(BF16), wg.astype(BF16), wo.astype(BF16), compress_consts


def _dil_weights(w_in, w_out):
    d = w_in.shape[0]
    q, k, v, z = jnp.split(w_in, 4, axis=1)
    n_pairs = len(DIL_HEAD_PAIRS)
    order = [hd for pair in DIL_HEAD_PAIRS for hd in pair]

    def tile_order(w, axis):
        parts = []
        for hd in order:
            sl = [slice(None)] * w.ndim
            sl[axis] = slice(hd * HEAD_DIM, (hd + 1) * HEAD_DIM)
            parts.append(w[tuple(sl)])
        return jnp.concatenate(parts, axis=axis)

    scale = HEAD_DIM ** -0.5
    wq = _pair_rope_layout(tile_order(q * scale, 1), n_pairs)
    wk = _pair_rope_layout(tile_order(k, 1), n_pairs)
    return (wq.astype(BF16), wk.astype(BF16), tile_order(v, 1).astype(BF16), tile_order(z, 1).astype(BF16),
            tile_order(w_out, 0).astype(BF16))


def _selection_constants(t):
    n_cmp_rows = t // CMP_STRIDE
    n_slc = t // SLC_LEN
    c_start = CMP_STRIDE * np.arange(n_cmp_rows)
    s_start = SLC_LEN * np.arange(n_slc)
    ovl_t = ((c_start[None, :] < s_start[:, None] + SLC_LEN)
             & (c_start[None, :] + CMP_LEN > s_start[:, None])).astype(np.float32)
    ovl_t[:, (t - CMP_LEN) // CMP_STRIDE + 1:] = 0.0
    e = np.zeros((NSA_KV_HEADS, t // KEY_CHUNK, LANES, KEY_CHUNK), np.float32)
    keys = np.arange(t)
    for g in range(NSA_KV_HEADS):
        e[g, keys // KEY_CHUNK, g * n_slc + keys // SLC_LEN, keys % KEY_CHUNK] = 1.0
    return jnp.asarray(ovl_t), jnp.asarray(e, dtype=BF16)


@jax.jit
def kernel(x, c, norm_g, ada_w, ada_b, nsa_w_in, nsa_pe_k, nsa_pe_v, nsa_ck_w1, nsa_ck_w2,
           nsa_cv_w1, nsa_cv_w2, nsa_w_out, dil_w_in, dil_w_out, final_g):
    b, t, d = x.shape
    assert t % SLC_CK == 0 and t % TM == 0 and d % LANES == 0
    assert NSA_KV_HEADS * (t // SLC_LEN) <= LANES and t // CMP_STRIDE == LANES
    mod = _adaln_mod(c, ada_w, ada_b).reshape(ada_w.shape[0], b, 3, d)
    cos, sin = _rope_tables(t)

    wq, wkv, wz, wg, wo, compress_consts = _nsa_weights(
        nsa_w_in[0], nsa_pe_k[0], nsa_pe_v[0], nsa_ck_w1[0], nsa_ck_w2[0], nsa_cv_w1[0], nsa_cv_w2[0], nsa_w_out[0])
    qp, qr, kc, vc, ks, vs, kw, vw, z, gl = _inproj0(x, mod[0], norm_g[0:1], cos, sin, wq, wkv, wz, wg)
    kcmp, vcmp = _compress(kc, vc, compress_consts)
    ovl_t, e_sel = _selection_constants(t)
    ocmp, msel = _cmp_attn(qp, kcmp, vcmp, ovl_t, gl)
    og = _nsa_attn(qr, ks, vs, kw, vw, msel, e_sel, ocmp, z, gl)

    dq, dk, dv, dz, dwo = _dil_weights(dil_w_in[0], dil_w_out[0])
    n_grp = len(DIL_PATTERNS)
    x1, *qkv, z1 = _outproj0_inproj1(og, wo, x, mod[0], mod[1], norm_g[1:2], cos, sin, dq, dk, dv, dz)
    assert all(win // dil == DIL_WIN and TM % (dil * 16) == 0 for win, dil in DIL_PATTERNS)
    o_groups, lse_groups = _dil_attn(qkv[0:n_grp], qkv[n_grp:2 * n_grp], qkv[2 * n_grp:3 * n_grp], t)
    return _outproj1(o_groups, lse_groups, z1, dwo, x1, mod[1], final_g.reshape(1, d))
```

```python
import functools
import math

import numpy as np
import jax
import jax.numpy as jnp
from jax import lax
from jax.experimental import pallas as pl
from jax.experimental.pallas import tpu as pltpu

F32 = jnp.float32
BF16 = jnp.bfloat16
HIGHEST = lax.Precision.HIGHEST

HEAD_DIM = 64
HALF = HEAD_DIM // 2
N_HEADS = 16
ROPE_THETA = 10000.0
NORM_EPS = 1e-6
NSA_KV_HEADS = 2
NSA_GROUP = N_HEADS // NSA_KV_HEADS
NSA_BRANCHES = 3
CMP_LEN = 32
CMP_STRIDE = 16
CMP_HIDDEN = 256
SLC_LEN = 64
SLC_TOP = 16
WIN_LEN = 512
DIL_PATTERNS = ((128, 1), (512, 4), (2048, 16))
DIL_GROUP_HEADS = (6, 5, 5)
DIL_WIN = 128


def _dil_tile_plan():
    offs = np.cumsum((0,) + DIL_GROUP_HEADS)
    pairs, group_tiles, leftovers = [], [], []
    for gi, hn in enumerate(DIL_GROUP_HEADS):
        heads = list(range(offs[gi], offs[gi + 1]))
        group_tiles.append([])
        for i in range(0, hn - hn % 2, 2):
            group_tiles[gi].append((len(pairs), (0, 1)))
            pairs.append((heads[i], heads[i + 1]))
        if hn % 2:
            leftovers.append((gi, heads[-1]))
    assert len(leftovers) % 2 == 0
    for (ga, ha), (gb, hb) in zip(leftovers[0::2], leftovers[1::2]):
        group_tiles[ga].append((len(pairs), (0,)))
        group_tiles[gb].append((len(pairs), (1,)))
        pairs.append((ha, hb))
    return tuple(pairs), tuple(tuple(tiles) for tiles in group_tiles)


DIL_HEAD_PAIRS, DIL_GROUP_TILES = _dil_tile_plan()
DIL_TILES_PER_GROUP = len(DIL_GROUP_TILES[0])
assert all(len(tiles) == DIL_TILES_PER_GROUP for tiles in DIL_GROUP_TILES)

LANES = 128
VMEM_LIMIT_BYTES = 48 * 1024 * 1024

NEG = -1e30
TQ = 128
KEY_CHUNK = 128
SLC_CK = 512
TM = 512
DIL_STEP_TOKENS = 512
CMP_STEP_TOKENS = 512


def _cparams(*sem):
    return pltpu.CompilerParams(dimension_semantics=sem, vmem_limit_bytes=VMEM_LIMIT_BYTES)


def _dot(a, b):
    return jnp.dot(a, b, preferred_element_type=F32)


def _dot_t(a, b):
    return lax.dot_general(a, b, (((1,), (1,)), ((), ())), preferred_element_type=F32)


def _silu(v):
    return v * jax.nn.sigmoid(v)


def _mod_kernel(c_ref, w_ref, b_ref, o_ref):
    s = _silu(c_ref[...])
    o_ref[...] = jnp.dot(s, w_ref[...], precision=HIGHEST, preferred_element_type=F32) + b_ref[...]


def _adaln_mod(c, ada_w, ada_b):
    depth, d, n3 = ada_w.shape
    b = c.shape[0]
    tn = 1024
    return pl.pallas_call(
        _mod_kernel,
        grid=(depth, n3 // tn),
        in_specs=[
            pl.BlockSpec((b, d), lambda i, j: (0, 0)),
            pl.BlockSpec((None, d, tn), lambda i, j: (i, 0, j)),
            pl.BlockSpec((None, 1, tn), lambda i, j: (i, 0, j)),
        ],
        out_specs=pl.BlockSpec((None, b, tn), lambda i, j: (i, 0, j)),
        out_shape=jax.ShapeDtypeStruct((depth, b, n3), F32),
        compiler_params=_cparams("arbitrary", "arbitrary"),
        name="adaln_mod",
    )(c, ada_w, ada_b.reshape(depth, 1, n3))


def _modulated_norm(x, mod_ref, g_ref):
    ms = jnp.mean(x * x, axis=-1, keepdims=True)
    y = x * lax.rsqrt(ms + NORM_EPS) * g_ref[...]
    return y * (1.0 + mod_ref[1:2, :]) + mod_ref[0:1, :]


def _rope_tile(v, cos, sin):
    return v * cos + pltpu.roll(v, 2 * HALF, axis=1) * sin


def _store_key_chunks_t(kt_ref, k):
    dim = lax.broadcasted_iota(jnp.int32, (LANES, KEY_CHUNK), 0)
    for c in range(kt_ref.shape[1]):
        kt = k[c * KEY_CHUNK:(c + 1) * KEY_CHUNK, :].T
        for g in range(kt_ref.shape[0]):
            kt_ref[g, c] = jnp.where((dim % HEAD_DIM) // HALF == g, kt, 0.0).astype(BF16)


def _inproj0_kernel(x_ref, mod_ref, g_ref, cos_ref, sin_ref, wq_ref, wkv_ref, wz_ref, wg_ref,
                    qp_ref, qr_ref, kc_ref, vc_ref, ks_ref, vs_ref, kw_ref, vw_ref, z_ref, gl_ref, stage):
    tm = x_ref.shape[0]
    h = _modulated_norm(x_ref[...], mod_ref, g_ref).astype(BF16)
    cos = cos_ref[...]
    sin = sin_ref[...]
    q = _dot(h, wq_ref[...])
    for j in range(q.shape[1] // LANES):
        sl = slice(j * LANES, (j + 1) * LANES)
        qp_ref[:, sl] = q[:, sl].astype(BF16)
        qr_ref[:, sl] = _rope_tile(q[:, sl], cos, sin).astype(BF16)
    kv = _dot(h, wkv_ref[...])
    tiles = [kv[:, i * LANES:(i + 1) * LANES] for i in range(6)]
    for slab, (tile, o_ref) in enumerate(((tiles[0], kc_ref), (tiles[1], vc_ref))):
        stage[slab] = tile
        for r in range(CMP_STRIDE):
            rows = stage[slab, pl.ds(r, tm // CMP_STRIDE, stride=CMP_STRIDE), :]
            o_ref[:, r * LANES:(r + 1) * LANES] = rows.astype(BF16)
    _store_key_chunks_t(ks_ref, _rope_tile(tiles[2], cos, sin))
    vs_ref[...] = tiles[3].astype(BF16)
    _store_key_chunks_t(kw_ref, _rope_tile(tiles[4], cos, sin))
    vw_ref[...] = tiles[5].astype(BF16)
    z_ref[...] = _silu(_dot(h, wz_ref[...])).astype(BF16)
    gl_ref[...] = _dot(h, wg_ref[...])


def _inproj0(x, mod, g, cos, sin, wq, wkv, wz, wg):
    b, t, d = x.shape
    nq, nz = wq.shape[1], wz.shape[1]
    tok = lambda n: pl.BlockSpec((None, TM, n), lambda i, j: (i, j, 0))
    full = lambda a: pl.BlockSpec(a.shape, lambda i, j: (0,) * a.ndim)
    tab = pl.BlockSpec((TM, LANES), lambda i, j: (j, 0))
    shp = lambda n, dt: jax.ShapeDtypeStruct((b, t, n), dt)
    kt = pl.BlockSpec((None, NSA_KV_HEADS, TM // KEY_CHUNK, LANES, KEY_CHUNK), lambda i, j: (i, 0, j, 0, 0))
    kt_shp = jax.ShapeDtypeStruct((b, NSA_KV_HEADS, t // KEY_CHUNK, LANES, KEY_CHUNK), BF16)
    kv = tok(LANES)
    kv_shp = shp(LANES, BF16)
    cm = pl.BlockSpec((None, TM // CMP_STRIDE, CMP_STRIDE * LANES), lambda i, j: (i, j, 0))
    cm_shp = jax.ShapeDtypeStruct((b, t // CMP_STRIDE, CMP_STRIDE * LANES), BF16)
    return pl.pallas_call(
        _inproj0_kernel,
        grid=(b, t // TM),
        in_specs=[tok(d), pl.BlockSpec((None, 3, d), lambda i, j: (i, 0, 0)), full(g), tab, tab,
                  full(wq), full(wkv), full(wz), full(wg)],
        out_specs=[tok(nq), tok(nq), cm, cm, kt, kv, kt, kv, tok(nz), tok(LANES)],
        out_shape=[shp(nq, BF16), shp(nq, BF16), cm_shp, cm_shp, kt_shp, kv_shp, kt_shp, kv_shp,
                   shp(nz, BF16), shp(LANES, F32)],
        scratch_shapes=[pltpu.VMEM((2, TM, LANES), F32)],
        compiler_params=_cparams("parallel", "parallel"),
        name="inproj_nsa",
    )(x, mod, g, cos, sin, wq, wkv, wz, wg)


def _compress_kernel(ak_ref, av_ref, pekt_ref, pekb_ref, pevt_ref, pevb_ref,
                     w1kt_ref, w1kb_ref, w1vt_ref, w1vb_ref, w2k_ref, w2v_ref, kcmp_ref, vcmp_ref):
    def mlp(a_ref, pet_ref, peb_ref, w1t_ref, w1b_ref, w2_ref):
        a = a_ref[...].astype(F32)
        top = _dot((a + pet_ref[...]).astype(BF16), w1t_ref[...])
        bot = _dot((a + peb_ref[...]).astype(BF16), w1b_ref[...])
        n = bot.shape[0]
        hid = top + pltpu.roll(bot, n - 1, axis=0)
        return _dot(_silu(hid).astype(BF16), w2_ref[...])

    kcmp_ref[...] = mlp(ak_ref, pekt_ref, pekb_ref, w1kt_ref, w1kb_ref, w2k_ref).astype(BF16)
    vcmp_ref[...] = mlp(av_ref, pevt_ref, pevb_ref, w1vt_ref, w1vb_ref, w2v_ref).astype(BF16)


def _compress(ak, av, consts):
    b, n, w = ak.shape
    full = lambda a: pl.BlockSpec(a.shape, lambda i: (0,) * a.ndim)
    blk = pl.BlockSpec((None, n, w), lambda i: (i, 0, 0))
    out = pl.BlockSpec((None, n, LANES), lambda i: (i, 0, 0))
    return pl.pallas_call(
        _compress_kernel,
        grid=(b,),
        in_specs=[blk, blk] + [full(a) for a in consts],
        out_specs=[out, out],
        out_shape=[jax.ShapeDtypeStruct((b, n, LANES), BF16)] * 2,
        compiler_params=_cparams("parallel"),
        name="compress",
    )(ak, av, *consts)


def _stack_heads(q_ref, rows, lane_sets):
    n_tiles = q_ref.shape[1] // LANES
    parts = []
    for j in range(n_tiles):
        qj = q_ref[rows, j * LANES:(j + 1) * LANES]
        parts.append(jnp.where(lane_sets, qj, jnp.zeros_like(qj)))
    return jnp.concatenate(parts, axis=0)


def _head_gate_rows(gates, g, branches, lane):
    blocks = []
    for j in range(NSA_GROUP):
        base = (g * NSA_GROUP + j) * NSA_BRANCHES
        kept = jnp.zeros_like(gates)
        for br in branches:
            kept = jnp.where(lane == base + br, gates, kept)
        hi = kept.astype(BF16)
        blocks.append(jnp.concatenate([hi, (kept - hi.astype(F32)).astype(BF16)], axis=1))
    return jnp.concatenate(blocks, axis=0)


def _gate_spread(branches):
    col = np.arange(2 * LANES) % LANES
    spread = np.zeros((2 * LANES, LANES * len(branches)), np.float32)
    for i, br in enumerate(branches):
        spread[col % NSA_BRANCHES == br, i * LANES:(i + 1) * LANES] = 1.0
    return jnp.asarray(spread, dtype=BF16)


def _cmp_attn_kernel(qp_ref, kcmp_ref, vcmp_ref, ovl_ref, gl_ref, ocmp_ref, msel_ref, *, n_slc, tq):
    for sub in range(qp_ref.shape[0] // tq):
        rows = slice(sub * tq, (sub + 1) * tq)
        t0 = pl.program_id(1) * qp_ref.shape[0] + sub * tq
        _cmp_attn_tile(qp_ref, kcmp_ref, vcmp_ref, ovl_ref, gl_ref, ocmp_ref, msel_ref, rows, t0, n_slc)


def _cmp_attn_tile(qp_ref, kcmp_ref, vcmp_ref, ovl_ref, gl_ref, ocmp_ref, msel_ref, tok_rows, t0, n_slc):
    tq = tok_rows.stop - tok_rows.start
    n_cmp = vcmp_ref.shape[0]
    hg = qp_ref.shape[1] // LANES
    assert n_cmp == LANES and tq == LANES
    row = lax.broadcasted_iota(jnp.int32, (LANES, tq), 0)
    col = lax.broadcasted_iota(jnp.int32, (LANES, tq), 1)
    eye = jnp.where(row == col, 1.0, 0.0).astype(BF16)
    seen = jnp.where(CMP_STRIDE * row + (CMP_LEN - 1) <= t0 + col, 0.0, NEG).astype(BF16)
    k_sel = jnp.concatenate([kcmp_ref[...], eye], axis=1)
    v_t = vcmp_ref[...].T
    q_t = [qp_ref[tok_rows, j * LANES:(j + 1) * LANES].T for j in range(hg)]
    gates_t = jax.nn.sigmoid(gl_ref[tok_rows, :]).T

    jb = lax.broadcasted_iota(jnp.int32, (n_slc, tq), 0)
    cur = (t0 + lax.broadcasted_iota(jnp.int32, (n_slc, tq), 1)) // SLC_LEN
    visible = jb <= cur
    forced = (jb == 0) | (jb == cur) | (jb == cur - 1)

    o_groups = []
    sel_groups = []
    for g in range(NSA_KV_HEADS):
        in_group = ((row % HEAD_DIM) // HALF) == g
        q_cols = jnp.concatenate([jnp.where(in_group, v, jnp.zeros_like(v)) for v in q_t], axis=1)
        s = _dot(k_sel, jnp.concatenate([q_cols, jnp.concatenate([seen] * hg, axis=1)], axis=0))
        m = jnp.max(s, axis=0, keepdims=True)
        m = jnp.where(m > 0.5 * NEG, m, 0.0)
        e = jnp.exp(s - m)
        den = jnp.sum(e, axis=0, keepdims=True)
        p = e * (1.0 / jnp.where(den > 0, den, 1.0))
        gate = jnp.concatenate([gates_t[(g * NSA_GROUP + j) * NSA_BRANCHES:(g * NSA_GROUP + j) * NSA_BRANCHES + 1, :]
                                for j in range(hg)], axis=1)
        o_groups.append(gate * _dot(v_t, p.astype(BF16)))
        p_sum = p[:, 0:tq]
        for j in range(1, hg):
            p_sum = p_sum + p[:, j * tq:(j + 1) * tq]
        imp = jnp.dot(ovl_ref[...], p_sum, precision=HIGHEST, preferred_element_type=F32)
        rank = jnp.where(forced, -NEG, jnp.where(visible, imp, NEG))
        cnt = jnp.zeros((n_slc, tq), F32)
        for jp in range(n_slc):
            rj = rank[jp:jp + 1, :]
            tie = jnp.where(jb > jp, 1.0, 0.0)
            cnt = cnt + jnp.where(rj > rank, 1.0, jnp.where(rj == rank, tie, 0.0))
        sel_groups.append(jnp.where(visible, jnp.where(cnt < SLC_TOP, 1.0, 0.0), 0.0))

    for j in range(hg):
        cols = slice(j * tq, (j + 1) * tq)
        o_t = jnp.where(row < HEAD_DIM, o_groups[0][:, cols], o_groups[1][:, cols])
        ocmp_ref[tok_rows, j * LANES:(j + 1) * LANES] = o_t.T.astype(BF16)
    pad = jnp.zeros((LANES - NSA_KV_HEADS * n_slc, tq), F32)
    sel_t = jnp.concatenate(sel_groups + [pad], axis=0)
    msel_ref[tok_rows, :] = sel_t.T.astype(BF16)


def _cmp_attn(qp, kcmp, vcmp, ovl_t, gl):
    b, t, nq = qp.shape
    n_cmp = kcmp.shape[1]
    n_slc = t // SLC_LEN
    tok = lambda n: pl.BlockSpec((None, CMP_STEP_TOKENS, n), lambda i, j: (i, j, 0))
    per_b = pl.BlockSpec((None, n_cmp, LANES), lambda i, j: (i, 0, 0))
    full = lambda a: pl.BlockSpec(a.shape, lambda i, j: (0,) * a.ndim)
    return pl.pallas_call(
        functools.partial(_cmp_attn_kernel, n_slc=n_slc, tq=TQ),
        grid=(b, t // CMP_STEP_TOKENS),
        in_specs=[tok(nq), per_b, per_b, full(ovl_t), tok(LANES)],
        out_specs=[tok(nq), tok(LANES)],
        out_shape=[jax.ShapeDtypeStruct((b, t, nq), BF16), jax.ShapeDtypeStruct((b, t, LANES), BF16)],
        compiler_params=_cparams("parallel", "parallel"),
        name="cmp_attn_select",
    )(qp, kcmp, vcmp, ovl_t, gl)


def _nsa_attn_kernel(qr_ref, ks_ref, vs_ref, kw_ref, vw_ref, msel_ref, e_ref, gsp_ref, ocmp_ref, z_ref, gl_ref,
                     og_ref, m_scr, acc_scr, *, n_slc, ck):
    tq = qr_ref.shape[0]
    hg = qr_ref.shape[1] // LANES
    rows = hg * tq
    n_wc = WIN_LEN // tq + 1
    per_ck = ck // KEY_CHUNK
    qi = pl.program_id(1)
    t0 = qi * tq
    lane = lax.broadcasted_iota(jnp.int32, (tq, LANES), 1)
    a_idx = lax.broadcasted_iota(jnp.int32, (tq, LANES), 0)

    eye = jnp.where(lane == a_idx, 1.0, 0.0).astype(BF16)
    earlier = jnp.where((lane % n_slc) < t0 // SLC_LEN, 0.0, NEG)
    mneg = jnp.where(msel_ref[...].astype(F32) > 0.5, earlier, NEG).astype(BF16)
    mneg_rows = jnp.concatenate([mneg] * hg, axis=0)
    eye_rows = jnp.concatenate([eye] * hg, axis=0)

    def with_selector(sel_rows):
        q_rows = jnp.concatenate([qr_ref[:, j * LANES:(j + 1) * LANES] for j in range(hg)], axis=0)
        return jnp.concatenate([q_rows, sel_rows], axis=1)

    zero_b = jnp.zeros((tq, LANES), BF16)
    neg_b = jnp.full((tq, LANES), NEG, BF16)
    causal_b = jnp.where(lane <= a_idx, 0.0, NEG).astype(BF16)
    far_b = jnp.where(lane > a_idx, 0.0, NEG).astype(BF16)
    own_half = [lane < HEAD_DIM, lane >= HEAD_DIM]

    def values_with_ones(v, g):
        in_own_half = (lax.broadcasted_iota(jnp.int32, v.shape, 1) // HEAD_DIM) == g
        return jnp.where(in_own_half, v, jnp.ones_like(v))

    def lane_tiles(v):
        return [v[:, i * LANES:(i + 1) * LANES] for i in range(v.shape[1] // LANES)]

    def tile_max(tiles):
        mx = tiles[0]
        for v in tiles[1:]:
            mx = jnp.maximum(mx, v)
        return jnp.max(mx, axis=-1, keepdims=True)

    def probs(tiles, m):
        return jnp.concatenate([jnp.exp(v - m) for v in tiles], axis=1).astype(BF16)

    m_scr[...] = jnp.full(m_scr.shape, NEG, F32)
    acc_scr[...] = jnp.zeros(acc_scr.shape, F32)

    def attend_chunks(c0, n):
        k0 = pl.multiple_of(c0 * KEY_CHUNK, KEY_CHUNK)
        q_sel = with_selector(mneg_rows)
        scores = []
        for g in range(NSA_KV_HEADS):
            kt = jnp.concatenate([ks_ref[g, c0 + i] for i in range(n)], axis=1)
            eb = jnp.concatenate([e_ref[g, c0 + i] for i in range(n)], axis=1)
            scores.append(_dot(q_sel, jnp.concatenate([kt, eb], axis=0)))
        for g in range(NSA_KV_HEADS):
            tiles = lane_tiles(scores[g])
            m_old = m_scr[g]
            m_new = jnp.maximum(m_old, tile_max(tiles))
            pv = _dot(probs(tiles, m_new), values_with_ones(vs_ref[pl.ds(k0, n * KEY_CHUNK), :], g))
            acc_scr[g] = jnp.exp(m_old - m_new) * acc_scr[g] + pv
            m_scr[g] = m_new

    def earlier_keys(kc, carry):
        attend_chunks(kc * per_ck, per_ck)
        return carry

    lax.fori_loop(0, qi // per_ck, earlier_keys, 0)
    for rem in range(1, per_ck):
        @pl.when(qi % per_ck == rem)
        def _():
            attend_chunks((qi // per_ck) * per_ck, rem)

    kd = pl.multiple_of(t0, tq)
    zeros_v = jnp.zeros((tq, LANES), BF16)
    win_chunks = [qi - (n_wc - 1 - c) for c in range(n_wc - 1)]
    win_bias = [jnp.where(cidx >= 0, far_b if c == 0 else zero_b, neg_b) for c, cidx in enumerate(win_chunks)]
    pairs = [(c, c + 1) for c in range(0, n_wc - 1, 2)]
    q_eye = with_selector(eye_rows)
    s_own, scores = [], []
    for g in range(NSA_KV_HEADS):
        own_k = jnp.concatenate([jnp.concatenate([kw_ref[g, qi], causal_b], axis=0),
                                 jnp.concatenate([ks_ref[g, qi], causal_b], axis=0)], axis=1)
        kbs = [jnp.concatenate([kw_ref[g, jnp.maximum(cidx, 0)], win_bias[c]], axis=0)
               for c, cidx in enumerate(win_chunks)]
        s_own.append(_dot(q_eye, own_k))
        scores.append([_dot(q_eye, jnp.concatenate([kbs[c] for c in pr], axis=1)) for pr in pairs])

    acc_slc, acc_win = [], []
    for g in range(NSA_KV_HEADS):
        s_win_own, s_slc_own = s_own[g][:, 0:tq], s_own[g][:, tq:2 * tq]
        m_win = tile_max([v for s in scores[g] for v in lane_tiles(s)] + [s_win_own])
        m_old = m_scr[g]
        m_slc = jnp.maximum(m_old, jnp.max(s_slc_own, axis=-1, keepdims=True))
        vbs = [values_with_ones(vw_ref[pl.ds(pl.multiple_of(jnp.maximum(cidx, 0) * tq, tq), tq), :], g)
               for cidx in win_chunks] + [values_with_ones(vw_ref[pl.ds(kd, tq), :], g)]
        win_v = [jnp.concatenate([v, zeros_v], axis=1) for v in vbs]
        slc_v = jnp.concatenate([zeros_v, values_with_ones(vs_ref[pl.ds(kd, tq), :], g)], axis=1)
        p_all = jnp.concatenate([probs(lane_tiles(s), m_win) for s in scores[g]]
                                + [probs([s_win_own], m_win), probs([s_slc_own], m_slc)], axis=1)
        pv = _dot(p_all, jnp.concatenate(win_v + [slc_v], axis=0))
        acc_win.append(pv[:, 0:LANES])
        acc_slc.append(jnp.exp(m_old - m_slc) * acc_scr[g] + pv[:, LANES:2 * LANES])

    gates = jax.nn.sigmoid(gl_ref[...])
    row_gates = [_dot(_head_gate_rows(gates, g, (1, 2), lane), gsp_ref[...]) for g in range(NSA_KV_HEADS)]

    def gated(accs, rs, branch_lanes):
        num = [accs[g][rs] * row_gates[g][rs, branch_lanes] for g in range(NSA_KV_HEADS)]
        sums = pltpu.roll(jnp.where(own_half[0], accs[1][rs], accs[0][rs]), HEAD_DIM, axis=1)
        return jnp.where(own_half[0], num[0], num[1]) / sums

    for j in range(hg):
        rs = slice(j * tq, (j + 1) * tq)
        cols = slice(j * LANES, (j + 1) * LANES)
        o = (ocmp_ref[:, cols].astype(F32) + gated(acc_slc, rs, slice(0, LANES))
             + gated(acc_win, rs, slice(LANES, 2 * LANES)))
        og_ref[:, cols] = (o * z_ref[:, cols].astype(F32)).astype(BF16)


def _nsa_attn(qr, ks, vs, kw, vw, msel, e, ocmp, z, gl):
    b, t, nq = qr.shape
    gsp = _gate_spread((1, 2))
    hg = nq // LANES
    rows = hg * TQ
    assert TQ == KEY_CHUNK and WIN_LEN % TQ == 0
    tok = lambda n: pl.BlockSpec((None, TQ, n), lambda i, j: (i, j, 0))
    per_b = pl.BlockSpec((None, t, LANES), lambda i, j: (i, 0, 0))
    per_b_t = pl.BlockSpec((None, NSA_KV_HEADS, t // KEY_CHUNK, LANES, KEY_CHUNK), lambda i, j: (i, 0, 0, 0, 0))
    full = lambda a: pl.BlockSpec(a.shape, lambda i, j: (0,) * a.ndim)
    return pl.pallas_call(
        functools.partial(_nsa_attn_kernel, n_slc=t // SLC_LEN, ck=SLC_CK),
        grid=(b, t // TQ),
        in_specs=[tok(nq), per_b_t, per_b, per_b_t, per_b, tok(LANES), full(e), full(gsp), tok(nq), tok(nq), tok(LANES)],
        out_specs=tok(nq),
        out_shape=jax.ShapeDtypeStruct((b, t, nq), BF16),
        scratch_shapes=[
            pltpu.VMEM((NSA_KV_HEADS, rows, LANES), F32),
            pltpu.VMEM((NSA_KV_HEADS, rows, LANES), F32),
        ],
        compiler_params=_cparams("parallel", "arbitrary"),
        name="nsa_slc_win_attn",
    )(qr, ks, vs, kw, vw, msel, e, gsp, ocmp, z, gl)


def _inproj1_kernel(og_ref, wo_ref, x_ref, mod0_ref, mod_ref, g_ref, cos_ref, sin_ref, wq_ref, wk_ref, wv_ref, wz_ref,
                    x1_ref, *refs):
    n_grp = len(DIL_PATTERNS)
    outs = [refs[a * n_grp:(a + 1) * n_grp] for a in range(3)]
    z_ref, stage = refs[3 * n_grp], refs[3 * n_grp + 1]
    tm = x_ref.shape[0]
    gw = DIL_TILES_PER_GROUP * LANES
    x1 = x_ref[...] + mod0_ref[2:3, :] * _dot(og_ref[...], wo_ref[...])
    x1_ref[...] = x1
    h = _modulated_norm(x1, mod_ref, g_ref).astype(BF16)
    cos = cos_ref[...]
    sin = sin_ref[...]
    for a, w_ref in enumerate((wq_ref, wk_ref, wv_ref)):
        u = _dot(h, w_ref[...])
        tiles = [u[:, pt * LANES:(pt + 1) * LANES] for pt in range(len(DIL_HEAD_PAIRS))]
        if a < 2:
            tiles = [_rope_tile(v, cos, sin) for v in tiles]
        for gi, (_, dil) in enumerate(DIL_PATTERNS):
            for jt, (pt, _) in enumerate(DIL_GROUP_TILES[gi]):
                tile = tiles[pt]
                if dil == 1:
                    outs[a][gi][:, jt * LANES:(jt + 1) * LANES] = tile.astype(BF16)
                    continue
                slab = (a * n_grp + gi) * DIL_TILES_PER_GROUP + jt
                stage[slab] = tile
                for r in range(dil):
                    rows = stage[slab, pl.ds(r, tm // dil, stride=dil), :]
                    outs[a][gi][:, r * gw + jt * LANES:r * gw + (jt + 1) * LANES] = rows.astype(BF16)
    z_ref[...] = _silu(_dot(h, wz_ref[...])).astype(BF16)


def _outproj0_inproj1(og, wo, x, mod0, mod, g, cos, sin, wq, wk, wv, wz):
    b, t, d = x.shape
    n = wq.shape[1]
    gw = DIL_TILES_PER_GROUP * LANES
    tok = lambda k: pl.BlockSpec((None, TM, k), lambda i, j: (i, j, 0))
    full = lambda a: pl.BlockSpec(a.shape, lambda i, j: (0,) * a.ndim)
    tab = pl.BlockSpec((TM, LANES), lambda i, j: (j, 0))
    mods = pl.BlockSpec((None, 3, d), lambda i, j: (i, 0, 0))
    grp_specs = [pl.BlockSpec((None, TM // dil, dil * gw), lambda i, j: (i, j, 0)) for _, dil in DIL_PATTERNS]
    grp_shapes = [jax.ShapeDtypeStruct((b, t // dil, dil * gw), BF16) for _, dil in DIL_PATTERNS]
    n_slabs = 3 * len(DIL_PATTERNS) * DIL_TILES_PER_GROUP
    return pl.pallas_call(
        _inproj1_kernel,
        grid=(b, t // TM),
        in_specs=[tok(og.shape[2]), full(wo), tok(d), mods, mods, full(g), tab, tab,
                  full(wq), full(wk), full(wv), full(wz)],
        out_specs=[tok(d)] + grp_specs * 3 + [tok(n)],
        out_shape=[jax.ShapeDtypeStruct((b, t, d), F32)] + grp_shapes * 3 + [jax.ShapeDtypeStruct((b, t, n), BF16)],
        scratch_shapes=[pltpu.VMEM((n_slabs, TM, LANES), F32)],
        compiler_params=_cparams("parallel", "parallel"),
        name="outproj_nsa_inproj_dil",
    )(og, wo, x, mod0, mod, g, cos, sin, wq, wk, wv, wz)


def _dil_attn_kernel(*refs, plans):
    n_grp = len(plans)
    step = pl.program_id(1)
    for gi, (members, tq, n_seq, blocks_per_seq) in enumerate(plans):
        q_ref, k_ref, v_ref = refs[3 * gi:3 * gi + 3]
        o_ref, lse_ref = refs[3 * n_grp + 2 * gi:3 * n_grp + 2 * gi + 2]
        _dil_attn_group(q_ref, k_ref, v_ref, o_ref, lse_ref, step % blocks_per_seq, members, tq, n_seq)


def _dil_attn_group(q_ref, k_ref, v_ref, o_ref, lse_ref, i, members, tq, n_seq):
    n_heads = sum(len(m) for m in members)
    tq_blk = q_ref.shape[0]
    gw = q_ref.shape[1] // n_seq
    n_tiles = gw // LANES
    lane = lax.broadcasted_iota(jnp.int32, (tq, LANES), 1)
    a_idx = lax.broadcasted_iota(jnp.int32, (tq, LANES), 0)
    first = ((lane % HEAD_DIM) // HALF) == 0
    eye = jnp.where(lane == a_idx, 1.0, 0.0).astype(BF16)
    diag_b = jnp.where(lane <= a_idx, 0.0, NEG).astype(BF16)
    prev_b = jnp.where(lane >= a_idx, 0.0, NEG).astype(BF16)
    neg_b = jnp.full((tq, LANES), NEG, BF16)
    for sq in range(n_seq):
        kt_prev = [None] * n_tiles
        for sub in range(tq_blk // tq):
            cdiag = i * (tq_blk // tq) + sub
            kdiag = pl.multiple_of(cdiag * tq, tq)
            kprev = pl.multiple_of(jnp.maximum(cdiag - 1, 0) * tq, tq)
            prev_bias = prev_b if sub > 0 else jnp.where(i > 0, prev_b, neg_b)
            rows = slice(sub * tq, (sub + 1) * tq)
            lses = []
            for jt in range(n_tiles):
                cols = slice(sq * gw + jt * LANES, sq * gw + (jt + 1) * LANES)
                qj = q_ref[rows, cols]
                zero = jnp.zeros_like(qj)
                qs = jnp.concatenate([jnp.concatenate([jnp.where(first, qj, zero), eye], axis=1),
                                      jnp.concatenate([jnp.where(first, zero, qj), eye], axis=1)], axis=0)
                kt_diag = k_ref[pl.ds(kdiag, tq), cols].T
                kt_before = kt_prev[jt] if sub > 0 else k_ref[pl.ds(kprev, tq), cols].T
                kt_prev[jt] = kt_diag
                kb = jnp.concatenate([jnp.concatenate([kt_before, prev_bias], axis=0),
                                      jnp.concatenate([kt_diag, diag_b], axis=0)], axis=1)
                vb = jnp.concatenate([v_ref[pl.ds(kprev, tq), cols], v_ref[pl.ds(kdiag, tq), cols]], axis=0)
                s = _dot(qs, kb)
                m = jnp.max(s, axis=-1, keepdims=True)
                p = jnp.exp(s - m)
                l = jnp.sum(p, axis=-1, keepdims=True)
                o = _dot(p.astype(BF16), vb) / l
                o_ref[rows, cols] = jnp.where(lane < HEAD_DIM, o[0:tq], o[tq:2 * tq]).astype(BF16)
                lse = m + jnp.log(l)
                lses += [lse[mb * tq:(mb + 1) * tq] for mb in members[jt]]
            top = lses[0]
            for v in lses[1:]:
                top = jnp.maximum(top, v)
            tot = jnp.zeros_like(top)
            for v in lses:
                tot = tot + jnp.exp(v - top)
            group_lse = top + jnp.log(tot) - math.log(n_heads)
            lse_ref[rows, sq * LANES:(sq + 1) * LANES] = jnp.broadcast_to(group_lse, (tq, LANES))


def _dil_attn(qs, ks, vs, t):
    b = qs[0].shape[0]
    steps = t // DIL_STEP_TOKENS
    plans, in_specs, out_specs, out_shapes, operands = [], [], [], [], []
    for gi, (_, dil) in enumerate(DIL_PATTERNS):
        length = t // dil
        gw = qs[gi].shape[2] // dil
        tq_blk = min(DIL_STEP_TOKENS, length)
        blocks_per_seq = length // tq_blk
        n_seq = DIL_STEP_TOKENS // tq_blk
        assert dil * blocks_per_seq == steps * n_seq
        tok = lambda w, bps=blocks_per_seq, ns=n_seq, tb=tq_blk: pl.BlockSpec(
            (None, tb, ns * w), lambda bi, s: (bi, s % bps, s // bps))
        seq = pl.BlockSpec((None, length, n_seq * gw), lambda bi, s, bps=blocks_per_seq: (bi, 0, s // bps))
        plans.append((tuple(m for _, m in DIL_GROUP_TILES[gi]), min(TQ, length), n_seq, blocks_per_seq))
        in_specs += [tok(gw), seq, seq]
        operands += [qs[gi], ks[gi], vs[gi]]
        out_specs += [tok(gw), tok(LANES)]
        out_shapes += [jax.ShapeDtypeStruct(qs[gi].shape, BF16), jax.ShapeDtypeStruct((b, length, dil * LANES), F32)]
    outs = pl.pallas_call(
        functools.partial(_dil_attn_kernel, plans=tuple(plans)),
        grid=(b, steps),
        in_specs=in_specs,
        out_specs=out_specs,
        out_shape=out_shapes,
        compiler_params=_cparams("parallel", "parallel"),
        name="dil_attn",
    )(*operands)
    return outs[0::2], outs[1::2]


def _outproj1_kernel(*refs):
    n_grp = len(DIL_PATTERNS)
    o_refs, l_refs = refs[0:n_grp], refs[n_grp:2 * n_grp]
    z_ref, w_ref, x_ref, mod_ref, fg_ref, out_ref, stage = refs[2 * n_grp:]
    tm = x_ref.shape[0]
    gw = DIL_TILES_PER_GROUP * LANES

    def token_order(ref, gi, width, col0, slab):
        dil = DIL_PATTERNS[gi][1]
        if dil == 1:
            return ref[:, col0:col0 + LANES].astype(F32)
        for r in range(dil):
            stage[slab, pl.ds(r, tm // dil, stride=dil), :] = ref[:, r * width + col0:r * width + col0 + LANES].astype(F32)
        return stage[slab]

    ls = [token_order(l_refs[gi], gi, LANES, 0, gi) for gi in range(n_grp)]
    top = jnp.maximum(jnp.maximum(ls[0], ls[1]), ls[2])
    es = [jnp.exp(v - top) for v in ls]
    den = es[0] + es[1] + es[2]
    alphas = [e / den * float(n_grp) for e in es]
    lane = lax.broadcasted_iota(jnp.int32, (tm, LANES), 1)
    weighted = [None] * len(DIL_HEAD_PAIRS)
    for gi in range(n_grp):
        for jt, (pt, members) in enumerate(DIL_GROUP_TILES[gi]):
            ot = token_order(o_refs[gi], gi, gw, jt * LANES, n_grp + gi * DIL_TILES_PER_GROUP + jt) * alphas[gi]
            if len(members) == 2:
                weighted[pt] = ot
            else:
                mine = (lane // HEAD_DIM) == members[0]
                weighted[pt] = jnp.where(mine, ot, 0.0 if weighted[pt] is None else weighted[pt])
    parts = [(weighted[pt] * z_ref[:, pt * LANES:(pt + 1) * LANES].astype(F32)).astype(BF16)
             for pt in range(len(DIL_HEAD_PAIRS))]
    og = jnp.concatenate(parts, axis=1)
    y = _dot(og, w_ref[...])
    x2 = x_ref[...] + mod_ref[2:3, :] * y
    ms = jnp.mean(x2 * x2, axis=-1, keepdims=True)
    out_ref[...] = x2 * lax.rsqrt(ms + NORM_EPS) * fg_ref[...]


def _outproj1(o_groups, lse_groups, z, w, x, mod, final_g):
    b, t, d = x.shape
    tok = lambda k: pl.BlockSpec((None, TM, k), lambda i, j: (i, j, 0))
    full = lambda a: pl.BlockSpec(a.shape, lambda i, j: (0,) * a.ndim)
    grp = lambda a, dil: pl.BlockSpec((None, TM // dil, a.shape[2]), lambda i, j: (i, j, 0))
    dils = [dil for _, dil in DIL_PATTERNS]
    n_slabs = len(dils) * (1 + DIL_TILES_PER_GROUP)
    return pl.pallas_call(
        _outproj1_kernel,
        grid=(b, t // TM),
        in_specs=[grp(a, dl) for a, dl in zip(o_groups, dils)] + [grp(a, dl) for a, dl in zip(lse_groups, dils)]
                 + [tok(z.shape[2]), full(w), tok(d), pl.BlockSpec((None, 3, d), lambda i, j: (i, 0, 0)), full(final_g)],
        out_specs=tok(d),
        out_shape=jax.ShapeDtypeStruct((b, t, d), F32),
        scratch_shapes=[pltpu.VMEM((n_slabs, TM, LANES), F32)],
        compiler_params=_cparams("parallel", "parallel"),
        name="outproj_dil_final",
    )(*o_groups, *lse_groups, z, w, x, mod, final_g)


def _rope_tables(t):
    inv = ROPE_THETA ** (-jnp.arange(HALF, dtype=F32) / HALF)
    ang = jnp.arange(t, dtype=F32)[:, None] * inv[None, :]
    cos = jnp.tile(jnp.cos(ang), (1, 4))
    sin = jnp.tile(jnp.sin(ang), (1, 4))
    sign = jnp.where(jnp.arange(LANES) < 2 * HALF, -1.0, 1.0).astype(F32)
    return cos, sin * sign[None, :]


def _pair_rope_layout(w, n_pairs):
    d = w.shape[0]
    return w.reshape(d, n_pairs, 2, 2, HALF).transpose(0, 1, 3, 2, 4).reshape(d, n_pairs * LANES)


def _nsa_weights(w_in, pe_k, pe_v, ck_w1, ck_w2, cv_w1, cv_w2, w_out):
    d = w_in.shape[0]
    aw = N_HEADS * HEAD_DIM
    kvw = NSA_KV_HEADS * HEAD_DIM
    cuts = np.cumsum([aw] + [kvw] * 6 + [aw]).tolist()
    q, k_c, v_c, k_s, v_s, k_w, v_w, z, gl = jnp.split(w_in, cuts, axis=1)
    scale = HEAD_DIM ** -0.5
    wq = (q * scale).reshape(d, NSA_KV_HEADS, NSA_GROUP, 2, HALF).transpose(0, 2, 3, 1, 4).reshape(d, aw)
    k_lay = lambda w: _pair_rope_layout(w, 1)
    wkv = jnp.concatenate([k_lay(k_c), v_c, k_lay(k_s), v_s, k_lay(k_w), v_w], axis=1)
    wz = z.reshape(d, NSA_KV_HEADS, NSA_GROUP, HEAD_DIM).transpose(0, 2, 1, 3).reshape(d, aw)
    wo = w_out.reshape(NSA_KV_HEADS, NSA_GROUP, HEAD_DIM, -1).transpose(1, 0, 2, 3).reshape(aw, -1)
    wg = jnp.pad(gl, ((0, 0), (0, LANES - gl.shape[1])))

    def grouped(w1_half, rope_lanes):
        out = []
        for g in range(NSA_KV_HEADS):
            if rope_lanes:
                src = w1_half.reshape(CMP_STRIDE, 2, 1, HALF, CMP_HIDDEN)
                pads = ((0, 0), (0, 0), (g, NSA_KV_HEADS - 1 - g), (0, 0), (0, 0))
            else:
                src = w1_half.reshape(CMP_STRIDE, 1, HEAD_DIM, CMP_HIDDEN)
                pads = ((0, 0), (g, NSA_KV_HEADS - 1 - g), (0, 0), (0, 0))
            out.append(jnp.pad(src, pads).reshape(CMP_STRIDE * LANES, CMP_HIDDEN))
        return jnp.concatenate(out, axis=1)

    def pe_rows(pe_half, rope_lanes):
        if rope_lanes:
            v = jnp.broadcast_to(pe_half.reshape(CMP_STRIDE, 2, 1, HALF), (CMP_STRIDE, 2, NSA_KV_HEADS, HALF))
        else:
            v = jnp.broadcast_to(pe_half.reshape(CMP_STRIDE, 1, HEAD_DIM), (CMP_STRIDE, NSA_KV_HEADS, HEAD_DIM))
        return v.reshape(1, CMP_STRIDE * LANES)

    def w2_padded(w2, rope_lanes):
        out = []
        for g in range(NSA_KV_HEADS):
            if rope_lanes:
                src = w2.reshape(CMP_HIDDEN, 2, 1, HALF)
                pads = ((0, 0), (0, 0), (g, NSA_KV_HEADS - 1 - g), (0, 0))
            else:
                src = w2.reshape(CMP_HIDDEN, 1, HEAD_DIM)
                pads = ((0, 0), (g, NSA_KV_HEADS - 1 - g), (0, 0))
            out.append(jnp.pad(src, pads).reshape(CMP_HIDDEN, LANES))
        return jnp.concatenate(out, axis=0)

    k1 = ck_w1.reshape(2, CMP_STRIDE, HEAD_DIM, CMP_HIDDEN)
    v1 = cv_w1.reshape(2, CMP_STRIDE, HEAD_DIM, CMP_HIDDEN)
    compress_consts = (
        pe_rows(pe_k[:CMP_STRIDE], True), pe_rows(pe_k[CMP_STRIDE:], True),
        pe_rows(pe_v[:CMP_STRIDE], False), pe_rows(pe_v[CMP_STRIDE:], False),
        grouped(k1[0], True).astype(BF16), grouped(k1[1], True).astype(BF16),
        grouped(v1[0], False).astype(BF16), grouped(v1[1], False).astype(BF16),
        w2_padded(ck_w2, True).astype(BF16), w2_padded(cv_w2, False).astype(BF16),
    )
    return wq.astype(BF16), wkv.astype(BF16), wz.astype(BF16), wg.astype(BF16), wo.astype(BF16), compress_consts


def _dil_weights(w_in, w_out):
    d = w_in.shape[0]
    q, k, v, z = jnp.split(w_in, 4, axis=1)
    n_pairs = len(DIL_HEAD_PAIRS)
    order = [hd for pair in DIL_HEAD_PAIRS for hd in pair]

    def tile_order(w, axis):
        parts = []
        for hd in order:
            sl = [slice(None)] * w.ndim
            sl[axis] = slice(hd * HEAD_DIM, (hd + 1) * HEAD_DIM)
            parts.append(w[tuple(sl)])
        return jnp.concatenate(parts, axis=axis)

    scale = HEAD_DIM ** -0.5
    wq = _pair_rope_layout(tile_order(q * scale, 1), n_pairs)
    wk = _pair_rope_layout(tile_order(k, 1), n_pairs)
    return (wq.astype(BF16), wk.astype(BF16), tile_order(v, 1).astype(BF16), tile_order(z, 1).astype(BF16),
            tile_order(w_out, 0).astype(BF16))


def _selection_constants(t):
    n_cmp_rows = t // CMP_STRIDE
    n_slc = t // SLC_LEN
    c_start = CMP_STRIDE * np.arange(n_cmp_rows)
    s_start = SLC_LEN * np.arange(n_slc)
    ovl_t = ((c_start[None, :] < s_start[:, None] + SLC_LEN)
             & (c_start[None, :] + CMP_LEN > s_start[:, None])).astype(np.float32)
    ovl_t[:, (t - CMP_LEN) // CMP_STRIDE + 1:] = 0.0
    e = np.zeros((NSA_KV_HEADS, t // KEY_CHUNK, LANES, KEY_CHUNK), np.float32)
    keys = np.arange(t)
    for g in range(NSA_KV_HEADS):
        e[g, keys // KEY_CHUNK, g * n_slc + keys // SLC_LEN, keys % KEY_CHUNK] = 1.0
    return jnp.asarray(ovl_t), jnp.asarray(e, dtype=BF16)


@jax.jit
def kernel(x, c, norm_g, ada_w, ada_b, nsa_w_in, nsa_pe_k, nsa_pe_v, nsa_ck_w1, nsa_ck_w2,
           nsa_cv_w1, nsa_cv_w2, nsa_w_out, dil_w_in, dil_w_out, final_g):
    b, t, d = x.shape
    assert t % SLC_CK == 0 and t % TM == 0 and d % LANES == 0
    assert NSA_KV_HEADS * (t // SLC_LEN) <= LANES and t // CMP_STRIDE == LANES
    mod = _adaln_mod(c, ada_w, ada_b).reshape(ada_w.shape[0], b, 3, d)
    cos, sin = _rope_tables(t)

    wq, wkv, wz, wg, wo, compress_consts = _nsa_weights(
        nsa_w_in[0], nsa_pe_k[0], nsa_pe_v[0], nsa_ck_w1[0], nsa_ck_w2[0], nsa_cv_w1[0], nsa_cv_w2[0], nsa_w_out[0])
    qp, qr, kc, vc, ks, vs, kw, vw, z, gl = _inproj0(x, mod[0], norm_g[0:1], cos, sin, wq, wkv, wz, wg)
    kcmp, vcmp = _compress(kc, vc, compress_consts)
    ovl_t, e_sel = _selection_constants(t)
    ocmp, msel = _cmp_attn(qp, kcmp, vcmp, ovl_t, gl)
    og = _nsa_attn(qr, ks, vs, kw, vw, msel, e_sel, ocmp, z, gl)

    dq, dk, dv, dz, dwo = _dil_weights(dil_w_in[0], dil_w_out[0])
    n_grp = len(DIL_PATTERNS)
    x1, *qkv, z1 = _outproj0_inproj1(og, wo, x, mod[0], mod[1], norm_g[1:2], cos, sin, dq, dk, dv, dz)
    assert all(win // dil == DIL_WIN and TM % (dil * 16) == 0 for win, dil in DIL_PATTERNS)
    o_groups, lse_groups = _dil_attn(qkv[0:n_grp], qkv[n_grp:2 * n_grp], qkv[2 * n_grp:3 * n_grp], t)
    return _outproj1(o_groups, lse_groups, z1, dwo, x1, mod[1], final_g.reshape(1, d))
```

```python
import functools
import math

import numpy as np
import jax
import jax.numpy as jnp
from jax import lax
from jax.experimental import pallas as pl
from jax.experimental.pallas import tpu as pltpu

F32 = jnp.float32
BF16 = jnp.bfloat16
HIGHEST = lax.Precision.HIGHEST

HEAD_DIM = 64
HALF = HEAD_DIM // 2
N_HEADS = 16
ROPE_THETA = 10000.0
NORM_EPS = 1e-6
NSA_KV_HEADS = 2
NSA_GROUP = N_HEADS // NSA_KV_HEADS
NSA_BRANCHES = 3
CMP_LEN = 32
CMP_STRIDE = 16
CMP_HIDDEN = 256
SLC_LEN = 64
SLC_TOP = 16
WIN_LEN = 512
DIL_PATTERNS = ((128, 1), (512, 4), (2048, 16))
DIL_GROUP_HEADS = (6, 5, 5)
DIL_WIN = 128


def _dil_tile_plan():
    offs = np.cumsum((0,) + DIL_GROUP_HEADS)
    pairs, group_tiles, leftovers = [], [], []
    for gi, hn in enumerate(DIL_GROUP_HEADS):
        heads = list(range(offs[gi], offs[gi + 1]))
        group_tiles.append([])
        for i in range(0, hn - hn % 2, 2):
            group_tiles[gi].append((len(pairs), (0, 1)))
            pairs.append((heads[i], heads[i + 1]))
        if hn % 2:
            leftovers.append((gi, heads[-1]))
    assert len(leftovers) % 2 == 0
    for (ga, ha), (gb, hb) in zip(leftovers[0::2], leftovers[1::2]):
        group_tiles[ga].append((len(pairs), (0,)))
        group_tiles[gb].append((len(pairs), (1,)))
        pairs.append((ha, hb))
    return tuple(pairs), tuple(tuple(tiles) for tiles in group_tiles)


DIL_HEAD_PAIRS, DIL_GROUP_TILES = _dil_tile_plan()
DIL_TILES_PER_GROUP = len(DIL_GROUP_TILES[0])
assert all(len(tiles) == DIL_TILES_PER_GROUP for tiles in DIL_GROUP_TILES)

LANES = 128
VMEM_LIMIT_BYTES = 48 * 1024 * 1024

NEG = -1e30
TQ = 128
KEY_CHUNK = 128
SLC_CK = 512
TM = 512
DIL_STEP_TOKENS = 512
CMP_STEP_TOKENS = 512


def _cparams(*sem):
    return pltpu.CompilerParams(dimension_semantics=sem, vmem_limit_bytes=VMEM_LIMIT_BYTES)


def _dot(a, b):
    return jnp.dot(a, b, preferred_element_type=F32)


def _dot_t(a, b):
    return lax.dot_general(a, b, (((1,), (1,)), ((), ())), preferred_element_type=F32)


def _silu(v):
    return v * jax.nn.sigmoid(v)


def _mod_kernel(c_ref, w_ref, b_ref, o_ref):
    s = _silu(c_ref[...])
    o_ref[...] = jnp.dot(s, w_ref[...], precision=HIGHEST, preferred_element_type=F32) + b_ref[...]


def _adaln_mod(c, ada_w, ada_b):
    depth, d, n3 = ada_w.shape
    b = c.shape[0]
    tn = 1024
    return pl.pallas_call(
        _mod_kernel,
        grid=(depth, n3 // tn),
        in_specs=[
            pl.BlockSpec((b, d), lambda i, j: (0, 0)),
            pl.BlockSpec((None, d, tn), lambda i, j: (i, 0, j)),
            pl.BlockSpec((None, 1, tn), lambda i, j: (i, 0, j)),
        ],
        out_specs=pl.BlockSpec((None, b, tn), lambda i, j: (i, 0, j)),
        out_shape=jax.ShapeDtypeStruct((depth, b, n3), F32),
        compiler_params=_cparams("arbitrary", "arbitrary"),
        name="adaln_mod",
    )(c, ada_w, ada_b.reshape(depth, 1, n3))


def _modulated_norm(x, mod_ref, g_ref):
    ms = jnp.mean(x * x, axis=-1, keepdims=True)
    y = x * lax.rsqrt(ms + NORM_EPS) * g_ref[...]
    return y * (1.0 + mod_ref[1:2, :]) + mod_ref[0:1, :]


def _rope_tile(v, cos, sin):
    return v * cos + pltpu.roll(v, 2 * HALF, axis=1) * sin


def _store_key_chunks_t(kt_ref, k):
    dim = lax.broadcasted_iota(jnp.int32, (LANES, KEY_CHUNK), 0)
    for c in range(kt_ref.shape[1]):
        kt = k[c * KEY_CHUNK:(c + 1) * KEY_CHUNK, :].T
        for g in range(kt_ref.shape[0]):
            kt_ref[g, c] = jnp.where((dim % HEAD_DIM) // HALF == g, kt, 0.0).astype(BF16)


def _inproj0_kernel(x_ref, mod_ref, g_ref, cos_ref, sin_ref, wq_ref, wkv_ref, wz_ref, wg_ref,
                    qp_ref, qr_ref, kc_ref, vc_ref, ks_ref, vs_ref, kw_ref, vw_ref, z_ref, gl_ref, stage):
    tm = x_ref.shape[0]
    h = _modulated_norm(x_ref[...], mod_ref, g_ref).astype(BF16)
    cos = cos_ref[...]
    sin = sin_ref[...]
    q = _dot(h, wq_ref[...])
    for j in range(q.shape[1] // LANES):
        sl = slice(j * LANES, (j + 1) * LANES)
        qp_ref[:, sl] = q[:, sl].astype(BF16)
        qr_ref[:, sl] = _rope_tile(q[:, sl], cos, sin).astype(BF16)
    kv = _dot(h, wkv_ref[...])
    tiles = [kv[:, i * LANES:(i + 1) * LANES] for i in range(6)]
    for slab, (tile, o_ref) in enumerate(((tiles[0], kc_ref), (tiles[1], vc_ref))):
        stage[slab] = tile
        for r in range(CMP_STRIDE):
            rows = stage[slab, pl.ds(r, tm // CMP_STRIDE, stride=CMP_STRIDE), :]
            o_ref[:, r * LANES:(r + 1) * LANES] = rows.astype(BF16)
    _store_key_chunks_t(ks_ref, _rope_tile(tiles[2], cos, sin))
    vs_ref[...] = tiles[3].astype(BF16)
    _store_key_chunks_t(kw_ref, _rope_tile(tiles[4], cos, sin))
    vw_ref[...] = tiles[5].astype(BF16)
    z_ref[...] = _silu(_dot(h, wz_ref[...])).astype(BF16)
    gl_ref[...] = _dot(h, wg_ref[...])


def _inproj0(x, mod, g, cos, sin, wq, wkv, wz, wg):
    b, t, d = x.shape
    nq, nz = wq.shape[1], wz.shape[1]
    tok = lambda n: pl.BlockSpec((None, TM, n), lambda i, j: (i, j, 0))
    full = lambda a: pl.BlockSpec(a.shape, lambda i, j: (0,) * a.ndim)
    tab = pl.BlockSpec((TM, LANES), lambda i, j: (j, 0))
    shp = lambda n, dt: jax.ShapeDtypeStruct((b, t, n), dt)
    kt = pl.BlockSpec((None, NSA_KV_HEADS, TM // KEY_CHUNK, LANES, KEY_CHUNK), lambda i, j: (i, 0, j, 0, 0))
    kt_shp = jax.ShapeDtypeStruct((b, NSA_KV_HEADS, t // KEY_CHUNK, LANES, KEY_CHUNK), BF16)
    kv = tok(LANES)
    kv_shp = shp(LANES, BF16)
    cm = pl.BlockSpec((None, TM // CMP_STRIDE, CMP_STRIDE * LANES), lambda i, j: (i, j, 0))
    cm_shp = jax.ShapeDtypeStruct((b, t // CMP_STRIDE, CMP_STRIDE * LANES), BF16)
    return pl.pallas_call(
        _inproj0_kernel,
        grid=(b, t // TM),
        in_specs=[tok(d), pl.BlockSpec((None, 3, d), lambda i, j: (i, 0, 0)), full(g), tab, tab,
                  full(wq), full(wkv), full(wz), full(wg)],
        out_specs=[tok(nq), tok(nq), cm, cm, kt, kv, kt, kv, tok(nz), tok(LANES)],
        out_shape=[shp(nq, BF16), shp(nq, BF16), cm_shp, cm_shp, kt_shp, kv_shp, kt_shp, kv_shp,
                   shp(nz, BF16), shp(LANES, F32)],
        scratch_shapes=[pltpu.VMEM((2, TM, LANES), F32)],
        compiler_params=_cparams("parallel", "parallel"),
        name="inproj_nsa",
    )(x, mod, g, cos, sin, wq, wkv, wz, wg)


def _compress_kernel(ak_ref, av_ref, pekt_ref, pekb_ref, pevt_ref, pevb_ref,
                     w1kt_ref, w1kb_ref, w1vt_ref, w1vb_ref, w2k_ref, w2v_ref, kcmp_ref, vcmp_ref):
    def mlp(a_ref, pet_ref, peb_ref, w1t_ref, w1b_ref, w2_ref):
        a = a_ref[...].astype(F32)
        top = _dot((a + pet_ref[...]).astype(BF16), w1t_ref[...])
        bot = _dot((a + peb_ref[...]).astype(BF16), w1b_ref[...])
        n = bot.shape[0]
        hid = top + pltpu.roll(bot, n - 1, axis=0)
        return _dot(_silu(hid).astype(BF16), w2_ref[...])

    kcmp_ref[...] = mlp(ak_ref, pekt_ref, pekb_ref, w1kt_ref, w1kb_ref, w2k_ref).astype(BF16)
    vcmp_ref[...] = mlp(av_ref, pevt_ref, pevb_ref, w1vt_ref, w1vb_ref, w2v_ref).astype(BF16)


def _compress(ak, av, consts):
    b, n, w = ak.shape
    full = lambda a: pl.BlockSpec(a.shape, lambda i: (0,) * a.ndim)
    blk = pl.BlockSpec((None, n, w), lambda i: (i, 0, 0))
    out = pl.BlockSpec((None, n, LANES), lambda i: (i, 0, 0))
    return pl.pallas_call(
        _compress_kernel,
        grid=(b,),
        in_specs=[blk, blk] + [full(a) for a in consts],
        out_specs=[out, out],
        out_shape=[jax.ShapeDtypeStruct((b, n, LANES), BF16)] * 2,
        compiler_params=_cparams("parallel"),
        name="compress",
    )(ak, av, *consts)


def _stack_heads(q_ref, rows, lane_sets):
    n_tiles = q_ref.shape[1] // LANES
    parts = []
    for j in range(n_tiles):
        qj = q_ref[rows, j * LANES:(j + 1) * LANES]
        parts.append(jnp.where(lane_sets, qj, jnp.zeros_like(qj)))
    return jnp.concatenate(parts, axis=0)


def _head_gate_rows(gates, g, branches, lane):
    blocks = []
    for j in range(NSA_GROUP):
        base = (g * NSA_GROUP + j) * NSA_BRANCHES
        kept = jnp.zeros_like(gates)
        for br in branches:
            kept = jnp.where(lane == base + br, gates, kept)
        hi = kept.astype(BF16)
        blocks.append(jnp.concatenate([hi, (kept - hi.astype(F32)).astype(BF16)], axis=1))
    return jnp.concatenate(blocks, axis=0)


def _gate_spread(branches):
    col = np.arange(2 * LANES) % LANES
    spread = np.zeros((2 * LANES, LANES * len(branches)), np.float32)
    for i, br in enumerate(branches):
        spread[col % NSA_BRANCHES == br, i * LANES:(i + 1) * LANES] = 1.0
    return jnp.asarray(spread, dtype=BF16)


def _cmp_attn_kernel(qp_ref, kcmp_ref, vcmp_ref, ovl_ref, gl_ref, ocmp_ref, msel_ref, *, n_slc, tq):
    n_cmp = vcmp_ref.shape[0]
    hg = qp_ref.shape[1] // LANES
    n_sub = qp_ref.shape[0] // tq
    assert n_cmp == LANES and tq == LANES
    row = lax.broadcasted_iota(jnp.int32, (LANES, tq), 0)
    col = lax.broadcasted_iota(jnp.int32, (LANES, tq), 1)
    jb = lax.broadcasted_iota(jnp.int32, (n_slc, tq), 0)
    tok_rows = lambda sub: slice(sub * tq, (sub + 1) * tq)
    t0 = lambda sub: pl.program_id(1) * qp_ref.shape[0] + sub * tq
    units = [(sub, g) for sub in range(n_sub) for g in range(NSA_KV_HEADS)]
    eye = jnp.where(row == col, 1.0, 0.0).astype(BF16)
    k_sel = jnp.concatenate([kcmp_ref[...], eye], axis=1)
    v_t = vcmp_ref[...].T

    q_t = [[qp_ref[tok_rows(sub), j * LANES:(j + 1) * LANES].T for j in range(hg)] for sub in range(n_sub)]
    gates_t = [jax.nn.sigmoid(gl_ref[tok_rows(sub), :]).T for sub in range(n_sub)]
    seen = [jnp.concatenate([jnp.where(CMP_STRIDE * row + (CMP_LEN - 1) <= t0(sub) + col, 0.0, NEG).astype(BF16)] * hg,
                            axis=1) for sub in range(n_sub)]
    weights = {}
    for sub, g in units:
        in_group = ((row % HEAD_DIM) // HALF) == g
        q_cols = jnp.concatenate([jnp.where(in_group, v, jnp.zeros_like(v)) for v in q_t[sub]], axis=1)
        s = _dot(k_sel, jnp.concatenate([q_cols, seen[sub]], axis=0))
        m = jnp.max(s, axis=0, keepdims=True)
        m = jnp.where(m > 0.5 * NEG, m, 0.0)
        e = jnp.exp(s - m)
        den = jnp.sum(e, axis=0, keepdims=True)
        weights[sub, g] = e * (1.0 / jnp.where(den > 0, den, 1.0))
    outs, imps = {}, {}
    for sub, g in units:
        p = weights[sub, g]
        gate = jnp.concatenate([gates_t[sub][(g * NSA_GROUP + j) * NSA_BRANCHES:(g * NSA_GROUP + j) * NSA_BRANCHES + 1, :]
                                for j in range(hg)], axis=1)
        outs[sub, g] = gate * _dot(v_t, p.astype(BF16))
        p_sum = p[:, 0:tq]
        for j in range(1, hg):
            p_sum = p_sum + p[:, j * tq:(j + 1) * tq]
        imps[sub, g] = jnp.dot(ovl_ref[...], p_sum, precision=HIGHEST, preferred_element_type=F32)
    picked = {}
    for sub, g in units:
        cur = (t0(sub) + lax.broadcasted_iota(jnp.int32, (n_slc, tq), 1)) // SLC_LEN
        visible = jb <= cur
        forced = (jb == 0) | (jb == cur) | (jb == cur - 1)
        rank = jnp.where(forced, -NEG, jnp.where(visible, imps[sub, g], NEG))
        cnt = jnp.zeros((n_slc, tq), F32)
        for jp in range(n_slc):
            rj = rank[jp:jp + 1, :]
            tie = jnp.where(jb > jp, 1.0, 0.0)
            cnt = cnt + jnp.where(rj > rank, 1.0, jnp.where(rj == rank, tie, 0.0))
        picked[sub, g] = jnp.where(visible, jnp.where(cnt < SLC_TOP, 1.0, 0.0), 0.0)
    pad = jnp.zeros((LANES - NSA_KV_HEADS * n_slc, tq), F32)
    for sub in range(n_sub):
        for j in range(hg):
            cols = slice(j * tq, (j + 1) * tq)
            o_t = jnp.where(row < HEAD_DIM, outs[sub, 0][:, cols], outs[sub, 1][:, cols])
            ocmp_ref[tok_rows(sub), j * LANES:(j + 1) * LANES] = o_t.T.astype(BF16)
        sel_t = jnp.concatenate([picked[sub, g] for g in range(NSA_KV_HEADS)] + [pad], axis=0)
        msel_ref[tok_rows(sub), :] = sel_t.T.astype(BF16)


def _cmp_attn(qp, kcmp, vcmp, ovl_t, gl):
    b, t, nq = qp.shape
    n_cmp = kcmp.shape[1]
    n_slc = t // SLC_LEN
    tok = lambda n: pl.BlockSpec((None, CMP_STEP_TOKENS, n), lambda i, j: (i, j, 0))
    per_b = pl.BlockSpec((None, n_cmp, LANES), lambda i, j: (i, 0, 0))
    full = lambda a: pl.BlockSpec(a.shape, lambda i, j: (0,) * a.ndim)
    return pl.pallas_call(
        functools.partial(_cmp_attn_kernel, n_slc=n_slc, tq=TQ),
        grid=(b, t // CMP_STEP_TOKENS),
        in_specs=[tok(nq), per_b, per_b, full(ovl_t), tok(LANES)],
        out_specs=[tok(nq), tok(LANES)],
        out_shape=[jax.ShapeDtypeStruct((b, t, nq), BF16), jax.ShapeDtypeStruct((b, t, LANES), BF16)],
        compiler_params=_cparams("parallel", "parallel"),
        name="cmp_attn_select",
    )(qp, kcmp, vcmp, ovl_t, gl)


def _nsa_attn_kernel(qr_ref, ks_ref, vs_ref, kw_ref, vw_ref, msel_ref, e_ref, gsp_ref, ocmp_ref, z_ref, gl_ref,
                     og_ref, m_scr, acc_scr, *, n_slc, ck):
    tq = qr_ref.shape[0]
    hg = qr_ref.shape[1] // LANES
    rows = hg * tq
    n_wc = WIN_LEN // tq + 1
    per_ck = ck // KEY_CHUNK
    qi = pl.program_id(1)
    t0 = qi * tq
    lane = lax.broadcasted_iota(jnp.int32, (tq, LANES), 1)
    a_idx = lax.broadcasted_iota(jnp.int32, (tq, LANES), 0)

    eye = jnp.where(lane == a_idx, 1.0, 0.0).astype(BF16)
    earlier = jnp.where((lane % n_slc) < t0 // SLC_LEN, 0.0, NEG)
    mneg = jnp.where(msel_ref[...].astype(F32) > 0.5, earlier, NEG).astype(BF16)
    mneg_rows = jnp.concatenate([mneg] * hg, axis=0)
    eye_rows = jnp.concatenate([eye] * hg, axis=0)

    def with_selector(sel_rows):
        q_rows = jnp.concatenate([qr_ref[:, j * LANES:(j + 1) * LANES] for j in range(hg)], axis=0)
        return jnp.concatenate([q_rows, sel_rows], axis=1)

    zero_b = jnp.zeros((tq, LANES), BF16)
    neg_b = jnp.full((tq, LANES), NEG, BF16)
    causal_b = jnp.where(lane <= a_idx, 0.0, NEG).astype(BF16)
    far_b = jnp.where(lane > a_idx, 0.0, NEG).astype(BF16)
    own_half = [lane < HEAD_DIM, lane >= HEAD_DIM]

    def values_with_ones(v, g):
        in_own_half = (lax.broadcasted_iota(jnp.int32, v.shape, 1) // HEAD_DIM) == g
        return jnp.where(in_own_half, v, jnp.ones_like(v))

    def lane_tiles(v):
        return [v[:, i * LANES:(i + 1) * LANES] for i in range(v.shape[1] // LANES)]

    def tile_max(tiles):
        mx = tiles[0]
        for v in tiles[1:]:
            mx = jnp.maximum(mx, v)
        return jnp.max(mx, axis=-1, keepdims=True)

    def probs(tiles, m):
        return jnp.concatenate([jnp.exp(v - m) for v in tiles], axis=1).astype(BF16)

    m_scr[...] = jnp.full(m_scr.shape, NEG, F32)
    acc_scr[...] = jnp.zeros(acc_scr.shape, F32)

    def attend_chunks(c0, n):
        k0 = pl.multiple_of(c0 * KEY_CHUNK, KEY_CHUNK)
        q_sel = with_selector(mneg_rows)
        scores = []
        for g in range(NSA_KV_HEADS):
            kt = jnp.concatenate([ks_ref[g, c0 + i] for i in range(n)], axis=1)
            eb = jnp.concatenate([e_ref[g, c0 + i] for i in range(n)], axis=1)
            scores.append(_dot(q_sel, jnp.concatenate([kt, eb], axis=0)))
        groups = range(NSA_KV_HEADS)
        m_old = [m_scr[g] for g in groups]
        m_new = [jnp.maximum(m_old[g], tile_max(lane_tiles(scores[g]))) for g in groups]
        p = [probs(lane_tiles(scores[g]), m_new[g]) for g in groups]
        pv = [_dot(p[g], values_with_ones(vs_ref[pl.ds(k0, n * KEY_CHUNK), :], g)) for g in groups]
        for g in groups:
            acc_scr[g] = jnp.exp(m_old[g] - m_new[g]) * acc_scr[g] + pv[g]
            m_scr[g] = m_new[g]

    def earlier_keys(kc, carry):
        attend_chunks(kc * per_ck, per_ck)
        return carry

    lax.fori_loop(0, qi // per_ck, earlier_keys, 0)
    for rem in range(1, per_ck):
        @pl.when(qi % per_ck == rem)
        def _():
            attend_chunks((qi // per_ck) * per_ck, rem)

    kd = pl.multiple_of(t0, tq)
    zeros_v = jnp.zeros((tq, LANES), BF16)
    win_chunks = [qi - (n_wc - 1 - c) for c in range(n_wc - 1)]
    win_bias = [jnp.where(cidx >= 0, far_b if c == 0 else zero_b, neg_b) for c, cidx in enumerate(win_chunks)]
    pairs = [(c, c + 1) for c in range(0, n_wc - 1, 2)]
    q_eye = with_selector(eye_rows)
    s_own, scores = [], []
    for g in range(NSA_KV_HEADS):
        own_k = jnp.concatenate([jnp.concatenate([kw_ref[g, qi], causal_b], axis=0),
                                 jnp.concatenate([ks_ref[g, qi], causal_b], axis=0)], axis=1)
        kbs = [jnp.concatenate([kw_ref[g, jnp.maximum(cidx, 0)], win_bias[c]], axis=0)
               for c, cidx in enumerate(win_chunks)]
        s_own.append(_dot(q_eye, own_k))
        scores.append([_dot(q_eye, jnp.concatenate([kbs[c] for c in pr], axis=1)) for pr in pairs])

    groups = range(NSA_KV_HEADS)
    s_win_own = [s_own[g][:, 0:tq] for g in groups]
    s_slc_own = [s_own[g][:, tq:2 * tq] for g in groups]
    m_win = [tile_max([v for s in scores[g] for v in lane_tiles(s)] + [s_win_own[g]]) for g in groups]
    m_old = [m_scr[g] for g in groups]
    m_slc = [jnp.maximum(m_old[g], jnp.max(s_slc_own[g], axis=-1, keepdims=True)) for g in groups]
    p_all = [jnp.concatenate([probs(lane_tiles(s), m_win[g]) for s in scores[g]]
                             + [probs([s_win_own[g]], m_win[g]), probs([s_slc_own[g]], m_slc[g])], axis=1) for g in groups]
    acc_slc, acc_win = [], []
    for g in groups:
        vbs = [values_with_ones(vw_ref[pl.ds(pl.multiple_of(jnp.maximum(cidx, 0) * tq, tq), tq), :], g)
               for cidx in win_chunks] + [values_with_ones(vw_ref[pl.ds(kd, tq), :], g)]
        win_v = [jnp.concatenate([v, zeros_v], axis=1) for v in vbs]
        slc_v = jnp.concatenate([zeros_v, values_with_ones(vs_ref[pl.ds(kd, tq), :], g)], axis=1)
        pv = _dot(p_all[g], jnp.concatenate(win_v + [slc_v], axis=0))
        acc_win.append(pv[:, 0:LANES])
        acc_slc.append(jnp.exp(m_old[g] - m_slc[g]) * acc_scr[g] + pv[:, LANES:2 * LANES])

    gates = jax.nn.sigmoid(gl_ref[...])
    row_gates = [_dot(_head_gate_rows(gates, g, (1, 2), lane), gsp_ref[...]) for g in range(NSA_KV_HEADS)]

    def gated(accs, rs, branch_lanes):
        num = [accs[g][rs] * row_gates[g][rs, branch_lanes] for g in range(NSA_KV_HEADS)]
        sums = pltpu.roll(jnp.where(own_half[0], accs[1][rs], accs[0][rs]), HEAD_DIM, axis=1)
        return jnp.where(own_half[0], num[0], num[1]) / sums

    for j in range(hg):
        rs = slice(j * tq, (j + 1) * tq)
        cols = slice(j * LANES, (j + 1) * LANES)
        o = (ocmp_ref[:, cols].astype(F32) + gated(acc_slc, rs, slice(0, LANES))
             + gated(acc_win, rs, slice(LANES, 2 * LANES)))
        og_ref[:, cols] = (o * z_ref[:, cols].astype(F32)).astype(BF16)


def _nsa_attn(qr, ks, vs, kw, vw, msel, e, ocmp, z, gl):
    b, t, nq = qr.shape
    gsp = _gate_spread((1, 2))
    hg = nq // LANES
    rows = hg * TQ
    assert TQ == KEY_CHUNK and WIN_LEN % TQ == 0
    tok = lambda n: pl.BlockSpec((None, TQ, n), lambda i, j: (i, j, 0))
    per_b = pl.BlockSpec((None, t, LANES), lambda i, j: (i, 0, 0))
    per_b_t = pl.BlockSpec((None, NSA_KV_HEADS, t // KEY_CHUNK, LANES, KEY_CHUNK), lambda i, j: (i, 0, 0, 0, 0))
    full = lambda a: pl.BlockSpec(a.shape, lambda i, j: (0,) * a.ndim)
    return pl.pallas_call(
        functools.partial(_nsa_attn_kernel, n_slc=t // SLC_LEN, ck=SLC_CK),
        grid=(b, t // TQ),
        in_specs=[tok(nq), per_b_t, per_b, per_b_t, per_b, tok(LANES), full(e), full(gsp), tok(nq), tok(nq), tok(LANES)],
        out_specs=tok(nq),
        out_shape=jax.ShapeDtypeStruct((b, t, nq), BF16),
        scratch_shapes=[
            pltpu.VMEM((NSA_KV_HEADS, rows, LANES), F32),
            pltpu.VMEM((NSA_KV_HEADS, rows, LANES), F32),
        ],
        compiler_params=_cparams("parallel", "arbitrary"),
        name="nsa_slc_win_attn",
    )(qr, ks, vs, kw, vw, msel, e, gsp, ocmp, z, gl)


def _inproj1_kernel(og_ref, wo_ref, x_ref, mod0_ref, mod_ref, g_ref, cos_ref, sin_ref, wq_ref, wk_ref, wv_ref, wz_ref,
                    x1_ref, *refs):
    n_grp = len(DIL_PATTERNS)
    outs = [refs[a * n_grp:(a + 1) * n_grp] for a in range(3)]
    z_ref, stage = refs[3 * n_grp], refs[3 * n_grp + 1]
    tm = x_ref.shape[0]
    gw = DIL_TILES_PER_GROUP * LANES
    x1 = x_ref[...] + mod0_ref[2:3, :] * _dot(og_ref[...], wo_ref[...])
    x1_ref[...] = x1
    h = _modulated_norm(x1, mod_ref, g_ref).astype(BF16)
    cos = cos_ref[...]
    sin = sin_ref[...]
    for a, w_ref in enumerate((wq_ref, wk_ref, wv_ref)):
        u = _dot(h, w_ref[...])
        tiles = [u[:, pt * LANES:(pt + 1) * LANES] for pt in range(len(DIL_HEAD_PAIRS))]
        if a < 2:
            tiles = [_rope_tile(v, cos, sin) for v in tiles]
        for gi, (_, dil) in enumerate(DIL_PATTERNS):
            for jt, (pt, _) in enumerate(DIL_GROUP_TILES[gi]):
                tile = tiles[pt]
                if dil == 1:
                    outs[a][gi][:, jt * LANES:(jt + 1) * LANES] = tile.astype(BF16)
                    continue
                slab = (a * n_grp + gi) * DIL_TILES_PER_GROUP + jt
                stage[slab] = tile
                for r in range(dil):
                    rows = stage[slab, pl.ds(r, tm // dil, stride=dil), :]
                    outs[a][gi][:, r * gw + jt * LANES:r * gw + (jt + 1) * LANES] = rows.astype(BF16)
    z_ref[...] = _silu(_dot(h, wz_ref[...])).astype(BF16)


def _outproj0_inproj1(og, wo, x, mod0, mod, g, cos, sin, wq, wk, wv, wz):
    b, t, d = x.shape
    n = wq.shape[1]
    gw = DIL_TILES_PER_GROUP * LANES
    tok = lambda k: pl.BlockSpec((None, TM, k), lambda i, j: (i, j, 0))
    full = lambda a: pl.BlockSpec(a.shape, lambda i, j: (0,) * a.ndim)
    tab = pl.BlockSpec((TM, LANES), lambda i, j: (j, 0))
    mods = pl.BlockSpec((None, 3, d), lambda i, j: (i, 0, 0))
    grp_specs = [pl.BlockSpec((None, TM // dil, dil * gw), lambda i, j: (i, j, 0)) for _, dil in DIL_PATTERNS]
    grp_shapes = [jax.ShapeDtypeStruct((b, t // dil, dil * gw), BF16) for _, dil in DIL_PATTERNS]
    n_slabs = 3 * len(DIL_PATTERNS) * DIL_TILES_PER_GROUP
    return pl.pallas_call(
        _inproj1_kernel,
        grid=(b, t // TM),
        in_specs=[tok(og.shape[2]), full(wo), tok(d), mods, mods, full(g), tab, tab,
                  full(wq), full(wk), full(wv), full(wz)],
        out_specs=[tok(d)] + grp_specs * 3 + [tok(n)],
        out_shape=[jax.ShapeDtypeStruct((b, t, d), F32)] + grp_shapes * 3 + [jax.ShapeDtypeStruct((b, t, n), BF16)],
        scratch_shapes=[pltpu.VMEM((n_slabs, TM, LANES), F32)],
        compiler_params=_cparams("parallel", "parallel"),
        name="outproj_nsa_inproj_dil",
    )(og, wo, x, mod0, mod, g, cos, sin, wq, wk, wv, wz)


def _dil_attn_kernel(*refs, plans):
    n_grp = len(plans)
    step = pl.program_id(1)
    for gi, (members, tq, n_seq, blocks_per_seq) in enumerate(plans):
        q_ref, k_ref, v_ref = refs[3 * gi:3 * gi + 3]
        o_ref, lse_ref = refs[3 * n_grp + 2 * gi:3 * n_grp + 2 * gi + 2]
        _dil_attn_group(q_ref, k_ref, v_ref, o_ref, lse_ref, step % blocks_per_seq, members, tq, n_seq)


def _dil_attn_group(q_ref, k_ref, v_ref, o_ref, lse_ref, i, members, tq, n_seq):
    n_heads = sum(len(m) for m in members)
    tq_blk = q_ref.shape[0]
    gw = q_ref.shape[1] // n_seq
    n_tiles = gw // LANES
    n_sub = tq_blk // tq
    assert tq == LANES
    row = lax.broadcasted_iota(jnp.int32, (LANES, tq), 0)
    col = lax.broadcasted_iota(jnp.int32, (LANES, tq), 1)
    head_a = ((row % HEAD_DIM) // HALF) == 0
    diag_b = jnp.where(row <= col, 0.0, NEG)
    prev_b = jnp.where(row >= col, 0.0, NEG)
    neg_b = jnp.full((LANES, tq), NEG, F32)
    two = lambda v: jnp.concatenate([v, v], axis=1)
    units = [(sq, sub, jt) for sq in range(n_seq) for sub in range(n_sub) for jt in range(n_tiles)]
    rows_of = lambda sub: slice(sub * tq, (sub + 1) * tq)
    cols_of = lambda sq, jt: slice(sq * gw + jt * LANES, sq * gw + (jt + 1) * LANES)
    key_rows = lambda c: pl.ds(pl.multiple_of(jnp.maximum(i * n_sub + c, 0) * tq, tq), tq)
    q_cols, v_t = {}, {}
    for sq, sub, jt in units:
        q_t = q_ref[rows_of(sub), cols_of(sq, jt)].T
        zero = jnp.zeros_like(q_t)
        q_cols[sq, sub, jt] = jnp.concatenate([jnp.where(head_a, q_t, zero), jnp.where(head_a, zero, q_t)], axis=1)
        v_t[sq, sub, jt] = v_ref[key_rows(sub), cols_of(sq, jt)].T
        if sub == 0:
            v_t[sq, -1, jt] = v_ref[key_rows(-1), cols_of(sq, jt)].T
    probs_t, stats = {}, {}
    for sq, sub, jt in units:
        prev_bias = prev_b if sub > 0 else jnp.where(i > 0, prev_b, neg_b)
        bias = jnp.concatenate([two(prev_bias), two(diag_b)], axis=0)
        keys = jnp.concatenate([k_ref[key_rows(sub - 1), cols_of(sq, jt)], k_ref[key_rows(sub), cols_of(sq, jt)]], axis=0)
        s = _dot(keys, q_cols[sq, sub, jt]) + bias
        m = jnp.max(s, axis=0, keepdims=True)
        p = jnp.exp(s - m)
        l = jnp.sum(p, axis=0, keepdims=True)
        probs_t[sq, sub, jt] = p.astype(BF16)
        stats[sq, sub, jt] = (m, l)
    for sq, sub, jt in units:
        m, l = stats[sq, sub, jt]
        o_t = _dot(jnp.concatenate([v_t[sq, sub - 1, jt], v_t[sq, sub, jt]], axis=1), probs_t[sq, sub, jt])
        inv = 1.0 / l
        o_t = jnp.where(row < HEAD_DIM, o_t[:, 0:tq] * inv[:, 0:tq], o_t[:, tq:2 * tq] * inv[:, tq:2 * tq])
        o_ref[rows_of(sub), cols_of(sq, jt)] = o_t.T.astype(BF16)
    for sq in range(n_seq):
        for sub in range(n_sub):
            lses = []
            for jt in range(n_tiles):
                m, l = stats[sq, sub, jt]
                lse = m + jnp.log(l)
                lses += [lse[:, mb * tq:(mb + 1) * tq] for mb in members[jt]]
            top = lses[0]
            for v in lses[1:]:
                top = jnp.maximum(top, v)
            tot = jnp.zeros_like(top)
            for v in lses:
                tot = tot + jnp.exp(v - top)
            group_lse = top + jnp.log(tot) - math.log(n_heads)
            lse_ref[rows_of(sub), sq * LANES:(sq + 1) * LANES] = jnp.broadcast_to(group_lse, (LANES, tq)).T


def _dil_attn(qs, ks, vs, t):
    b = qs[0].shape[0]
    steps = t // DIL_STEP_TOKENS
    plans, in_specs, out_specs, out_shapes, operands = [], [], [], [], []
    for gi, (_, dil) in enumerate(DIL_PATTERNS):
        length = t // dil
        gw = qs[gi].shape[2] // dil
        tq_blk = min(DIL_STEP_TOKENS, length)
        blocks_per_seq = length // tq_blk
        n_seq = DIL_STEP_TOKENS // tq_blk
        assert dil * blocks_per_seq == steps * n_seq
        tok = lambda w, bps=blocks_per_seq, ns=n_seq, tb=tq_blk: pl.BlockSpec(
            (None, tb, ns * w), lambda bi, s: (bi, s % bps, s // bps))
        seq = pl.BlockSpec((None, length, n_seq * gw), lambda bi, s, bps=blocks_per_seq: (bi, 0, s // bps))
        plans.append((tuple(m for _, m in DIL_GROUP_TILES[gi]), min(TQ, length), n_seq, blocks_per_seq))
        in_specs += [tok(gw), seq, seq]
        operands += [qs[gi], ks[gi], vs[gi]]
        out_specs += [tok(gw), tok(LANES)]
        out_shapes += [jax.ShapeDtypeStruct(qs[gi].shape, BF16), jax.ShapeDtypeStruct((b, length, dil * LANES), F32)]
    outs = pl.pallas_call(
        functools.partial(_dil_attn_kernel, plans=tuple(plans)),
        grid=(b, steps),
        in_specs=in_specs,
        out_specs=out_specs,
        out_shape=out_shapes,
        compiler_params=_cparams("parallel", "parallel"),
        name="dil_attn",
    )(*operands)
    return outs[0::2], outs[1::2]


def _outproj1_kernel(*refs):
    n_grp = len(DIL_PATTERNS)
    o_refs, l_refs = refs[0:n_grp], refs[n_grp:2 * n_grp]
    z_ref, w_ref, x_ref, mod_ref, fg_ref, out_ref, stage = refs[2 * n_grp:]
    tm = x_ref.shape[0]
    gw = DIL_TILES_PER_GROUP * LANES

    def token_order(ref, gi, width, col0, slab):
        dil = DIL_PATTERNS[gi][1]
        if dil == 1:
            return ref[:, col0:col0 + LANES].astype(F32)
        for r in range(dil):
            stage[slab, pl.ds(r, tm // dil, stride=dil), :] = ref[:, r * width + col0:r * width + col0 + LANES].astype(F32)
        return stage[slab]

    ls = [token_order(l_refs[gi], gi, LANES, 0, gi) for gi in range(n_grp)]
    top = jnp.maximum(jnp.maximum(ls[0], ls[1]), ls[2])
    es = [jnp.exp(v - top) for v in ls]
    den = es[0] + es[1] + es[2]
    alphas = [e / den * float(n_grp) for e in es]
    lane = lax.broadcasted_iota(jnp.int32, (tm, LANES), 1)
    weighted = [None] * len(DIL_HEAD_PAIRS)
    for gi in range(n_grp):
        for jt, (pt, members) in enumerate(DIL_GROUP_TILES[gi]):
            ot = token_order(o_refs[gi], gi, gw, jt * LANES, n_grp + gi * DIL_TILES_PER_GROUP + jt) * alphas[gi]
            if len(members) == 2:
                weighted[pt] = ot
            else:
                mine = (lane // HEAD_DIM) == members[0]
                weighted[pt] = jnp.where(mine, ot, 0.0 if weighted[pt] is None else weighted[pt])
    parts = [(weighted[pt] * z_ref[:, pt * LANES:(pt + 1) * LANES].astype(F32)).astype(BF16)
             for pt in range(len(DIL_HEAD_PAIRS))]
    og = jnp.concatenate(parts, axis=1)
    y = _dot(og, w_ref[...])
    x2 = x_ref[...] + mod_ref[2:3, :] * y
    ms = jnp.mean(x2 * x2, axis=-1, keepdims=True)
    out_ref[...] = x2 * lax.rsqrt(ms + NORM_EPS) * fg_ref[...]


def _outproj1(o_groups, lse_groups, z, w, x, mod, final_g):
    b, t, d = x.shape
    tok = lambda k: pl.BlockSpec((None, TM, k), lambda i, j: (i, j, 0))
    full = lambda a: pl.BlockSpec(a.shape, lambda i, j: (0,) * a.ndim)
    grp = lambda a, dil: pl.BlockSpec((None, TM // dil, a.shape[2]), lambda i, j: (i, j, 0))
    dils = [dil for _, dil in DIL_PATTERNS]
    n_slabs = len(dils) * (1 + DIL_TILES_PER_GROUP)
    return pl.pallas_call(
        _outproj1_kernel,
        grid=(b, t // TM),
        in_specs=[grp(a, dl) for a, dl in zip(o_groups, dils)] + [grp(a, dl) for a, dl in zip(lse_groups, dils)]
                 + [tok(z.shape[2]), full(w), tok(d), pl.BlockSpec((None, 3, d), lambda i, j: (i, 0, 0)), full(final_g)],
        out_specs=tok(d),
        out_shape=jax.ShapeDtypeStruct((b, t, d), F32),
        scratch_shapes=[pltpu.VMEM((n_slabs, TM, LANES), F32)],
        compiler_params=_cparams("parallel", "parallel"),
        name="outproj_dil_final",
    )(*o_groups, *lse_groups, z, w, x, mod, final_g)


def _rope_tables(t):
    inv = ROPE_THETA ** (-jnp.arange(HALF, dtype=F32) / HALF)
    ang = jnp.arange(t, dtype=F32)[:, None] * inv[None, :]
    cos = jnp.tile(jnp.cos(ang), (1, 4))
    sin = jnp.tile(jnp.sin(ang), (1, 4))
    sign = jnp.where(jnp.arange(LANES) < 2 * HALF, -1.0, 1.0).astype(F32)
    return cos, sin * sign[None, :]


def _pair_rope_layout(w, n_pairs):
    d = w.shape[0]
    return w.reshape(d, n_pairs, 2, 2, HALF).transpose(0, 1, 3, 2, 4).reshape(d, n_pairs * LANES)


def _nsa_weights(w_in, pe_k, pe_v, ck_w1, ck_w2, cv_w1, cv_w2, w_out):
    d = w_in.shape[0]
    aw = N_HEADS * HEAD_DIM
    kvw = NSA_KV_HEADS * HEAD_DIM
    cuts = np.cumsum([aw] + [kvw] * 6 + [aw]).tolist()
    q, k_c, v_c, k_s, v_s, k_w, v_w, z, gl = jnp.split(w_in, cuts, axis=1)
    scale = HEAD_DIM ** -0.5
    wq = (q * scale).reshape(d, NSA_KV_HEADS, NSA_GROUP, 2, HALF).transpose(0, 2, 3, 1, 4).reshape(d, aw)
    k_lay = lambda w: _pair_rope_layout(w, 1)
    wkv = jnp.concatenate([k_lay(k_c), v_c, k_lay(k_s), v_s, k_lay(k_w), v_w], axis=1)
    wz = z.reshape(d, NSA_KV_HEADS, NSA_GROUP, HEAD_DIM).transpose(0, 2, 1, 3).reshape(d, aw)
    wo = w_out.reshape(NSA_KV_HEADS, NSA_GROUP, HEAD_DIM, -1).transpose(1, 0, 2, 3).reshape(aw, -1)
    wg = jnp.pad(gl, ((0, 0), (0, LANES - gl.shape[1])))

    def grouped(w1_half, rope_lanes):
        out = []
        for g in range(NSA_KV_HEADS):
            if rope_lanes:
                src = w1_half.reshape(CMP_STRIDE, 2, 1, HALF, CMP_HIDDEN)
                pads = ((0, 0), (0, 0), (g, NSA_KV_HEADS - 1 - g), (0, 0), (0, 0))
            else:
                src = w1_half.reshape(CMP_STRIDE, 1, HEAD_DIM, CMP_HIDDEN)
                pads = ((0, 0), (g, NSA_KV_HEADS - 1 - g), (0, 0), (0, 0))
            out.append(jnp.pad(src, pads).reshape(CMP_STRIDE * LANES, CMP_HIDDEN))
        return jnp.concatenate(out, axis=1)

    def pe_rows(pe_half, rope_lanes):
        if rope_lanes:
            v = jnp.broadcast_to(pe_half.reshape(CMP_STRIDE, 2, 1, HALF), (CMP_STRIDE, 2, NSA_KV_HEADS, HALF))
        else:
            v = jnp.broadcast_to(pe_half.reshape(CMP_STRIDE, 1, HEAD_DIM), (CMP_STRIDE, NSA_KV_HEADS, HEAD_DIM))
        return v.reshape(1, CMP_STRIDE * LANES)

    def w2_padded(w2, rope_lanes):
        out = []
        for g in range(NSA_KV_HEADS):
            if rope_lanes:
                src = w2.reshape(CMP_HIDDEN, 2, 1, HALF)
                pads = ((0, 0), (0, 0), (g, NSA_KV_HEADS - 1 - g), (0, 0))
            else:
                src = w2.reshape(CMP_HIDDEN, 1, HEAD_DIM)
                pads = ((0, 0), (g, NSA_KV_HEADS - 1 - g), (0, 0))
            out.append(jnp.pad(src, pads).reshape(CMP_HIDDEN, LANES))
        return jnp.concatenate(out, axis=0)

    k1 = ck_w1.reshape(2, CMP_STRIDE, HEAD_DIM, CMP_HIDDEN)
    v1 = cv_w1.reshape(2, CMP_STRIDE, HEAD_DIM, CMP_HIDDEN)
    compress_consts = (
        pe_rows(pe_k[:CMP_STRIDE], True), pe_rows(pe_k[CMP_STRIDE:], True),
        pe_rows(pe_v[:CMP_STRIDE], False), pe_rows(pe_v[CMP_STRIDE:], False),
        grouped(k1[0], True).astype(BF16), grouped(k1[1], True).astype(BF16),
        grouped(v1[0], False).astype(BF16), grouped(v1[1], False).astype(BF16),
        w2_padded(ck_w2, True).astype(BF16), w2_padded(cv_w2, False).astype(BF16),
    )
    return wq.astype(BF16), wkv.astype(BF16), wz.astype(BF16), wg.astype(BF16), wo.astype(BF16), compress_consts


def _dil_weights(w_in, w_out):
    d = w_in.shape[0]
    q, k, v, z = jnp.split(w_in, 4, axis=1)
    n_pairs = len(DIL_HEAD_PAIRS)
    order = [hd for pair in DIL_HEAD_PAIRS for hd in pair]

    def tile_order(w, axis):
        parts = []
        for hd in order:
            sl = [slice(None)] * w.ndim
            sl[axis] = slice(hd * HEAD_DIM, (hd + 1) * HEAD_DIM)
            parts.append(w[tuple(sl)])
        return jnp.concatenate(parts, axis=axis)

    scale = HEAD_DIM ** -0.5
    wq = _pair_rope_layout(tile_order(q * scale, 1), n_pairs)
    wk = _pair_rope_layout(tile_order(k, 1), n_pairs)
    return (wq.astype(BF16), wk.astype(BF16), tile_order(v, 1).astype(BF16), tile_order(z, 1).astype(BF16),
            tile_order(w_out, 0).astype(BF16))


def _selection_constants(t):
    n_cmp_rows = t // CMP_STRIDE
    n_slc = t // SLC_LEN
    c_start = CMP_STRIDE * np.arange(n_cmp_rows)
    s_start = SLC_LEN * np.arange(n_slc)
    ovl_t = ((c_start[None, :] < s_start[:, None] + SLC_LEN)
             & (c_start[None, :] + CMP_LEN > s_start[:, None])).astype(np.float32)
    ovl_t[:, (t - CMP_LEN) // CMP_STRIDE + 1:] = 0.0
    e = np.zeros((NSA_KV_HEADS, t // KEY_CHUNK, LANES, KEY_CHUNK), np.float32)
    keys = np.arange(t)
    for g in range(NSA_KV_HEADS):
        e[g, keys // KEY_CHUNK, g * n_slc + keys // SLC_LEN, keys % KEY_CHUNK] = 1.0
    return jnp.asarray(ovl_t), jnp.asarray(e, dtype=BF16)


@jax.jit
def kernel(x, c, norm_g, ada_w, ada_b, nsa_w_in, nsa_pe_k, nsa_pe_v, nsa_ck_w1, nsa_ck_w2,
           nsa_cv_w1, nsa_cv_w2, nsa_w_out, dil_w_in, dil_w_out, final_g):
    b, t, d = x.shape
    assert t % SLC_CK == 0 and t % TM == 0 and d % LANES == 0
    assert NSA_KV_HEADS * (t // SLC_LEN) <= LANES and t // CMP_STRIDE == LANES
    mod = _adaln_mod(c, ada_w, ada_b).reshape(ada_w.shape[0], b, 3, d)
    cos, sin = _rope_tables(t)

    wq, wkv, wz, wg, wo, compress_consts = _nsa_weights(
        nsa_w_in[0], nsa_pe_k[0], nsa_pe_v[0], nsa_ck_w1[0], nsa_ck_w2[0], nsa_cv_w1[0], nsa_cv_w2[0], nsa_w_out[0])
    qp, qr, kc, vc, ks, vs, kw, vw, z, gl = _inproj0(x, mod[0], norm_g[0:1], cos, sin, wq, wkv, wz, wg)
    kcmp, vcmp = _compress(kc, vc, compress_consts)
    ovl_t, e_sel = _selection_constants(t)
    ocmp, msel = _cmp_attn(qp, kcmp, vcmp, ovl_t, gl)
    og = _nsa_attn(qr, ks, vs, kw, vw, msel, e_sel, ocmp, z, gl)

    dq, dk, dv, dz, dwo = _dil_weights(dil_w_in[0], dil_w_out[0])
    n_grp = len(DIL_PATTERNS)
    x1, *qkv, z1 = _outproj0_inproj1(og, wo, x, mod[0], mod[1], norm_g[1:2], cos, sin, dq, dk, dv, dz)
    assert all(win // dil == DIL_WIN and TM % (dil * 16) == 0 for win, dil in DIL_PATTERNS)
    o_groups, lse_groups = _dil_attn(qkv[0:n_grp], qkv[n_grp:2 * n_grp], qkv[2 * n_grp:3 * n_grp], t)
    return _outproj1(o_groups, lse_groups, z1, dwo, x1, mod[1], final_g.reshape(1, d))
```

```python
import functools
import math

import numpy as np
import jax
import jax.numpy as jnp
from jax import lax
from jax.experimental import pallas as pl
from jax.experimental.pallas import tpu as pltpu

F32 = jnp.float32
BF16 = jnp.bfloat16
HIGHEST = lax.Precision.HIGHEST

HEAD_DIM = 64
HALF = HEAD_DIM // 2
N_HEADS = 16
ROPE_THETA = 10000.0
NORM_EPS = 1e-6
NSA_KV_HEADS = 2
NSA_GROUP = N_HEADS // NSA_KV_HEADS
NSA_BRANCHES = 3
CMP_LEN = 32
CMP_STRIDE = 16
CMP_HIDDEN = 256
SLC_LEN = 64
SLC_TOP = 16
WIN_LEN = 512
DIL_PATTERNS = ((128, 1), (512, 4), (2048, 16))
DIL_GROUP_HEADS = (6, 5, 5)
DIL_WIN = 128


def _dil_tile_plan():
    offs = np.cumsum((0,) + DIL_GROUP_HEADS)
    pairs, group_tiles, leftovers = [], [], []
    for gi, hn in enumerate(DIL_GROUP_HEADS):
        heads = list(range(offs[gi], offs[gi + 1]))
        group_tiles.append([])
        for i in range(0, hn - hn % 2, 2):
            group_tiles[gi].append((len(pairs), (0, 1)))
            pairs.append((heads[i], heads[i + 1]))
        if hn % 2:
            leftovers.append((gi, heads[-1]))
    assert len(leftovers) % 2 == 0
    for (ga, ha), (gb, hb) in zip(leftovers[0::2], leftovers[1::2]):
        group_tiles[ga].append((len(pairs), (0,)))
        group_tiles[gb].append((len(pairs), (1,)))
        pairs.append((ha, hb))
    return tuple(pairs), tuple(tuple(tiles) for tiles in group_tiles)


DIL_HEAD_PAIRS, DIL_GROUP_TILES = _dil_tile_plan()
DIL_TILES_PER_GROUP = len(DIL_GROUP_TILES[0])
assert all(len(tiles) == DIL_TILES_PER_GROUP for tiles in DIL_GROUP_TILES)

LANES = 128
VMEM_LIMIT_BYTES = 48 * 1024 * 1024

NEG = -1e30
TQ = 128
KEY_CHUNK = 128
SLC_CK = 512
TM = 512
DIL_STEP_TOKENS = 1024
NSA_STEP_TOKENS = 256
CMP_STEP_TOKENS = 1024


def _cparams(*sem):
    return pltpu.CompilerParams(dimension_semantics=sem, vmem_limit_bytes=VMEM_LIMIT_BYTES)


def _dot(a, b):
    return jnp.dot(a, b, preferred_element_type=F32)


def _dot_t(a, b):
    return lax.dot_general(a, b, (((1,), (1,)), ((), ())), preferred_element_type=F32)


def _silu(v):
    return v * jax.nn.sigmoid(v)


def _mod_kernel(c_ref, w_ref, b_ref, o_ref):
    s = _silu(c_ref[...])
    o_ref[...] = jnp.dot(s, w_ref[...], precision=HIGHEST, preferred_element_type=F32) + b_ref[...]


def _adaln_mod(c, ada_w, ada_b):
    depth, d, n3 = ada_w.shape
    b = c.shape[0]
    tn = 1024
    return pl.pallas_call(
        _mod_kernel,
        grid=(depth, n3 // tn),
        in_specs=[
            pl.BlockSpec((b, d), lambda i, j: (0, 0)),
            pl.BlockSpec((None, d, tn), lambda i, j: (i, 0, j)),
            pl.BlockSpec((None, 1, tn), lambda i, j: (i, 0, j)),
        ],
        out_specs=pl.BlockSpec((None, b, tn), lambda i, j: (i, 0, j)),
        out_shape=jax.ShapeDtypeStruct((depth, b, n3), F32),
        compiler_params=_cparams("arbitrary", "arbitrary"),
        name="adaln_mod",
    )(c, ada_w, ada_b.reshape(depth, 1, n3))


def _modulated_norm(x, mod_ref, g_ref):
    ms = jnp.mean(x * x, axis=-1, keepdims=True)
    y = x * lax.rsqrt(ms + NORM_EPS) * g_ref[...]
    return y * (1.0 + mod_ref[1:2, :]) + mod_ref[0:1, :]


def _rope_tile(v, cos, sin):
    return v * cos + pltpu.roll(v, 2 * HALF, axis=1) * sin


def _store_key_chunks_t(kt_ref, k):
    dim = lax.broadcasted_iota(jnp.int32, (LANES, KEY_CHUNK), 0)
    for c in range(kt_ref.shape[1]):
        kt = k[c * KEY_CHUNK:(c + 1) * KEY_CHUNK, :].T
        for g in range(kt_ref.shape[0]):
            kt_ref[g, c] = jnp.where((dim % HEAD_DIM) // HALF == g, kt, 0.0).astype(BF16)


def _inproj0_kernel(x_ref, mod_ref, g_ref, cos_ref, sin_ref, wq_ref, wkv_ref, wz_ref, wg_ref,
                    qp_ref, qr_ref, kc_ref, vc_ref, ks_ref, vs_ref, kw_ref, vw_ref, z_ref, gl_ref, stage):
    tm = x_ref.shape[0]
    h = _modulated_norm(x_ref[...], mod_ref, g_ref).astype(BF16)
    cos = cos_ref[...]
    sin = sin_ref[...]
    q = _dot(h, wq_ref[...])
    for j in range(q.shape[1] // LANES):
        sl = slice(j * LANES, (j + 1) * LANES)
        qp_ref[:, sl] = q[:, sl].astype(BF16)
        qr_ref[:, sl] = _rope_tile(q[:, sl], cos, sin).astype(BF16)
    kv = _dot(h, wkv_ref[...])
    tiles = [kv[:, i * LANES:(i + 1) * LANES] for i in range(6)]
    for slab, (tile, o_ref) in enumerate(((tiles[0], kc_ref), (tiles[1], vc_ref))):
        stage[slab] = tile
        for r in range(CMP_STRIDE):
            rows = stage[slab, pl.ds(r, tm // CMP_STRIDE, stride=CMP_STRIDE), :]
            o_ref[:, r * LANES:(r + 1) * LANES] = rows.astype(BF16)
    _store_key_chunks_t(ks_ref, _rope_tile(tiles[2], cos, sin))
    vs_ref[...] = tiles[3].astype(BF16)
    _store_key_chunks_t(kw_ref, _rope_tile(tiles[4], cos, sin))
    vw_ref[...] = tiles[5].astype(BF16)
    z_ref[...] = _silu(_dot(h, wz_ref[...])).astype(BF16)
    gl_ref[...] = _dot(h, wg_ref[...])


def _inproj0(x, mod, g, cos, sin, wq, wkv, wz, wg):
    b, t, d = x.shape
    nq, nz = wq.shape[1], wz.shape[1]
    tok = lambda n: pl.BlockSpec((None, TM, n), lambda i, j: (i, j, 0))
    full = lambda a: pl.BlockSpec(a.shape, lambda i, j: (0,) * a.ndim)
    tab = pl.BlockSpec((TM, LANES), lambda i, j: (j, 0))
    shp = lambda n, dt: jax.ShapeDtypeStruct((b, t, n), dt)
    kt = pl.BlockSpec((None, NSA_KV_HEADS, TM // KEY_CHUNK, LANES, KEY_CHUNK), lambda i, j: (i, 0, j, 0, 0))
    kt_shp = jax.ShapeDtypeStruct((b, NSA_KV_HEADS, t // KEY_CHUNK, LANES, KEY_CHUNK), BF16)
    kv = tok(LANES)
    kv_shp = shp(LANES, BF16)
    cm = pl.BlockSpec((None, TM // CMP_STRIDE, CMP_STRIDE * LANES), lambda i, j: (i, j, 0))
    cm_shp = jax.ShapeDtypeStruct((b, t // CMP_STRIDE, CMP_STRIDE * LANES), BF16)
    return pl.pallas_call(
        _inproj0_kernel,
        grid=(b, t // TM),
        in_specs=[tok(d), pl.BlockSpec((None, 3, d), lambda i, j: (i, 0, 0)), full(g), tab, tab,
                  full(wq), full(wkv), full(wz), full(wg)],
        out_specs=[tok(nq), tok(nq), cm, cm, kt, kv, kt, kv, tok(nz), tok(LANES)],
        out_shape=[shp(nq, BF16), shp(nq, BF16), cm_shp, cm_shp, kt_shp, kv_shp, kt_shp, kv_shp,
                   shp(nz, BF16), shp(LANES, F32)],
        scratch_shapes=[pltpu.VMEM((2, TM, LANES), F32)],
        compiler_params=_cparams("parallel", "parallel"),
        name="inproj_nsa",
    )(x, mod, g, cos, sin, wq, wkv, wz, wg)


def _compress_kernel(ak_ref, av_ref, pekt_ref, pekb_ref, pevt_ref, pevb_ref,
                     w1kt_ref, w1kb_ref, w1vt_ref, w1vb_ref, w2k_ref, w2v_ref, kcmp_ref, vcmp_ref):
    def mlp(a_ref, pet_ref, peb_ref, w1t_ref, w1b_ref, w2_ref):
        a = a_ref[...].astype(F32)
        top = _dot((a + pet_ref[...]).astype(BF16), w1t_ref[...])
        bot = _dot((a + peb_ref[...]).astype(BF16), w1b_ref[...])
        n = bot.shape[0]
        hid = top + pltpu.roll(bot, n - 1, axis=0)
        return _dot(_silu(hid).astype(BF16), w2_ref[...])

    kcmp_ref[...] = mlp(ak_ref, pekt_ref, pekb_ref, w1kt_ref, w1kb_ref, w2k_ref).astype(BF16)
    vcmp_ref[...] = mlp(av_ref, pevt_ref, pevb_ref, w1vt_ref, w1vb_ref, w2v_ref).astype(BF16)


def _compress(ak, av, consts):
    b, n, w = ak.shape
    full = lambda a: pl.BlockSpec(a.shape, lambda i: (0,) * a.ndim)
    blk = pl.BlockSpec((None, n, w), lambda i: (i, 0, 0))
    out = pl.BlockSpec((None, n, LANES), lambda i: (i, 0, 0))
    return pl.pallas_call(
        _compress_kernel,
        grid=(b,),
        in_specs=[blk, blk] + [full(a) for a in consts],
        out_specs=[out, out],
        out_shape=[jax.ShapeDtypeStruct((b, n, LANES), BF16)] * 2,
        compiler_params=_cparams("parallel"),
        name="compress",
    )(ak, av, *consts)


def _stack_heads(q_ref, rows, lane_sets):
    n_tiles = q_ref.shape[1] // LANES
    parts = []
    for j in range(n_tiles):
        qj = q_ref[rows, j * LANES:(j + 1) * LANES]
        parts.append(jnp.where(lane_sets, qj, jnp.zeros_like(qj)))
    return jnp.concatenate(parts, axis=0)


def _head_gate_rows(gates, g, branches, lane):
    blocks = []
    for j in range(NSA_GROUP):
        base = (g * NSA_GROUP + j) * NSA_BRANCHES
        kept = jnp.zeros_like(gates)
        for br in branches:
            kept = jnp.where(lane == base + br, gates, kept)
        hi = kept.astype(BF16)
        blocks.append(jnp.concatenate([hi, (kept - hi.astype(F32)).astype(BF16)], axis=1))
    return jnp.concatenate(blocks, axis=0)


def _gate_spread(branches):
    col = np.arange(2 * LANES) % LANES
    spread = np.zeros((2 * LANES, LANES * len(branches)), np.float32)
    for i, br in enumerate(branches):
        spread[col % NSA_BRANCHES == br, i * LANES:(i + 1) * LANES] = 1.0
    return jnp.asarray(spread, dtype=BF16)


def _cmp_attn_kernel(qp_ref, kcmp_ref, vcmp_ref, ovl_ref, gl_ref, ocmp_ref, msel_ref, *, n_slc, tq):
    n_cmp = vcmp_ref.shape[0]
    hg = qp_ref.shape[1] // LANES
    n_sub = qp_ref.shape[0] // tq
    assert n_cmp == LANES and tq == LANES
    row = lax.broadcasted_iota(jnp.int32, (LANES, tq), 0)
    col = lax.broadcasted_iota(jnp.int32, (LANES, tq), 1)
    jb = lax.broadcasted_iota(jnp.int32, (n_slc, tq), 0)
    tok_rows = lambda sub: slice(sub * tq, (sub + 1) * tq)
    t0 = lambda sub: pl.program_id(1) * qp_ref.shape[0] + sub * tq
    units = [(sub, g) for sub in range(n_sub) for g in range(NSA_KV_HEADS)]
    eye = jnp.where(row == col, 1.0, 0.0).astype(BF16)
    k_sel = jnp.concatenate([kcmp_ref[...], eye], axis=1)
    v_t = vcmp_ref[...].T

    q_t = [[qp_ref[tok_rows(sub), j * LANES:(j + 1) * LANES].T for j in range(hg)] for sub in range(n_sub)]
    gates_t = [jax.nn.sigmoid(gl_ref[tok_rows(sub), :]).T for sub in range(n_sub)]
    seen = [jnp.concatenate([jnp.where(CMP_STRIDE * row + (CMP_LEN - 1) <= t0(sub) + col, 0.0, NEG).astype(BF16)] * hg,
                            axis=1) for sub in range(n_sub)]
    weights = {}
    for sub, g in units:
        in_group = ((row % HEAD_DIM) // HALF) == g
        q_cols = jnp.concatenate([jnp.where(in_group, v, jnp.zeros_like(v)) for v in q_t[sub]], axis=1)
        s = _dot(k_sel, jnp.concatenate([q_cols, seen[sub]], axis=0))
        m = jnp.max(s, axis=0, keepdims=True)
        m = jnp.where(m > 0.5 * NEG, m, 0.0)
        e = jnp.exp(s - m)
        den = jnp.sum(e, axis=0, keepdims=True)
        weights[sub, g] = e * (1.0 / jnp.where(den > 0, den, 1.0))
    outs, imps = {}, {}
    for sub, g in units:
        p = weights[sub, g]
        gate = jnp.concatenate([gates_t[sub][(g * NSA_GROUP + j) * NSA_BRANCHES:(g * NSA_GROUP + j) * NSA_BRANCHES + 1, :]
                                for j in range(hg)], axis=1)
        outs[sub, g] = gate * _dot(v_t, p.astype(BF16))
        p_sum = p[:, 0:tq]
        for j in range(1, hg):
            p_sum = p_sum + p[:, j * tq:(j + 1) * tq]
        imps[sub, g] = jnp.dot(ovl_ref[...], p_sum, precision=HIGHEST, preferred_element_type=F32)
    picked = {}
    for sub, g in units:
        cur = (t0(sub) + lax.broadcasted_iota(jnp.int32, (n_slc, tq), 1)) // SLC_LEN
        visible = jb <= cur
        forced = (jb == 0) | (jb == cur) | (jb == cur - 1)
        rank = jnp.where(forced, -NEG, jnp.where(visible, imps[sub, g], NEG))
        cnt = jnp.zeros((n_slc, tq), F32)
        for jp in range(n_slc):
            rj = rank[jp:jp + 1, :]
            tie = jnp.where(jb > jp, 1.0, 0.0)
            cnt = cnt + jnp.where(rj > rank, 1.0, jnp.where(rj == rank, tie, 0.0))
        picked[sub, g] = jnp.where(visible, jnp.where(cnt < SLC_TOP, 1.0, 0.0), 0.0)
    pad = jnp.zeros((LANES - NSA_KV_HEADS * n_slc, tq), F32)
    for sub in range(n_sub):
        for j in range(hg):
            cols = slice(j * tq, (j + 1) * tq)
            o_t = jnp.where(row < HEAD_DIM, outs[sub, 0][:, cols], outs[sub, 1][:, cols])
            ocmp_ref[tok_rows(sub), j * LANES:(j + 1) * LANES] = o_t.T.astype(BF16)
        sel_t = jnp.concatenate([picked[sub, g] for g in range(NSA_KV_HEADS)] + [pad], axis=0)
        msel_ref[tok_rows(sub), :] = sel_t.T.astype(BF16)


def _cmp_attn(qp, kcmp, vcmp, ovl_t, gl):
    b, t, nq = qp.shape
    n_cmp = kcmp.shape[1]
    n_slc = t // SLC_LEN
    tok = lambda n: pl.BlockSpec((None, CMP_STEP_TOKENS, n), lambda i, j: (i, j, 0))
    per_b = pl.BlockSpec((None, n_cmp, LANES), lambda i, j: (i, 0, 0))
    full = lambda a: pl.BlockSpec(a.shape, lambda i, j: (0,) * a.ndim)
    return pl.pallas_call(
        functools.partial(_cmp_attn_kernel, n_slc=n_slc, tq=TQ),
        grid=(b, t // CMP_STEP_TOKENS),
        in_specs=[tok(nq), per_b, per_b, full(ovl_t), tok(LANES)],
        out_specs=[tok(nq), tok(LANES)],
        out_shape=[jax.ShapeDtypeStruct((b, t, nq), BF16), jax.ShapeDtypeStruct((b, t, LANES), BF16)],
        compiler_params=_cparams("parallel", "parallel"),
        name="cmp_attn_select",
    )(qp, kcmp, vcmp, ovl_t, gl)


def _nsa_attn_kernel(qr_ref, *refs, tq, **kw):
    n_sub = qr_ref.shape[0] // tq
    for sub in range(n_sub):
        _nsa_attn_tile(qr_ref, *refs, tok=slice(sub * tq, (sub + 1) * tq), qi=pl.program_id(1) * n_sub + sub, **kw)


def _nsa_attn_tile(qr_ref, ks_ref, vs_ref, kw_ref, vw_ref, msel_ref, e_ref, gsp_ref, ocmp_ref, z_ref, gl_ref,
                   og_ref, m_scr, acc_scr, *, tok, qi, n_slc, ck):
    tq = tok.stop - tok.start
    hg = qr_ref.shape[1] // LANES
    rows = hg * tq
    n_wc = WIN_LEN // tq + 1
    per_ck = ck // KEY_CHUNK
    t0 = qi * tq
    lane = lax.broadcasted_iota(jnp.int32, (tq, LANES), 1)
    a_idx = lax.broadcasted_iota(jnp.int32, (tq, LANES), 0)

    eye = jnp.where(lane == a_idx, 1.0, 0.0).astype(BF16)
    earlier = jnp.where((lane % n_slc) < t0 // SLC_LEN, 0.0, NEG)
    mneg = jnp.where(msel_ref[tok, :].astype(F32) > 0.5, earlier, NEG).astype(BF16)
    mneg_rows = jnp.concatenate([mneg] * hg, axis=0)
    eye_rows = jnp.concatenate([eye] * hg, axis=0)

    def with_selector(sel_rows):
        q_rows = jnp.concatenate([qr_ref[tok, j * LANES:(j + 1) * LANES] for j in range(hg)], axis=0)
        return jnp.concatenate([q_rows, sel_rows], axis=1)

    zero_b = jnp.zeros((tq, LANES), BF16)
    neg_b = jnp.full((tq, LANES), NEG, BF16)
    causal_b = jnp.where(lane <= a_idx, 0.0, NEG).astype(BF16)
    far_b = jnp.where(lane > a_idx, 0.0, NEG).astype(BF16)
    own_half = [lane < HEAD_DIM, lane >= HEAD_DIM]

    def values_with_ones(v, g):
        in_own_half = (lax.broadcasted_iota(jnp.int32, v.shape, 1) // HEAD_DIM) == g
        return jnp.where(in_own_half, v, jnp.ones_like(v))

    def lane_tiles(v):
        return [v[:, i * LANES:(i + 1) * LANES] for i in range(v.shape[1] // LANES)]

    def tile_max(tiles):
        mx = tiles[0]
        for v in tiles[1:]:
            mx = jnp.maximum(mx, v)
        return jnp.max(mx, axis=-1, keepdims=True)

    def probs(tiles, m):
        return jnp.concatenate([jnp.exp(v - m) for v in tiles], axis=1).astype(BF16)

    m_scr[...] = jnp.full(m_scr.shape, NEG, F32)
    acc_scr[...] = jnp.zeros(acc_scr.shape, F32)

    def attend_chunks(c0, n):
        k0 = pl.multiple_of(c0 * KEY_CHUNK, KEY_CHUNK)
        q_sel = with_selector(mneg_rows)
        scores = []
        for g in range(NSA_KV_HEADS):
            kt = jnp.concatenate([ks_ref[g, c0 + i] for i in range(n)], axis=1)
            eb = jnp.concatenate([e_ref[g, c0 + i] for i in range(n)], axis=1)
            scores.append(_dot(q_sel, jnp.concatenate([kt, eb], axis=0)))
        groups = range(NSA_KV_HEADS)
        m_old = [m_scr[g] for g in groups]
        m_new = [jnp.maximum(m_old[g], tile_max(lane_tiles(scores[g]))) for g in groups]
        p = [probs(lane_tiles(scores[g]), m_new[g]) for g in groups]
        pv = [_dot(p[g], values_with_ones(vs_ref[pl.ds(k0, n * KEY_CHUNK), :], g)) for g in groups]
        for g in groups:
            acc_scr[g] = jnp.exp(m_old[g] - m_new[g]) * acc_scr[g] + pv[g]
            m_scr[g] = m_new[g]

    def earlier_keys(kc, carry):
        attend_chunks(kc * per_ck, per_ck)
        return carry

    lax.fori_loop(0, qi // per_ck, earlier_keys, 0)
    for rem in range(1, per_ck):
        @pl.when(qi % per_ck == rem)
        def _():
            attend_chunks((qi // per_ck) * per_ck, rem)

    kd = pl.multiple_of(t0, tq)
    zeros_v = jnp.zeros((tq, LANES), BF16)
    win_chunks = [qi - (n_wc - 1 - c) for c in range(n_wc - 1)]
    win_bias = [jnp.where(cidx >= 0, far_b if c == 0 else zero_b, neg_b) for c, cidx in enumerate(win_chunks)]
    pairs = [(c, c + 1) for c in range(0, n_wc - 1, 2)]
    q_eye = with_selector(eye_rows)
    s_own, scores = [], []
    for g in range(NSA_KV_HEADS):
        own_k = jnp.concatenate([jnp.concatenate([kw_ref[g, qi], causal_b], axis=0),
                                 jnp.concatenate([ks_ref[g, qi], causal_b], axis=0)], axis=1)
        kbs = [jnp.concatenate([kw_ref[g, jnp.maximum(cidx, 0)], win_bias[c]], axis=0)
               for c, cidx in enumerate(win_chunks)]
        s_own.append(_dot(q_eye, own_k))
        scores.append([_dot(q_eye, jnp.concatenate([kbs[c] for c in pr], axis=1)) for pr in pairs])

    groups = range(NSA_KV_HEADS)
    s_win_own = [s_own[g][:, 0:tq] for g in groups]
    s_slc_own = [s_own[g][:, tq:2 * tq] for g in groups]
    m_win = [tile_max([v for s in scores[g] for v in lane_tiles(s)] + [s_win_own[g]]) for g in groups]
    m_old = [m_scr[g] for g in groups]
    m_slc = [jnp.maximum(m_old[g], jnp.max(s_slc_own[g], axis=-1, keepdims=True)) for g in groups]
    p_all = [jnp.concatenate([probs(lane_tiles(s), m_win[g]) for s in scores[g]]
                             + [probs([s_win_own[g]], m_win[g]), probs([s_slc_own[g]], m_slc[g])], axis=1) for g in groups]
    acc_slc, acc_win = [], []
    for g in groups:
        vbs = [values_with_ones(vw_ref[pl.ds(pl.multiple_of(jnp.maximum(cidx, 0) * tq, tq), tq), :], g)
               for cidx in win_chunks] + [values_with_ones(vw_ref[pl.ds(kd, tq), :], g)]
        win_v = [jnp.concatenate([v, zeros_v], axis=1) for v in vbs]
        slc_v = jnp.concatenate([zeros_v, values_with_ones(vs_ref[pl.ds(kd, tq), :], g)], axis=1)
        pv = _dot(p_all[g], jnp.concatenate(win_v + [slc_v], axis=0))
        acc_win.append(pv[:, 0:LANES])
        acc_slc.append(jnp.exp(m_old[g] - m_slc[g]) * acc_scr[g] + pv[:, LANES:2 * LANES])

    gates = jax.nn.sigmoid(gl_ref[tok, :])
    row_gates = [_dot(_head_gate_rows(gates, g, (1, 2), lane), gsp_ref[...]) for g in range(NSA_KV_HEADS)]

    def gated(accs, rs, branch_lanes):
        num = [accs[g][rs] * row_gates[g][rs, branch_lanes] for g in range(NSA_KV_HEADS)]
        sums = pltpu.roll(jnp.where(own_half[0], accs[1][rs], accs[0][rs]), HEAD_DIM, axis=1)
        return jnp.where(own_half[0], num[0], num[1]) / sums

    for j in range(hg):
        rs = slice(j * tq, (j + 1) * tq)
        cols = slice(j * LANES, (j + 1) * LANES)
        o = (ocmp_ref[tok, cols].astype(F32) + gated(acc_slc, rs, slice(0, LANES))
             + gated(acc_win, rs, slice(LANES, 2 * LANES)))
        og_ref[tok, cols] = (o * z_ref[tok, cols].astype(F32)).astype(BF16)


def _nsa_attn(qr, ks, vs, kw, vw, msel, e, ocmp, z, gl):
    b, t, nq = qr.shape
    gsp = _gate_spread((1, 2))
    hg = nq // LANES
    rows = hg * TQ
    assert TQ == KEY_CHUNK and WIN_LEN % TQ == 0
    tok = lambda n: pl.BlockSpec((None, NSA_STEP_TOKENS, n), lambda i, j: (i, j, 0))
    per_b = pl.BlockSpec((None, t, LANES), lambda i, j: (i, 0, 0))
    per_b_t = pl.BlockSpec((None, NSA_KV_HEADS, t // KEY_CHUNK, LANES, KEY_CHUNK), lambda i, j: (i, 0, 0, 0, 0))
    full = lambda a: pl.BlockSpec(a.shape, lambda i, j: (0,) * a.ndim)
    return pl.pallas_call(
        functools.partial(_nsa_attn_kernel, tq=TQ, n_slc=t // SLC_LEN, ck=SLC_CK),
        grid=(b, t // NSA_STEP_TOKENS),
        in_specs=[tok(nq), per_b_t, per_b, per_b_t, per_b, tok(LANES), full(e), full(gsp), tok(nq), tok(nq), tok(LANES)],
        out_specs=tok(nq),
        out_shape=jax.ShapeDtypeStruct((b, t, nq), BF16),
        scratch_shapes=[
            pltpu.VMEM((NSA_KV_HEADS, rows, LANES), F32),
            pltpu.VMEM((NSA_KV_HEADS, rows, LANES), F32),
        ],
        compiler_params=_cparams("parallel", "arbitrary"),
        name="nsa_slc_win_attn",
    )(qr, ks, vs, kw, vw, msel, e, gsp, ocmp, z, gl)


def _inproj1_kernel(og_ref, wo_ref, x_ref, mod0_ref, mod_ref, g_ref, cos_ref, sin_ref, wq_ref, wk_ref, wv_ref, wz_ref,
                    x1_ref, *refs):
    n_grp = len(DIL_PATTERNS)
    outs = [refs[a * n_grp:(a + 1) * n_grp] for a in range(3)]
    z_ref, stage = refs[3 * n_grp], refs[3 * n_grp + 1]
    tm = x_ref.shape[0]
    gw = DIL_TILES_PER_GROUP * LANES
    x1 = x_ref[...] + mod0_ref[2:3, :] * _dot(og_ref[...], wo_ref[...])
    x1_ref[...] = x1
    h = _modulated_norm(x1, mod_ref, g_ref).astype(BF16)
    cos = cos_ref[...]
    sin = sin_ref[...]
    projected = [_dot(h, w_ref[...]) for w_ref in (wq_ref, wk_ref, wv_ref, wz_ref)]
    for a, u in enumerate(projected[0:3]):
        tiles = [u[:, pt * LANES:(pt + 1) * LANES] for pt in range(len(DIL_HEAD_PAIRS))]
        if a < 2:
            tiles = [_rope_tile(v, cos, sin) for v in tiles]
        for gi, (_, dil) in enumerate(DIL_PATTERNS):
            for jt, (pt, _) in enumerate(DIL_GROUP_TILES[gi]):
                tile = tiles[pt]
                if dil == 1:
                    outs[a][gi][:, jt * LANES:(jt + 1) * LANES] = tile.astype(BF16)
                    continue
                slab = (a * n_grp + gi) * DIL_TILES_PER_GROUP + jt
                stage[slab] = tile
                for r in range(dil):
                    rows = stage[slab, pl.ds(r, tm // dil, stride=dil), :]
                    outs[a][gi][:, r * gw + jt * LANES:r * gw + (jt + 1) * LANES] = rows.astype(BF16)
    z_ref[...] = _silu(projected[3]).astype(BF16)


def _outproj0_inproj1(og, wo, x, mod0, mod, g, cos, sin, wq, wk, wv, wz):
    b, t, d = x.shape
    n = wq.shape[1]
    gw = DIL_TILES_PER_GROUP * LANES
    tok = lambda k: pl.BlockSpec((None, TM, k), lambda i, j: (i, j, 0))
    full = lambda a: pl.BlockSpec(a.shape, lambda i, j: (0,) * a.ndim)
    tab = pl.BlockSpec((TM, LANES), lambda i, j: (j, 0))
    mods = pl.BlockSpec((None, 3, d), lambda i, j: (i, 0, 0))
    grp_specs = [pl.BlockSpec((None, TM // dil, dil * gw), lambda i, j: (i, j, 0)) for _, dil in DIL_PATTERNS]
    grp_shapes = [jax.ShapeDtypeStruct((b, t // dil, dil * gw), BF16) for _, dil in DIL_PATTERNS]
    n_slabs = 3 * len(DIL_PATTERNS) * DIL_TILES_PER_GROUP
    return pl.pallas_call(
        _inproj1_kernel,
        grid=(b, t // TM),
        in_specs=[tok(og.shape[2]), full(wo), tok(d), mods, mods, full(g), tab, tab,
                  full(wq), full(wk), full(wv), full(wz)],
        out_specs=[tok(d)] + grp_specs * 3 + [tok(n)],
        out_shape=[jax.ShapeDtypeStruct((b, t, d), F32)] + grp_shapes * 3 + [jax.ShapeDtypeStruct((b, t, n), BF16)],
        scratch_shapes=[pltpu.VMEM((n_slabs, TM, LANES), F32)],
        compiler_params=_cparams("parallel", "parallel"),
        name="outproj_nsa_inproj_dil",
    )(og, wo, x, mod0, mod, g, cos, sin, wq, wk, wv, wz)


def _dil_attn_kernel(*refs, plans):
    n_grp = len(plans)
    step = pl.program_id(1)
    for gi, (members, tq, n_seq, blocks_per_seq) in enumerate(plans):
        q_ref, k_ref, v_ref = refs[3 * gi:3 * gi + 3]
        o_ref, lse_ref = refs[3 * n_grp + 2 * gi:3 * n_grp + 2 * gi + 2]
        _dil_attn_group(q_ref, k_ref, v_ref, o_ref, lse_ref, step % blocks_per_seq, members, tq, n_seq)


def _dil_attn_group(q_ref, k_ref, v_ref, o_ref, lse_ref, i, members, tq, n_seq):
    n_heads = sum(len(m) for m in members)
    tq_blk = q_ref.shape[0]
    gw = q_ref.shape[1] // n_seq
    n_tiles = gw // LANES
    n_sub = tq_blk // tq
    assert tq == LANES
    row = lax.broadcasted_iota(jnp.int32, (LANES, tq), 0)
    col = lax.broadcasted_iota(jnp.int32, (LANES, tq), 1)
    head_a = ((row % HEAD_DIM) // HALF) == 0
    diag_b = jnp.where(row <= col, 0.0, NEG)
    prev_b = jnp.where(row >= col, 0.0, NEG)
    neg_b = jnp.full((LANES, tq), NEG, F32)
    two = lambda v: jnp.concatenate([v, v], axis=1)
    units = [(sq, sub, jt) for sq in range(n_seq) for sub in range(n_sub) for jt in range(n_tiles)]
    rows_of = lambda sub: slice(sub * tq, (sub + 1) * tq)
    cols_of = lambda sq, jt: slice(sq * gw + jt * LANES, sq * gw + (jt + 1) * LANES)
    key_rows = lambda c: pl.ds(pl.multiple_of(jnp.maximum(i * n_sub + c, 0) * tq, tq), tq)
    q_cols, v_t = {}, {}
    for sq, sub, jt in units:
        q_t = q_ref[rows_of(sub), cols_of(sq, jt)].T
        zero = jnp.zeros_like(q_t)
        q_cols[sq, sub, jt] = jnp.concatenate([jnp.where(head_a, q_t, zero), jnp.where(head_a, zero, q_t)], axis=1)
        v_t[sq, sub, jt] = v_ref[key_rows(sub), cols_of(sq, jt)].T
        if sub == 0:
            v_t[sq, -1, jt] = v_ref[key_rows(-1), cols_of(sq, jt)].T
    probs_t, stats = {}, {}
    for sq, sub, jt in units:
        prev_bias = prev_b if sub > 0 else jnp.where(i > 0, prev_b, neg_b)
        bias = jnp.concatenate([two(prev_bias), two(diag_b)], axis=0)
        keys = jnp.concatenate([k_ref[key_rows(sub - 1), cols_of(sq, jt)], k_ref[key_rows(sub), cols_of(sq, jt)]], axis=0)
        s = _dot(keys, q_cols[sq, sub, jt]) + bias
        m = jnp.max(s, axis=0, keepdims=True)
        p = jnp.exp(s - m)
        l = jnp.sum(p, axis=0, keepdims=True)
        probs_t[sq, sub, jt] = p.astype(BF16)
        stats[sq, sub, jt] = (m, l)
    for sq, sub, jt in units:
        m, l = stats[sq, sub, jt]
        o_t = _dot(jnp.concatenate([v_t[sq, sub - 1, jt], v_t[sq, sub, jt]], axis=1), probs_t[sq, sub, jt])
        inv = 1.0 / l
        o_t = jnp.where(row < HEAD_DIM, o_t[:, 0:tq] * inv[:, 0:tq], o_t[:, tq:2 * tq] * inv[:, tq:2 * tq])
        o_ref[rows_of(sub), cols_of(sq, jt)] = o_t.T.astype(BF16)
    for sq in range(n_seq):
        for sub in range(n_sub):
            lses = []
            for jt in range(n_tiles):
                m, l = stats[sq, sub, jt]
                lse = m + jnp.log(l)
                lses += [lse[:, mb * tq:(mb + 1) * tq] for mb in members[jt]]
            top = lses[0]
            for v in lses[1:]:
                top = jnp.maximum(top, v)
            tot = jnp.zeros_like(top)
            for v in lses:
                tot = tot + jnp.exp(v - top)
            group_lse = top + jnp.log(tot) - math.log(n_heads)
            lse_ref[rows_of(sub), sq * LANES:(sq + 1) * LANES] = jnp.broadcast_to(group_lse, (LANES, tq)).T


def _dil_attn(qs, ks, vs, t):
    b = qs[0].shape[0]
    steps = t // DIL_STEP_TOKENS
    plans, in_specs, out_specs, out_shapes, operands = [], [], [], [], []
    for gi, (_, dil) in enumerate(DIL_PATTERNS):
        length = t // dil
        gw = qs[gi].shape[2] // dil
        tq_blk = min(DIL_STEP_TOKENS, length)
        blocks_per_seq = length // tq_blk
        n_seq = DIL_STEP_TOKENS // tq_blk
        assert dil * blocks_per_seq == steps * n_seq
        tok = lambda w, bps=blocks_per_seq, ns=n_seq, tb=tq_blk: pl.BlockSpec(
            (None, tb, ns * w), lambda bi, s: (bi, s % bps, s // bps))
        seq = pl.BlockSpec((None, length, n_seq * gw), lambda bi, s, bps=blocks_per_seq: (bi, 0, s // bps))
        plans.append((tuple(m for _, m in DIL_GROUP_TILES[gi]), min(TQ, length), n_seq, blocks_per_seq))
        in_specs += [tok(gw), seq, seq]
        operands += [qs[gi], ks[gi], vs[gi]]
        out_specs += [tok(gw), tok(LANES)]
        out_shapes += [jax.ShapeDtypeStruct(qs[gi].shape, BF16), jax.ShapeDtypeStruct((b, length, dil * LANES), F32)]
    outs = pl.pallas_call(
        functools.partial(_dil_attn_kernel, plans=tuple(plans)),
        grid=(b, steps),
        in_specs=in_specs,
        out_specs=out_specs,
        out_shape=out_shapes,
        compiler_params=_cparams("parallel", "parallel"),
        name="dil_attn",
    )(*operands)
    return outs[0::2], outs[1::2]


def _outproj1_kernel(*refs):
    n_grp = len(DIL_PATTERNS)
    o_refs, l_refs = refs[0:n_grp], refs[n_grp:2 * n_grp]
    z_ref, w_ref, x_ref, mod_ref, fg_ref, out_ref, stage = refs[2 * n_grp:]
    tm = x_ref.shape[0]
    gw = DIL_TILES_PER_GROUP * LANES

    def token_order(ref, gi, width, col0, slab):
        dil = DIL_PATTERNS[gi][1]
        if dil == 1:
            return ref[:, col0:col0 + LANES].astype(F32)
        for r in range(dil):
            stage[slab, pl.ds(r, tm // dil, stride=dil), :] = ref[:, r * width + col0:r * width + col0 + LANES].astype(F32)
        return stage[slab]

    ls = [token_order(l_refs[gi], gi, LANES, 0, gi) for gi in range(n_grp)]
    top = jnp.maximum(jnp.maximum(ls[0], ls[1]), ls[2])
    es = [jnp.exp(v - top) for v in ls]
    den = es[0] + es[1] + es[2]
    alphas = [e / den * float(n_grp) for e in es]
    lane = lax.broadcasted_iota(jnp.int32, (tm, LANES), 1)
    weighted = [None] * len(DIL_HEAD_PAIRS)
    for gi in range(n_grp):
        for jt, (pt, members) in enumerate(DIL_GROUP_TILES[gi]):
            ot = token_order(o_refs[gi], gi, gw, jt * LANES, n_grp + gi * DIL_TILES_PER_GROUP + jt) * alphas[gi]
            if len(members) == 2:
                weighted[pt] = ot
            else:
                mine = (lane // HEAD_DIM) == members[0]
                weighted[pt] = jnp.where(mine, ot, 0.0 if weighted[pt] is None else weighted[pt])
    parts = [(weighted[pt] * z_ref[:, pt * LANES:(pt + 1) * LANES].astype(F32)).astype(BF16)
             for pt in range(len(DIL_HEAD_PAIRS))]
    og = jnp.concatenate(parts, axis=1)
    y = _dot(og, w_ref[...])
    x2 = x_ref[...] + mod_ref[2:3, :] * y
    ms = jnp.mean(x2 * x2, axis=-1, keepdims=True)
    out_ref[...] = x2 * lax.rsqrt(ms + NORM_EPS) * fg_ref[...]


def _outproj1(o_groups, lse_groups, z, w, x, mod, final_g):
    b, t, d = x.shape
    tok = lambda k: pl.BlockSpec((None, TM, k), lambda i, j: (i, j, 0))
    full = lambda a: pl.BlockSpec(a.shape, lambda i, j: (0,) * a.ndim)
    grp = lambda a, dil: pl.BlockSpec((None, TM // dil, a.shape[2]), lambda i, j: (i, j, 0))
    dils = [dil for _, dil in DIL_PATTERNS]
    n_slabs = len(dils) * (1 + DIL_TILES_PER_GROUP)
    return pl.pallas_call(
        _outproj1_kernel,
        grid=(b, t // TM),
        in_specs=[grp(a, dl) for a, dl in zip(o_groups, dils)] + [grp(a, dl) for a, dl in zip(lse_groups, dils)]
                 + [tok(z.shape[2]), full(w), tok(d), pl.BlockSpec((None, 3, d), lambda i, j: (i, 0, 0)), full(final_g)],
        out_specs=tok(d),
        out_shape=jax.ShapeDtypeStruct((b, t, d), F32),
        scratch_shapes=[pltpu.VMEM((n_slabs, TM, LANES), F32)],
        compiler_params=_cparams("parallel", "parallel"),
        name="outproj_dil_final",
    )(*o_groups, *lse_groups, z, w, x, mod, final_g)


def _rope_tables(t):
    inv = ROPE_THETA ** (-jnp.arange(HALF, dtype=F32) / HALF)
    ang = jnp.arange(t, dtype=F32)[:, None] * inv[None, :]
    cos = jnp.tile(jnp.cos(ang), (1, 4))
    sin = jnp.tile(jnp.sin(ang), (1, 4))
    sign = jnp.where(jnp.arange(LANES) < 2 * HALF, -1.0, 1.0).astype(F32)
    return cos, sin * sign[None, :]


def _pair_rope_layout(w, n_pairs):
    d = w.shape[0]
    return w.reshape(d, n_pairs, 2, 2, HALF).transpose(0, 1, 3, 2, 4).reshape(d, n_pairs * LANES)


def _nsa_weights(w_in, pe_k, pe_v, ck_w1, ck_w2, cv_w1, cv_w2, w_out):
    d = w_in.shape[0]
    aw = N_HEADS * HEAD_DIM
    kvw = NSA_KV_HEADS * HEAD_DIM
    cuts = np.cumsum([aw] + [kvw] * 6 + [aw]).tolist()
    q, k_c, v_c, k_s, v_s, k_w, v_w, z, gl = jnp.split(w_in, cuts, axis=1)
    scale = HEAD_DIM ** -0.5
    wq = (q * scale).reshape(d, NSA_KV_HEADS, NSA_GROUP, 2, HALF).transpose(0, 2, 3, 1, 4).reshape(d, aw)
    k_lay = lambda w: _pair_rope_layout(w, 1)
    wkv = jnp.concatenate([k_lay(k_c), v_c, k_lay(k_s), v_s, k_lay(k_w), v_w], axis=1)
    wz = z.reshape(d, NSA_KV_HEADS, NSA_GROUP, HEAD_DIM).transpose(0, 2, 1, 3).reshape(d, aw)
    wo = w_out.reshape(NSA_KV_HEADS, NSA_GROUP, HEAD_DIM, -1).transpose(1, 0, 2, 3).reshape(aw, -1)
    wg = jnp.pad(gl, ((0, 0), (0, LANES - gl.shape[1])))

    def grouped(w1_half, rope_lanes):
        out = []
        for g in range(NSA_KV_HEADS):
            if rope_lanes:
                src = w1_half.reshape(CMP_STRIDE, 2, 1, HALF, CMP_HIDDEN)
                pads = ((0, 0), (0, 0), (g, NSA_KV_HEADS - 1 - g), (0, 0), (0, 0))
            else:
                src = w1_half.reshape(CMP_STRIDE, 1, HEAD_DIM, CMP_HIDDEN)
                pads = ((0, 0), (g, NSA_KV_HEADS - 1 - g), (0, 0), (0, 0))
            out.append(jnp.pad(src, pads).reshape(CMP_STRIDE * LANES, CMP_HIDDEN))
        return jnp.concatenate(out, axis=1)

    def pe_rows(pe_half, rope_lanes):
        if rope_lanes:
            v = jnp.broadcast_to(pe_half.reshape(CMP_STRIDE, 2, 1, HALF), (CMP_STRIDE, 2, NSA_KV_HEADS, HALF))
        else:
            v = jnp.broadcast_to(pe_half.reshape(CMP_STRIDE, 1, HEAD_DIM), (CMP_STRIDE, NSA_KV_HEADS, HEAD_DIM))
        return v.reshape(1, CMP_STRIDE * LANES)

    def w2_padded(w2, rope_lanes):
        out = []
        for g in range(NSA_KV_HEADS):
            if rope_lanes:
                src = w2.reshape(CMP_HIDDEN, 2, 1, HALF)
                pads = ((0, 0), (0, 0), (g, NSA_KV_HEADS - 1 - g), (0, 0))
            else:
                src = w2.reshape(CMP_HIDDEN, 1, HEAD_DIM)
                pads = ((0, 0), (g, NSA_KV_HEADS - 1 - g), (0, 0))
            out.append(jnp.pad(src, pads).reshape(CMP_HIDDEN, LANES))
        return jnp.concatenate(out, axis=0)

    k1 = ck_w1.reshape(2, CMP_STRIDE, HEAD_DIM, CMP_HIDDEN)
    v1 = cv_w1.reshape(2, CMP_STRIDE, HEAD_DIM, CMP_HIDDEN)
    compress_consts = (
        pe_rows(pe_k[:CMP_STRIDE], True), pe_rows(pe_k[CMP_STRIDE:], True),
        pe_rows(pe_v[:CMP_STRIDE], False), pe_rows(pe_v[CMP_STRIDE:], False),
        grouped(k1[0], True).astype(BF16), grouped(k1[1], True).astype(BF16),
        grouped(v1[0], False).astype(BF16), grouped(v1[1], False).astype(BF16),
        w2_padded(ck_w2, True).astype(BF16), w2_padded(cv_w2, False).astype(BF16),
    )
    return wq.astype(BF16), wkv.astype(BF16), wz.astype(BF16), wg.astype(BF16), wo.astype(BF16), compress_consts


def _dil_weights(w_in, w_out):
    d = w_in.shape[0]
    q, k, v, z = jnp.split(w_in, 4, axis=1)
    n_pairs = len(DIL_HEAD_PAIRS)
    order = [hd for pair in DIL_HEAD_PAIRS for hd in pair]

    def tile_order(w, axis):
        parts = []
        for hd in order:
            sl = [slice(None)] * w.ndim
            sl[axis] = slice(hd * HEAD_DIM, (hd + 1) * HEAD_DIM)
            parts.append(w[tuple(sl)])
        return jnp.concatenate(parts, axis=axis)

    scale = HEAD_DIM ** -0.5
    wq = _pair_rope_layout(tile_order(q * scale, 1), n_pairs)
    wk = _pair_rope_layout(tile_order(k, 1), n_pairs)
    return (wq.astype(BF16), wk.astype(BF16), tile_order(v, 1).astype(BF16), tile_order(z, 1).astype(BF16),
            tile_order(w_out, 0).astype(BF16))


def _selection_constants(t):
    n_cmp_rows = t // CMP_STRIDE
    n_slc = t // SLC_LEN
    c_start = CMP_STRIDE * np.arange(n_cmp_rows)
    s_start = SLC_LEN * np.arange(n_slc)
    ovl_t = ((c_start[None, :] < s_start[:, None] + SLC_LEN)
             & (c_start[None, :] + CMP_LEN > s_start[:, None])).astype(np.float32)
    ovl_t[:, (t - CMP_LEN) // CMP_STRIDE + 1:] = 0.0
    e = np.zeros((NSA_KV_HEADS, t // KEY_CHUNK, LANES, KEY_CHUNK), np.float32)
    keys = np.arange(t)
    for g in range(NSA_KV_HEADS):
        e[g, keys // KEY_CHUNK, g * n_slc + keys // SLC_LEN, keys % KEY_CHUNK] = 1.0
    return jnp.asarray(ovl_t), jnp.asarray(e, dtype=BF16)


@jax.jit
def kernel(x, c, norm_g, ada_w, ada_b, nsa_w_in, nsa_pe_k, nsa_pe_v, nsa_ck_w1, nsa_ck_w2,
           nsa_cv_w1, nsa_cv_w2, nsa_w_out, dil_w_in, dil_w_out, final_g):
    b, t, d = x.shape
    assert t % SLC_CK == 0 and t % TM == 0 and d % LANES == 0
    assert NSA_KV_HEADS * (t // SLC_LEN) <= LANES and t // CMP_STRIDE == LANES
    mod = _adaln_mod(c, ada_w, ada_b).reshape(ada_w.shape[0], b, 3, d)
    cos, sin = _rope_tables(t)

    wq, wkv, wz, wg, wo, compress_consts = _nsa_weights(
        nsa_w_in[0], nsa_pe_k[0], nsa_pe_v[0], nsa_ck_w1[0], nsa_ck_w2[0], nsa_cv_w1[0], nsa_cv_w2[0], nsa_w_out[0])
    qp, qr, kc, vc, ks, vs, kw, vw, z, gl = _inproj0(x, mod[0], norm_g[0:1], cos, sin, wq, wkv, wz, wg)
    kcmp, vcmp = _compress(kc, vc, compress_consts)
    ovl_t, e_sel = _selection_constants(t)
    ocmp, msel = _cmp_attn(qp, kcmp, vcmp, ovl_t, gl)
    og = _nsa_attn(qr, ks, vs, kw, vw, msel, e_sel, ocmp, z, gl)

    dq, dk, dv, dz, dwo = _dil_weights(dil_w_in[0], dil_w_out[0])
    n_grp = len(DIL_PATTERNS)
    x1, *qkv, z1 = _outproj0_inproj1(og, wo, x, mod[0], mod[1], norm_g[1:2], cos, sin, dq, dk, dv, dz)
    assert all(win // dil == DIL_WIN and TM % (dil * 16) == 0 for win, dil in DIL_PATTERNS)
    o_groups, lse_groups = _dil_attn(qkv[0:n_grp], qkv[n_grp:2 * n_grp], qkv[2 * n_grp:3 * n_grp], t)
    return _outproj1(o_groups, lse_groups, z1, dwo, x1, mod[1], final_g.reshape(1, d))
```

```python
import functools
import math

import numpy as np
import jax
import jax.numpy as jnp
from jax import lax
from jax.experimental import pallas as pl
from jax.experimental.pallas import tpu as pltpu

F32 = jnp.float32
BF16 = jnp.bfloat16
HIGHEST = lax.Precision.HIGHEST

HEAD_DIM = 64
HALF = HEAD_DIM // 2
N_HEADS = 16
ROPE_THETA = 10000.0
NORM_EPS = 1e-6
NSA_KV_HEADS = 2
NSA_GROUP = N_HEADS // NSA_KV_HEADS
NSA_BRANCHES = 3
CMP_LEN = 32
CMP_STRIDE = 16
CMP_HIDDEN = 256
SLC_LEN = 64
SLC_TOP = 16
WIN_LEN = 512
DIL_PATTERNS = ((128, 1), (512, 4), (2048, 16))
DIL_GROUP_HEADS = (6, 5, 5)
DIL_WIN = 128


def _dil_tile_plan():
    offs = np.cumsum((0,) + DIL_GROUP_HEADS)
    pairs, group_tiles, leftovers = [], [], []
    for gi, hn in enumerate(DIL_GROUP_HEADS):
        heads = list(range(offs[gi], offs[gi + 1]))
        group_tiles.append([])
        for i in range(0, hn - hn % 2, 2):
            group_tiles[gi].append((len(pairs), (0, 1)))
            pairs.append((heads[i], heads[i + 1]))
        if hn % 2:
            leftovers.append((gi, heads[-1]))
    assert len(leftovers) % 2 == 0
    for (ga, ha), (gb, hb) in zip(leftovers[0::2], leftovers[1::2]):
        group_tiles[ga].append((len(pairs), (0,)))
        group_tiles[gb].append((len(pairs), (1,)))
        pairs.append((ha, hb))
    return tuple(pairs), tuple(tuple(tiles) for tiles in group_tiles)


DIL_HEAD_PAIRS, DIL_GROUP_TILES = _dil_tile_plan()
DIL_TILES_PER_GROUP = len(DIL_GROUP_TILES[0])
assert all(len(tiles) == DIL_TILES_PER_GROUP for tiles in DIL_GROUP_TILES)

LANES = 128
VMEM_LIMIT_BYTES = 48 * 1024 * 1024

NEG = -1e30
LOG2E = math.log2(math.e)
LN2 = math.log(2.0)
TQ = 128
KEY_CHUNK = 128
SLC_CK = 512
TM = 512
TM_OUT = 1024
DIL_STEP_TOKENS = 1024
NSA_STEP_TOKENS = 256
CMP_STEP_TOKENS = 1024


def _cparams(*sem):
    return pltpu.CompilerParams(dimension_semantics=sem, vmem_limit_bytes=VMEM_LIMIT_BYTES)


def _dot(a, b):
    return jnp.dot(a, b, preferred_element_type=F32)


def _dot_t(a, b):
    return lax.dot_general(a, b, (((1,), (1,)), ((), ())), preferred_element_type=F32)


def _silu(v):
    return v * jax.nn.sigmoid(v)


def _mod_kernel(c_ref, w_ref, b_ref, o_ref):
    s = _silu(c_ref[...])
    o_ref[...] = jnp.dot(s, w_ref[...], precision=HIGHEST, preferred_element_type=F32) + b_ref[...]


def _adaln_mod(c, ada_w, ada_b):
    depth, d, n3 = ada_w.shape
    b = c.shape[0]
    tn = 1024
    return pl.pallas_call(
        _mod_kernel,
        grid=(depth, n3 // tn),
        in_specs=[
            pl.BlockSpec((b, d), lambda i, j: (0, 0)),
            pl.BlockSpec((None, d, tn), lambda i, j: (i, 0, j)),
            pl.BlockSpec((None, 1, tn), lambda i, j: (i, 0, j)),
        ],
        out_specs=pl.BlockSpec((None, b, tn), lambda i, j: (i, 0, j)),
        out_shape=jax.ShapeDtypeStruct((depth, b, n3), F32),
        compiler_params=_cparams("arbitrary", "arbitrary"),
        name="adaln_mod",
    )(c, ada_w, ada_b.reshape(depth, 1, n3))


def _modulated_norm(x, mod_ref, g_ref):
    ms = jnp.mean(x * x, axis=-1, keepdims=True)
    y = x * lax.rsqrt(ms + NORM_EPS) * g_ref[...]
    return y * (1.0 + mod_ref[1:2, :]) + mod_ref[0:1, :]


def _rope_tile(v, cos, sin):
    return v * cos + pltpu.roll(v, 2 * HALF, axis=1) * sin


def _store_key_chunks_t(kt_ref, k):
    dim = lax.broadcasted_iota(jnp.int32, (LANES, KEY_CHUNK), 0)
    for c in range(kt_ref.shape[1]):
        kt = k[c * KEY_CHUNK:(c + 1) * KEY_CHUNK, :].T
        for g in range(kt_ref.shape[0]):
            kt_ref[g, c] = jnp.where((dim % HEAD_DIM) // HALF == g, kt, 0.0).astype(BF16)


def _inproj0_kernel(x_ref, mod_ref, g_ref, cos_ref, sin_ref, wq_ref, wkv_ref, wz_ref, wg_ref,
                    qp_ref, qr_ref, kc_ref, vc_ref, ks_ref, vs_ref, kw_ref, vw_ref, z_ref, gl_ref, stage):
    tm = x_ref.shape[0]
    h = _modulated_norm(x_ref[...], mod_ref, g_ref).astype(BF16)
    cos = cos_ref[...]
    sin = sin_ref[...]
    q = _dot(h, wq_ref[...])
    for j in range(q.shape[1] // LANES):
        sl = slice(j * LANES, (j + 1) * LANES)
        qp_ref[:, sl] = q[:, sl].astype(BF16)
        qr_ref[:, sl] = _rope_tile(q[:, sl], cos, sin).astype(BF16)
    kv = _dot(h, wkv_ref[...])
    tiles = [kv[:, i * LANES:(i + 1) * LANES] for i in range(6)]
    for slab, (tile, o_ref) in enumerate(((tiles[0], kc_ref), (tiles[1], vc_ref))):
        stage[slab] = tile
        for r in range(CMP_STRIDE):
            rows = stage[slab, pl.ds(r, tm // CMP_STRIDE, stride=CMP_STRIDE), :]
            o_ref[:, r * LANES:(r + 1) * LANES] = rows.astype(BF16)
    _store_key_chunks_t(ks_ref, _rope_tile(tiles[2], cos, sin))
    vs_ref[...] = tiles[3].astype(BF16)
    _store_key_chunks_t(kw_ref, _rope_tile(tiles[4], cos, sin))
    vw_ref[...] = tiles[5].astype(BF16)
    z_ref[...] = _silu(_dot(h, wz_ref[...])).astype(BF16)
    gl_ref[...] = _dot(h, wg_ref[...])


def _inproj0(x, mod, g, cos, sin, wq, wkv, wz, wg):
    b, t, d = x.shape
    nq, nz = wq.shape[1], wz.shape[1]
    tok = lambda n: pl.BlockSpec((None, TM, n), lambda i, j: (i, j, 0))
    full = lambda a: pl.BlockSpec(a.shape, lambda i, j: (0,) * a.ndim)
    tab = pl.BlockSpec((TM, LANES), lambda i, j: (j, 0))
    shp = lambda n, dt: jax.ShapeDtypeStruct((b, t, n), dt)
    kt = pl.BlockSpec((None, NSA_KV_HEADS, TM // KEY_CHUNK, LANES, KEY_CHUNK), lambda i, j: (i, 0, j, 0, 0))
    kt_shp = jax.ShapeDtypeStruct((b, NSA_KV_HEADS, t // KEY_CHUNK, LANES, KEY_CHUNK), BF16)
    kv = tok(LANES)
    kv_shp = shp(LANES, BF16)
    cm = pl.BlockSpec((None, TM // CMP_STRIDE, CMP_STRIDE * LANES), lambda i, j: (i, j, 0))
    cm_shp = jax.ShapeDtypeStruct((b, t // CMP_STRIDE, CMP_STRIDE * LANES), BF16)
    return pl.pallas_call(
        _inproj0_kernel,
        grid=(b, t // TM),
        in_specs=[tok(d), pl.BlockSpec((None, 3, d), lambda i, j: (i, 0, 0)), full(g), tab, tab,
                  full(wq), full(wkv), full(wz), full(wg)],
        out_specs=[tok(nq), tok(nq), cm, cm, kt, kv, kt, kv, tok(nz), tok(LANES)],
        out_shape=[shp(nq, BF16), shp(nq, BF16), cm_shp, cm_shp, kt_shp, kv_shp, kt_shp, kv_shp,
                   shp(nz, BF16), shp(LANES, F32)],
        scratch_shapes=[pltpu.VMEM((2, TM, LANES), F32)],
        compiler_params=_cparams("parallel", "parallel"),
        name="inproj_nsa",
    )(x, mod, g, cos, sin, wq, wkv, wz, wg)


def _compress_kernel(ak_ref, av_ref, pekt_ref, pekb_ref, pevt_ref, pevb_ref,
                     w1kt_ref, w1kb_ref, w1vt_ref, w1vb_ref, w2k_ref, w2v_ref, kcmp_ref, vcmp_ref):
    def mlp(a_ref, pet_ref, peb_ref, w1t_ref, w1b_ref, w2_ref):
        a = a_ref[...].astype(F32)
        top = _dot((a + pet_ref[...]).astype(BF16), w1t_ref[...])
        bot = _dot((a + peb_ref[...]).astype(BF16), w1b_ref[...])
        n = bot.shape[0]
        hid = top + pltpu.roll(bot, n - 1, axis=0)
        return _dot(_silu(hid).astype(BF16), w2_ref[...])

    kcmp_ref[...] = mlp(ak_ref, pekt_ref, pekb_ref, w1kt_ref, w1kb_ref, w2k_ref).astype(BF16)
    vcmp_ref[...] = mlp(av_ref, pevt_ref, pevb_ref, w1vt_ref, w1vb_ref, w2v_ref).astype(BF16)


def _compress(ak, av, consts):
    b, n, w = ak.shape
    full = lambda a: pl.BlockSpec(a.shape, lambda i: (0,) * a.ndim)
    blk = pl.BlockSpec((None, n, w), lambda i: (i, 0, 0))
    out = pl.BlockSpec((None, n, LANES), lambda i: (i, 0, 0))
    return pl.pallas_call(
        _compress_kernel,
        grid=(b,),
        in_specs=[blk, blk] + [full(a) for a in consts],
        out_specs=[out, out],
        out_shape=[jax.ShapeDtypeStruct((b, n, LANES), BF16)] * 2,
        compiler_params=_cparams("parallel"),
        name="compress",
    )(ak, av, *consts)


def _stack_heads(q_ref, rows, lane_sets):
    n_tiles = q_ref.shape[1] // LANES
    parts = []
    for j in range(n_tiles):
        qj = q_ref[rows, j * LANES:(j + 1) * LANES]
        parts.append(jnp.where(lane_sets, qj, jnp.zeros_like(qj)))
    return jnp.concatenate(parts, axis=0)


def _head_gate_rows(gates, g, branches, lane):
    blocks = []
    for j in range(NSA_GROUP):
        base = (g * NSA_GROUP + j) * NSA_BRANCHES
        kept = jnp.zeros_like(gates)
        for br in branches:
            kept = jnp.where(lane == base + br, gates, kept)
        hi = kept.astype(BF16)
        blocks.append(jnp.concatenate([hi, (kept - hi.astype(F32)).astype(BF16)], axis=1))
    return jnp.concatenate(blocks, axis=0)


def _gate_spread(branches):
    col = np.arange(2 * LANES) % LANES
    spread = np.zeros((2 * LANES, LANES * len(branches)), np.float32)
    for i, br in enumerate(branches):
        spread[col % NSA_BRANCHES == br, i * LANES:(i + 1) * LANES] = 1.0
    return jnp.asarray(spread, dtype=BF16)


def _cmp_attn_kernel(qp_ref, kcmp_ref, vcmp_ref, ovl_ref, gl_ref, ocmp_ref, msel_ref, *, n_slc, tq):
    n_cmp = vcmp_ref.shape[0]
    hg = qp_ref.shape[1] // LANES
    n_sub = qp_ref.shape[0] // tq
    assert n_cmp == LANES and tq == LANES
    row = lax.broadcasted_iota(jnp.int32, (LANES, tq), 0)
    col = lax.broadcasted_iota(jnp.int32, (LANES, tq), 1)
    jb = lax.broadcasted_iota(jnp.int32, (n_slc, tq), 0)
    tok_rows = lambda sub: slice(sub * tq, (sub + 1) * tq)
    t0 = lambda sub: pl.program_id(1) * qp_ref.shape[0] + sub * tq
    units = [(sub, g) for sub in range(n_sub) for g in range(NSA_KV_HEADS)]
    eye = jnp.where(row == col, 1.0, 0.0).astype(BF16)
    k_sel = jnp.concatenate([kcmp_ref[...], eye], axis=1)
    v_t = vcmp_ref[...].T

    q_t = [[qp_ref[tok_rows(sub), j * LANES:(j + 1) * LANES].T for j in range(hg)] for sub in range(n_sub)]
    gates_t = [jax.nn.sigmoid(gl_ref[tok_rows(sub), :]).T for sub in range(n_sub)]
    seen = [jnp.concatenate([jnp.where(CMP_STRIDE * row + (CMP_LEN - 1) <= t0(sub) + col, 0.0, NEG).astype(BF16)] * hg,
                            axis=1) for sub in range(n_sub)]
    weights = {}
    for sub, g in units:
        in_group = ((row % HEAD_DIM) // HALF) == g
        q_cols = jnp.concatenate([jnp.where(in_group, v, jnp.zeros_like(v)) for v in q_t[sub]], axis=1)
        s = _dot(k_sel, jnp.concatenate([q_cols, seen[sub]], axis=0))
        m = jnp.max(s, axis=0, keepdims=True)
        m = jnp.where(m > 0.5 * NEG, m, 0.0)
        e = jnp.exp2(s - m)
        den = jnp.sum(e, axis=0, keepdims=True)
        weights[sub, g] = e * (1.0 / jnp.where(den > 0, den, 1.0))
    outs, imps = {}, {}
    for sub, g in units:
        p = weights[sub, g]
        gate = jnp.concatenate([gates_t[sub][(g * NSA_GROUP + j) * NSA_BRANCHES:(g * NSA_GROUP + j) * NSA_BRANCHES + 1, :]
                                for j in range(hg)], axis=1)
        outs[sub, g] = gate * _dot(v_t, p.astype(BF16))
        p_sum = p[:, 0:tq]
        for j in range(1, hg):
            p_sum = p_sum + p[:, j * tq:(j + 1) * tq]
        imps[sub, g] = jnp.dot(ovl_ref[...], p_sum, precision=HIGHEST, preferred_element_type=F32)
    picked = {}
    for sub, g in units:
        cur = (t0(sub) + lax.broadcasted_iota(jnp.int32, (n_slc, tq), 1)) // SLC_LEN
        visible = jb <= cur
        forced = (jb == 0) | (jb == cur) | (jb == cur - 1)
        rank = jnp.where(forced, -NEG, jnp.where(visible, imps[sub, g], NEG))
        cnt = jnp.zeros((n_slc, tq), F32)
        for jp in range(n_slc):
            rj = rank[jp:jp + 1, :]
            tie = jnp.where(jb > jp, 1.0, 0.0)
            cnt = cnt + jnp.where(rj > rank, 1.0, jnp.where(rj == rank, tie, 0.0))
        picked[sub, g] = jnp.where(visible, jnp.where(cnt < SLC_TOP, 1.0, 0.0), 0.0)
    pad = jnp.zeros((LANES - NSA_KV_HEADS * n_slc, tq), F32)
    for sub in range(n_sub):
        for j in range(hg):
            cols = slice(j * tq, (j + 1) * tq)
            o_t = jnp.where(row < HEAD_DIM, outs[sub, 0][:, cols], outs[sub, 1][:, cols])
            ocmp_ref[tok_rows(sub), j * LANES:(j + 1) * LANES] = o_t.T.astype(BF16)
        sel_t = jnp.concatenate([picked[sub, g] for g in range(NSA_KV_HEADS)] + [pad], axis=0)
        msel_ref[tok_rows(sub), :] = sel_t.T.astype(BF16)


def _cmp_attn(qp, kcmp, vcmp, ovl_t, gl):
    b, t, nq = qp.shape
    n_cmp = kcmp.shape[1]
    n_slc = t // SLC_LEN
    tok = lambda n: pl.BlockSpec((None, CMP_STEP_TOKENS, n), lambda i, j: (i, j, 0))
    per_b = pl.BlockSpec((None, n_cmp, LANES), lambda i, j: (i, 0, 0))
    full = lambda a: pl.BlockSpec(a.shape, lambda i, j: (0,) * a.ndim)
    return pl.pallas_call(
        functools.partial(_cmp_attn_kernel, n_slc=n_slc, tq=TQ),
        grid=(b, t // CMP_STEP_TOKENS),
        in_specs=[tok(nq), per_b, per_b, full(ovl_t), tok(LANES)],
        out_specs=[tok(nq), tok(LANES)],
        out_shape=[jax.ShapeDtypeStruct((b, t, nq), BF16), jax.ShapeDtypeStruct((b, t, LANES), BF16)],
        compiler_params=_cparams("parallel", "parallel"),
        name="cmp_attn_select",
    )(qp, kcmp, vcmp, ovl_t, gl)


def _nsa_attn_kernel(qr_ref, *refs, tq, **kw):
    n_sub = qr_ref.shape[0] // tq
    for sub in range(n_sub):
        _nsa_attn_tile(qr_ref, *refs, tok=slice(sub * tq, (sub + 1) * tq), qi=pl.program_id(1) * n_sub + sub, **kw)


def _nsa_attn_tile(qr_ref, ks_ref, vs_ref, kw_ref, vw_ref, msel_ref, e_ref, gsp_ref, ocmp_ref, z_ref, gl_ref,
                   og_ref, m_scr, acc_scr, *, tok, qi, n_slc, ck):
    tq = tok.stop - tok.start
    hg = qr_ref.shape[1] // LANES
    rows = hg * tq
    n_wc = WIN_LEN // tq + 1
    per_ck = ck // KEY_CHUNK
    t0 = qi * tq
    lane = lax.broadcasted_iota(jnp.int32, (tq, LANES), 1)
    a_idx = lax.broadcasted_iota(jnp.int32, (tq, LANES), 0)

    eye = jnp.where(lane == a_idx, 1.0, 0.0).astype(BF16)
    earlier = jnp.where((lane % n_slc) < t0 // SLC_LEN, 0.0, NEG)
    mneg = jnp.where(msel_ref[tok, :].astype(F32) > 0.5, earlier, NEG).astype(BF16)
    mneg_rows = jnp.concatenate([mneg] * hg, axis=0)
    eye_rows = jnp.concatenate([eye] * hg, axis=0)

    def with_selector(sel_rows):
        q_rows = jnp.concatenate([qr_ref[tok, j * LANES:(j + 1) * LANES] for j in range(hg)], axis=0)
        return jnp.concatenate([q_rows, sel_rows], axis=1)

    zero_b = jnp.zeros((tq, LANES), BF16)
    neg_b = jnp.full((tq, LANES), NEG, BF16)
    causal_b = jnp.where(lane <= a_idx, 0.0, NEG).astype(BF16)
    far_b = jnp.where(lane > a_idx, 0.0, NEG).astype(BF16)
    own_half = [lane < HEAD_DIM, lane >= HEAD_DIM]

    def values_with_ones(v, g):
        in_own_half = (lax.broadcasted_iota(jnp.int32, v.shape, 1) // HEAD_DIM) == g
        return jnp.where(in_own_half, v, jnp.ones_like(v))

    def lane_tiles(v):
        return [v[:, i * LANES:(i + 1) * LANES] for i in range(v.shape[1] // LANES)]

    def tile_max(tiles):
        mx = tiles[0]
        for v in tiles[1:]:
            mx = jnp.maximum(mx, v)
        return jnp.max(mx, axis=-1, keepdims=True)

    def probs(tiles, m):
        return jnp.concatenate([jnp.exp2(v - m) for v in tiles], axis=1).astype(BF16)

    m_scr[...] = jnp.full(m_scr.shape, NEG, F32)
    acc_scr[...] = jnp.zeros(acc_scr.shape, F32)

    def attend_chunks(c0, n):
        k0 = pl.multiple_of(c0 * KEY_CHUNK, KEY_CHUNK)
        q_sel = with_selector(mneg_rows)
        scores = []
        for g in range(NSA_KV_HEADS):
            kt = jnp.concatenate([ks_ref[g, c0 + i] for i in range(n)], axis=1)
            eb = jnp.concatenate([e_ref[g, c0 + i] for i in range(n)], axis=1)
            scores.append(_dot(q_sel, jnp.concatenate([kt, eb], axis=0)))
        groups = range(NSA_KV_HEADS)
        m_old = [m_scr[g] for g in groups]
        m_new = [jnp.maximum(m_old[g], tile_max(lane_tiles(scores[g]))) for g in groups]
        p = [probs(lane_tiles(scores[g]), m_new[g]) for g in groups]
        pv = [_dot(p[g], values_with_ones(vs_ref[pl.ds(k0, n * KEY_CHUNK), :], g)) for g in groups]
        for g in groups:
            acc_scr[g] = jnp.exp2(m_old[g] - m_new[g]) * acc_scr[g] + pv[g]
            m_scr[g] = m_new[g]

    def earlier_keys(kc, carry):
        attend_chunks(kc * per_ck, per_ck)
        return carry

    lax.fori_loop(0, qi // per_ck, earlier_keys, 0)
    for rem in range(1, per_ck):
        @pl.when(qi % per_ck == rem)
        def _():
            attend_chunks((qi // per_ck) * per_ck, rem)

    kd = pl.multiple_of(t0, tq)
    zeros_v = jnp.zeros((tq, LANES), BF16)
    win_chunks = [qi - (n_wc - 1 - c) for c in range(n_wc - 1)]
    win_bias = [jnp.where(cidx >= 0, far_b if c == 0 else zero_b, neg_b) for c, cidx in enumerate(win_chunks)]
    pairs = [(c, c + 1) for c in range(0, n_wc - 1, 2)]
    q_eye = with_selector(eye_rows)
    s_own, scores = [], []
    for g in range(NSA_KV_HEADS):
        own_k = jnp.concatenate([jnp.concatenate([kw_ref[g, qi], causal_b], axis=0),
                                 jnp.concatenate([ks_ref[g, qi], causal_b], axis=0)], axis=1)
        kbs = [jnp.concatenate([kw_ref[g, jnp.maximum(cidx, 0)], win_bias[c]], axis=0)
               for c, cidx in enumerate(win_chunks)]
        s_own.append(_dot(q_eye, own_k))
        scores.append([_dot(q_eye, jnp.concatenate([kbs[c] for c in pr], axis=1)) for pr in pairs])

    groups = range(NSA_KV_HEADS)
    s_win_own = [s_own[g][:, 0:tq] for g in groups]
    s_slc_own = [s_own[g][:, tq:2 * tq] for g in groups]
    m_win = [tile_max([v for s in scores[g] for v in lane_tiles(s)] + [s_win_own[g]]) for g in groups]
    m_old = [m_scr[g] for g in groups]
    m_slc = [jnp.maximum(m_old[g], jnp.max(s_slc_own[g], axis=-1, keepdims=True)) for g in groups]
    p_all = [jnp.concatenate([probs(lane_tiles(s), m_win[g]) for s in scores[g]]
                             + [probs([s_win_own[g]], m_win[g]), probs([s_slc_own[g]], m_slc[g])], axis=1) for g in groups]
    acc_slc, acc_win = [], []
    for g in groups:
        vbs = [values_with_ones(vw_ref[pl.ds(pl.multiple_of(jnp.maximum(cidx, 0) * tq, tq), tq), :], g)
               for cidx in win_chunks] + [values_with_ones(vw_ref[pl.ds(kd, tq), :], g)]
        win_v = [jnp.concatenate([v, zeros_v], axis=1) for v in vbs]
        slc_v = jnp.concatenate([zeros_v, values_with_ones(vs_ref[pl.ds(kd, tq), :], g)], axis=1)
        pv = _dot(p_all[g], jnp.concatenate(win_v + [slc_v], axis=0))
        acc_win.append(pv[:, 0:LANES])
        acc_slc.append(jnp.exp2(m_old[g] - m_slc[g]) * acc_scr[g] + pv[:, LANES:2 * LANES])

    gates = jax.nn.sigmoid(gl_ref[tok, :])
    row_gates = [_dot(_head_gate_rows(gates, g, (1, 2), lane), gsp_ref[...]) for g in range(NSA_KV_HEADS)]

    def gated(accs, rs, branch_lanes):
        num = [accs[g][rs] * row_gates[g][rs, branch_lanes] for g in range(NSA_KV_HEADS)]
        sums = pltpu.roll(jnp.where(own_half[0], accs[1][rs], accs[0][rs]), HEAD_DIM, axis=1)
        return jnp.where(own_half[0], num[0], num[1]) / sums

    for j in range(hg):
        rs = slice(j * tq, (j + 1) * tq)
        cols = slice(j * LANES, (j + 1) * LANES)
        o = (ocmp_ref[tok, cols].astype(F32) + gated(acc_slc, rs, slice(0, LANES))
             + gated(acc_win, rs, slice(LANES, 2 * LANES)))
        og_ref[tok, cols] = (o * z_ref[tok, cols].astype(F32)).astype(BF16)


def _nsa_attn(qr, ks, vs, kw, vw, msel, e, ocmp, z, gl):
    b, t, nq = qr.shape
    gsp = _gate_spread((1, 2))
    hg = nq // LANES
    rows = hg * TQ
    assert TQ == KEY_CHUNK and WIN_LEN % TQ == 0
    tok = lambda n: pl.BlockSpec((None, NSA_STEP_TOKENS, n), lambda i, j: (i, j, 0))
    per_b = pl.BlockSpec((None, t, LANES), lambda i, j: (i, 0, 0))
    per_b_t = pl.BlockSpec((None, NSA_KV_HEADS, t // KEY_CHUNK, LANES, KEY_CHUNK), lambda i, j: (i, 0, 0, 0, 0))
    full = lambda a: pl.BlockSpec(a.shape, lambda i, j: (0,) * a.ndim)
    return pl.pallas_call(
        functools.partial(_nsa_attn_kernel, tq=TQ, n_slc=t // SLC_LEN, ck=SLC_CK),
        grid=(b, t // NSA_STEP_TOKENS),
        in_specs=[tok(nq), per_b_t, per_b, per_b_t, per_b, tok(LANES), full(e), full(gsp), tok(nq), tok(nq), tok(LANES)],
        out_specs=tok(nq),
        out_shape=jax.ShapeDtypeStruct((b, t, nq), BF16),
        scratch_shapes=[
            pltpu.VMEM((NSA_KV_HEADS, rows, LANES), F32),
            pltpu.VMEM((NSA_KV_HEADS, rows, LANES), F32),
        ],
        compiler_params=_cparams("parallel", "arbitrary"),
        name="nsa_slc_win_attn",
    )(qr, ks, vs, kw, vw, msel, e, gsp, ocmp, z, gl)


def _inproj1_kernel(og_ref, wo_ref, x_ref, mod0_ref, mod_ref, g_ref, cos_ref, sin_ref, wq_ref, wk_ref, wv_ref, wz_ref,
                    x1_ref, *refs):
    n_grp = len(DIL_PATTERNS)
    outs = [refs[a * n_grp:(a + 1) * n_grp] for a in range(3)]
    z_ref, stage = refs[3 * n_grp], refs[3 * n_grp + 1]
    tm = x_ref.shape[0]
    gw = DIL_TILES_PER_GROUP * LANES
    x1 = x_ref[...] + mod0_ref[2:3, :] * _dot(og_ref[...], wo_ref[...])
    x1_ref[...] = x1
    h = _modulated_norm(x1, mod_ref, g_ref).astype(BF16)
    cos = cos_ref[...]
    sin = sin_ref[...]
    projected = [_dot(h, w_ref[...]) for w_ref in (wq_ref, wk_ref, wv_ref, wz_ref)]
    for a, u in enumerate(projected[0:3]):
        tiles = [u[:, pt * LANES:(pt + 1) * LANES] for pt in range(len(DIL_HEAD_PAIRS))]
        if a < 2:
            tiles = [_rope_tile(v, cos, sin) for v in tiles]
        for gi, (_, dil) in enumerate(DIL_PATTERNS):
            for jt, (pt, _) in enumerate(DIL_GROUP_TILES[gi]):
                tile = tiles[pt]
                if dil == 1:
                    outs[a][gi][:, jt * LANES:(jt + 1) * LANES] = tile.astype(BF16)
                    continue
                slab = (a * n_grp + gi) * DIL_TILES_PER_GROUP + jt
                stage[slab] = tile
                for r in range(dil):
                    rows = stage[slab, pl.ds(r, tm // dil, stride=dil), :]
                    outs[a][gi][:, r * gw + jt * LANES:r * gw + (jt + 1) * LANES] = rows.astype(BF16)
    z_ref[...] = _silu(projected[3]).astype(BF16)


def _outproj0_inproj1(og, wo, x, mod0, mod, g, cos, sin, wq, wk, wv, wz):
    b, t, d = x.shape
    n = wq.shape[1]
    gw = DIL_TILES_PER_GROUP * LANES
    tok = lambda k: pl.BlockSpec((None, TM, k), lambda i, j: (i, j, 0))
    full = lambda a: pl.BlockSpec(a.shape, lambda i, j: (0,) * a.ndim)
    tab = pl.BlockSpec((TM, LANES), lambda i, j: (j, 0))
    mods = pl.BlockSpec((None, 3, d), lambda i, j: (i, 0, 0))
    grp_specs = [pl.BlockSpec((None, TM // dil, dil * gw), lambda i, j: (i, j, 0)) for _, dil in DIL_PATTERNS]
    grp_shapes = [jax.ShapeDtypeStruct((b, t // dil, dil * gw), BF16) for _, dil in DIL_PATTERNS]
    n_slabs = 3 * len(DIL_PATTERNS) * DIL_TILES_PER_GROUP
    return pl.pallas_call(
        _inproj1_kernel,
        grid=(b, t // TM),
        in_specs=[tok(og.shape[2]), full(wo), tok(d), mods, mods, full(g), tab, tab,
                  full(wq), full(wk), full(wv), full(wz)],
        out_specs=[tok(d)] + grp_specs * 3 + [tok(n)],
        out_shape=[jax.ShapeDtypeStruct((b, t, d), F32)] + grp_shapes * 3 + [jax.ShapeDtypeStruct((b, t, n), BF16)],
        scratch_shapes=[pltpu.VMEM((n_slabs, TM, LANES), F32)],
        compiler_params=_cparams("parallel", "parallel"),
        name="outproj_nsa_inproj_dil",
    )(og, wo, x, mod0, mod, g, cos, sin, wq, wk, wv, wz)


def _dil_attn_kernel(*refs, plans):
    n_grp = len(plans)
    step = pl.program_id(1)
    for gi, (members, tq, n_seq, blocks_per_seq) in enumerate(plans):
        q_ref, k_ref, v_ref = refs[3 * gi:3 * gi + 3]
        o_ref, lse_ref = refs[3 * n_grp + 2 * gi:3 * n_grp + 2 * gi + 2]
        _dil_attn_group(q_ref, k_ref, v_ref, o_ref, lse_ref, step % blocks_per_seq, members, tq, n_seq)


def _dil_attn_group(q_ref, k_ref, v_ref, o_ref, lse_ref, i, members, tq, n_seq):
    n_heads = sum(len(m) for m in members)
    tq_blk = q_ref.shape[0]
    gw = q_ref.shape[1] // n_seq
    n_tiles = gw // LANES
    n_sub = tq_blk // tq
    assert tq == LANES
    row = lax.broadcasted_iota(jnp.int32, (LANES, tq), 0)
    col = lax.broadcasted_iota(jnp.int32, (LANES, tq), 1)
    head_a = ((row % HEAD_DIM) // HALF) == 0
    diag_b = jnp.where(row <= col, 0.0, NEG)
    prev_b = jnp.where(row >= col, 0.0, NEG)
    neg_b = jnp.full((LANES, tq), NEG, F32)
    two = lambda v: jnp.concatenate([v, v], axis=1)
    units = [(sq, sub, jt) for sq in range(n_seq) for sub in range(n_sub) for jt in range(n_tiles)]
    rows_of = lambda sub: slice(sub * tq, (sub + 1) * tq)
    cols_of = lambda sq, jt: slice(sq * gw + jt * LANES, sq * gw + (jt + 1) * LANES)
    key_rows = lambda c: pl.ds(pl.multiple_of(jnp.maximum(i * n_sub + c, 0) * tq, tq), tq)
    q_cols, v_t = {}, {}
    for sq, sub, jt in units:
        q_t = q_ref[rows_of(sub), cols_of(sq, jt)].T
        zero = jnp.zeros_like(q_t)
        q_cols[sq, sub, jt] = jnp.concatenate([jnp.where(head_a, q_t, zero), jnp.where(head_a, zero, q_t)], axis=1)
        v_t[sq, sub, jt] = v_ref[key_rows(sub), cols_of(sq, jt)].T
        if sub == 0:
            v_t[sq, -1, jt] = v_ref[key_rows(-1), cols_of(sq, jt)].T
    probs_t, stats = {}, {}
    for sq, sub, jt in units:
        prev_bias = prev_b if sub > 0 else jnp.where(i > 0, prev_b, neg_b)
        bias = jnp.concatenate([two(prev_bias), two(diag_b)], axis=0)
        keys = jnp.concatenate([k_ref[key_rows(sub - 1), cols_of(sq, jt)], k_ref[key_rows(sub), cols_of(sq, jt)]], axis=0)
        s = _dot(keys, q_cols[sq, sub, jt]) + bias
        m = jnp.max(s, axis=0, keepdims=True)
        p = jnp.exp2(s - m)
        l = jnp.sum(p, axis=0, keepdims=True)
        probs_t[sq, sub, jt] = p.astype(BF16)
        stats[sq, sub, jt] = (m, l)
    for sq, sub, jt in units:
        m, l = stats[sq, sub, jt]
        o_t = _dot(jnp.concatenate([v_t[sq, sub - 1, jt], v_t[sq, sub, jt]], axis=1), probs_t[sq, sub, jt])
        inv = 1.0 / l
        o_t = jnp.where(row < HEAD_DIM, o_t[:, 0:tq] * inv[:, 0:tq], o_t[:, tq:2 * tq] * inv[:, tq:2 * tq])
        o_ref[rows_of(sub), cols_of(sq, jt)] = o_t.T.astype(BF16)
    for sq in range(n_seq):
        for sub in range(n_sub):
            lses = []
            for jt in range(n_tiles):
                m, l = stats[sq, sub, jt]
                lse = m * LN2 + jnp.log(l)
                lses += [lse[:, mb * tq:(mb + 1) * tq] for mb in members[jt]]
            top = lses[0]
            for v in lses[1:]:
                top = jnp.maximum(top, v)
            tot = jnp.zeros_like(top)
            for v in lses:
                tot = tot + jnp.exp(v - top)
            group_lse = top + jnp.log(tot) - math.log(n_heads)
            lse_ref[rows_of(sub), sq * LANES:(sq + 1) * LANES] = jnp.broadcast_to(group_lse, (LANES, tq)).T


def _dil_attn(qs, ks, vs, t):
    b = qs[0].shape[0]
    steps = t // DIL_STEP_TOKENS
    plans, in_specs, out_specs, out_shapes, operands = [], [], [], [], []
    for gi, (_, dil) in enumerate(DIL_PATTERNS):
        length = t // dil
        gw = qs[gi].shape[2] // dil
        tq_blk = min(DIL_STEP_TOKENS, length)
        blocks_per_seq = length // tq_blk
        n_seq = DIL_STEP_TOKENS // tq_blk
        assert dil * blocks_per_seq == steps * n_seq
        tok = lambda w, bps=blocks_per_seq, ns=n_seq, tb=tq_blk: pl.BlockSpec(
            (None, tb, ns * w), lambda bi, s: (bi, s % bps, s // bps))
        seq = pl.BlockSpec((None, length, n_seq * gw), lambda bi, s, bps=blocks_per_seq: (bi, 0, s // bps))
        plans.append((tuple(m for _, m in DIL_GROUP_TILES[gi]), min(TQ, length), n_seq, blocks_per_seq))
        in_specs += [tok(gw), seq, seq]
        operands += [qs[gi], ks[gi], vs[gi]]
        out_specs += [tok(gw), tok(LANES)]
        out_shapes += [jax.ShapeDtypeStruct(qs[gi].shape, BF16), jax.ShapeDtypeStruct((b, length, dil * LANES), F32)]
    outs = pl.pallas_call(
        functools.partial(_dil_attn_kernel, plans=tuple(plans)),
        grid=(b, steps),
        in_specs=in_specs,
        out_specs=out_specs,
        out_shape=out_shapes,
        compiler_params=_cparams("parallel", "parallel"),
        name="dil_attn",
    )(*operands)
    return outs[0::2], outs[1::2]


def _outproj1_kernel(*refs):
    n_grp = len(DIL_PATTERNS)
    o_refs, l_refs = refs[0:n_grp], refs[n_grp:2 * n_grp]
    z_ref, w_ref, x_ref, mod_ref, fg_ref, out_ref, stage = refs[2 * n_grp:]
    tm = x_ref.shape[0]
    gw = DIL_TILES_PER_GROUP * LANES

    def token_order(ref, gi, width, col0, slab):
        dil = DIL_PATTERNS[gi][1]
        if dil == 1:
            return ref[:, col0:col0 + LANES].astype(F32)
        for r in range(dil):
            stage[slab, pl.ds(r, tm // dil, stride=dil), :] = ref[:, r * width + col0:r * width + col0 + LANES].astype(F32)
        return stage[slab]

    ls = [token_order(l_refs[gi], gi, LANES, 0, gi) for gi in range(n_grp)]
    top = jnp.maximum(jnp.maximum(ls[0], ls[1]), ls[2])
    es = [jnp.exp(v - top) for v in ls]
    den = es[0] + es[1] + es[2]
    alphas = [e / den * float(n_grp) for e in es]
    lane = lax.broadcasted_iota(jnp.int32, (tm, LANES), 1)
    weighted = [None] * len(DIL_HEAD_PAIRS)
    for gi in range(n_grp):
        for jt, (pt, members) in enumerate(DIL_GROUP_TILES[gi]):
            ot = token_order(o_refs[gi], gi, gw, jt * LANES, n_grp + gi * DIL_TILES_PER_GROUP + jt) * alphas[gi]
            if len(members) == 2:
                weighted[pt] = ot
            else:
                mine = (lane // HEAD_DIM) == members[0]
                weighted[pt] = jnp.where(mine, ot, 0.0 if weighted[pt] is None else weighted[pt])
    parts = [(weighted[pt] * z_ref[:, pt * LANES:(pt + 1) * LANES].astype(F32)).astype(BF16)
             for pt in range(len(DIL_HEAD_PAIRS))]
    og = jnp.concatenate(parts, axis=1)
    y = _dot(og, w_ref[...])
    x2 = x_ref[...] + mod_ref[2:3, :] * y
    ms = jnp.mean(x2 * x2, axis=-1, keepdims=True)
    out_ref[...] = x2 * lax.rsqrt(ms + NORM_EPS) * fg_ref[...]


def _outproj1(o_groups, lse_groups, z, w, x, mod, final_g):
    b, t, d = x.shape
    tm = TM_OUT
    tok = lambda k: pl.BlockSpec((None, tm, k), lambda i, j: (i, j, 0))
    full = lambda a: pl.BlockSpec(a.shape, lambda i, j: (0,) * a.ndim)
    grp = lambda a, dil: pl.BlockSpec((None, tm // dil, a.shape[2]), lambda i, j: (i, j, 0))
    dils = [dil for _, dil in DIL_PATTERNS]
    n_slabs = len(dils) * (1 + DIL_TILES_PER_GROUP)
    return pl.pallas_call(
        _outproj1_kernel,
        grid=(b, t // tm),
        in_specs=[grp(a, dl) for a, dl in zip(o_groups, dils)] + [grp(a, dl) for a, dl in zip(lse_groups, dils)]
                 + [tok(z.shape[2]), full(w), tok(d), pl.BlockSpec((None, 3, d), lambda i, j: (i, 0, 0)), full(final_g)],
        out_specs=tok(d),
        out_shape=jax.ShapeDtypeStruct((b, t, d), F32),
        scratch_shapes=[pltpu.VMEM((n_slabs, tm, LANES), F32)],
        compiler_params=_cparams("parallel", "parallel"),
        name="outproj_dil_final",
    )(*o_groups, *lse_groups, z, w, x, mod, final_g)


def _rope_tables(t):
    inv = ROPE_THETA ** (-jnp.arange(HALF, dtype=F32) / HALF)
    ang = jnp.arange(t, dtype=F32)[:, None] * inv[None, :]
    cos = jnp.tile(jnp.cos(ang), (1, 4))
    sin = jnp.tile(jnp.sin(ang), (1, 4))
    sign = jnp.where(jnp.arange(LANES) < 2 * HALF, -1.0, 1.0).astype(F32)
    return cos, sin * sign[None, :]


def _pair_rope_layout(w, n_pairs):
    d = w.shape[0]
    return w.reshape(d, n_pairs, 2, 2, HALF).transpose(0, 1, 3, 2, 4).reshape(d, n_pairs * LANES)


def _nsa_weights(w_in, pe_k, pe_v, ck_w1, ck_w2, cv_w1, cv_w2, w_out):
    d = w_in.shape[0]
    aw = N_HEADS * HEAD_DIM
    kvw = NSA_KV_HEADS * HEAD_DIM
    cuts = np.cumsum([aw] + [kvw] * 6 + [aw]).tolist()
    q, k_c, v_c, k_s, v_s, k_w, v_w, z, gl = jnp.split(w_in, cuts, axis=1)
    scale = HEAD_DIM ** -0.5 * LOG2E
    wq = (q * scale).reshape(d, NSA_KV_HEADS, NSA_GROUP, 2, HALF).transpose(0, 2, 3, 1, 4).reshape(d, aw)
    k_lay = lambda w: _pair_rope_layout(w, 1)
    wkv = jnp.concatenate([k_lay(k_c), v_c, k_lay(k_s), v_s, k_lay(k_w), v_w], axis=1)
    wz = z.reshape(d, NSA_KV_HEADS, NSA_GROUP, HEAD_DIM).transpose(0, 2, 1, 3).reshape(d, aw)
    wo = w_out.reshape(NSA_KV_HEADS, NSA_GROUP, HEAD_DIM, -1).transpose(1, 0, 2, 3).reshape(aw, -1)
    wg = jnp.pad(gl, ((0, 0), (0, LANES - gl.shape[1])))

    def grouped(w1_half, rope_lanes):
        out = []
        for g in range(NSA_KV_HEADS):
            if rope_lanes:
                src = w1_half.reshape(CMP_STRIDE, 2, 1, HALF, CMP_HIDDEN)
                pads = ((0, 0), (0, 0), (g, NSA_KV_HEADS - 1 - g), (0, 0), (0, 0))
            else:
                src = w1_half.reshape(CMP_STRIDE, 1, HEAD_DIM, CMP_HIDDEN)
                pads = ((0, 0), (g, NSA_KV_HEADS - 1 - g), (0, 0), (0, 0))
            out.append(jnp.pad(src, pads).reshape(CMP_STRIDE * LANES, CMP_HIDDEN))
        return jnp.concatenate(out, axis=1)

    def pe_rows(pe_half, rope_lanes):
        if rope_lanes:
            v = jnp.broadcast_to(pe_half.reshape(CMP_STRIDE, 2, 1, HALF), (CMP_STRIDE, 2, NSA_KV_HEADS, HALF))
        else:
            v = jnp.broadcast_to(pe_half.reshape(CMP_STRIDE, 1, HEAD_DIM), (CMP_STRIDE, NSA_KV_HEADS, HEAD_DIM))
        return v.reshape(1, CMP_STRIDE * LANES)

    def w2_padded(w2, rope_lanes):
        out = []
        for g in range(NSA_KV_HEADS):
            if rope_lanes:
                src = w2.reshape(CMP_HIDDEN, 2, 1, HALF)
                pads = ((0, 0), (0, 0), (g, NSA_KV_HEADS - 1 - g), (0, 0))
            else:
                src = w2.reshape(CMP_HIDDEN, 1, HEAD_DIM)
                pads = ((0, 0), (g, NSA_KV_HEADS - 1 - g), (0, 0))
            out.append(jnp.pad(src, pads).reshape(CMP_HIDDEN, LANES))
        return jnp.concatenate(out, axis=0)

    k1 = ck_w1.reshape(2, CMP_STRIDE, HEAD_DIM, CMP_HIDDEN)
    v1 = cv_w1.reshape(2, CMP_STRIDE, HEAD_DIM, CMP_HIDDEN)
    compress_consts = (
        pe_rows(pe_k[:CMP_STRIDE], True), pe_rows(pe_k[CMP_STRIDE:], True),
        pe_rows(pe_v[:CMP_STRIDE], False), pe_rows(pe_v[CMP_STRIDE:], False),
        grouped(k1[0], True).astype(BF16), grouped(k1[1], True).astype(BF16),
        grouped(v1[0], False).astype(BF16), grouped(v1[1], False).astype(BF16),
        w2_padded(ck_w2, True).astype(BF16), w2_padded(cv_w2, False).astype(BF16),
    )
    return wq.astype(BF16), wkv.astype(BF16), wz.astype(BF16), wg.astype(BF16), wo.astype(BF16), compress_consts


def _dil_weights(w_in, w_out):
    d = w_in.shape[0]
    q, k, v, z = jnp.split(w_in, 4, axis=1)
    n_pairs = len(DIL_HEAD_PAIRS)
    order = [hd for pair in DIL_HEAD_PAIRS for hd in pair]

    def tile_order(w, axis):
        parts = []
        for hd in order:
            sl = [slice(None)] * w.ndim
            sl[axis] = slice(hd * HEAD_DIM, (hd + 1) * HEAD_DIM)
            parts.append(w[tuple(sl)])
        return jnp.concatenate(parts, axis=axis)

    scale = HEAD_DIM ** -0.5 * LOG2E
    wq = _pair_rope_layout(tile_order(q * scale, 1), n_pairs)
    wk = _pair_rope_layout(tile_order(k, 1), n_pairs)
    return (wq.astype(BF16), wk.astype(BF16), tile_order(v, 1).astype(BF16), tile_order(z, 1).astype(BF16),
            tile_order(w_out, 0).astype(BF16))


def _selection_constants(t):
    n_cmp_rows = t // CMP_STRIDE
    n_slc = t // SLC_LEN
    c_start = CMP_STRIDE * np.arange(n_cmp_rows)
    s_start = SLC_LEN * np.arange(n_slc)
    ovl_t = ((c_start[None, :] < s_start[:, None] + SLC_LEN)
             & (c_start[None, :] + CMP_LEN > s_start[:, None])).astype(np.float32)
    ovl_t[:, (t - CMP_LEN) // CMP_STRIDE + 1:] = 0.0
    e = np.zeros((NSA_KV_HEADS, t // KEY_CHUNK, LANES, KEY_CHUNK), np.float32)
    keys = np.arange(t)
    for g in range(NSA_KV_HEADS):
        e[g, keys // KEY_CHUNK, g * n_slc + keys // SLC_LEN, keys % KEY_CHUNK] = 1.0
    return jnp.asarray(ovl_t), jnp.asarray(e, dtype=BF16)


@jax.jit
def kernel(x, c, norm_g, ada_w, ada_b, nsa_w_in, nsa_pe_k, nsa_pe_v, nsa_ck_w1, nsa_ck_w2,
           nsa_cv_w1, nsa_cv_w2, nsa_w_out, dil_w_in, dil_w_out, final_g):
    b, t, d = x.shape
    assert t % SLC_CK == 0 and t % TM == 0 and d % LANES == 0
    assert NSA_KV_HEADS * (t // SLC_LEN) <= LANES and t // CMP_STRIDE == LANES
    mod = _adaln_mod(c, ada_w, ada_b).reshape(ada_w.shape[0], b, 3, d)
    cos, sin = _rope_tables(t)

    wq, wkv, wz, wg, wo, compress_consts = _nsa_weights(
        nsa_w_in[0], nsa_pe_k[0], nsa_pe_v[0], nsa_ck_w1[0], nsa_ck_w2[0], nsa_cv_w1[0], nsa_cv_w2[0], nsa_w_out[0])
    qp, qr, kc, vc, ks, vs, kw, vw, z, gl = _inproj0(x, mod[0], norm_g[0:1], cos, sin, wq, wkv, wz, wg)
    kcmp, vcmp = _compress(kc, vc, compress_consts)
    ovl_t, e_sel = _selection_constants(t)
    ocmp, msel = _cmp_attn(qp, kcmp, vcmp, ovl_t, gl)
    og = _nsa_attn(qr, ks, vs, kw, vw, msel, e_sel, ocmp, z, gl)

    dq, dk, dv, dz, dwo = _dil_weights(dil_w_in[0], dil_w_out[0])
    n_grp = len(DIL_PATTERNS)
    x1, *qkv, z1 = _outproj0_inproj1(og, wo, x, mod[0], mod[1], norm_g[1:2], cos, sin, dq, dk, dv, dz)
    assert all(win // dil == DIL_WIN and TM % (dil * 16) == 0 for win, dil in DIL_PATTERNS)
    o_groups, lse_groups = _dil_attn(qkv[0:n_grp], qkv[n_grp:2 * n_grp], qkv[2 * n_grp:3 * n_grp], t)
    return _outproj1(o_groups, lse_groups, z1, dwo, x1, mod[1], final_g.reshape(1, d))
```

```python
import functools
import math

import numpy as np
import jax
import jax.numpy as jnp
from jax import lax
from jax.experimental import pallas as pl
from jax.experimental.pallas import tpu as pltpu

F32 = jnp.float32
BF16 = jnp.bfloat16
HIGHEST = lax.Precision.HIGHEST

HEAD_DIM = 64
HALF = HEAD_DIM // 2
N_HEADS = 16
ROPE_THETA = 10000.0
NORM_EPS = 1e-6
NSA_KV_HEADS = 2
NSA_GROUP = N_HEADS // NSA_KV_HEADS
NSA_BRANCHES = 3
CMP_LEN = 32
CMP_STRIDE = 16
CMP_HIDDEN = 256
SLC_LEN = 64
SLC_TOP = 16
WIN_LEN = 512
DIL_PATTERNS = ((128, 1), (512, 4), (2048, 16))
DIL_GROUP_HEADS = (6, 5, 5)
DIL_WIN = 128


def _dil_tile_plan():
    offs = np.cumsum((0,) + DIL_GROUP_HEADS)
    pairs, group_tiles, leftovers = [], [], []
    for gi, hn in enumerate(DIL_GROUP_HEADS):
        heads = list(range(offs[gi], offs[gi + 1]))
        group_tiles.append([])
        for i in range(0, hn - hn % 2, 2):
            group_tiles[gi].append((len(pairs), (0, 1)))
            pairs.append((heads[i], heads[i + 1]))
        if hn % 2:
            leftovers.append((gi, heads[-1]))
    assert len(leftovers) % 2 == 0
    for (ga, ha), (gb, hb) in zip(leftovers[0::2], leftovers[1::2]):
        group_tiles[ga].append((len(pairs), (0,)))
        group_tiles[gb].append((len(pairs), (1,)))
        pairs.append((ha, hb))
    return tuple(pairs), tuple(tuple(tiles) for tiles in group_tiles)


DIL_HEAD_PAIRS, DIL_GROUP_TILES = _dil_tile_plan()
DIL_TILES_PER_GROUP = len(DIL_GROUP_TILES[0])
assert all(len(tiles) == DIL_TILES_PER_GROUP for tiles in DIL_GROUP_TILES)

LANES = 128
VMEM_LIMIT_BYTES = 48 * 1024 * 1024

NEG = -1e30
LOG2E = math.log2(math.e)
LN2 = math.log(2.0)
TQ = 128
KEY_CHUNK = 128
SLC_CK = 512
TM = 512
TM_OUT = 1024
DIL_STEP_TOKENS = 1024
NSA_STEP_TOKENS = 256
CMP_STEP_TOKENS = 1024


def _cparams(*sem):
    return pltpu.CompilerParams(dimension_semantics=sem, vmem_limit_bytes=VMEM_LIMIT_BYTES)


def _dot(a, b):
    return jnp.dot(a, b, preferred_element_type=F32)


def _dot_t(a, b):
    return lax.dot_general(a, b, (((1,), (1,)), ((), ())), preferred_element_type=F32)


def _silu(v):
    return v * jax.nn.sigmoid(v)


def _mod_kernel(c_ref, w_ref, b_ref, o_ref):
    s = _silu(c_ref[...])
    o_ref[...] = jnp.dot(s, w_ref[...], precision=HIGHEST, preferred_element_type=F32) + b_ref[...]


def _adaln_mod(c, ada_w, ada_b):
    depth, d, n3 = ada_w.shape
    b = c.shape[0]
    tn = 1024
    return pl.pallas_call(
        _mod_kernel,
        grid=(depth, n3 // tn),
        in_specs=[
            pl.BlockSpec((b, d), lambda i, j: (0, 0)),
            pl.BlockSpec((None, d, tn), lambda i, j: (i, 0, j)),
            pl.BlockSpec((None, 1, tn), lambda i, j: (i, 0, j)),
        ],
        out_specs=pl.BlockSpec((None, b, tn), lambda i, j: (i, 0, j)),
        out_shape=jax.ShapeDtypeStruct((depth, b, n3), F32),
        compiler_params=_cparams("arbitrary", "arbitrary"),
        name="adaln_mod",
    )(c, ada_w, ada_b.reshape(depth, 1, n3))


def _modulated_norm(x, mod_ref, g_ref):
    ms = jnp.mean(x * x, axis=-1, keepdims=True)
    y = x * lax.rsqrt(ms + NORM_EPS) * g_ref[...]
    return y * (1.0 + mod_ref[1:2, :]) + mod_ref[0:1, :]


def _rope_tile(v, cos, sin):
    return v * cos + pltpu.roll(v, 2 * HALF, axis=1) * sin


def _store_key_chunks_t(kt_ref, k):
    dim = lax.broadcasted_iota(jnp.int32, (LANES, KEY_CHUNK), 0)
    for c in range(kt_ref.shape[1]):
        kt = k[c * KEY_CHUNK:(c + 1) * KEY_CHUNK, :].T
        for g in range(kt_ref.shape[0]):
            kt_ref[g, c] = jnp.where((dim % HEAD_DIM) // HALF == g, kt, 0.0).astype(BF16)


def _inproj0_kernel(x_ref, mod_ref, g_ref, cos_ref, sin_ref, wq_ref, wkv_ref, wz_ref, wg_ref,
                    qp_ref, qr_ref, kc_ref, vc_ref, ks_ref, vs_ref, kw_ref, vw_ref, z_ref, gl_ref, stage):
    tm = x_ref.shape[0]
    h = _modulated_norm(x_ref[...], mod_ref, g_ref).astype(BF16)
    cos = cos_ref[...]
    sin = sin_ref[...]
    q = _dot(h, wq_ref[...])
    for j in range(q.shape[1] // LANES):
        sl = slice(j * LANES, (j + 1) * LANES)
        qp_ref[:, sl] = q[:, sl].astype(BF16)
        qr_ref[:, sl] = _rope_tile(q[:, sl], cos, sin).astype(BF16)
    kv = _dot(h, wkv_ref[...])
    tiles = [kv[:, i * LANES:(i + 1) * LANES] for i in range(6)]
    for slab, (tile, o_ref) in enumerate(((tiles[0], kc_ref), (tiles[1], vc_ref))):
        stage[slab] = tile
        for r in range(CMP_STRIDE):
            rows = stage[slab, pl.ds(r, tm // CMP_STRIDE, stride=CMP_STRIDE), :]
            o_ref[:, r * LANES:(r + 1) * LANES] = rows.astype(BF16)
    _store_key_chunks_t(ks_ref, _rope_tile(tiles[2], cos, sin))
    vs_ref[...] = tiles[3].astype(BF16)
    _store_key_chunks_t(kw_ref, _rope_tile(tiles[4], cos, sin))
    vw_ref[...] = tiles[5].astype(BF16)
    z_ref[...] = _silu(_dot(h, wz_ref[...])).astype(BF16)
    gl_ref[...] = _dot(h, wg_ref[...])


def _inproj0(x, mod, g, cos, sin, wq, wkv, wz, wg):
    b, t, d = x.shape
    nq, nz = wq.shape[1], wz.shape[1]
    tok = lambda n: pl.BlockSpec((None, TM, n), lambda i, j: (i, j, 0))
    full = lambda a: pl.BlockSpec(a.shape, lambda i, j: (0,) * a.ndim)
    tab = pl.BlockSpec((TM, LANES), lambda i, j: (j, 0))
    shp = lambda n, dt: jax.ShapeDtypeStruct((b, t, n), dt)
    kt = pl.BlockSpec((None, NSA_KV_HEADS, TM // KEY_CHUNK, LANES, KEY_CHUNK), lambda i, j: (i, 0, j, 0, 0))
    kt_shp = jax.ShapeDtypeStruct((b, NSA_KV_HEADS, t // KEY_CHUNK, LANES, KEY_CHUNK), BF16)
    kv = tok(LANES)
    kv_shp = shp(LANES, BF16)
    cm = pl.BlockSpec((None, TM // CMP_STRIDE, CMP_STRIDE * LANES), lambda i, j: (i, j, 0))
    cm_shp = jax.ShapeDtypeStruct((b, t // CMP_STRIDE, CMP_STRIDE * LANES), BF16)
    return pl.pallas_call(
        _inproj0_kernel,
        grid=(b, t // TM),
        in_specs=[tok(d), pl.BlockSpec((None, 3, d), lambda i, j: (i, 0, 0)), full(g), tab, tab,
                  full(wq), full(wkv), full(wz), full(wg)],
        out_specs=[tok(nq), tok(nq), cm, cm, kt, kv, kt, kv, tok(nz), tok(LANES)],
        out_shape=[shp(nq, BF16), shp(nq, BF16), cm_shp, cm_shp, kt_shp, kv_shp, kt_shp, kv_shp,
                   shp(nz, BF16), shp(LANES, F32)],
        scratch_shapes=[pltpu.VMEM((2, TM, LANES), F32)],
        compiler_params=_cparams("parallel", "parallel"),
        name="inproj_nsa",
    )(x, mod, g, cos, sin, wq, wkv, wz, wg)


def _compress_kernel(ak_ref, av_ref, pekt_ref, pekb_ref, pevt_ref, pevb_ref,
                     w1kt_ref, w1kb_ref, w1vt_ref, w1vb_ref, w2k_ref, w2v_ref, kcmp_ref, vcmp_ref):
    def mlp(a_ref, pet_ref, peb_ref, w1t_ref, w1b_ref, w2_ref):
        a = a_ref[...].astype(F32)
        top = _dot((a + pet_ref[...]).astype(BF16), w1t_ref[...])
        bot = _dot((a + peb_ref[...]).astype(BF16), w1b_ref[...])
        n = bot.shape[0]
        hid = top + pltpu.roll(bot, n - 1, axis=0)
        return _dot(_silu(hid).astype(BF16), w2_ref[...])

    kcmp_ref[...] = mlp(ak_ref, pekt_ref, pekb_ref, w1kt_ref, w1kb_ref, w2k_ref).astype(BF16)
    vcmp_ref[...] = mlp(av_ref, pevt_ref, pevb_ref, w1vt_ref, w1vb_ref, w2v_ref).astype(BF16)


def _compress(ak, av, consts):
    b, n, w = ak.shape
    full = lambda a: pl.BlockSpec(a.shape, lambda i: (0,) * a.ndim)
    blk = pl.BlockSpec((None, n, w), lambda i: (i, 0, 0))
    out = pl.BlockSpec((None, n, LANES), lambda i: (i, 0, 0))
    return pl.pallas_call(
        _compress_kernel,
        grid=(b,),
        in_specs=[blk, blk] + [full(a) for a in consts],
        out_specs=[out, out],
        out_shape=[jax.ShapeDtypeStruct((b, n, LANES), BF16)] * 2,
        compiler_params=_cparams("parallel"),
        name="compress",
    )(ak, av, *consts)


def _stack_heads(q_ref, rows, lane_sets):
    n_tiles = q_ref.shape[1] // LANES
    parts = []
    for j in range(n_tiles):
        qj = q_ref[rows, j * LANES:(j + 1) * LANES]
        parts.append(jnp.where(lane_sets, qj, jnp.zeros_like(qj)))
    return jnp.concatenate(parts, axis=0)


def _head_gate_rows(gates, g, branches, lane):
    blocks = []
    for j in range(NSA_GROUP):
        base = (g * NSA_GROUP + j) * NSA_BRANCHES
        kept = jnp.zeros_like(gates)
        for br in branches:
            kept = jnp.where(lane == base + br, gates, kept)
        hi = kept.astype(BF16)
        blocks.append(jnp.concatenate([hi, (kept - hi.astype(F32)).astype(BF16)], axis=1))
    return jnp.concatenate(blocks, axis=0)


def _gate_spread(branches):
    col = np.arange(2 * LANES) % LANES
    spread = np.zeros((2 * LANES, LANES * len(branches)), np.float32)
    for i, br in enumerate(branches):
        spread[col % NSA_BRANCHES == br, i * LANES:(i + 1) * LANES] = 1.0
    return jnp.asarray(spread, dtype=BF16)


def _cmp_attn_kernel(qp_ref, kcmp_ref, vcmp_ref, ovl_ref, gl_ref, ocmp_ref, msel_ref, *, n_slc, tq):
    n_cmp = vcmp_ref.shape[0]
    hg = qp_ref.shape[1] // LANES
    n_sub = qp_ref.shape[0] // tq
    assert n_cmp == LANES and tq == LANES
    row = lax.broadcasted_iota(jnp.int32, (LANES, tq), 0)
    col = lax.broadcasted_iota(jnp.int32, (LANES, tq), 1)
    jb = lax.broadcasted_iota(jnp.int32, (n_slc, tq), 0)
    tok_rows = lambda sub: slice(sub * tq, (sub + 1) * tq)
    t0 = lambda sub: pl.program_id(1) * qp_ref.shape[0] + sub * tq
    units = [(sub, g) for sub in range(n_sub) for g in range(NSA_KV_HEADS)]
    eye = jnp.where(row == col, 1.0, 0.0).astype(BF16)
    k_sel = jnp.concatenate([kcmp_ref[...], eye], axis=1)
    v_t = vcmp_ref[...].T

    q_t = [[qp_ref[tok_rows(sub), j * LANES:(j + 1) * LANES].T for j in range(hg)] for sub in range(n_sub)]
    gates_t = [jax.nn.sigmoid(gl_ref[tok_rows(sub), :]).T for sub in range(n_sub)]
    seen = [jnp.concatenate([jnp.where(CMP_STRIDE * row + (CMP_LEN - 1) <= t0(sub) + col, 0.0, NEG).astype(BF16)] * hg,
                            axis=1) for sub in range(n_sub)]
    weights = {}
    for sub, g in units:
        in_group = ((row % HEAD_DIM) // HALF) == g
        q_cols = jnp.concatenate([jnp.where(in_group, v, jnp.zeros_like(v)) for v in q_t[sub]], axis=1)
        s = _dot(k_sel, jnp.concatenate([q_cols, seen[sub]], axis=0))
        m = jnp.max(s, axis=0, keepdims=True)
        m = jnp.where(m > 0.5 * NEG, m, 0.0)
        e = jnp.exp(s - m)
        den = jnp.sum(e, axis=0, keepdims=True)
        weights[sub, g] = e * (1.0 / jnp.where(den > 0, den, 1.0))
    outs, imps = {}, {}
    for sub, g in units:
        p = weights[sub, g]
        gate = jnp.concatenate([gates_t[sub][(g * NSA_GROUP + j) * NSA_BRANCHES:(g * NSA_GROUP + j) * NSA_BRANCHES + 1, :]
                                for j in range(hg)], axis=1)
        outs[sub, g] = gate * _dot(v_t, p.astype(BF16))
        p_sum = p[:, 0:tq]
        for j in range(1, hg):
            p_sum = p_sum + p[:, j * tq:(j + 1) * tq]
        imps[sub, g] = jnp.dot(ovl_ref[...], p_sum, precision=HIGHEST, preferred_element_type=F32)
    picked = {}
    for sub, g in units:
        cur = (t0(sub) + lax.broadcasted_iota(jnp.int32, (n_slc, tq), 1)) // SLC_LEN
        visible = jb <= cur
        forced = (jb == 0) | (jb == cur) | (jb == cur - 1)
        rank = jnp.where(forced, -NEG, jnp.where(visible, imps[sub, g], NEG))
        cnt = jnp.zeros((n_slc, tq), F32)
        for jp in range(n_slc):
            rj = rank[jp:jp + 1, :]
            tie = jnp.where(jb > jp, 1.0, 0.0)
            cnt = cnt + jnp.where(rj > rank, 1.0, jnp.where(rj == rank, tie, 0.0))
        picked[sub, g] = jnp.where(visible, jnp.where(cnt < SLC_TOP, 1.0, 0.0), 0.0)
    pad = jnp.zeros((LANES - NSA_KV_HEADS * n_slc, tq), F32)
    for sub in range(n_sub):
        for j in range(hg):
            cols = slice(j * tq, (j + 1) * tq)
            o_t = jnp.where(row < HEAD_DIM, outs[sub, 0][:, cols], outs[sub, 1][:, cols])
            ocmp_ref[tok_rows(sub), j * LANES:(j + 1) * LANES] = o_t.T.astype(BF16)
        sel_t = jnp.concatenate([picked[sub, g] for g in range(NSA_KV_HEADS)] + [pad], axis=0)
        msel_ref[tok_rows(sub), :] = sel_t.T.astype(BF16)


def _cmp_attn(qp, kcmp, vcmp, ovl_t, gl):
    b, t, nq = qp.shape
    n_cmp = kcmp.shape[1]
    n_slc = t // SLC_LEN
    tok = lambda n: pl.BlockSpec((None, CMP_STEP_TOKENS, n), lambda i, j: (i, j, 0))
    per_b = pl.BlockSpec((None, n_cmp, LANES), lambda i, j: (i, 0, 0))
    full = lambda a: pl.BlockSpec(a.shape, lambda i, j: (0,) * a.ndim)
    return pl.pallas_call(
        functools.partial(_cmp_attn_kernel, n_slc=n_slc, tq=TQ),
        grid=(b, t // CMP_STEP_TOKENS),
        in_specs=[tok(nq), per_b, per_b, full(ovl_t), tok(LANES)],
        out_specs=[tok(nq), tok(LANES)],
        out_shape=[jax.ShapeDtypeStruct((b, t, nq), BF16), jax.ShapeDtypeStruct((b, t, LANES), BF16)],
        compiler_params=_cparams("parallel", "parallel"),
        name="cmp_attn_select",
    )(qp, kcmp, vcmp, ovl_t, gl)


def _nsa_attn_kernel(*refs, tq, **kw):
    *io_refs, m_scr, acc_scr = refs
    n_sub = io_refs[0].shape[0] // tq
    tiles = [_nsa_attn_tile(*io_refs, m_scr.at[sub], acc_scr.at[sub], tok=slice(sub * tq, (sub + 1) * tq),
                            qi=pl.program_id(1) * n_sub + sub, **kw) for sub in range(n_sub)]
    for _ in range(NSA_STAGES):
        for tile in tiles:
            next(tile, None)


NSA_STAGES = 5


def _nsa_attn_tile(qr_ref, ks_ref, vs_ref, kw_ref, vw_ref, msel_ref, e_ref, gsp_ref, ocmp_ref, z_ref, gl_ref,
                   og_ref, m_scr, acc_scr, *, tok, qi, n_slc, ck):
    tq = tok.stop - tok.start
    hg = qr_ref.shape[1] // LANES
    rows = hg * tq
    n_wc = WIN_LEN // tq + 1
    per_ck = ck // KEY_CHUNK
    t0 = qi * tq
    lane = lax.broadcasted_iota(jnp.int32, (tq, LANES), 1)
    a_idx = lax.broadcasted_iota(jnp.int32, (tq, LANES), 0)

    eye = jnp.where(lane == a_idx, 1.0, 0.0).astype(BF16)
    earlier = jnp.where((lane % n_slc) < t0 // SLC_LEN, 0.0, NEG)
    mneg = jnp.where(msel_ref[tok, :].astype(F32) > 0.5, earlier, NEG).astype(BF16)
    mneg_rows = jnp.concatenate([mneg] * hg, axis=0)
    eye_rows = jnp.concatenate([eye] * hg, axis=0)

    def with_selector(sel_rows):
        q_rows = jnp.concatenate([qr_ref[tok, j * LANES:(j + 1) * LANES] for j in range(hg)], axis=0)
        return jnp.concatenate([q_rows, sel_rows], axis=1)

    zero_b = jnp.zeros((tq, LANES), BF16)
    neg_b = jnp.full((tq, LANES), NEG, BF16)
    causal_b = jnp.where(lane <= a_idx, 0.0, NEG).astype(BF16)
    far_b = jnp.where(lane > a_idx, 0.0, NEG).astype(BF16)
    own_half = [lane < HEAD_DIM, lane >= HEAD_DIM]

    def values_with_ones(v, g):
        in_own_half = (lax.broadcasted_iota(jnp.int32, v.shape, 1) // HEAD_DIM) == g
        return jnp.where(in_own_half, v, jnp.ones_like(v))

    def lane_tiles(v):
        return [v[:, i * LANES:(i + 1) * LANES] for i in range(v.shape[1] // LANES)]

    def tile_max(tiles):
        mx = tiles[0]
        for v in tiles[1:]:
            mx = jnp.maximum(mx, v)
        return jnp.max(mx, axis=-1, keepdims=True)

    def probs(tiles, m):
        return jnp.concatenate([jnp.exp(v - m) for v in tiles], axis=1).astype(BF16)

    m_scr[...] = jnp.full(m_scr.shape, NEG, F32)
    acc_scr[...] = jnp.zeros(acc_scr.shape, F32)

    def attend_chunks(c0, n):
        k0 = pl.multiple_of(c0 * KEY_CHUNK, KEY_CHUNK)
        q_sel = with_selector(mneg_rows)
        scores = []
        for g in range(NSA_KV_HEADS):
            kt = jnp.concatenate([ks_ref[g, c0 + i] for i in range(n)], axis=1)
            eb = jnp.concatenate([e_ref[g, c0 + i] for i in range(n)], axis=1)
            scores.append(_dot(q_sel, jnp.concatenate([kt, eb], axis=0)))
        groups = range(NSA_KV_HEADS)
        m_old = [m_scr[g] for g in groups]
        m_new = [jnp.maximum(m_old[g], tile_max(lane_tiles(scores[g]))) for g in groups]
        p = [probs(lane_tiles(scores[g]), m_new[g]) for g in groups]
        pv = [_dot(p[g], values_with_ones(vs_ref[pl.ds(k0, n * KEY_CHUNK), :], g)) for g in groups]
        for g in groups:
            acc_scr[g] = jnp.exp(m_old[g] - m_new[g]) * acc_scr[g] + pv[g]
            m_scr[g] = m_new[g]

    def earlier_keys(kc, carry):
        attend_chunks(kc * per_ck, per_ck)
        return carry

    lax.fori_loop(0, qi // per_ck, earlier_keys, 0)
    for rem in range(1, per_ck):
        @pl.when(qi % per_ck == rem)
        def _():
            attend_chunks((qi // per_ck) * per_ck, rem)
    yield

    kd = pl.multiple_of(t0, tq)
    zeros_v = jnp.zeros((tq, LANES), BF16)
    win_chunks = [qi - (n_wc - 1 - c) for c in range(n_wc - 1)]
    win_bias = [jnp.where(cidx >= 0, far_b if c == 0 else zero_b, neg_b) for c, cidx in enumerate(win_chunks)]
    pairs = [(c, c + 1) for c in range(0, n_wc - 1, 2)]
    q_eye = with_selector(eye_rows)
    s_own, scores = [], []
    for g in range(NSA_KV_HEADS):
        own_k = jnp.concatenate([jnp.concatenate([kw_ref[g, qi], causal_b], axis=0),
                                 jnp.concatenate([ks_ref[g, qi], causal_b], axis=0)], axis=1)
        kbs = [jnp.concatenate([kw_ref[g, jnp.maximum(cidx, 0)], win_bias[c]], axis=0)
               for c, cidx in enumerate(win_chunks)]
        s_own.append(_dot(q_eye, own_k))
        scores.append([_dot(q_eye, jnp.concatenate([kbs[c] for c in pr], axis=1)) for pr in pairs])

    yield
    groups = range(NSA_KV_HEADS)
    s_win_own = [s_own[g][:, 0:tq] for g in groups]
    s_slc_own = [s_own[g][:, tq:2 * tq] for g in groups]
    m_win = [tile_max([v for s in scores[g] for v in lane_tiles(s)] + [s_win_own[g]]) for g in groups]
    m_old = [m_scr[g] for g in groups]
    m_slc = [jnp.maximum(m_old[g], jnp.max(s_slc_own[g], axis=-1, keepdims=True)) for g in groups]
    p_all = [jnp.concatenate([probs(lane_tiles(s), m_win[g]) for s in scores[g]]
                             + [probs([s_win_own[g]], m_win[g]), probs([s_slc_own[g]], m_slc[g])], axis=1) for g in groups]
    yield
    acc_slc, acc_win = [], []
    for g in groups:
        vbs = [values_with_ones(vw_ref[pl.ds(pl.multiple_of(jnp.maximum(cidx, 0) * tq, tq), tq), :], g)
               for cidx in win_chunks] + [values_with_ones(vw_ref[pl.ds(kd, tq), :], g)]
        win_v = [jnp.concatenate([v, zeros_v], axis=1) for v in vbs]
        slc_v = jnp.concatenate([zeros_v, values_with_ones(vs_ref[pl.ds(kd, tq), :], g)], axis=1)
        pv = _dot(p_all[g], jnp.concatenate(win_v + [slc_v], axis=0))
        acc_win.append(pv[:, 0:LANES])
        acc_slc.append(jnp.exp(m_old[g] - m_slc[g]) * acc_scr[g] + pv[:, LANES:2 * LANES])
    yield

    gates = jax.nn.sigmoid(gl_ref[tok, :])
    row_gates = [_dot(_head_gate_rows(gates, g, (1, 2), lane), gsp_ref[...]) for g in range(NSA_KV_HEADS)]

    def gated(accs, rs, branch_lanes):
        num = [accs[g][rs] * row_gates[g][rs, branch_lanes] for g in range(NSA_KV_HEADS)]
        sums = pltpu.roll(jnp.where(own_half[0], accs[1][rs], accs[0][rs]), HEAD_DIM, axis=1)
        return jnp.where(own_half[0], num[0], num[1]) / sums

    for j in range(hg):
        rs = slice(j * tq, (j + 1) * tq)
        cols = slice(j * LANES, (j + 1) * LANES)
        o = (ocmp_ref[tok, cols].astype(F32) + gated(acc_slc, rs, slice(0, LANES))
             + gated(acc_win, rs, slice(LANES, 2 * LANES)))
        og_ref[tok, cols] = (o * z_ref[tok, cols].astype(F32)).astype(BF16)


def _nsa_attn(qr, ks, vs, kw, vw, msel, e, ocmp, z, gl):
    b, t, nq = qr.shape
    gsp = _gate_spread((1, 2))
    hg = nq // LANES
    rows = hg * TQ
    assert TQ == KEY_CHUNK and WIN_LEN % TQ == 0
    tok = lambda n: pl.BlockSpec((None, NSA_STEP_TOKENS, n), lambda i, j: (i, j, 0))
    per_b = pl.BlockSpec((None, t, LANES), lambda i, j: (i, 0, 0))
    per_b_t = pl.BlockSpec((None, NSA_KV_HEADS, t // KEY_CHUNK, LANES, KEY_CHUNK), lambda i, j: (i, 0, 0, 0, 0))
    full = lambda a: pl.BlockSpec(a.shape, lambda i, j: (0,) * a.ndim)
    return pl.pallas_call(
        functools.partial(_nsa_attn_kernel, tq=TQ, n_slc=t // SLC_LEN, ck=SLC_CK),
        grid=(b, t // NSA_STEP_TOKENS),
        in_specs=[tok(nq), per_b_t, per_b, per_b_t, per_b, tok(LANES), full(e), full(gsp), tok(nq), tok(nq), tok(LANES)],
        out_specs=tok(nq),
        out_shape=jax.ShapeDtypeStruct((b, t, nq), BF16),
        scratch_shapes=[
            pltpu.VMEM((NSA_STEP_TOKENS // TQ, NSA_KV_HEADS, rows, LANES), F32),
            pltpu.VMEM((NSA_STEP_TOKENS // TQ, NSA_KV_HEADS, rows, LANES), F32),
        ],
        compiler_params=_cparams("parallel", "arbitrary"),
        name="nsa_slc_win_attn",
    )(qr, ks, vs, kw, vw, msel, e, gsp, ocmp, z, gl)


def _inproj1_kernel(og_ref, wo_ref, x_ref, mod0_ref, mod_ref, g_ref, cos_ref, sin_ref, wq_ref, wk_ref, wv_ref, wz_ref,
                    x1_ref, *refs):
    n_grp = len(DIL_PATTERNS)
    outs = [refs[a * n_grp:(a + 1) * n_grp] for a in range(3)]
    z_ref, stage = refs[3 * n_grp], refs[3 * n_grp + 1]
    tm = x_ref.shape[0]
    gw = DIL_TILES_PER_GROUP * LANES
    x1 = x_ref[...] + mod0_ref[2:3, :] * _dot(og_ref[...], wo_ref[...])
    x1_ref[...] = x1
    h = _modulated_norm(x1, mod_ref, g_ref).astype(BF16)
    cos = cos_ref[...]
    sin = sin_ref[...]
    projected = [_dot(h, w_ref[...]) for w_ref in (wq_ref, wk_ref, wv_ref, wz_ref)]
    for a, u in enumerate(projected[0:3]):
        tiles = [u[:, pt * LANES:(pt + 1) * LANES] for pt in range(len(DIL_HEAD_PAIRS))]
        if a < 2:
            tiles = [_rope_tile(v, cos, sin) for v in tiles]
        for gi, (_, dil) in enumerate(DIL_PATTERNS):
            for jt, (pt, _) in enumerate(DIL_GROUP_TILES[gi]):
                tile = tiles[pt]
                if dil == 1:
                    outs[a][gi][:, jt * LANES:(jt + 1) * LANES] = tile.astype(BF16)
                    continue
                slab = (a * n_grp + gi) * DIL_TILES_PER_GROUP + jt
                stage[slab] = tile
                for r in range(dil):
                    rows = stage[slab, pl.ds(r, tm // dil, stride=dil), :]
                    outs[a][gi][:, r * gw + jt * LANES:r * gw + (jt + 1) * LANES] = rows.astype(BF16)
    z_ref[...] = _silu(projected[3]).astype(BF16)


def _outproj0_inproj1(og, wo, x, mod0, mod, g, cos, sin, wq, wk, wv, wz):
    b, t, d = x.shape
    n = wq.shape[1]
    gw = DIL_TILES_PER_GROUP * LANES
    tok = lambda k: pl.BlockSpec((None, TM, k), lambda i, j: (i, j, 0))
    full = lambda a: pl.BlockSpec(a.shape, lambda i, j: (0,) * a.ndim)
    tab = pl.BlockSpec((TM, LANES), lambda i, j: (j, 0))
    mods = pl.BlockSpec((None, 3, d), lambda i, j: (i, 0, 0))
    grp_specs = [pl.BlockSpec((None, TM // dil, dil * gw), lambda i, j: (i, j, 0)) for _, dil in DIL_PATTERNS]
    grp_shapes = [jax.ShapeDtypeStruct((b, t // dil, dil * gw), BF16) for _, dil in DIL_PATTERNS]
    n_slabs = 3 * len(DIL_PATTERNS) * DIL_TILES_PER_GROUP
    return pl.pallas_call(
        _inproj1_kernel,
        grid=(b, t // TM),
        in_specs=[tok(og.shape[2]), full(wo), tok(d), mods, mods, full(g), tab, tab,
                  full(wq), full(wk), full(wv), full(wz)],
        out_specs=[tok(d)] + grp_specs * 3 + [tok(n)],
        out_shape=[jax.ShapeDtypeStruct((b, t, d), F32)] + grp_shapes * 3 + [jax.ShapeDtypeStruct((b, t, n), BF16)],
        scratch_shapes=[pltpu.VMEM((n_slabs, TM, LANES), F32)],
        compiler_params=_cparams("parallel", "parallel"),
        name="outproj_nsa_inproj_dil",
    )(og, wo, x, mod0, mod, g, cos, sin, wq, wk, wv, wz)


def _dil_attn_kernel(*refs, plans):
    n_grp = len(plans)
    step = pl.program_id(1)
    groups = []
    for gi, (members, tq, n_seq, blocks_per_seq) in enumerate(plans):
        q_ref, k_ref, v_ref = refs[3 * gi:3 * gi + 3]
        o_ref, lse_ref = refs[3 * n_grp + 2 * gi:3 * n_grp + 2 * gi + 2]
        groups.append(_dil_attn_group(q_ref, k_ref, v_ref, o_ref, lse_ref, step % blocks_per_seq, members, tq, n_seq))
    for _ in range(DIL_STAGES):
        for grp in groups:
            next(grp, None)


DIL_STAGES = 4


def _dil_attn_group(q_ref, k_ref, v_ref, o_ref, lse_ref, i, members, tq, n_seq):
    n_heads = sum(len(m) for m in members)
    tq_blk = q_ref.shape[0]
    gw = q_ref.shape[1] // n_seq
    n_tiles = gw // LANES
    n_sub = tq_blk // tq
    assert tq == LANES
    row = lax.broadcasted_iota(jnp.int32, (LANES, tq), 0)
    col = lax.broadcasted_iota(jnp.int32, (LANES, tq), 1)
    head_a = ((row % HEAD_DIM) // HALF) == 0
    diag_b = jnp.where(row <= col, 0.0, NEG)
    prev_b = jnp.where(row >= col, 0.0, NEG)
    neg_b = jnp.full((LANES, tq), NEG, F32)
    two = lambda v: jnp.concatenate([v, v], axis=1)
    units = [(sq, sub, jt) for sq in range(n_seq) for sub in range(n_sub) for jt in range(n_tiles)]
    rows_of = lambda sub: slice(sub * tq, (sub + 1) * tq)
    cols_of = lambda sq, jt: slice(sq * gw + jt * LANES, sq * gw + (jt + 1) * LANES)
    key_rows = lambda c: pl.ds(pl.multiple_of(jnp.maximum(i * n_sub + c, 0) * tq, tq), tq)
    q_cols, v_t = {}, {}
    for sq, sub, jt in units:
        q_t = q_ref[rows_of(sub), cols_of(sq, jt)].T
        zero = jnp.zeros_like(q_t)
        q_cols[sq, sub, jt] = jnp.concatenate([jnp.where(head_a, q_t, zero), jnp.where(head_a, zero, q_t)], axis=1)
        v_t[sq, sub, jt] = v_ref[key_rows(sub), cols_of(sq, jt)].T
        if sub == 0:
            v_t[sq, -1, jt] = v_ref[key_rows(-1), cols_of(sq, jt)].T
    yield
    probs_t, stats = {}, {}
    for sq, sub, jt in units:
        prev_bias = prev_b if sub > 0 else jnp.where(i > 0, prev_b, neg_b)
        bias = jnp.concatenate([two(prev_bias), two(diag_b)], axis=0)
        keys = jnp.concatenate([k_ref[key_rows(sub - 1), cols_of(sq, jt)], k_ref[key_rows(sub), cols_of(sq, jt)]], axis=0)
        s = _dot(keys, q_cols[sq, sub, jt]) + bias
        m = jnp.max(s, axis=0, keepdims=True)
        p = jnp.exp2(s - m)
        l = jnp.sum(p, axis=0, keepdims=True)
        probs_t[sq, sub, jt] = p.astype(BF16)
        stats[sq, sub, jt] = (m, l)
    yield
    for sq, sub, jt in units:
        m, l = stats[sq, sub, jt]
        o_t = _dot(jnp.concatenate([v_t[sq, sub - 1, jt], v_t[sq, sub, jt]], axis=1), probs_t[sq, sub, jt])
        inv = 1.0 / l
        o_t = jnp.where(row < HEAD_DIM, o_t[:, 0:tq] * inv[:, 0:tq], o_t[:, tq:2 * tq] * inv[:, tq:2 * tq])
        o_ref[rows_of(sub), cols_of(sq, jt)] = o_t.T.astype(BF16)
    yield
    for sq in range(n_seq):
        for sub in range(n_sub):
            lses = []
            for jt in range(n_tiles):
                m, l = stats[sq, sub, jt]
                lse = m * LN2 + jnp.log(l)
                lses += [lse[:, mb * tq:(mb + 1) * tq] for mb in members[jt]]
            top = lses[0]
            for v in lses[1:]:
                top = jnp.maximum(top, v)
            tot = jnp.zeros_like(top)
            for v in lses:
                tot = tot + jnp.exp(v - top)
            group_lse = top + jnp.log(tot) - math.log(n_heads)
            lse_ref[rows_of(sub), sq * LANES:(sq + 1) * LANES] = jnp.broadcast_to(group_lse, (LANES, tq)).T


def _dil_attn(qs, ks, vs, t):
    b = qs[0].shape[0]
    steps = t // DIL_STEP_TOKENS
    plans, in_specs, out_specs, out_shapes, operands = [], [], [], [], []
    for gi, (_, dil) in enumerate(DIL_PATTERNS):
        length = t // dil
        gw = qs[gi].shape[2] // dil
        tq_blk = min(DIL_STEP_TOKENS, length)
        blocks_per_seq = length // tq_blk
        n_seq = DIL_STEP_TOKENS // tq_blk
        assert dil * blocks_per_seq == steps * n_seq
        tok = lambda w, bps=blocks_per_seq, ns=n_seq, tb=tq_blk: pl.BlockSpec(
            (None, tb, ns * w), lambda bi, s: (bi, s % bps, s // bps))
        seq = pl.BlockSpec((None, length, n_seq * gw), lambda bi, s, bps=blocks_per_seq: (bi, 0, s // bps))
        plans.append((tuple(m for _, m in DIL_GROUP_TILES[gi]), min(TQ, length), n_seq, blocks_per_seq))
        in_specs += [tok(gw), seq, seq]
        operands += [qs[gi], ks[gi], vs[gi]]
        out_specs += [tok(gw), tok(LANES)]
        out_shapes += [jax.ShapeDtypeStruct(qs[gi].shape, BF16), jax.ShapeDtypeStruct((b, length, dil * LANES), F32)]
    outs = pl.pallas_call(
        functools.partial(_dil_attn_kernel, plans=tuple(plans)),
        grid=(b, steps),
        in_specs=in_specs,
        out_specs=out_specs,
        out_shape=out_shapes,
        compiler_params=_cparams("parallel", "parallel"),
        name="dil_attn",
    )(*operands)
    return outs[0::2], outs[1::2]


def _outproj1_kernel(*refs):
    n_grp = len(DIL_PATTERNS)
    o_refs, l_refs = refs[0:n_grp], refs[n_grp:2 * n_grp]
    z_ref, w_ref, x_ref, mod_ref, fg_ref, out_ref, stage = refs[2 * n_grp:]
    tm = x_ref.shape[0]
    gw = DIL_TILES_PER_GROUP * LANES

    def token_order(ref, gi, width, col0, slab):
        dil = DIL_PATTERNS[gi][1]
        if dil == 1:
            return ref[:, col0:col0 + LANES].astype(F32)
        for r in range(dil):
            stage[slab, pl.ds(r, tm // dil, stride=dil), :] = ref[:, r * width + col0:r * width + col0 + LANES].astype(F32)
        return stage[slab]

    ls = [token_order(l_refs[gi], gi, LANES, 0, gi) for gi in range(n_grp)]
    top = jnp.maximum(jnp.maximum(ls[0], ls[1]), ls[2])
    es = [jnp.exp(v - top) for v in ls]
    den = es[0] + es[1] + es[2]
    alphas = [e / den * float(n_grp) for e in es]
    lane = lax.broadcasted_iota(jnp.int32, (tm, LANES), 1)
    weighted = [None] * len(DIL_HEAD_PAIRS)
    for gi in range(n_grp):
        for jt, (pt, members) in enumerate(DIL_GROUP_TILES[gi]):
            ot = token_order(o_refs[gi], gi, gw, jt * LANES, n_grp + gi * DIL_TILES_PER_GROUP + jt) * alphas[gi]
            if len(members) == 2:
                weighted[pt] = ot
            else:
                mine = (lane // HEAD_DIM) == members[0]
                weighted[pt] = jnp.where(mine, ot, 0.0 if weighted[pt] is None else weighted[pt])
    parts = [(weighted[pt] * z_ref[:, pt * LANES:(pt + 1) * LANES].astype(F32)).astype(BF16)
             for pt in range(len(DIL_HEAD_PAIRS))]
    og = jnp.concatenate(parts, axis=1)
    y = _dot(og, w_ref[...])
    x2 = x_ref[...] + mod_ref[2:3, :] * y
    ms = jnp.mean(x2 * x2, axis=-1, keepdims=True)
    out_ref[...] = x2 * lax.rsqrt(ms + NORM_EPS) * fg_ref[...]


def _outproj1(o_groups, lse_groups, z, w, x, mod, final_g):
    b, t, d = x.shape
    tm = TM_OUT
    tok = lambda k: pl.BlockSpec((None, tm, k), lambda i, j: (i, j, 0))
    full = lambda a: pl.BlockSpec(a.shape, lambda i, j: (0,) * a.ndim)
    grp = lambda a, dil: pl.BlockSpec((None, tm // dil, a.shape[2]), lambda i, j: (i, j, 0))
    dils = [dil for _, dil in DIL_PATTERNS]
    n_slabs = len(dils) * (1 + DIL_TILES_PER_GROUP)
    return pl.pallas_call(
        _outproj1_kernel,
        grid=(b, t // tm),
        in_specs=[grp(a, dl) for a, dl in zip(o_groups, dils)] + [grp(a, dl) for a, dl in zip(lse_groups, dils)]
                 + [tok(z.shape[2]), full(w), tok(d), pl.BlockSpec((None, 3, d), lambda i, j: (i, 0, 0)), full(final_g)],
        out_specs=tok(d),
        out_shape=jax.ShapeDtypeStruct((b, t, d), F32),
        scratch_shapes=[pltpu.VMEM((n_slabs, tm, LANES), F32)],
        compiler_params=_cparams("parallel", "parallel"),
        name="outproj_dil_final",
    )(*o_groups, *lse_groups, z, w, x, mod, final_g)


def _rope_tables(t):
    inv = ROPE_THETA ** (-jnp.arange(HALF, dtype=F32) / HALF)
    ang = jnp.arange(t, dtype=F32)[:, None] * inv[None, :]
    cos = jnp.tile(jnp.cos(ang), (1, 4))
    sin = jnp.tile(jnp.sin(ang), (1, 4))
    sign = jnp.where(jnp.arange(LANES) < 2 * HALF, -1.0, 1.0).astype(F32)
    return cos, sin * sign[None, :]


def _pair_rope_layout(w, n_pairs):
    d = w.shape[0]
    return w.reshape(d, n_pairs, 2, 2, HALF).transpose(0, 1, 3, 2, 4).reshape(d, n_pairs * LANES)


def _nsa_weights(w_in, pe_k, pe_v, ck_w1, ck_w2, cv_w1, cv_w2, w_out):
    d = w_in.shape[0]
    aw = N_HEADS * HEAD_DIM
    kvw = NSA_KV_HEADS * HEAD_DIM
    cuts = np.cumsum([aw] + [kvw] * 6 + [aw]).tolist()
    q, k_c, v_c, k_s, v_s, k_w, v_w, z, gl = jnp.split(w_in, cuts, axis=1)
    scale = HEAD_DIM ** -0.5
    wq = (q * scale).reshape(d, NSA_KV_HEADS, NSA_GROUP, 2, HALF).transpose(0, 2, 3, 1, 4).reshape(d, aw)
    k_lay = lambda w: _pair_rope_layout(w, 1)
    wkv = jnp.concatenate([k_lay(k_c), v_c, k_lay(k_s), v_s, k_lay(k_w), v_w], axis=1)
    wz = z.reshape(d, NSA_KV_HEADS, NSA_GROUP, HEAD_DIM).transpose(0, 2, 1, 3).reshape(d, aw)
    wo = w_out.reshape(NSA_KV_HEADS, NSA_GROUP, HEAD_DIM, -1).transpose(1, 0, 2, 3).reshape(aw, -1)
    wg = jnp.pad(gl, ((0, 0), (0, LANES - gl.shape[1])))

    def grouped(w1_half, rope_lanes):
        out = []
        for g in range(NSA_KV_HEADS):
            if rope_lanes:
                src = w1_half.reshape(CMP_STRIDE, 2, 1, HALF, CMP_HIDDEN)
                pads = ((0, 0), (0, 0), (g, NSA_KV_HEADS - 1 - g), (0, 0), (0, 0))
            else:
                src = w1_half.reshape(CMP_STRIDE, 1, HEAD_DIM, CMP_HIDDEN)
                pads = ((0, 0), (g, NSA_KV_HEADS - 1 - g), (0, 0), (0, 0))
            out.append(jnp.pad(src, pads).reshape(CMP_STRIDE * LANES, CMP_HIDDEN))
        return jnp.concatenate(out, axis=1)

    def pe_rows(pe_half, rope_lanes):
        if rope_lanes:
            v = jnp.broadcast_to(pe_half.reshape(CMP_STRIDE, 2, 1, HALF), (CMP_STRIDE, 2, NSA_KV_HEADS, HALF))
        else:
            v = jnp.broadcast_to(pe_half.reshape(CMP_STRIDE, 1, HEAD_DIM), (CMP_STRIDE, NSA_KV_HEADS, HEAD_DIM))
        return v.reshape(1, CMP_STRIDE * LANES)

    def w2_padded(w2, rope_lanes):
        out = []
        for g in range(NSA_KV_HEADS):
            if rope_lanes:
                src = w2.reshape(CMP_HIDDEN, 2, 1, HALF)
                pads = ((0, 0), (0, 0), (g, NSA_KV_HEADS - 1 - g), (0, 0))
            else:
                src = w2.reshape(CMP_HIDDEN, 1, HEAD_DIM)
                pads = ((0, 0), (g, NSA_KV_HEADS - 1 - g), (0, 0))
            out.append(jnp.pad(src, pads).reshape(CMP_HIDDEN, LANES))
        return jnp.concatenate(out, axis=0)

    k1 = ck_w1.reshape(2, CMP_STRIDE, HEAD_DIM, CMP_HIDDEN)
    v1 = cv_w1.reshape(2, CMP_STRIDE, HEAD_DIM, CMP_HIDDEN)
    compress_consts = (
        pe_rows(pe_k[:CMP_STRIDE], True), pe_rows(pe_k[CMP_STRIDE:], True),
        pe_rows(pe_v[:CMP_STRIDE], False), pe_rows(pe_v[CMP_STRIDE:], False),
        grouped(k1[0], True).astype(BF16), grouped(k1[1], True).astype(BF16),
        grouped(v1[0], False).astype(BF16), grouped(v1[1], False).astype(BF16),
        w2_padded(ck_w2, True).astype(BF16), w2_padded(cv_w2, False).astype(BF16),
    )
    return wq.astype(BF16), wkv.astype(BF16), wz.astype(BF16), wg.astype(BF16), wo.astype(BF16), compress_consts


def _dil_weights(w_in, w_out):
    d = w_in.shape[0]
    q, k, v, z = jnp.split(w_in, 4, axis=1)
    n_pairs = len(DIL_HEAD_PAIRS)
    order = [hd for pair in DIL_HEAD_PAIRS for hd in pair]

    def tile_order(w, axis):
        parts = []
        for hd in order:
            sl = [slice(None)] * w.ndim
            sl[axis] = slice(hd * HEAD_DIM, (hd + 1) * HEAD_DIM)
            parts.append(w[tuple(sl)])
        return jnp.concatenate(parts, axis=axis)

    scale = HEAD_DIM ** -0.5 * LOG2E
    wq = _pair_rope_layout(tile_order(q * scale, 1), n_pairs)
    wk = _pair_rope_layout(tile_order(k, 1), n_pairs)
    return (wq.astype(BF16), wk.astype(BF16), tile_order(v, 1).astype(BF16), tile_order(z, 1).astype(BF16),
            tile_order(w_out, 0).astype(BF16))


def _selection_constants(t):
    n_cmp_rows = t // CMP_STRIDE
    n_slc = t // SLC_LEN
    c_start = CMP_STRIDE * np.arange(n_cmp_rows)
    s_start = SLC_LEN * np.arange(n_slc)
    ovl_t = ((c_start[None, :] < s_start[:, None] + SLC_LEN)
             & (c_start[None, :] + CMP_LEN > s_start[:, None])).astype(np.float32)
    ovl_t[:, (t - CMP_LEN) // CMP_STRIDE + 1:] = 0.0
    e = np.zeros((NSA_KV_HEADS, t // KEY_CHUNK, LANES, KEY_CHUNK), np.float32)
    keys = np.arange(t)
    for g in range(NSA_KV_HEADS):
        e[g, keys // KEY_CHUNK, g * n_slc + keys // SLC_LEN, keys % KEY_CHUNK] = 1.0
    return jnp.asarray(ovl_t), jnp.asarray(e, dtype=BF16)


@jax.jit
def kernel(x, c, norm_g, ada_w, ada_b, nsa_w_in, nsa_pe_k, nsa_pe_v, nsa_ck_w1, nsa_ck_w2,
           nsa_cv_w1, nsa_cv_w2, nsa_w_out, dil_w_in, dil_w_out, final_g):
    b, t, d = x.shape
    assert t % SLC_CK == 0 and t % TM == 0 and d % LANES == 0
    assert NSA_KV_HEADS * (t // SLC_LEN) <= LANES and t // CMP_STRIDE == LANES
    mod = _adaln_mod(c, ada_w, ada_b).reshape(ada_w.shape[0], b, 3, d)
    cos, sin = _rope_tables(t)

    wq, wkv, wz, wg, wo, compress_consts = _nsa_weights(
        nsa_w_in[0], nsa_pe_k[0], nsa_pe_v[0], nsa_ck_w1[0], nsa_ck_w2[0], nsa_cv_w1[0], nsa_cv_w2[0], nsa_w_out[0])
    qp, qr, kc, vc, ks, vs, kw, vw, z, gl = _inproj0(x, mod[0], norm_g[0:1], cos, sin, wq, wkv, wz, wg)
    kcmp, vcmp = _compress(kc, vc, compress_consts)
    ovl_t, e_sel = _selection_constants(t)
    ocmp, msel = _cmp_attn(qp, kcmp, vcmp, ovl_t, gl)
    og = _nsa_attn(qr, ks, vs, kw, vw, msel, e_sel, ocmp, z, gl)

    dq, dk, dv, dz, dwo = _dil_weights(dil_w_in[0], dil_w_out[0])
    n_grp = len(DIL_PATTERNS)
    x1, *qkv, z1 = _outproj0_inproj1(og, wo, x, mod[0], mod[1], norm_g[1:2], cos, sin, dq, dk, dv, dz)
    assert all(win // dil == DIL_WIN and TM % (dil * 16) == 0 for win, dil in DIL_PATTERNS)
    o_groups, lse_groups = _dil_attn(qkv[0:n_grp], qkv[n_grp:2 * n_grp], qkv[2 * n_grp:3 * n_grp], t)
    return _outproj1(o_groups, lse_groups, z1, dwo, x1, mod[1], final_g.reshape(1, d))
```

```python
import functools
import math

import numpy as np
import jax
import jax.numpy as jnp
from jax import lax
from jax.experimental import pallas as pl
from jax.experimental.pallas import tpu as pltpu

F32 = jnp.float32
BF16 = jnp.bfloat16
HIGHEST = lax.Precision.HIGHEST

HEAD_DIM = 64
HALF = HEAD_DIM // 2
N_HEADS = 16
ROPE_THETA = 10000.0
NORM_EPS = 1e-6
NSA_KV_HEADS = 2
NSA_GROUP = N_HEADS // NSA_KV_HEADS
NSA_BRANCHES = 3
CMP_LEN = 32
CMP_STRIDE = 16
CMP_HIDDEN = 256
SLC_LEN = 64
SLC_TOP = 16
WIN_LEN = 512
DIL_PATTERNS = ((128, 1), (512, 4), (2048, 16))
DIL_GROUP_HEADS = (6, 5, 5)
DIL_WIN = 128


def _dil_tile_plan():
    offs = np.cumsum((0,) + DIL_GROUP_HEADS)
    pairs, group_tiles, leftovers = [], [], []
    for gi, hn in enumerate(DIL_GROUP_HEADS):
        heads = list(range(offs[gi], offs[gi + 1]))
        group_tiles.append([])
        for i in range(0, hn - hn % 2, 2):
            group_tiles[gi].append((len(pairs), (0, 1)))
            pairs.append((heads[i], heads[i + 1]))
        if hn % 2:
            leftovers.append((gi, heads[-1]))
    assert len(leftovers) % 2 == 0
    for (ga, ha), (gb, hb) in zip(leftovers[0::2], leftovers[1::2]):
        group_tiles[ga].append((len(pairs), (0,)))
        group_tiles[gb].append((len(pairs), (1,)))
        pairs.append((ha, hb))
    return tuple(pairs), tuple(tuple(tiles) for tiles in group_tiles)


DIL_HEAD_PAIRS, DIL_GROUP_TILES = _dil_tile_plan()
DIL_TILES_PER_GROUP = len(DIL_GROUP_TILES[0])
assert all(len(tiles) == DIL_TILES_PER_GROUP for tiles in DIL_GROUP_TILES)

LANES = 128
VMEM_LIMIT_BYTES = 48 * 1024 * 1024

NEG = -1e30
LOG2E = math.log2(math.e)
LN2 = math.log(2.0)
TQ = 128
KEY_CHUNK = 128
SLC_CK = 512
MOD_COLS = 1024
TM = 512
TM_OUT = 1024
DIL_STEP_TOKENS = 1024
NSA_STEP_TOKENS = 256
CMP_STEP_TOKENS = 1024


def _cparams(*sem):
    return pltpu.CompilerParams(dimension_semantics=sem, vmem_limit_bytes=VMEM_LIMIT_BYTES)


def _dot(a, b):
    return jnp.dot(a, b, preferred_element_type=F32)


def _silu(v):
    return v * jax.nn.sigmoid(v)


def _run_in_lockstep(stage_generators):
    pending = list(stage_generators)
    done = object()
    while pending:
        pending = [gen for gen in pending if next(gen, done) is not done]


def _mod_kernel(c_ref, w_ref, b_ref, o_ref):
    s = _silu(c_ref[...])
    o_ref[...] = jnp.dot(s, w_ref[...], precision=HIGHEST, preferred_element_type=F32) + b_ref[...]


def _adaln_mod(c, ada_w, ada_b):
    depth, d, n3 = ada_w.shape
    b = c.shape[0]
    tn = MOD_COLS
    return pl.pallas_call(
        _mod_kernel,
        grid=(depth, n3 // tn),
        in_specs=[
            pl.BlockSpec((b, d), lambda i, j: (0, 0)),
            pl.BlockSpec((None, d, tn), lambda i, j: (i, 0, j)),
            pl.BlockSpec((None, 1, tn), lambda i, j: (i, 0, j)),
        ],
        out_specs=pl.BlockSpec((None, b, tn), lambda i, j: (i, 0, j)),
        out_shape=jax.ShapeDtypeStruct((depth, b, n3), F32),
        compiler_params=_cparams("arbitrary", "arbitrary"),
        name="adaln_mod",
    )(c, ada_w, ada_b.reshape(depth, 1, n3))


def _modulated_norm(x, mod_ref, g_ref):
    ms = jnp.mean(x * x, axis=-1, keepdims=True)
    y = x * lax.rsqrt(ms + NORM_EPS) * g_ref[...]
    return y * (1.0 + mod_ref[1:2, :]) + mod_ref[0:1, :]


def _rope_tile(v, cos, sin):
    return v * cos + pltpu.roll(v, 2 * HALF, axis=1) * sin


def _store_key_chunks_t(kt_ref, k):
    dim = lax.broadcasted_iota(jnp.int32, (LANES, KEY_CHUNK), 0)
    chunks_t = [k[c * KEY_CHUNK:(c + 1) * KEY_CHUNK, :].T for c in range(kt_ref.shape[1])]
    for c, kt in enumerate(chunks_t):
        for g in range(kt_ref.shape[0]):
            kt_ref[g, c] = jnp.where((dim % HEAD_DIM) // HALF == g, kt, 0.0).astype(BF16)


def _inproj0_kernel(x_ref, mod_ref, g_ref, cos_ref, sin_ref, wq_ref, wkv_ref, wz_ref, wg_ref,
                    qp_ref, qr_ref, kc_ref, vc_ref, ks_ref, vs_ref, kw_ref, vw_ref, z_ref, gl_ref, stage):
    tm = x_ref.shape[0]
    h = _modulated_norm(x_ref[...], mod_ref, g_ref).astype(BF16)
    cos = cos_ref[...]
    sin = sin_ref[...]
    q = _dot(h, wq_ref[...])
    for j in range(q.shape[1] // LANES):
        sl = slice(j * LANES, (j + 1) * LANES)
        qp_ref[:, sl] = q[:, sl].astype(BF16)
        qr_ref[:, sl] = _rope_tile(q[:, sl], cos, sin).astype(BF16)
    kv = _dot(h, wkv_ref[...])
    tiles = [kv[:, i * LANES:(i + 1) * LANES] for i in range(6)]
    for slab, (tile, o_ref) in enumerate(((tiles[0], kc_ref), (tiles[1], vc_ref))):
        stage[slab] = tile
        for r in range(CMP_STRIDE):
            rows = stage[slab, pl.ds(r, tm // CMP_STRIDE, stride=CMP_STRIDE), :]
            o_ref[:, r * LANES:(r + 1) * LANES] = rows.astype(BF16)
    _store_key_chunks_t(ks_ref, _rope_tile(tiles[2], cos, sin))
    vs_ref[...] = tiles[3].astype(BF16)
    _store_key_chunks_t(kw_ref, _rope_tile(tiles[4], cos, sin))
    vw_ref[...] = tiles[5].astype(BF16)
    z_ref[...] = _silu(_dot(h, wz_ref[...])).astype(BF16)
    gl_ref[...] = _dot(h, wg_ref[...])


def _inproj0(x, mod, g, cos, sin, wq, wkv, wz, wg):
    b, t, d = x.shape
    nq, nz = wq.shape[1], wz.shape[1]
    tok = lambda n: pl.BlockSpec((None, TM, n), lambda i, j: (i, j, 0))
    full = lambda a: pl.BlockSpec(a.shape, lambda i, j: (0,) * a.ndim)
    tab = pl.BlockSpec((TM, LANES), lambda i, j: (j, 0))
    shp = lambda n, dt: jax.ShapeDtypeStruct((b, t, n), dt)
    kt = pl.BlockSpec((None, NSA_KV_HEADS, TM // KEY_CHUNK, LANES, KEY_CHUNK), lambda i, j: (i, 0, j, 0, 0))
    kt_shp = jax.ShapeDtypeStruct((b, NSA_KV_HEADS, t // KEY_CHUNK, LANES, KEY_CHUNK), BF16)
    kv = tok(LANES)
    kv_shp = shp(LANES, BF16)
    cm = pl.BlockSpec((None, TM // CMP_STRIDE, CMP_STRIDE * LANES), lambda i, j: (i, j, 0))
    cm_shp = jax.ShapeDtypeStruct((b, t // CMP_STRIDE, CMP_STRIDE * LANES), BF16)
    return pl.pallas_call(
        _inproj0_kernel,
        grid=(b, t // TM),
        in_specs=[tok(d), pl.BlockSpec((None, 3, d), lambda i, j: (i, 0, 0)), full(g), tab, tab,
                  full(wq), full(wkv), full(wz), full(wg)],
        out_specs=[tok(nq), tok(nq), cm, cm, kt, kv, kt, kv, tok(nz), tok(LANES)],
        out_shape=[shp(nq, BF16), shp(nq, BF16), cm_shp, cm_shp, kt_shp, kv_shp, kt_shp, kv_shp,
                   shp(nz, BF16), shp(LANES, F32)],
        scratch_shapes=[pltpu.VMEM((2, TM, LANES), F32)],
        compiler_params=_cparams("parallel", "parallel"),
        name="inproj_nsa",
    )(x, mod, g, cos, sin, wq, wkv, wz, wg)


def _compress_kernel(ak_ref, av_ref, pekt_ref, pekb_ref, pevt_ref, pevb_ref,
                     w1kt_ref, w1kb_ref, w1vt_ref, w1vb_ref, w2k_ref, w2v_ref, kcmp_ref, vcmp_ref):
    def mlp(a_ref, pet_ref, peb_ref, w1t_ref, w1b_ref, w2_ref):
        a = a_ref[...].astype(F32)
        top = _dot((a + pet_ref[...]).astype(BF16), w1t_ref[...])
        bot = _dot((a + peb_ref[...]).astype(BF16), w1b_ref[...])
        n = bot.shape[0]
        hid = top + pltpu.roll(bot, n - 1, axis=0)
        return _dot(_silu(hid).astype(BF16), w2_ref[...])

    kcmp_ref[...] = mlp(ak_ref, pekt_ref, pekb_ref, w1kt_ref, w1kb_ref, w2k_ref).astype(BF16)
    vcmp_ref[...] = mlp(av_ref, pevt_ref, pevb_ref, w1vt_ref, w1vb_ref, w2v_ref).astype(BF16)


def _compress(ak, av, consts):
    b, n, w = ak.shape
    full = lambda a: pl.BlockSpec(a.shape, lambda i: (0,) * a.ndim)
    blk = pl.BlockSpec((None, n, w), lambda i: (i, 0, 0))
    out = pl.BlockSpec((None, n, LANES), lambda i: (i, 0, 0))
    return pl.pallas_call(
        _compress_kernel,
        grid=(b,),
        in_specs=[blk, blk] + [full(a) for a in consts],
        out_specs=[out, out],
        out_shape=[jax.ShapeDtypeStruct((b, n, LANES), BF16)] * 2,
        compiler_params=_cparams("parallel"),
        name="compress",
    )(ak, av, *consts)


def _head_gate_rows(gates, g, branches, lane):
    blocks = []
    for j in range(NSA_GROUP):
        base = (g * NSA_GROUP + j) * NSA_BRANCHES
        kept = jnp.zeros_like(gates)
        for br in branches:
            kept = jnp.where(lane == base + br, gates, kept)
        hi = kept.astype(BF16)
        blocks.append(jnp.concatenate([hi, (kept - hi.astype(F32)).astype(BF16)], axis=1))
    return jnp.concatenate(blocks, axis=0)


def _gate_spread(branches):
    col = np.arange(2 * LANES) % LANES
    spread = np.zeros((2 * LANES, LANES * len(branches)), np.float32)
    for i, br in enumerate(branches):
        spread[col % NSA_BRANCHES == br, i * LANES:(i + 1) * LANES] = 1.0
    return jnp.asarray(spread, dtype=BF16)


def _cmp_attn_kernel(qp_ref, kcmp_ref, vcmp_ref, ovl_ref, gl_ref, ocmp_ref, msel_ref, *, n_slc, tq):
    n_cmp = vcmp_ref.shape[0]
    hg = qp_ref.shape[1] // LANES
    n_sub = qp_ref.shape[0] // tq
    assert n_cmp == LANES and tq == LANES
    row = lax.broadcasted_iota(jnp.int32, (LANES, tq), 0)
    col = lax.broadcasted_iota(jnp.int32, (LANES, tq), 1)
    jb = lax.broadcasted_iota(jnp.int32, (n_slc, tq), 0)
    tok_rows = lambda sub: slice(sub * tq, (sub + 1) * tq)
    t0 = lambda sub: pl.program_id(1) * qp_ref.shape[0] + sub * tq
    units = [(sub, g) for sub in range(n_sub) for g in range(NSA_KV_HEADS)]
    eye = jnp.where(row == col, 1.0, 0.0).astype(BF16)
    k_sel = jnp.concatenate([kcmp_ref[...], eye], axis=1)
    v_t = vcmp_ref[...].T

    q_t = [[qp_ref[tok_rows(sub), j * LANES:(j + 1) * LANES].T for j in range(hg)] for sub in range(n_sub)]
    gates_t = [jax.nn.sigmoid(gl_ref[tok_rows(sub), :]).T for sub in range(n_sub)]
    seen = [jnp.concatenate([jnp.where(CMP_STRIDE * row + (CMP_LEN - 1) <= t0(sub) + col, 0.0, NEG).astype(BF16)] * hg,
                            axis=1) for sub in range(n_sub)]
    weights = {}
    for sub, g in units:
        in_group = ((row % HEAD_DIM) // HALF) == g
        q_cols = jnp.concatenate([jnp.where(in_group, v, jnp.zeros_like(v)) for v in q_t[sub]], axis=1)
        s = _dot(k_sel, jnp.concatenate([q_cols, seen[sub]], axis=0))
        m = jnp.max(s, axis=0, keepdims=True)
        m = jnp.where(m > 0.5 * NEG, m, 0.0)
        e = jnp.exp(s - m)
        den = jnp.sum(e, axis=0, keepdims=True)
        weights[sub, g] = e * (1.0 / jnp.where(den > 0, den, 1.0))
    outs, imps = {}, {}
    for sub, g in units:
        p = weights[sub, g]
        gate = jnp.concatenate([gates_t[sub][(g * NSA_GROUP + j) * NSA_BRANCHES:(g * NSA_GROUP + j) * NSA_BRANCHES + 1, :]
                                for j in range(hg)], axis=1)
        outs[sub, g] = gate * _dot(v_t, p.astype(BF16))
        p_sum = p[:, 0:tq]
        for j in range(1, hg):
            p_sum = p_sum + p[:, j * tq:(j + 1) * tq]
        imps[sub, g] = jnp.dot(ovl_ref[...], p_sum, precision=HIGHEST, preferred_element_type=F32)
    picked = {}
    for sub, g in units:
        cur = (t0(sub) + lax.broadcasted_iota(jnp.int32, (n_slc, tq), 1)) // SLC_LEN
        visible = jb <= cur
        forced = (jb == 0) | (jb == cur) | (jb == cur - 1)
        rank = jnp.where(forced, -NEG, jnp.where(visible, imps[sub, g], NEG))
        cnt = jnp.zeros((n_slc, tq), F32)
        for jp in range(n_slc):
            rj = rank[jp:jp + 1, :]
            tie = jnp.where(jb > jp, 1.0, 0.0)
            cnt = cnt + jnp.where(rj > rank, 1.0, jnp.where(rj == rank, tie, 0.0))
        picked[sub, g] = jnp.where(visible, jnp.where(cnt < SLC_TOP, 1.0, 0.0), 0.0)
    pad = jnp.zeros((LANES - NSA_KV_HEADS * n_slc, tq), F32)
    for sub in range(n_sub):
        for j in range(hg):
            cols = slice(j * tq, (j + 1) * tq)
            o_t = jnp.where(row < HEAD_DIM, outs[sub, 0][:, cols], outs[sub, 1][:, cols])
            ocmp_ref[tok_rows(sub), j * LANES:(j + 1) * LANES] = o_t.T.astype(BF16)
        sel_t = jnp.concatenate([picked[sub, g] for g in range(NSA_KV_HEADS)] + [pad], axis=0)
        msel_ref[tok_rows(sub), :] = sel_t.T.astype(BF16)


def _cmp_attn(qp, kcmp, vcmp, ovl_t, gl):
    b, t, nq = qp.shape
    n_cmp = kcmp.shape[1]
    n_slc = t // SLC_LEN
    tok = lambda n: pl.BlockSpec((None, CMP_STEP_TOKENS, n), lambda i, j: (i, j, 0))
    per_b = pl.BlockSpec((None, n_cmp, LANES), lambda i, j: (i, 0, 0))
    full = lambda a: pl.BlockSpec(a.shape, lambda i, j: (0,) * a.ndim)
    return pl.pallas_call(
        functools.partial(_cmp_attn_kernel, n_slc=n_slc, tq=TQ),
        grid=(b, t // CMP_STEP_TOKENS),
        in_specs=[tok(nq), per_b, per_b, full(ovl_t), tok(LANES)],
        out_specs=[tok(nq), tok(LANES)],
        out_shape=[jax.ShapeDtypeStruct((b, t, nq), BF16), jax.ShapeDtypeStruct((b, t, LANES), BF16)],
        compiler_params=_cparams("parallel", "parallel"),
        name="cmp_attn_select",
    )(qp, kcmp, vcmp, ovl_t, gl)


def _nsa_attn_kernel(*refs, tq, **kw):
    *io_refs, m_scr, acc_scr = refs
    n_sub = io_refs[0].shape[0] // tq
    _run_in_lockstep(_nsa_attn_tile(*io_refs, m_scr.at[sub], acc_scr.at[sub], tok=slice(sub * tq, (sub + 1) * tq),
                                    qi=pl.program_id(1) * n_sub + sub, **kw) for sub in range(n_sub))


def _nsa_attn_tile(qr_ref, ks_ref, vs_ref, kw_ref, vw_ref, msel_ref, e_ref, gsp_ref, ocmp_ref, z_ref, gl_ref,
                   og_ref, m_scr, acc_scr, *, tok, qi, n_slc, ck):
    tq = tok.stop - tok.start
    hg = qr_ref.shape[1] // LANES
    rows = hg * tq
    n_wc = WIN_LEN // tq + 1
    per_ck = ck // KEY_CHUNK
    t0 = qi * tq
    lane = lax.broadcasted_iota(jnp.int32, (tq, LANES), 1)
    a_idx = lax.broadcasted_iota(jnp.int32, (tq, LANES), 0)

    eye = jnp.where(lane == a_idx, 1.0, 0.0).astype(BF16)
    earlier = jnp.where((lane % n_slc) < t0 // SLC_LEN, 0.0, NEG)
    mneg = jnp.where(msel_ref[tok, :].astype(F32) > 0.5, earlier, NEG).astype(BF16)
    mneg_rows = jnp.concatenate([mneg] * hg, axis=0)
    eye_rows = jnp.concatenate([eye] * hg, axis=0)

    def with_selector(sel_rows):
        q_rows = jnp.concatenate([qr_ref[tok, j * LANES:(j + 1) * LANES] for j in range(hg)], axis=0)
        return jnp.concatenate([q_rows, sel_rows], axis=1)

    zero_b = jnp.zeros((tq, LANES), BF16)
    neg_b = jnp.full((tq, LANES), NEG, BF16)
    causal_b = jnp.where(lane <= a_idx, 0.0, NEG).astype(BF16)
    far_b = jnp.where(lane > a_idx, 0.0, NEG).astype(BF16)
    own_half = [lane < HEAD_DIM, lane >= HEAD_DIM]

    def values_with_ones(v, g):
        in_own_half = (lax.broadcasted_iota(jnp.int32, v.shape, 1) // HEAD_DIM) == g
        return jnp.where(in_own_half, v, jnp.ones_like(v))

    def lane_tiles(v):
        return [v[:, i * LANES:(i + 1) * LANES] for i in range(v.shape[1] // LANES)]

    def tile_max(tiles):
        mx = tiles[0]
        for v in tiles[1:]:
            mx = jnp.maximum(mx, v)
        return jnp.max(mx, axis=-1, keepdims=True)

    def probs(tiles, m):
        return jnp.concatenate([jnp.exp(v - m) for v in tiles], axis=1).astype(BF16)

    m_scr[...] = jnp.full(m_scr.shape, NEG, F32)
    acc_scr[...] = jnp.zeros(acc_scr.shape, F32)

    def attend_chunks(c0, n):
        k0 = pl.multiple_of(c0 * KEY_CHUNK, KEY_CHUNK)
        q_sel = with_selector(mneg_rows)
        scores = []
        for g in range(NSA_KV_HEADS):
            kt = jnp.concatenate([ks_ref[g, c0 + i] for i in range(n)], axis=1)
            eb = jnp.concatenate([e_ref[g, c0 + i] for i in range(n)], axis=1)
            scores.append(_dot(q_sel, jnp.concatenate([kt, eb], axis=0)))
        groups = range(NSA_KV_HEADS)
        m_old = [m_scr[g] for g in groups]
        m_new = [jnp.maximum(m_old[g], tile_max(lane_tiles(scores[g]))) for g in groups]
        p = [probs(lane_tiles(scores[g]), m_new[g]) for g in groups]
        pv = [_dot(p[g], values_with_ones(vs_ref[pl.ds(k0, n * KEY_CHUNK), :], g)) for g in groups]
        for g in groups:
            acc_scr[g] = jnp.exp(m_old[g] - m_new[g]) * acc_scr[g] + pv[g]
            m_scr[g] = m_new[g]

    def earlier_keys(kc, carry):
        attend_chunks(kc * per_ck, per_ck)
        return carry

    lax.fori_loop(0, qi // per_ck, earlier_keys, 0)
    for rem in range(1, per_ck):
        @pl.when(qi % per_ck == rem)
        def _():
            attend_chunks((qi // per_ck) * per_ck, rem)
    yield

    kd = pl.multiple_of(t0, tq)
    zeros_v = jnp.zeros((tq, LANES), BF16)
    win_chunks = [qi - (n_wc - 1 - c) for c in range(n_wc - 1)]
    win_bias = [jnp.where(cidx >= 0, far_b if c == 0 else zero_b, neg_b) for c, cidx in enumerate(win_chunks)]
    pairs = [(c, c + 1) for c in range(0, n_wc - 1, 2)]
    q_eye = with_selector(eye_rows)
    s_own, scores = [], []
    for g in range(NSA_KV_HEADS):
        own_k = jnp.concatenate([jnp.concatenate([kw_ref[g, qi], causal_b], axis=0),
                                 jnp.concatenate([ks_ref[g, qi], causal_b], axis=0)], axis=1)
        kbs = [jnp.concatenate([kw_ref[g, jnp.maximum(cidx, 0)], win_bias[c]], axis=0)
               for c, cidx in enumerate(win_chunks)]
        s_own.append(_dot(q_eye, own_k))
        scores.append([_dot(q_eye, jnp.concatenate([kbs[c] for c in pr], axis=1)) for pr in pairs])

    yield
    groups = range(NSA_KV_HEADS)
    s_win_own = [s_own[g][:, 0:tq] for g in groups]
    s_slc_own = [s_own[g][:, tq:2 * tq] for g in groups]
    m_win = [tile_max([v for s in scores[g] for v in lane_tiles(s)] + [s_win_own[g]]) for g in groups]
    m_old = [m_scr[g] for g in groups]
    m_slc = [jnp.maximum(m_old[g], jnp.max(s_slc_own[g], axis=-1, keepdims=True)) for g in groups]
    p_all = [jnp.concatenate([probs(lane_tiles(s), m_win[g]) for s in scores[g]]
                             + [probs([s_win_own[g]], m_win[g]), probs([s_slc_own[g]], m_slc[g])], axis=1) for g in groups]
    yield
    acc_slc, acc_win = [], []
    for g in groups:
        vbs = [values_with_ones(vw_ref[pl.ds(pl.multiple_of(jnp.maximum(cidx, 0) * tq, tq), tq), :], g)
               for cidx in win_chunks] + [values_with_ones(vw_ref[pl.ds(kd, tq), :], g)]
        win_v = [jnp.concatenate([v, zeros_v], axis=1) for v in vbs]
        slc_v = jnp.concatenate([zeros_v, values_with_ones(vs_ref[pl.ds(kd, tq), :], g)], axis=1)
        pv = _dot(p_all[g], jnp.concatenate(win_v + [slc_v], axis=0))
        acc_win.append(pv[:, 0:LANES])
        acc_slc.append(jnp.exp(m_old[g] - m_slc[g]) * acc_scr[g] + pv[:, LANES:2 * LANES])
    yield

    gates = jax.nn.sigmoid(gl_ref[tok, :])
    row_gates = [_dot(_head_gate_rows(gates, g, (1, 2), lane), gsp_ref[...]) for g in range(NSA_KV_HEADS)]

    def gated(accs, rs, branch_lanes):
        num = [accs[g][rs] * row_gates[g][rs, branch_lanes] for g in range(NSA_KV_HEADS)]
        sums = pltpu.roll(jnp.where(own_half[0], accs[1][rs], accs[0][rs]), HEAD_DIM, axis=1)
        return jnp.where(own_half[0], num[0], num[1]) / sums

    for j in range(hg):
        rs = slice(j * tq, (j + 1) * tq)
        cols = slice(j * LANES, (j + 1) * LANES)
        o = (ocmp_ref[tok, cols].astype(F32) + gated(acc_slc, rs, slice(0, LANES))
             + gated(acc_win, rs, slice(LANES, 2 * LANES)))
        og_ref[tok, cols] = (o * z_ref[tok, cols].astype(F32)).astype(BF16)


def _nsa_attn(qr, ks, vs, kw, vw, msel, e, ocmp, z, gl):
    b, t, nq = qr.shape
    gsp = _gate_spread((1, 2))
    hg = nq // LANES
    rows = hg * TQ
    assert TQ == KEY_CHUNK and WIN_LEN % TQ == 0
    tok = lambda n: pl.BlockSpec((None, NSA_STEP_TOKENS, n), lambda i, j: (i, j, 0))
    per_b = pl.BlockSpec((None, t, LANES), lambda i, j: (i, 0, 0))
    per_b_t = pl.BlockSpec((None, NSA_KV_HEADS, t // KEY_CHUNK, LANES, KEY_CHUNK), lambda i, j: (i, 0, 0, 0, 0))
    full = lambda a: pl.BlockSpec(a.shape, lambda i, j: (0,) * a.ndim)
    return pl.pallas_call(
        functools.partial(_nsa_attn_kernel, tq=TQ, n_slc=t // SLC_LEN, ck=SLC_CK),
        grid=(b, t // NSA_STEP_TOKENS),
        in_specs=[tok(nq), per_b_t, per_b, per_b_t, per_b, tok(LANES), full(e), full(gsp), tok(nq), tok(nq), tok(LANES)],
        out_specs=tok(nq),
        out_shape=jax.ShapeDtypeStruct((b, t, nq), BF16),
        scratch_shapes=[
            pltpu.VMEM((NSA_STEP_TOKENS // TQ, NSA_KV_HEADS, rows, LANES), F32),
            pltpu.VMEM((NSA_STEP_TOKENS // TQ, NSA_KV_HEADS, rows, LANES), F32),
        ],
        compiler_params=_cparams("parallel", "arbitrary"),
        name="nsa_slc_win_attn",
    )(qr, ks, vs, kw, vw, msel, e, gsp, ocmp, z, gl)


def _inproj1_kernel(og_ref, wo_ref, x_ref, mod0_ref, mod_ref, g_ref, cos_ref, sin_ref, wq_ref, wk_ref, wv_ref, wz_ref,
                    x1_ref, *refs):
    n_grp = len(DIL_PATTERNS)
    outs = [refs[a * n_grp:(a + 1) * n_grp] for a in range(3)]
    z_ref, stage = refs[3 * n_grp], refs[3 * n_grp + 1]
    tm = x_ref.shape[0]
    gw = DIL_TILES_PER_GROUP * LANES
    x1 = x_ref[...] + mod0_ref[2:3, :] * _dot(og_ref[...], wo_ref[...])
    x1_ref[...] = x1
    h = _modulated_norm(x1, mod_ref, g_ref).astype(BF16)
    cos = cos_ref[...]
    sin = sin_ref[...]
    projected = [_dot(h, w_ref[...]) for w_ref in (wq_ref, wk_ref, wv_ref, wz_ref)]
    for a, u in enumerate(projected[0:3]):
        tiles = [u[:, pt * LANES:(pt + 1) * LANES] for pt in range(len(DIL_HEAD_PAIRS))]
        if a < 2:
            tiles = [_rope_tile(v, cos, sin) for v in tiles]
        for gi, (_, dil) in enumerate(DIL_PATTERNS):
            for jt, (pt, _) in enumerate(DIL_GROUP_TILES[gi]):
                tile = tiles[pt]
                if dil == 1:
                    outs[a][gi][:, jt * LANES:(jt + 1) * LANES] = tile.astype(BF16)
                    continue
                slab = (a * n_grp + gi) * DIL_TILES_PER_GROUP + jt
                stage[slab] = tile
                for r in range(dil):
                    rows = stage[slab, pl.ds(r, tm // dil, stride=dil), :]
                    outs[a][gi][:, r * gw + jt * LANES:r * gw + (jt + 1) * LANES] = rows.astype(BF16)
    z_ref[...] = _silu(projected[3]).astype(BF16)


def _outproj0_inproj1(og, wo, x, mod0, mod, g, cos, sin, wq, wk, wv, wz):
    b, t, d = x.shape
    n = wq.shape[1]
    gw = DIL_TILES_PER_GROUP * LANES
    tok = lambda k: pl.BlockSpec((None, TM, k), lambda i, j: (i, j, 0))
    full = lambda a: pl.BlockSpec(a.shape, lambda i, j: (0,) * a.ndim)
    tab = pl.BlockSpec((TM, LANES), lambda i, j: (j, 0))
    mods = pl.BlockSpec((None, 3, d), lambda i, j: (i, 0, 0))
    grp_specs = [pl.BlockSpec((None, TM // dil, dil * gw), lambda i, j: (i, j, 0)) for _, dil in DIL_PATTERNS]
    grp_shapes = [jax.ShapeDtypeStruct((b, t // dil, dil * gw), BF16) for _, dil in DIL_PATTERNS]
    n_slabs = 3 * len(DIL_PATTERNS) * DIL_TILES_PER_GROUP
    return pl.pallas_call(
        _inproj1_kernel,
        grid=(b, t // TM),
        in_specs=[tok(og.shape[2]), full(wo), tok(d), mods, mods, full(g), tab, tab,
                  full(wq), full(wk), full(wv), full(wz)],
        out_specs=[tok(d)] + grp_specs * 3 + [tok(n)],
        out_shape=[jax.ShapeDtypeStruct((b, t, d), F32)] + grp_shapes * 3 + [jax.ShapeDtypeStruct((b, t, n), BF16)],
        scratch_shapes=[pltpu.VMEM((n_slabs, TM, LANES), F32)],
        compiler_params=_cparams("parallel", "parallel"),
        name="outproj_nsa_inproj_dil",
    )(og, wo, x, mod0, mod, g, cos, sin, wq, wk, wv, wz)


def _dil_attn_kernel(*refs, plans):
    n_grp = len(plans)
    step = pl.program_id(1)
    groups = []
    for gi, (members, tq, n_seq, blocks_per_seq) in enumerate(plans):
        q_ref, k_ref, v_ref = refs[3 * gi:3 * gi + 3]
        o_ref, lse_ref = refs[3 * n_grp + 2 * gi:3 * n_grp + 2 * gi + 2]
        groups.append(_dil_attn_group(q_ref, k_ref, v_ref, o_ref, lse_ref, step % blocks_per_seq, members, tq, n_seq))
    _run_in_lockstep(groups)


def _dil_attn_group(q_ref, k_ref, v_ref, o_ref, lse_ref, i, members, tq, n_seq):
    n_heads = sum(len(m) for m in members)
    tq_blk = q_ref.shape[0]
    gw = q_ref.shape[1] // n_seq
    n_tiles = gw // LANES
    n_sub = tq_blk // tq
    assert tq == LANES
    row = lax.broadcasted_iota(jnp.int32, (LANES, tq), 0)
    col = lax.broadcasted_iota(jnp.int32, (LANES, tq), 1)
    head_a = ((row % HEAD_DIM) // HALF) == 0
    diag_b = jnp.where(row <= col, 0.0, NEG)
    prev_b = jnp.where(row >= col, 0.0, NEG)
    neg_b = jnp.full((LANES, tq), NEG, F32)
    two = lambda v: jnp.concatenate([v, v], axis=1)
    units = [(sq, sub, jt) for sq in range(n_seq) for sub in range(n_sub) for jt in range(n_tiles)]
    rows_of = lambda sub: slice(sub * tq, (sub + 1) * tq)
    cols_of = lambda sq, jt: slice(sq * gw + jt * LANES, sq * gw + (jt + 1) * LANES)
    key_rows = lambda c: pl.ds(pl.multiple_of(jnp.maximum(i * n_sub + c, 0) * tq, tq), tq)
    q_cols, v_t = {}, {}
    for sq, sub, jt in units:
        q_t = q_ref[rows_of(sub), cols_of(sq, jt)].T
        zero = jnp.zeros_like(q_t)
        q_cols[sq, sub, jt] = jnp.concatenate([jnp.where(head_a, q_t, zero), jnp.where(head_a, zero, q_t)], axis=1)
        v_t[sq, sub, jt] = v_ref[key_rows(sub), cols_of(sq, jt)].T
        if sub == 0:
            v_t[sq, -1, jt] = v_ref[key_rows(-1), cols_of(sq, jt)].T
    yield
    probs_t, stats = {}, {}
    for sq, sub, jt in units:
        prev_bias = prev_b if sub > 0 else jnp.where(i > 0, prev_b, neg_b)
        bias = jnp.concatenate([two(prev_bias), two(diag_b)], axis=0)
        keys = jnp.concatenate([k_ref[key_rows(sub - 1), cols_of(sq, jt)], k_ref[key_rows(sub), cols_of(sq, jt)]], axis=0)
        s = _dot(keys, q_cols[sq, sub, jt]) + bias
        m = jnp.max(s, axis=0, keepdims=True)
        p = jnp.exp2(s - m)
        l = jnp.sum(p, axis=0, keepdims=True)
        probs_t[sq, sub, jt] = p.astype(BF16)
        stats[sq, sub, jt] = (m, l)
    yield
    for sq, sub, jt in units:
        m, l = stats[sq, sub, jt]
        o_t = _dot(jnp.concatenate([v_t[sq, sub - 1, jt], v_t[sq, sub, jt]], axis=1), probs_t[sq, sub, jt])
        inv = 1.0 / l
        o_t = jnp.where(row < HEAD_DIM, o_t[:, 0:tq] * inv[:, 0:tq], o_t[:, tq:2 * tq] * inv[:, tq:2 * tq])
        o_ref[rows_of(sub), cols_of(sq, jt)] = o_t.T.astype(BF16)
    yield
    for sq in range(n_seq):
        for sub in range(n_sub):
            lses = []
            for jt in range(n_tiles):
                m, l = stats[sq, sub, jt]
                lse = m * LN2 + jnp.log(l)
                lses += [lse[:, mb * tq:(mb + 1) * tq] for mb in members[jt]]
            top = lses[0]
            for v in lses[1:]:
                top = jnp.maximum(top, v)
            tot = jnp.zeros_like(top)
            for v in lses:
                tot = tot + jnp.exp(v - top)
            group_lse = top + jnp.log(tot) - math.log(n_heads)
            lse_ref[rows_of(sub), sq * LANES:(sq + 1) * LANES] = jnp.broadcast_to(group_lse, (LANES, tq)).T


def _dil_attn(qs, ks, vs, t):
    b = qs[0].shape[0]
    steps = t // DIL_STEP_TOKENS
    plans, in_specs, out_specs, out_shapes, operands = [], [], [], [], []
    for gi, (_, dil) in enumerate(DIL_PATTERNS):
        length = t // dil
        gw = qs[gi].shape[2] // dil
        tq_blk = min(DIL_STEP_TOKENS, length)
        blocks_per_seq = length // tq_blk
        n_seq = DIL_STEP_TOKENS // tq_blk
        assert dil * blocks_per_seq == steps * n_seq
        tok = lambda w, bps=blocks_per_seq, ns=n_seq, tb=tq_blk: pl.BlockSpec(
            (None, tb, ns * w), lambda bi, s: (bi, s % bps, s // bps))
        seq = pl.BlockSpec((None, length, n_seq * gw), lambda bi, s, bps=blocks_per_seq: (bi, 0, s // bps))
        plans.append((tuple(m for _, m in DIL_GROUP_TILES[gi]), min(TQ, length), n_seq, blocks_per_seq))
        in_specs += [tok(gw), seq, seq]
        operands += [qs[gi], ks[gi], vs[gi]]
        out_specs += [tok(gw), tok(LANES)]
        out_shapes += [jax.ShapeDtypeStruct(qs[gi].shape, BF16), jax.ShapeDtypeStruct((b, length, dil * LANES), F32)]
    outs = pl.pallas_call(
        functools.partial(_dil_attn_kernel, plans=tuple(plans)),
        grid=(b, steps),
        in_specs=in_specs,
        out_specs=out_specs,
        out_shape=out_shapes,
        compiler_params=_cparams("parallel", "parallel"),
        name="dil_attn",
    )(*operands)
    return outs[0::2], outs[1::2]


def _outproj1_kernel(*refs):
    n_grp = len(DIL_PATTERNS)
    o_refs, l_refs = refs[0:n_grp], refs[n_grp:2 * n_grp]
    z_ref, w_ref, x_ref, mod_ref, fg_ref, out_ref, stage = refs[2 * n_grp:]
    tm = x_ref.shape[0]
    gw = DIL_TILES_PER_GROUP * LANES

    def token_order(ref, gi, width, col0, slab):
        dil = DIL_PATTERNS[gi][1]
        if dil == 1:
            return ref[:, col0:col0 + LANES].astype(F32)
        for r in range(dil):
            stage[slab, pl.ds(r, tm // dil, stride=dil), :] = ref[:, r * width + col0:r * width + col0 + LANES].astype(F32)
        return stage[slab]

    ls = [token_order(l_refs[gi], gi, LANES, 0, gi) for gi in range(n_grp)]
    top = jnp.maximum(jnp.maximum(ls[0], ls[1]), ls[2])
    es = [jnp.exp(v - top) for v in ls]
    den = es[0] + es[1] + es[2]
    alphas = [e / den * float(n_grp) for e in es]
    lane = lax.broadcasted_iota(jnp.int32, (tm, LANES), 1)
    weighted = [None] * len(DIL_HEAD_PAIRS)
    for gi in range(n_grp):
        for jt, (pt, members) in enumerate(DIL_GROUP_TILES[gi]):
            ot = token_order(o_refs[gi], gi, gw, jt * LANES, n_grp + gi * DIL_TILES_PER_GROUP + jt) * alphas[gi]
            if len(members) == 2:
                weighted[pt] = ot
            else:
                mine = (lane // HEAD_DIM) == members[0]
                weighted[pt] = jnp.where(mine, ot, 0.0 if weighted[pt] is None else weighted[pt])
    parts = [(weighted[pt] * z_ref[:, pt * LANES:(pt + 1) * LANES].astype(F32)).astype(BF16)
             for pt in range(len(DIL_HEAD_PAIRS))]
    og = jnp.concatenate(parts, axis=1)
    y = _dot(og, w_ref[...])
    x2 = x_ref[...] + mod_ref[2:3, :] * y
    ms = jnp.mean(x2 * x2, axis=-1, keepdims=True)
    out_ref[...] = x2 * lax.rsqrt(ms + NORM_EPS) * fg_ref[...]


def _outproj1(o_groups, lse_groups, z, w, x, mod, final_g):
    b, t, d = x.shape
    tm = TM_OUT
    tok = lambda k: pl.BlockSpec((None, tm, k), lambda i, j: (i, j, 0))
    full = lambda a: pl.BlockSpec(a.shape, lambda i, j: (0,) * a.ndim)
    grp = lambda a, dil: pl.BlockSpec((None, tm // dil, a.shape[2]), lambda i, j: (i, j, 0))
    dils = [dil for _, dil in DIL_PATTERNS]
    n_slabs = len(dils) * (1 + DIL_TILES_PER_GROUP)
    return pl.pallas_call(
        _outproj1_kernel,
        grid=(b, t // tm),
        in_specs=[grp(a, dl) for a, dl in zip(o_groups, dils)] + [grp(a, dl) for a, dl in zip(lse_groups, dils)]
                 + [tok(z.shape[2]), full(w), tok(d), pl.BlockSpec((None, 3, d), lambda i, j: (i, 0, 0)), full(final_g)],
        out_specs=tok(d),
        out_shape=jax.ShapeDtypeStruct((b, t, d), F32),
        scratch_shapes=[pltpu.VMEM((n_slabs, tm, LANES), F32)],
        compiler_params=_cparams("parallel", "parallel"),
        name="outproj_dil_final",
    )(*o_groups, *lse_groups, z, w, x, mod, final_g)


def _rope_tables(t):
    inv = ROPE_THETA ** (-jnp.arange(HALF, dtype=F32) / HALF)
    ang = jnp.arange(t, dtype=F32)[:, None] * inv[None, :]
    cos = jnp.tile(jnp.cos(ang), (1, 4))
    sin = jnp.tile(jnp.sin(ang), (1, 4))
    sign = jnp.where(jnp.arange(LANES) < 2 * HALF, -1.0, 1.0).astype(F32)
    return cos, sin * sign[None, :]


def _pair_rope_layout(w, n_pairs):
    d = w.shape[0]
    return w.reshape(d, n_pairs, 2, 2, HALF).transpose(0, 1, 3, 2, 4).reshape(d, n_pairs * LANES)


def _nsa_weights(w_in, pe_k, pe_v, ck_w1, ck_w2, cv_w1, cv_w2, w_out):
    d = w_in.shape[0]
    aw = N_HEADS * HEAD_DIM
    kvw = NSA_KV_HEADS * HEAD_DIM
    cuts = np.cumsum([aw] + [kvw] * 6 + [aw]).tolist()
    q, k_c, v_c, k_s, v_s, k_w, v_w, z, gl = jnp.split(w_in, cuts, axis=1)
    scale = HEAD_DIM ** -0.5
    wq = (q * scale).reshape(d, NSA_KV_HEADS, NSA_GROUP, 2, HALF).transpose(0, 2, 3, 1, 4).reshape(d, aw)
    k_lay = lambda w: _pair_rope_layout(w, 1)
    wkv = jnp.concatenate([k_lay(k_c), v_c, k_lay(k_s), v_s, k_lay(k_w), v_w], axis=1)
    wz = z.reshape(d, NSA_KV_HEADS, NSA_GROUP, HEAD_DIM).transpose(0, 2, 1, 3).reshape(d, aw)
    wo = w_out.reshape(NSA_KV_HEADS, NSA_GROUP, HEAD_DIM, -1).transpose(1, 0, 2, 3).reshape(aw, -1)
    wg = jnp.pad(gl, ((0, 0), (0, LANES - gl.shape[1])))

    def grouped(w1_half, rope_lanes):
        out = []
        for g in range(NSA_KV_HEADS):
            if rope_lanes:
                src = w1_half.reshape(CMP_STRIDE, 2, 1, HALF, CMP_HIDDEN)
                pads = ((0, 0), (0, 0), (g, NSA_KV_HEADS - 1 - g), (0, 0), (0, 0))
            else:
                src = w1_half.reshape(CMP_STRIDE, 1, HEAD_DIM, CMP_HIDDEN)
                pads = ((0, 0), (g, NSA_KV_HEADS - 1 - g), (0, 0), (0, 0))
            out.append(jnp.pad(src, pads).reshape(CMP_STRIDE * LANES, CMP_HIDDEN))
        return jnp.concatenate(out, axis=1)

    def pe_rows(pe_half, rope_lanes):
        if rope_lanes:
            v = jnp.broadcast_to(pe_half.reshape(CMP_STRIDE, 2, 1, HALF), (CMP_STRIDE, 2, NSA_KV_HEADS, HALF))
        else:
            v = jnp.broadcast_to(pe_half.reshape(CMP_STRIDE, 1, HEAD_DIM), (CMP_STRIDE, NSA_KV_HEADS, HEAD_DIM))
        return v.reshape(1, CMP_STRIDE * LANES)

    def w2_padded(w2, rope_lanes):
        out = []
        for g in range(NSA_KV_HEADS):
            if rope_lanes:
                src = w2.reshape(CMP_HIDDEN, 2, 1, HALF)
                pads = ((0, 0), (0, 0), (g, NSA_KV_HEADS - 1 - g), (0, 0))
            else:
                src = w2.reshape(CMP_HIDDEN, 1, HEAD_DIM)
                pads = ((0, 0), (g, NSA_KV_HEADS - 1 - g), (0, 0))
            out.append(jnp.pad(src, pads).reshape(CMP_HIDDEN, LANES))
        return jnp.concatenate(out, axis=0)

    k1 = ck_w1.reshape(2, CMP_STRIDE, HEAD_DIM, CMP_HIDDEN)
    v1 = cv_w1.reshape(2, CMP_STRIDE, HEAD_DIM, CMP_HIDDEN)
    compress_consts = (
        pe_rows(pe_k[:CMP_STRIDE], True), pe_rows(pe_k[CMP_STRIDE:], True),
        pe_rows(pe_v[:CMP_STRIDE], False), pe_rows(pe_v[CMP_STRIDE:], False),
        grouped(k1[0], True).astype(BF16), grouped(k1[1], True).astype(BF16),
        grouped(v1[0], False).astype(BF16), grouped(v1[1], False).astype(BF16),
        w2_padded(ck_w2, True).astype(BF16), w2_padded(cv_w2, False).astype(BF16),
    )
    return wq.astype(BF16), wkv.astype(BF16), wz.astype(BF16), wg.astype(BF16), wo.astype(BF16), compress_consts


def _dil_weights(w_in, w_out):
    d = w_in.shape[0]
    q, k, v, z = jnp.split(w_in, 4, axis=1)
    n_pairs = len(DIL_HEAD_PAIRS)
    order = [hd for pair in DIL_HEAD_PAIRS for hd in pair]

    def tile_order(w, axis):
        parts = []
        for hd in order:
            sl = [slice(None)] * w.ndim
            sl[axis] = slice(hd * HEAD_DIM, (hd + 1) * HEAD_DIM)
            parts.append(w[tuple(sl)])
        return jnp.concatenate(parts, axis=axis)

    scale = HEAD_DIM ** -0.5 * LOG2E
    wq = _pair_rope_layout(tile_order(q * scale, 1), n_pairs)
    wk = _pair_rope_layout(tile_order(k, 1), n_pairs)
    return (wq.astype(BF16), wk.astype(BF16), tile_order(v, 1).astype(BF16), tile_order(z, 1).astype(BF16),
            tile_order(w_out, 0).astype(BF16))


def _selection_constants(t):
    n_cmp_rows = t // CMP_STRIDE
    n_slc = t // SLC_LEN
    c_start = CMP_STRIDE * np.arange(n_cmp_rows)
    s_start = SLC_LEN * np.arange(n_slc)
    ovl_t = ((c_start[None, :] < s_start[:, None] + SLC_LEN)
             & (c_start[None, :] + CMP_LEN > s_start[:, None])).astype(np.float32)
    ovl_t[:, (t - CMP_LEN) // CMP_STRIDE + 1:] = 0.0
    e = np.zeros((NSA_KV_HEADS, t // KEY_CHUNK, LANES, KEY_CHUNK), np.float32)
    keys = np.arange(t)
    for g in range(NSA_KV_HEADS):
        e[g, keys // KEY_CHUNK, g * n_slc + keys // SLC_LEN, keys % KEY_CHUNK] = 1.0
    return jnp.asarray(ovl_t), jnp.asarray(e, dtype=BF16)


@jax.jit
def kernel(x, c, norm_g, ada_w, ada_b, nsa_w_in, nsa_pe_k, nsa_pe_v, nsa_ck_w1, nsa_ck_w2,
           nsa_cv_w1, nsa_cv_w2, nsa_w_out, dil_w_in, dil_w_out, final_g):
    b, t, d = x.shape
    assert t % SLC_CK == 0 and t % TM == 0 and d % LANES == 0
    assert NSA_KV_HEADS * (t // SLC_LEN) <= LANES and t // CMP_STRIDE == LANES
    mod = _adaln_mod(c, ada_w, ada_b).reshape(ada_w.shape[0], b, 3, d)
    cos, sin = _rope_tables(t)

    wq, wkv, wz, wg, wo, compress_consts = _nsa_weights(
        nsa_w_in[0], nsa_pe_k[0], nsa_pe_v[0], nsa_ck_w1[0], nsa_ck_w2[0], nsa_cv_w1[0], nsa_cv_w2[0], nsa_w_out[0])
    qp, qr, kc, vc, ks, vs, kw, vw, z, gl = _inproj0(x, mod[0], norm_g[0:1], cos, sin, wq, wkv, wz, wg)
    kcmp, vcmp = _compress(kc, vc, compress_consts)
    ovl_t, e_sel = _selection_constants(t)
    ocmp, msel = _cmp_attn(qp, kcmp, vcmp, ovl_t, gl)
    og = _nsa_attn(qr, ks, vs, kw, vw, msel, e_sel, ocmp, z, gl)

    dq, dk, dv, dz, dwo = _dil_weights(dil_w_in[0], dil_w_out[0])
    n_grp = len(DIL_PATTERNS)
    x1, *qkv, z1 = _outproj0_inproj1(og, wo, x, mod[0], mod[1], norm_g[1:2], cos, sin, dq, dk, dv, dz)
    assert all(win // dil == DIL_WIN and TM % (dil * 16) == 0 for win, dil in DIL_PATTERNS)
    o_groups, lse_groups = _dil_attn(qkv[0:n_grp], qkv[n_grp:2 * n_grp], qkv[2 * n_grp:3 * n_grp], t)
    return _outproj1(o_groups, lse_groups, z1, dwo, x1, mod[1], final_g.reshape(1, d))
```

```python
import functools
import math

import numpy as np
import jax
import jax.numpy as jnp
from jax import lax
from jax.experimental import pallas as pl
from jax.experimental.pallas import tpu as pltpu

F32 = jnp.float32
BF16 = jnp.bfloat16
HIGHEST = lax.Precision.HIGHEST

HEAD_DIM = 64
HALF = HEAD_DIM // 2
N_HEADS = 16
ROPE_THETA = 10000.0
NORM_EPS = 1e-6
NSA_KV_HEADS = 2
NSA_GROUP = N_HEADS // NSA_KV_HEADS
NSA_BRANCHES = 3
CMP_LEN = 32
CMP_STRIDE = 16
CMP_HIDDEN = 256
SLC_LEN = 64
SLC_TOP = 16
WIN_LEN = 512
DIL_PATTERNS = ((128, 1), (512, 4), (2048, 16))
DIL_GROUP_HEADS = (6, 5, 5)
DIL_WIN = 128


def _dil_tile_plan():
    offs = np.cumsum((0,) + DIL_GROUP_HEADS)
    pairs, group_tiles, leftovers = [], [], []
    for gi, hn in enumerate(DIL_GROUP_HEADS):
        heads = list(range(offs[gi], offs[gi + 1]))
        group_tiles.append([])
        for i in range(0, hn - hn % 2, 2):
            group_tiles[gi].append((len(pairs), (0, 1)))
            pairs.append((heads[i], heads[i + 1]))
        if hn % 2:
            leftovers.append((gi, heads[-1]))
    assert len(leftovers) % 2 == 0
    for (ga, ha), (gb, hb) in zip(leftovers[0::2], leftovers[1::2]):
        group_tiles[ga].append((len(pairs), (0,)))
        group_tiles[gb].append((len(pairs), (1,)))
        pairs.append((ha, hb))
    return tuple(pairs), tuple(tuple(tiles) for tiles in group_tiles)


DIL_HEAD_PAIRS, DIL_GROUP_TILES = _dil_tile_plan()
DIL_TILES_PER_GROUP = len(DIL_GROUP_TILES[0])
assert all(len(tiles) == DIL_TILES_PER_GROUP for tiles in DIL_GROUP_TILES)

LANES = 128
VMEM_LIMIT_BYTES = 48 * 1024 * 1024

NEG = -1e30
LOG2E = math.log2(math.e)
LN2 = math.log(2.0)
TQ = 128
KEY_CHUNK = 128
SLC_CK = 512
MOD_COLS = 1024
TM = 512
TM_OUT = 1024
DIL_STEP_TOKENS = 1024
NSA_STEP_TOKENS = 256
CMP_STEP_TOKENS = 2048


def _cparams(*sem):
    return pltpu.CompilerParams(dimension_semantics=sem, vmem_limit_bytes=VMEM_LIMIT_BYTES)


def _dot(a, b):
    return jnp.dot(a, b, preferred_element_type=F32)


def _silu(v):
    return v * jax.nn.sigmoid(v)


def _run_in_lockstep(stage_generators):
    pending = list(stage_generators)
    done = object()
    while pending:
        pending = [gen for gen in pending if next(gen, done) is not done]


def _mod_kernel(c_ref, w_ref, b_ref, o_ref):
    s = _silu(c_ref[...])
    o_ref[...] = jnp.dot(s, w_ref[...], precision=HIGHEST, preferred_element_type=F32) + b_ref[...]


def _adaln_mod(c, ada_w, ada_b):
    depth, d, n3 = ada_w.shape
    b = c.shape[0]
    tn = MOD_COLS
    return pl.pallas_call(
        _mod_kernel,
        grid=(depth, n3 // tn),
        in_specs=[
            pl.BlockSpec((b, d), lambda i, j: (0, 0)),
            pl.BlockSpec((None, d, tn), lambda i, j: (i, 0, j)),
            pl.BlockSpec((None, 1, tn), lambda i, j: (i, 0, j)),
        ],
        out_specs=pl.BlockSpec((None, b, tn), lambda i, j: (i, 0, j)),
        out_shape=jax.ShapeDtypeStruct((depth, b, n3), F32),
        compiler_params=_cparams("arbitrary", "arbitrary"),
        name="adaln_mod",
    )(c, ada_w, ada_b.reshape(depth, 1, n3))


def _modulated_norm(x, mod_ref, g_ref):
    ms = jnp.mean(x * x, axis=-1, keepdims=True)
    y = x * lax.rsqrt(ms + NORM_EPS) * g_ref[...]
    return y * (1.0 + mod_ref[1:2, :]) + mod_ref[0:1, :]


def _rope_tile(v, cos, sin):
    return v * cos + pltpu.roll(v, 2 * HALF, axis=1) * sin


def _store_key_chunks_t(kt_ref, k):
    dim = lax.broadcasted_iota(jnp.int32, (LANES, KEY_CHUNK), 0)
    chunks_t = [k[c * KEY_CHUNK:(c + 1) * KEY_CHUNK, :].T for c in range(kt_ref.shape[1])]
    for c, kt in enumerate(chunks_t):
        for g in range(kt_ref.shape[0]):
            kt_ref[g, c] = jnp.where((dim % HEAD_DIM) // HALF == g, kt, 0.0).astype(BF16)


def _inproj0_kernel(x_ref, mod_ref, g_ref, cos_ref, sin_ref, wq_ref, wkv_ref, wz_ref, wg_ref,
                    qp_ref, qr_ref, kc_ref, vc_ref, ks_ref, vs_ref, kw_ref, vw_ref, z_ref, gl_ref, stage):
    tm = x_ref.shape[0]
    h = _modulated_norm(x_ref[...], mod_ref, g_ref).astype(BF16)
    cos = cos_ref[...]
    sin = sin_ref[...]
    q = _dot(h, wq_ref[...])
    for j in range(q.shape[1] // LANES):
        sl = slice(j * LANES, (j + 1) * LANES)
        qp_ref[:, sl] = q[:, sl].astype(BF16)
        qr_ref[:, sl] = _rope_tile(q[:, sl], cos, sin).astype(BF16)
    kv = _dot(h, wkv_ref[...])
    tiles = [kv[:, i * LANES:(i + 1) * LANES] for i in range(6)]
    for slab, (tile, o_ref) in enumerate(((tiles[0], kc_ref), (tiles[1], vc_ref))):
        stage[slab] = tile
        for r in range(CMP_STRIDE):
            rows = stage[slab, pl.ds(r, tm // CMP_STRIDE, stride=CMP_STRIDE), :]
            o_ref[:, r * LANES:(r + 1) * LANES] = rows.astype(BF16)
    _store_key_chunks_t(ks_ref, _rope_tile(tiles[2], cos, sin))
    vs_ref[...] = tiles[3].astype(BF16)
    _store_key_chunks_t(kw_ref, _rope_tile(tiles[4], cos, sin))
    vw_ref[...] = tiles[5].astype(BF16)
    z_ref[...] = _silu(_dot(h, wz_ref[...])).astype(BF16)
    gl_ref[...] = _dot(h, wg_ref[...])


def _inproj0(x, mod, g, cos, sin, wq, wkv, wz, wg):
    b, t, d = x.shape
    nq, nz = wq.shape[1], wz.shape[1]
    tok = lambda n: pl.BlockSpec((None, TM, n), lambda i, j: (i, j, 0))
    full = lambda a: pl.BlockSpec(a.shape, lambda i, j: (0,) * a.ndim)
    tab = pl.BlockSpec((TM, LANES), lambda i, j: (j, 0))
    shp = lambda n, dt: jax.ShapeDtypeStruct((b, t, n), dt)
    kt = pl.BlockSpec((None, NSA_KV_HEADS, TM // KEY_CHUNK, LANES, KEY_CHUNK), lambda i, j: (i, 0, j, 0, 0))
    kt_shp = jax.ShapeDtypeStruct((b, NSA_KV_HEADS, t // KEY_CHUNK, LANES, KEY_CHUNK), BF16)
    kv = tok(LANES)
    kv_shp = shp(LANES, BF16)
    cm = pl.BlockSpec((None, TM // CMP_STRIDE, CMP_STRIDE * LANES), lambda i, j: (i, j, 0))
    cm_shp = jax.ShapeDtypeStruct((b, t // CMP_STRIDE, CMP_STRIDE * LANES), BF16)
    return pl.pallas_call(
        _inproj0_kernel,
        grid=(b, t // TM),
        in_specs=[tok(d), pl.BlockSpec((None, 3, d), lambda i, j: (i, 0, 0)), full(g), tab, tab,
                  full(wq), full(wkv), full(wz), full(wg)],
        out_specs=[tok(nq), tok(nq), cm, cm, kt, kv, kt, kv, tok(nz), tok(LANES)],
        out_shape=[shp(nq, BF16), shp(nq, BF16), cm_shp, cm_shp, kt_shp, kv_shp, kt_shp, kv_shp,
                   shp(nz, BF16), shp(LANES, F32)],
        scratch_shapes=[pltpu.VMEM((2, TM, LANES), F32)],
        compiler_params=_cparams("parallel", "parallel"),
        name="inproj_nsa",
    )(x, mod, g, cos, sin, wq, wkv, wz, wg)


def _compress_kernel(ak_ref, av_ref, pekt_ref, pekb_ref, pevt_ref, pevb_ref,
                     w1kt_ref, w1kb_ref, w1vt_ref, w1vb_ref, w2k_ref, w2v_ref, kcmp_ref, vcmp_ref):
    def mlp(a_ref, pet_ref, peb_ref, w1t_ref, w1b_ref, w2_ref):
        a = a_ref[...].astype(F32)
        top = _dot((a + pet_ref[...]).astype(BF16), w1t_ref[...])
        bot = _dot((a + peb_ref[...]).astype(BF16), w1b_ref[...])
        n = bot.shape[0]
        hid = top + pltpu.roll(bot, n - 1, axis=0)
        return _dot(_silu(hid).astype(BF16), w2_ref[...])

    kcmp_ref[...] = mlp(ak_ref, pekt_ref, pekb_ref, w1kt_ref, w1kb_ref, w2k_ref).astype(BF16)
    vcmp_ref[...] = mlp(av_ref, pevt_ref, pevb_ref, w1vt_ref, w1vb_ref, w2v_ref).astype(BF16)


def _compress(ak, av, consts):
    b, n, w = ak.shape
    full = lambda a: pl.BlockSpec(a.shape, lambda i: (0,) * a.ndim)
    blk = pl.BlockSpec((None, n, w), lambda i: (i, 0, 0))
    out = pl.BlockSpec((None, n, LANES), lambda i: (i, 0, 0))
    return pl.pallas_call(
        _compress_kernel,
        grid=(b,),
        in_specs=[blk, blk] + [full(a) for a in consts],
        out_specs=[out, out],
        out_shape=[jax.ShapeDtypeStruct((b, n, LANES), BF16)] * 2,
        compiler_params=_cparams("parallel"),
        name="compress",
    )(ak, av, *consts)


def _head_gate_rows(gates, g, branches, lane):
    blocks = []
    for j in range(NSA_GROUP):
        base = (g * NSA_GROUP + j) * NSA_BRANCHES
        kept = jnp.zeros_like(gates)
        for br in branches:
            kept = jnp.where(lane == base + br, gates, kept)
        hi = kept.astype(BF16)
        blocks.append(jnp.concatenate([hi, (kept - hi.astype(F32)).astype(BF16)], axis=1))
    return jnp.concatenate(blocks, axis=0)


def _gate_spread(branches):
    col = np.arange(2 * LANES) % LANES
    spread = np.zeros((2 * LANES, LANES * len(branches)), np.float32)
    for i, br in enumerate(branches):
        spread[col % NSA_BRANCHES == br, i * LANES:(i + 1) * LANES] = 1.0
    return jnp.asarray(spread, dtype=BF16)


def _cmp_attn_kernel(qp_ref, kcmp_ref, vcmp_ref, ovl_ref, gl_ref, ocmp_ref, msel_ref, *, n_slc, tq):
    n_cmp = vcmp_ref.shape[0]
    hg = qp_ref.shape[1] // LANES
    n_sub = qp_ref.shape[0] // tq
    assert n_cmp == LANES and tq == LANES
    row = lax.broadcasted_iota(jnp.int32, (LANES, tq), 0)
    col = lax.broadcasted_iota(jnp.int32, (LANES, tq), 1)
    jb = lax.broadcasted_iota(jnp.int32, (n_slc, tq), 0)
    tok_rows = lambda sub: slice(sub * tq, (sub + 1) * tq)
    t0 = lambda sub: pl.program_id(1) * qp_ref.shape[0] + sub * tq
    units = [(sub, g) for sub in range(n_sub) for g in range(NSA_KV_HEADS)]
    eye = jnp.where(row == col, 1.0, 0.0).astype(BF16)
    k_sel = jnp.concatenate([kcmp_ref[...], eye], axis=1)
    v_t = vcmp_ref[...].T

    q_t = [[qp_ref[tok_rows(sub), j * LANES:(j + 1) * LANES].T for j in range(hg)] for sub in range(n_sub)]
    gates_t = [jax.nn.sigmoid(gl_ref[tok_rows(sub), :]).T for sub in range(n_sub)]
    seen = [jnp.concatenate([jnp.where(CMP_STRIDE * row + (CMP_LEN - 1) <= t0(sub) + col, 0.0, NEG).astype(BF16)] * hg,
                            axis=1) for sub in range(n_sub)]
    weights = {}
    for sub, g in units:
        in_group = ((row % HEAD_DIM) // HALF) == g
        q_cols = jnp.concatenate([jnp.where(in_group, v, jnp.zeros_like(v)) for v in q_t[sub]], axis=1)
        s = _dot(k_sel, jnp.concatenate([q_cols, seen[sub]], axis=0))
        m = jnp.max(s, axis=0, keepdims=True)
        m = jnp.where(m > 0.5 * NEG, m, 0.0)
        e = jnp.exp(s - m)
        den = jnp.sum(e, axis=0, keepdims=True)
        weights[sub, g] = e * (1.0 / jnp.where(den > 0, den, 1.0))
    outs, imps = {}, {}
    for sub, g in units:
        p = weights[sub, g]
        gate = jnp.concatenate([gates_t[sub][(g * NSA_GROUP + j) * NSA_BRANCHES:(g * NSA_GROUP + j) * NSA_BRANCHES + 1, :]
                                for j in range(hg)], axis=1)
        outs[sub, g] = gate * _dot(v_t, p.astype(BF16))
        p_sum = p[:, 0:tq]
        for j in range(1, hg):
            p_sum = p_sum + p[:, j * tq:(j + 1) * tq]
        imps[sub, g] = jnp.dot(ovl_ref[...], p_sum, precision=HIGHEST, preferred_element_type=F32)
    picked = {}
    for sub, g in units:
        cur = (t0(sub) + lax.broadcasted_iota(jnp.int32, (n_slc, tq), 1)) // SLC_LEN
        visible = jb <= cur
        forced = (jb == 0) | (jb == cur) | (jb == cur - 1)
        rank = jnp.where(forced, -NEG, jnp.where(visible, imps[sub, g], NEG))
        cnt = jnp.zeros((n_slc, tq), F32)
        for jp in range(n_slc):
            rj = rank[jp:jp + 1, :]
            tie = jnp.where(jb > jp, 1.0, 0.0)
            cnt = cnt + jnp.where(rj > rank, 1.0, jnp.where(rj == rank, tie, 0.0))
        picked[sub, g] = jnp.where(visible, jnp.where(cnt < SLC_TOP, 1.0, 0.0), 0.0)
    pad = jnp.zeros((LANES - NSA_KV_HEADS * n_slc, tq), F32)
    for sub in range(n_sub):
        for j in range(hg):
            cols = slice(j * tq, (j + 1) * tq)
            o_t = jnp.where(row < HEAD_DIM, outs[sub, 0][:, cols], outs[sub, 1][:, cols])
            ocmp_ref[tok_rows(sub), j * LANES:(j + 1) * LANES] = o_t.T.astype(BF16)
        sel_t = jnp.concatenate([picked[sub, g] for g in range(NSA_KV_HEADS)] + [pad], axis=0)
        msel_ref[tok_rows(sub), :] = sel_t.T.astype(BF16)


def _cmp_attn(qp, kcmp, vcmp, ovl_t, gl):
    b, t, nq = qp.shape
    n_cmp = kcmp.shape[1]
    n_slc = t // SLC_LEN
    tok = lambda n: pl.BlockSpec((None, CMP_STEP_TOKENS, n), lambda i, j: (i, j, 0))
    per_b = pl.BlockSpec((None, n_cmp, LANES), lambda i, j: (i, 0, 0))
    full = lambda a: pl.BlockSpec(a.shape, lambda i, j: (0,) * a.ndim)
    return pl.pallas_call(
        functools.partial(_cmp_attn_kernel, n_slc=n_slc, tq=TQ),
        grid=(b, t // CMP_STEP_TOKENS),
        in_specs=[tok(nq), per_b, per_b, full(ovl_t), tok(LANES)],
        out_specs=[tok(nq), tok(LANES)],
        out_shape=[jax.ShapeDtypeStruct((b, t, nq), BF16), jax.ShapeDtypeStruct((b, t, LANES), BF16)],
        compiler_params=_cparams("parallel", "parallel"),
        name="cmp_attn_select",
    )(qp, kcmp, vcmp, ovl_t, gl)


def _values_with_ones(v, g):
    in_own_half = (lax.broadcasted_iota(jnp.int32, v.shape, 1) // HEAD_DIM) == g
    return jnp.where(in_own_half, v, jnp.ones_like(v))


def _lane_tiles(v):
    return [v[:, i * LANES:(i + 1) * LANES] for i in range(v.shape[1] // LANES)]


def _tile_max(tiles):
    mx = tiles[0]
    for v in tiles[1:]:
        mx = jnp.maximum(mx, v)
    return jnp.max(mx, axis=-1, keepdims=True)


def _probs(tiles, m):
    return jnp.concatenate([jnp.exp(v - m) for v in tiles], axis=1).astype(BF16)


def _attend_chunks(q_sel, ks_ref, e_ref, vs_ref, m_ref, acc_ref, c0, n):
    k0 = pl.multiple_of(c0 * KEY_CHUNK, KEY_CHUNK)
    scores = []
    for g in range(NSA_KV_HEADS):
        kt = jnp.concatenate([ks_ref[g, c0 + i] for i in range(n)], axis=1)
        eb = jnp.concatenate([e_ref[g, c0 + i] for i in range(n)], axis=1)
        scores.append(_dot(q_sel, jnp.concatenate([kt, eb], axis=0)))
    groups = range(NSA_KV_HEADS)
    m_old = [m_ref[g] for g in groups]
    m_new = [jnp.maximum(m_old[g], _tile_max(_lane_tiles(scores[g]))) for g in groups]
    p = [_probs(_lane_tiles(scores[g]), m_new[g]) for g in groups]
    pv = [_dot(p[g], _values_with_ones(vs_ref[pl.ds(k0, n * KEY_CHUNK), :], g)) for g in groups]
    for g in groups:
        acc_ref[g] = jnp.exp(m_old[g] - m_new[g]) * acc_ref[g] + pv[g]
        m_ref[g] = m_new[g]


def _nsa_attn_kernel(*refs, tq, n_slc, ck):
    *io_refs, m_scr, acc_scr = refs
    qr_ref, ks_ref, vs_ref, _, _, _, e_ref = io_refs[:7]
    n_sub = qr_ref.shape[0] // tq
    per_ck = ck // KEY_CHUNK
    rows = (qr_ref.shape[1] // LANES) * tq
    assert per_ck % n_sub == 0
    first = pl.program_id(1) * n_sub
    tiles = [_nsa_attn_tile(*io_refs, m_scr.at[:, sub * rows:(sub + 1) * rows, :], acc_scr.at[:, sub * rows:(sub + 1) * rows, :],
                            tok=slice(sub * tq, (sub + 1) * tq), qi=first + sub, n_slc=n_slc, ck=ck) for sub in range(n_sub)]
    q_sel = jnp.concatenate([next(tile) for tile in tiles], axis=0)

    def whole_steps(kc, carry):
        _attend_chunks(q_sel, ks_ref, e_ref, vs_ref, m_scr, acc_scr, kc * per_ck, per_ck)
        return carry

    lax.fori_loop(0, first // per_ck, whole_steps, 0)
    _run_in_lockstep(tiles)


def _nsa_attn_tile(qr_ref, ks_ref, vs_ref, kw_ref, vw_ref, msel_ref, e_ref, gsp_ref, ocmp_ref, z_ref, gl_ref,
                   og_ref, m_scr, acc_scr, *, tok, qi, n_slc, ck):
    tq = tok.stop - tok.start
    hg = qr_ref.shape[1] // LANES
    rows = hg * tq
    n_wc = WIN_LEN // tq + 1
    per_ck = ck // KEY_CHUNK
    t0 = qi * tq
    lane = lax.broadcasted_iota(jnp.int32, (tq, LANES), 1)
    a_idx = lax.broadcasted_iota(jnp.int32, (tq, LANES), 0)

    eye = jnp.where(lane == a_idx, 1.0, 0.0).astype(BF16)
    earlier = jnp.where((lane % n_slc) < t0 // SLC_LEN, 0.0, NEG)
    mneg = jnp.where(msel_ref[tok, :].astype(F32) > 0.5, earlier, NEG).astype(BF16)
    mneg_rows = jnp.concatenate([mneg] * hg, axis=0)
    eye_rows = jnp.concatenate([eye] * hg, axis=0)

    def with_selector(sel_rows):
        q_rows = jnp.concatenate([qr_ref[tok, j * LANES:(j + 1) * LANES] for j in range(hg)], axis=0)
        return jnp.concatenate([q_rows, sel_rows], axis=1)

    zero_b = jnp.zeros((tq, LANES), BF16)
    neg_b = jnp.full((tq, LANES), NEG, BF16)
    causal_b = jnp.where(lane <= a_idx, 0.0, NEG).astype(BF16)
    far_b = jnp.where(lane > a_idx, 0.0, NEG).astype(BF16)
    own_half = [lane < HEAD_DIM, lane >= HEAD_DIM]
    values_with_ones, lane_tiles, tile_max, probs = _values_with_ones, _lane_tiles, _tile_max, _probs

    m_scr[...] = jnp.full(m_scr.shape, NEG, F32)
    acc_scr[...] = jnp.zeros(acc_scr.shape, F32)
    yield with_selector(mneg_rows)

    for rem in range(1, per_ck):
        @pl.when(qi % per_ck == rem)
        def _():
            _attend_chunks(with_selector(mneg_rows), ks_ref, e_ref, vs_ref, m_scr, acc_scr, (qi // per_ck) * per_ck, rem)
    yield

    kd = pl.multiple_of(t0, tq)
    zeros_v = jnp.zeros((tq, LANES), BF16)
    win_chunks = [qi - (n_wc - 1 - c) for c in range(n_wc - 1)]
    win_bias = [jnp.where(cidx >= 0, far_b if c == 0 else zero_b, neg_b) for c, cidx in enumerate(win_chunks)]
    pairs = [(c, c + 1) for c in range(0, n_wc - 1, 2)]
    q_eye = with_selector(eye_rows)
    s_own, scores = [], []
    for g in range(NSA_KV_HEADS):
        own_k = jnp.concatenate([jnp.concatenate([kw_ref[g, qi], causal_b], axis=0),
                                 jnp.concatenate([ks_ref[g, qi], causal_b], axis=0)], axis=1)
        kbs = [jnp.concatenate([kw_ref[g, jnp.maximum(cidx, 0)], win_bias[c]], axis=0)
               for c, cidx in enumerate(win_chunks)]
        s_own.append(_dot(q_eye, own_k))
        scores.append([_dot(q_eye, jnp.concatenate([kbs[c] for c in pr], axis=1)) for pr in pairs])

    yield
    groups = range(NSA_KV_HEADS)
    s_win_own = [s_own[g][:, 0:tq] for g in groups]
    s_slc_own = [s_own[g][:, tq:2 * tq] for g in groups]
    m_win = [tile_max([v for s in scores[g] for v in lane_tiles(s)] + [s_win_own[g]]) for g in groups]
    m_old = [m_scr[g] for g in groups]
    m_slc = [jnp.maximum(m_old[g], jnp.max(s_slc_own[g], axis=-1, keepdims=True)) for g in groups]
    p_all = [jnp.concatenate([probs(lane_tiles(s), m_win[g]) for s in scores[g]]
                             + [probs([s_win_own[g]], m_win[g]), probs([s_slc_own[g]], m_slc[g])], axis=1) for g in groups]
    yield
    acc_slc, acc_win = [], []
    for g in groups:
        vbs = [values_with_ones(vw_ref[pl.ds(pl.multiple_of(jnp.maximum(cidx, 0) * tq, tq), tq), :], g)
               for cidx in win_chunks] + [values_with_ones(vw_ref[pl.ds(kd, tq), :], g)]
        win_v = [jnp.concatenate([v, zeros_v], axis=1) for v in vbs]
        slc_v = jnp.concatenate([zeros_v, values_with_ones(vs_ref[pl.ds(kd, tq), :], g)], axis=1)
        pv = _dot(p_all[g], jnp.concatenate(win_v + [slc_v], axis=0))
        acc_win.append(pv[:, 0:LANES])
        acc_slc.append(jnp.exp(m_old[g] - m_slc[g]) * acc_scr[g] + pv[:, LANES:2 * LANES])
    yield

    gates = jax.nn.sigmoid(gl_ref[tok, :])
    row_gates = [_dot(_head_gate_rows(gates, g, (1, 2), lane), gsp_ref[...]) for g in range(NSA_KV_HEADS)]

    def gated(accs, rs, branch_lanes):
        num = [accs[g][rs] * row_gates[g][rs, branch_lanes] for g in range(NSA_KV_HEADS)]
        sums = pltpu.roll(jnp.where(own_half[0], accs[1][rs], accs[0][rs]), HEAD_DIM, axis=1)
        return jnp.where(own_half[0], num[0], num[1]) / sums

    for j in range(hg):
        rs = slice(j * tq, (j + 1) * tq)
        cols = slice(j * LANES, (j + 1) * LANES)
        o = (ocmp_ref[tok, cols].astype(F32) + gated(acc_slc, rs, slice(0, LANES))
             + gated(acc_win, rs, slice(LANES, 2 * LANES)))
        og_ref[tok, cols] = (o * z_ref[tok, cols].astype(F32)).astype(BF16)


def _nsa_attn(qr, ks, vs, kw, vw, msel, e, ocmp, z, gl):
    b, t, nq = qr.shape
    gsp = _gate_spread((1, 2))
    hg = nq // LANES
    rows = hg * TQ
    assert TQ == KEY_CHUNK and WIN_LEN % TQ == 0
    tok = lambda n: pl.BlockSpec((None, NSA_STEP_TOKENS, n), lambda i, j: (i, j, 0))
    per_b = pl.BlockSpec((None, t, LANES), lambda i, j: (i, 0, 0))
    per_b_t = pl.BlockSpec((None, NSA_KV_HEADS, t // KEY_CHUNK, LANES, KEY_CHUNK), lambda i, j: (i, 0, 0, 0, 0))
    full = lambda a: pl.BlockSpec(a.shape, lambda i, j: (0,) * a.ndim)
    return pl.pallas_call(
        functools.partial(_nsa_attn_kernel, tq=TQ, n_slc=t // SLC_LEN, ck=SLC_CK),
        grid=(b, t // NSA_STEP_TOKENS),
        in_specs=[tok(nq), per_b_t, per_b, per_b_t, per_b, tok(LANES), full(e), full(gsp), tok(nq), tok(nq), tok(LANES)],
        out_specs=tok(nq),
        out_shape=jax.ShapeDtypeStruct((b, t, nq), BF16),
        scratch_shapes=[
            pltpu.VMEM((NSA_KV_HEADS, NSA_STEP_TOKENS // TQ * rows, LANES), F32),
            pltpu.VMEM((NSA_KV_HEADS, NSA_STEP_TOKENS // TQ * rows, LANES), F32),
        ],
        compiler_params=_cparams("parallel", "arbitrary"),
        name="nsa_slc_win_attn",
    )(qr, ks, vs, kw, vw, msel, e, gsp, ocmp, z, gl)


def _inproj1_kernel(og_ref, wo_ref, x_ref, mod0_ref, mod_ref, g_ref, cos_ref, sin_ref, wq_ref, wk_ref, wv_ref, wz_ref,
                    x1_ref, *refs):
    n_grp = len(DIL_PATTERNS)
    outs = [refs[a * n_grp:(a + 1) * n_grp] for a in range(3)]
    z_ref, stage = refs[3 * n_grp], refs[3 * n_grp + 1]
    tm = x_ref.shape[0]
    gw = DIL_TILES_PER_GROUP * LANES
    x1 = x_ref[...] + mod0_ref[2:3, :] * _dot(og_ref[...], wo_ref[...])
    x1_ref[...] = x1
    h = _modulated_norm(x1, mod_ref, g_ref).astype(BF16)
    cos = cos_ref[...]
    sin = sin_ref[...]
    projected = [_dot(h, w_ref[...]) for w_ref in (wq_ref, wk_ref, wv_ref, wz_ref)]
    for a, u in enumerate(projected[0:3]):
        tiles = [u[:, pt * LANES:(pt + 1) * LANES] for pt in range(len(DIL_HEAD_PAIRS))]
        if a < 2:
            tiles = [_rope_tile(v, cos, sin) for v in tiles]
        for gi, (_, dil) in enumerate(DIL_PATTERNS):
            for jt, (pt, _) in enumerate(DIL_GROUP_TILES[gi]):
                tile = tiles[pt]
                if dil == 1:
                    outs[a][gi][:, jt * LANES:(jt + 1) * LANES] = tile.astype(BF16)
                    continue
                slab = (a * n_grp + gi) * DIL_TILES_PER_GROUP + jt
                stage[slab] = tile
                for r in range(dil):
                    rows = stage[slab, pl.ds(r, tm // dil, stride=dil), :]
                    outs[a][gi][:, r * gw + jt * LANES:r * gw + (jt + 1) * LANES] = rows.astype(BF16)
    z_ref[...] = _silu(projected[3]).astype(BF16)


def _outproj0_inproj1(og, wo, x, mod0, mod, g, cos, sin, wq, wk, wv, wz):
    b, t, d = x.shape
    n = wq.shape[1]
    gw = DIL_TILES_PER_GROUP * LANES
    tok = lambda k: pl.BlockSpec((None, TM, k), lambda i, j: (i, j, 0))
    full = lambda a: pl.BlockSpec(a.shape, lambda i, j: (0,) * a.ndim)
    tab = pl.BlockSpec((TM, LANES), lambda i, j: (j, 0))
    mods = pl.BlockSpec((None, 3, d), lambda i, j: (i, 0, 0))
    grp_specs = [pl.BlockSpec((None, TM // dil, dil * gw), lambda i, j: (i, j, 0)) for _, dil in DIL_PATTERNS]
    grp_shapes = [jax.ShapeDtypeStruct((b, t // dil, dil * gw), BF16) for _, dil in DIL_PATTERNS]
    n_slabs = 3 * len(DIL_PATTERNS) * DIL_TILES_PER_GROUP
    return pl.pallas_call(
        _inproj1_kernel,
        grid=(b, t // TM),
        in_specs=[tok(og.shape[2]), full(wo), tok(d), mods, mods, full(g), tab, tab,
                  full(wq), full(wk), full(wv), full(wz)],
        out_specs=[tok(d)] + grp_specs * 3 + [tok(n)],
        out_shape=[jax.ShapeDtypeStruct((b, t, d), F32)] + grp_shapes * 3 + [jax.ShapeDtypeStruct((b, t, n), BF16)],
        scratch_shapes=[pltpu.VMEM((n_slabs, TM, LANES), F32)],
        compiler_params=_cparams("parallel", "parallel"),
        name="outproj_nsa_inproj_dil",
    )(og, wo, x, mod0, mod, g, cos, sin, wq, wk, wv, wz)


def _dil_attn_kernel(*refs, plans):
    n_grp = len(plans)
    step = pl.program_id(1)
    groups = []
    for gi, (members, tq, n_seq, blocks_per_seq) in enumerate(plans):
        q_ref, k_ref, v_ref = refs[3 * gi:3 * gi + 3]
        o_ref, lse_ref = refs[3 * n_grp + 2 * gi:3 * n_grp + 2 * gi + 2]
        groups.append(_dil_attn_group(q_ref, k_ref, v_ref, o_ref, lse_ref, step % blocks_per_seq, members, tq, n_seq))
    _run_in_lockstep(groups)


def _dil_attn_group(q_ref, k_ref, v_ref, o_ref, lse_ref, i, members, tq, n_seq):
    n_heads = sum(len(m) for m in members)
    tq_blk = q_ref.shape[0]
    gw = q_ref.shape[1] // n_seq
    n_tiles = gw // LANES
    n_sub = tq_blk // tq
    assert tq == LANES
    row = lax.broadcasted_iota(jnp.int32, (LANES, tq), 0)
    col = lax.broadcasted_iota(jnp.int32, (LANES, tq), 1)
    head_a = ((row % HEAD_DIM) // HALF) == 0
    diag_b = jnp.where(row <= col, 0.0, NEG)
    prev_b = jnp.where(row >= col, 0.0, NEG)
    neg_b = jnp.full((LANES, tq), NEG, F32)
    two = lambda v: jnp.concatenate([v, v], axis=1)
    units = [(sq, sub, jt) for sq in range(n_seq) for sub in range(n_sub) for jt in range(n_tiles)]
    rows_of = lambda sub: slice(sub * tq, (sub + 1) * tq)
    cols_of = lambda sq, jt: slice(sq * gw + jt * LANES, sq * gw + (jt + 1) * LANES)
    key_rows = lambda c: pl.ds(pl.multiple_of(jnp.maximum(i * n_sub + c, 0) * tq, tq), tq)
    q_cols, v_t = {}, {}
    for sq, sub, jt in units:
        q_t = q_ref[rows_of(sub), cols_of(sq, jt)].T
        zero = jnp.zeros_like(q_t)
        q_cols[sq, sub, jt] = jnp.concatenate([jnp.where(head_a, q_t, zero), jnp.where(head_a, zero, q_t)], axis=1)
        v_t[sq, sub, jt] = v_ref[key_rows(sub), cols_of(sq, jt)].T
        if sub == 0:
            v_t[sq, -1, jt] = v_ref[key_rows(-1), cols_of(sq, jt)].T
    yield
    probs_t, stats = {}, {}
    for sq, sub, jt in units:
        prev_bias = prev_b if sub > 0 else jnp.where(i > 0, prev_b, neg_b)
        bias = jnp.concatenate([two(prev_bias), two(diag_b)], axis=0)
        keys = jnp.concatenate([k_ref[key_rows(sub - 1), cols_of(sq, jt)], k_ref[key_rows(sub), cols_of(sq, jt)]], axis=0)
        s = _dot(keys, q_cols[sq, sub, jt]) + bias
        m = jnp.max(s, axis=0, keepdims=True)
        p = jnp.exp2(s - m)
        l = jnp.sum(p, axis=0, keepdims=True)
        probs_t[sq, sub, jt] = p.astype(BF16)
        stats[sq, sub, jt] = (m, l)
    yield
    for sq, sub, jt in units:
        m, l = stats[sq, sub, jt]
        o_t = _dot(jnp.concatenate([v_t[sq, sub - 1, jt], v_t[sq, sub, jt]], axis=1), probs_t[sq, sub, jt])
        inv = 1.0 / l
        o_t = jnp.where(row < HEAD_DIM, o_t[:, 0:tq] * inv[:, 0:tq], o_t[:, tq:2 * tq] * inv[:, tq:2 * tq])
        o_ref[rows_of(sub), cols_of(sq, jt)] = o_t.T.astype(BF16)
    yield
    for sq in range(n_seq):
        for sub in range(n_sub):
            lses = []
            for jt in range(n_tiles):
                m, l = stats[sq, sub, jt]
                lse = m * LN2 + jnp.log(l)
                lses += [lse[:, mb * tq:(mb + 1) * tq] for mb in members[jt]]
            top = lses[0]
            for v in lses[1:]:
                top = jnp.maximum(top, v)
            tot = jnp.zeros_like(top)
            for v in lses:
                tot = tot + jnp.exp(v - top)
            group_lse = top + jnp.log(tot) - math.log(n_heads)
            lse_ref[rows_of(sub), sq * LANES:(sq + 1) * LANES] = jnp.broadcast_to(group_lse, (LANES, tq)).T


def _dil_attn(qs, ks, vs, t):
    b = qs[0].shape[0]
    steps = t // DIL_STEP_TOKENS
    plans, in_specs, out_specs, out_shapes, operands = [], [], [], [], []
    for gi, (_, dil) in enumerate(DIL_PATTERNS):
        length = t // dil
        gw = qs[gi].shape[2] // dil
        tq_blk = min(DIL_STEP_TOKENS, length)
        blocks_per_seq = length // tq_blk
        n_seq = DIL_STEP_TOKENS // tq_blk
        assert dil * blocks_per_seq == steps * n_seq
        tok = lambda w, bps=blocks_per_seq, ns=n_seq, tb=tq_blk: pl.BlockSpec(
            (None, tb, ns * w), lambda bi, s: (bi, s % bps, s // bps))
        seq = pl.BlockSpec((None, length, n_seq * gw), lambda bi, s, bps=blocks_per_seq: (bi, 0, s // bps))
        plans.append((tuple(m for _, m in DIL_GROUP_TILES[gi]), min(TQ, length), n_seq, blocks_per_seq))
        in_specs += [tok(gw), seq, seq]
        operands += [qs[gi], ks[gi], vs[gi]]
        out_specs += [tok(gw), tok(LANES)]
        out_shapes += [jax.ShapeDtypeStruct(qs[gi].shape, BF16), jax.ShapeDtypeStruct((b, length, dil * LANES), F32)]
    outs = pl.pallas_call(
        functools.partial(_dil_attn_kernel, plans=tuple(plans)),
        grid=(b, steps),
        in_specs=in_specs,
        out_specs=out_specs,
        out_shape=out_shapes,
        compiler_params=_cparams("parallel", "parallel"),
        name="dil_attn",
    )(*operands)
    return outs[0::2], outs[1::2]


def _outproj1_kernel(*refs):
    n_grp = len(DIL_PATTERNS)
    o_refs, l_refs = refs[0:n_grp], refs[n_grp:2 * n_grp]
    z_ref, w_ref, x_ref, mod_ref, fg_ref, out_ref, stage = refs[2 * n_grp:]
    tm = x_ref.shape[0]
    gw = DIL_TILES_PER_GROUP * LANES

    def token_order(ref, gi, width, col0, slab):
        dil = DIL_PATTERNS[gi][1]
        if dil == 1:
            return ref[:, col0:col0 + LANES].astype(F32)
        for r in range(dil):
            stage[slab, pl.ds(r, tm // dil, stride=dil), :] = ref[:, r * width + col0:r * width + col0 + LANES].astype(F32)
        return stage[slab]

    ls = [token_order(l_refs[gi], gi, LANES, 0, gi) for gi in range(n_grp)]
    top = jnp.maximum(jnp.maximum(ls[0], ls[1]), ls[2])
    es = [jnp.exp(v - top) for v in ls]
    den = es[0] + es[1] + es[2]
    alphas = [e / den * float(n_grp) for e in es]
    lane = lax.broadcasted_iota(jnp.int32, (tm, LANES), 1)
    weighted = [None] * len(DIL_HEAD_PAIRS)
    for gi in range(n_grp):
        for jt, (pt, members) in enumerate(DIL_GROUP_TILES[gi]):
            ot = token_order(o_refs[gi], gi, gw, jt * LANES, n_grp + gi * DIL_TILES_PER_GROUP + jt) * alphas[gi]
            if len(members) == 2:
                weighted[pt] = ot
            else:
                mine = (lane // HEAD_DIM) == members[0]
                weighted[pt] = jnp.where(mine, ot, 0.0 if weighted[pt] is None else weighted[pt])
    parts = [(weighted[pt] * z_ref[:, pt * LANES:(pt + 1) * LANES].astype(F32)).astype(BF16)
             for pt in range(len(DIL_HEAD_PAIRS))]
    og = jnp.concatenate(parts, axis=1)
    y = _dot(og, w_ref[...])
    x2 = x_ref[...] + mod_ref[2:3, :] * y
    ms = jnp.mean(x2 * x2, axis=-1, keepdims=True)
    out_ref[...] = x2 * lax.rsqrt(ms + NORM_EPS) * fg_ref[...]


def _outproj1(o_groups, lse_groups, z, w, x, mod, final_g):
    b, t, d = x.shape
    tm = TM_OUT
    tok = lambda k: pl.BlockSpec((None, tm, k), lambda i, j: (i, j, 0))
    full = lambda a: pl.BlockSpec(a.shape, lambda i, j: (0,) * a.ndim)
    grp = lambda a, dil: pl.BlockSpec((None, tm // dil, a.shape[2]), lambda i, j: (i, j, 0))
    dils = [dil for _, dil in DIL_PATTERNS]
    n_slabs = len(dils) * (1 + DIL_TILES_PER_GROUP)
    return pl.pallas_call(
        _outproj1_kernel,
        grid=(b, t // tm),
        in_specs=[grp(a, dl) for a, dl in zip(o_groups, dils)] + [grp(a, dl) for a, dl in zip(lse_groups, dils)]
                 + [tok(z.shape[2]), full(w), tok(d), pl.BlockSpec((None, 3, d), lambda i, j: (i, 0, 0)), full(final_g)],
        out_specs=tok(d),
        out_shape=jax.ShapeDtypeStruct((b, t, d), F32),
        scratch_shapes=[pltpu.VMEM((n_slabs, tm, LANES), F32)],
        compiler_params=_cparams("parallel", "parallel"),
        name="outproj_dil_final",
    )(*o_groups, *lse_groups, z, w, x, mod, final_g)


def _rope_tables(t):
    inv = ROPE_THETA ** (-jnp.arange(HALF, dtype=F32) / HALF)
    ang = jnp.arange(t, dtype=F32)[:, None] * inv[None, :]
    cos = jnp.tile(jnp.cos(ang), (1, 4))
    sin = jnp.tile(jnp.sin(ang), (1, 4))
    sign = jnp.where(jnp.arange(LANES) < 2 * HALF, -1.0, 1.0).astype(F32)
    return cos, sin * sign[None, :]


def _pair_rope_layout(w, n_pairs):
    d = w.shape[0]
    return w.reshape(d, n_pairs, 2, 2, HALF).transpose(0, 1, 3, 2, 4).reshape(d, n_pairs * LANES)


def _nsa_weights(w_in, pe_k, pe_v, ck_w1, ck_w2, cv_w1, cv_w2, w_out):
    d = w_in.shape[0]
    aw = N_HEADS * HEAD_DIM
    kvw = NSA_KV_HEADS * HEAD_DIM
    cuts = np.cumsum([aw] + [kvw] * 6 + [aw]).tolist()
    q, k_c, v_c, k_s, v_s, k_w, v_w, z, gl = jnp.split(w_in, cuts, axis=1)
    scale = HEAD_DIM ** -0.5
    wq = (q * scale).reshape(d, NSA_KV_HEADS, NSA_GROUP, 2, HALF).transpose(0, 2, 3, 1, 4).reshape(d, aw)
    k_lay = lambda w: _pair_rope_layout(w, 1)
    wkv = jnp.concatenate([k_lay(k_c), v_c, k_lay(k_s), v_s, k_lay(k_w), v_w], axis=1)
    wz = z.reshape(d, NSA_KV_HEADS, NSA_GROUP, HEAD_DIM).transpose(0, 2, 1, 3).reshape(d, aw)
    wo = w_out.reshape(NSA_KV_HEADS, NSA_GROUP, HEAD_DIM, -1).transpose(1, 0, 2, 3).reshape(aw, -1)
    wg = jnp.pad(gl, ((0, 0), (0, LANES - gl.shape[1])))

    def grouped(w1_half, rope_lanes):
        out = []
        for g in range(NSA_KV_HEADS):
            if rope_lanes:
                src = w1_half.reshape(CMP_STRIDE, 2, 1, HALF, CMP_HIDDEN)
                pads = ((0, 0), (0, 0), (g, NSA_KV_HEADS - 1 - g), (0, 0), (0, 0))
            else:
                src = w1_half.reshape(CMP_STRIDE, 1, HEAD_DIM, CMP_HIDDEN)
                pads = ((0, 0), (g, NSA_KV_HEADS - 1 - g), (0, 0), (0, 0))
            out.append(jnp.pad(src, pads).reshape(CMP_STRIDE * LANES, CMP_HIDDEN))
        return jnp.concatenate(out, axis=1)

    def pe_rows(pe_half, rope_lanes):
        if rope_lanes:
            v = jnp.broadcast_to(pe_half.reshape(CMP_STRIDE, 2, 1, HALF), (CMP_STRIDE, 2, NSA_KV_HEADS, HALF))
        else:
            v = jnp.broadcast_to(pe_half.reshape(CMP_STRIDE, 1, HEAD_DIM), (CMP_STRIDE, NSA_KV_HEADS, HEAD_DIM))
        return v.reshape(1, CMP_STRIDE * LANES)

    def w2_padded(w2, rope_lanes):
        out = []
        for g in range(NSA_KV_HEADS):
            if rope_lanes:
                src = w2.reshape(CMP_HIDDEN, 2, 1, HALF)
                pads = ((0, 0), (0, 0), (g, NSA_KV_HEADS - 1 - g), (0, 0))
            else:
                src = w2.reshape(CMP_HIDDEN, 1, HEAD_DIM)
                pads = ((0, 0), (g, NSA_KV_HEADS - 1 - g), (0, 0))
            out.append(jnp.pad(src, pads).reshape(CMP_HIDDEN, LANES))
        return jnp.concatenate(out, axis=0)

    k1 = ck_w1.reshape(2, CMP_STRIDE, HEAD_DIM, CMP_HIDDEN)
    v1 = cv_w1.reshape(2, CMP_STRIDE, HEAD_DIM, CMP_HIDDEN)
    compress_consts = (
        pe_rows(pe_k[:CMP_STRIDE], True), pe_rows(pe_k[CMP_STRIDE:], True),
        pe_rows(pe_v[:CMP_STRIDE], False), pe_rows(pe_v[CMP_STRIDE:], False),
        grouped(k1[0], True).astype(BF16), grouped(k1[1], True).astype(BF16),
        grouped(v1[0], False).astype(BF16), grouped(v1[1], False).astype(BF16),
        w2_padded(ck_w2, True).astype(BF16), w2_padded(cv_w2, False).astype(BF16),
    )
    return wq.astype(BF16), wkv.astype(BF16), wz.astype(BF16), wg.astype(BF16), wo.astype(BF16), compress_consts


def _dil_weights(w_in, w_out):
    d = w_in.shape[0]
    q, k, v, z = jnp.split(w_in, 4, axis=1)
    n_pairs = len(DIL_HEAD_PAIRS)
    order = [hd for pair in DIL_HEAD_PAIRS for hd in pair]

    def tile_order(w, axis):
        parts = []
        for hd in order:
            sl = [slice(None)] * w.ndim
            sl[axis] = slice(hd * HEAD_DIM, (hd + 1) * HEAD_DIM)
            parts.append(w[tuple(sl)])
        return jnp.concatenate(parts, axis=axis)

    scale = HEAD_DIM ** -0.5 * LOG2E
    wq = _pair_rope_layout(tile_order(q * scale, 1), n_pairs)
    wk = _pair_rope_layout(tile_order(k, 1), n_pairs)
    return (wq.astype(BF16), wk.astype(BF16), tile_order(v, 1).astype(BF16), tile_order(z, 1).astype(BF16),
            tile_order(w_out, 0).astype(BF16))


def _selection_constants(t):
    n_cmp_rows = t // CMP_STRIDE
    n_slc = t // SLC_LEN
    c_start = CMP_STRIDE * np.arange(n_cmp_rows)
    s_start = SLC_LEN * np.arange(n_slc)
    ovl_t = ((c_start[None, :] < s_start[:, None] + SLC_LEN)
             & (c_start[None, :] + CMP_LEN > s_start[:, None])).astype(np.float32)
    ovl_t[:, (t - CMP_LEN) // CMP_STRIDE + 1:] = 0.0
    e = np.zeros((NSA_KV_HEADS, t // KEY_CHUNK, LANES, KEY_CHUNK), np.float32)
    keys = np.arange(t)
    for g in range(NSA_KV_HEADS):
        e[g, keys // KEY_CHUNK, g * n_slc + keys // SLC_LEN, keys % KEY_CHUNK] = 1.0
    return jnp.asarray(ovl_t), jnp.asarray(e, dtype=BF16)


@jax.jit
def kernel(x, c, norm_g, ada_w, ada_b, nsa_w_in, nsa_pe_k, nsa_pe_v, nsa_ck_w1, nsa_ck_w2,
           nsa_cv_w1, nsa_cv_w2, nsa_w_out, dil_w_in, dil_w_out, final_g):
    b, t, d = x.shape
    assert t % SLC_CK == 0 and t % TM == 0 and d % LANES == 0
    assert NSA_KV_HEADS * (t // SLC_LEN) <= LANES and t // CMP_STRIDE == LANES
    mod = _adaln_mod(c, ada_w, ada_b).reshape(ada_w.shape[0], b, 3, d)
    cos, sin = _rope_tables(t)

    wq, wkv, wz, wg, wo, compress_consts = _nsa_weights(
        nsa_w_in[0], nsa_pe_k[0], nsa_pe_v[0], nsa_ck_w1[0], nsa_ck_w2[0], nsa_cv_w1[0], nsa_cv_w2[0], nsa_w_out[0])
    qp, qr, kc, vc, ks, vs, kw, vw, z, gl = _inproj0(x, mod[0], norm_g[0:1], cos, sin, wq, wkv, wz, wg)
    kcmp, vcmp = _compress(kc, vc, compress_consts)
    ovl_t, e_sel = _selection_constants(t)
    ocmp, msel = _cmp_attn(qp, kcmp, vcmp, ovl_t, gl)
    og = _nsa_attn(qr, ks, vs, kw, vw, msel, e_sel, ocmp, z, gl)

    dq, dk, dv, dz, dwo = _dil_weights(dil_w_in[0], dil_w_out[0])
    n_grp = len(DIL_PATTERNS)
    x1, *qkv, z1 = _outproj0_inproj1(og, wo, x, mod[0], mod[1], norm_g[1:2], cos, sin, dq, dk, dv, dz)
    assert all(win // dil == DIL_WIN and TM % (dil * 16) == 0 for win, dil in DIL_PATTERNS)
    o_groups, lse_groups = _dil_attn(qkv[0:n_grp], qkv[n_grp:2 * n_grp], qkv[2 * n_grp:3 * n_grp], t)
    return _outproj1(o_groups, lse_groups, z1, dwo, x1, mod[1], final_g.reshape(1, d))
```

```python
import functools
import math

import numpy as np
import jax
import jax.numpy as jnp
from jax import lax
from jax.experimental import pallas as pl
from jax.experimental.pallas import tpu as pltpu

F32 = jnp.float32
BF16 = jnp.bfloat16
HIGHEST = lax.Precision.HIGHEST

HEAD_DIM = 64
HALF = HEAD_DIM // 2
N_HEADS = 16
ROPE_THETA = 10000.0
NORM_EPS = 1e-6
NSA_KV_HEADS = 2
NSA_GROUP = N_HEADS // NSA_KV_HEADS
NSA_BRANCHES = 3
CMP_LEN = 32
CMP_STRIDE = 16
CMP_HIDDEN = 256
SLC_LEN = 64
SLC_TOP = 16
WIN_LEN = 512
DIL_PATTERNS = ((128, 1), (512, 4), (2048, 16))
DIL_GROUP_HEADS = (6, 5, 5)
DIL_WIN = 128


def _dil_tile_plan():
    offs = np.cumsum((0,) + DIL_GROUP_HEADS)
    pairs, group_tiles, leftovers = [], [], []
    for gi, hn in enumerate(DIL_GROUP_HEADS):
        heads = list(range(offs[gi], offs[gi + 1]))
        group_tiles.append([])
        for i in range(0, hn - hn % 2, 2):
            group_tiles[gi].append((len(pairs), (0, 1)))
            pairs.append((heads[i], heads[i + 1]))
        if hn % 2:
            leftovers.append((gi, heads[-1]))
    assert len(leftovers) % 2 == 0
    for (ga, ha), (gb, hb) in zip(leftovers[0::2], leftovers[1::2]):
        group_tiles[ga].append((len(pairs), (0,)))
        group_tiles[gb].append((len(pairs), (1,)))
        pairs.append((ha, hb))
    return tuple(pairs), tuple(tuple(tiles) for tiles in group_tiles)


DIL_HEAD_PAIRS, DIL_GROUP_TILES = _dil_tile_plan()
DIL_TILES_PER_GROUP = len(DIL_GROUP_TILES[0])
assert all(len(tiles) == DIL_TILES_PER_GROUP for tiles in DIL_GROUP_TILES)

LANES = 128
VMEM_LIMIT_BYTES = 48 * 1024 * 1024

NEG = -1e30
LOG2E = math.log2(math.e)
LN2 = math.log(2.0)
TQ = 128
KEY_CHUNK = 128
SLC_CK = 512
MOD_COLS = 1024
TM = 512
TM_IN = 1024
PART_ROWS = 256
TM_OUT = 1024
DIL_STEP_TOKENS = 1024
NSA_STEP_TOKENS = 256
CMP_STEP_TOKENS = 2048


def _cparams(*sem):
    return pltpu.CompilerParams(dimension_semantics=sem, vmem_limit_bytes=VMEM_LIMIT_BYTES)


def _dot(a, b):
    return jnp.dot(a, b, preferred_element_type=F32)


def _silu(v):
    return v * jax.nn.sigmoid(v)


def _run_in_lockstep(stage_generators):
    pending = list(stage_generators)
    done = object()
    while pending:
        pending = [gen for gen in pending if next(gen, done) is not done]


def _mod_kernel(c_ref, w_ref, b_ref, o_ref):
    s = _silu(c_ref[...])
    o_ref[...] = jnp.dot(s, w_ref[...], precision=HIGHEST, preferred_element_type=F32) + b_ref[...]


def _adaln_mod(c, ada_w, ada_b):
    depth, d, n3 = ada_w.shape
    b = c.shape[0]
    tn = MOD_COLS
    return pl.pallas_call(
        _mod_kernel,
        grid=(depth, n3 // tn),
        in_specs=[
            pl.BlockSpec((b, d), lambda i, j: (0, 0)),
            pl.BlockSpec((None, d, tn), lambda i, j: (i, 0, j)),
            pl.BlockSpec((None, 1, tn), lambda i, j: (i, 0, j)),
        ],
        out_specs=pl.BlockSpec((None, b, tn), lambda i, j: (i, 0, j)),
        out_shape=jax.ShapeDtypeStruct((depth, b, n3), F32),
        compiler_params=_cparams("arbitrary", "arbitrary"),
        name="adaln_mod",
    )(c, ada_w, ada_b.reshape(depth, 1, n3))


def _modulated_norm(x, mod_ref, g_ref):
    ms = jnp.mean(x * x, axis=-1, keepdims=True)
    y = x * lax.rsqrt(ms + NORM_EPS) * g_ref[...]
    return y * (1.0 + mod_ref[1:2, :]) + mod_ref[0:1, :]


def _rope_tile(v, cos, sin):
    return v * cos + pltpu.roll(v, 2 * HALF, axis=1) * sin


def _store_key_chunks_t(kt_ref, first_chunk, k):
    dim = lax.broadcasted_iota(jnp.int32, (LANES, KEY_CHUNK), 0)
    chunks_t = [k[c * KEY_CHUNK:(c + 1) * KEY_CHUNK, :].T for c in range(k.shape[0] // KEY_CHUNK)]
    for c, kt in enumerate(chunks_t):
        for g in range(kt_ref.shape[0]):
            kt_ref[g, first_chunk + c] = jnp.where((dim % HEAD_DIM) // HALF == g, kt, 0.0).astype(BF16)


def _inproj0_kernel(x_ref, mod_ref, g_ref, cos_ref, sin_ref, wq_ref, wkv_ref, wz_ref, wg_ref,
                    qp_ref, qr_ref, kc_ref, vc_ref, ks_ref, vs_ref, kw_ref, vw_ref, z_ref, gl_ref, stage):
    tm = PART_ROWS

    def row_part(part):
        tok = slice(part * tm, (part + 1) * tm)
        h = _modulated_norm(x_ref[tok, :], mod_ref, g_ref).astype(BF16)
        yield
        cos = cos_ref[tok, :]
        sin = sin_ref[tok, :]
        q = _dot(h, wq_ref[...])
        yield
        for j in range(q.shape[1] // LANES):
            sl = slice(j * LANES, (j + 1) * LANES)
            qp_ref[tok, sl] = q[:, sl].astype(BF16)
            qr_ref[tok, sl] = _rope_tile(q[:, sl], cos, sin).astype(BF16)
        kv = _dot(h, wkv_ref[...])
        yield
        tiles = [kv[:, i * LANES:(i + 1) * LANES] for i in range(6)]
        short = slice(part * tm // CMP_STRIDE, (part + 1) * tm // CMP_STRIDE)
        for slab, (tile, o_ref) in enumerate(((tiles[0], kc_ref), (tiles[1], vc_ref))):
            stage[slab, tok, :] = tile
            for r in range(CMP_STRIDE):
                rows = stage[slab, pl.ds(part * tm + r, tm // CMP_STRIDE, stride=CMP_STRIDE), :]
                o_ref[short, r * LANES:(r + 1) * LANES] = rows.astype(BF16)
        first_chunk = part * tm // KEY_CHUNK
        _store_key_chunks_t(ks_ref, first_chunk, _rope_tile(tiles[2], cos, sin))
        vs_ref[tok, :] = tiles[3].astype(BF16)
        _store_key_chunks_t(kw_ref, first_chunk, _rope_tile(tiles[4], cos, sin))
        vw_ref[tok, :] = tiles[5].astype(BF16)
        z_ref[tok, :] = _silu(_dot(h, wz_ref[...])).astype(BF16)
        gl_ref[tok, :] = _dot(h, wg_ref[...])

    _run_in_lockstep([row_part(part) for part in range(x_ref.shape[0] // PART_ROWS)])


def _inproj0(x, mod, g, cos, sin, wq, wkv, wz, wg):
    b, t, d = x.shape
    nq, nz = wq.shape[1], wz.shape[1]
    tok = lambda n: pl.BlockSpec((None, TM_IN, n), lambda i, j: (i, j, 0))
    full = lambda a: pl.BlockSpec(a.shape, lambda i, j: (0,) * a.ndim)
    tab = pl.BlockSpec((TM_IN, LANES), lambda i, j: (j, 0))
    shp = lambda n, dt: jax.ShapeDtypeStruct((b, t, n), dt)
    kt = pl.BlockSpec((None, NSA_KV_HEADS, TM_IN // KEY_CHUNK, LANES, KEY_CHUNK), lambda i, j: (i, 0, j, 0, 0))
    kt_shp = jax.ShapeDtypeStruct((b, NSA_KV_HEADS, t // KEY_CHUNK, LANES, KEY_CHUNK), BF16)
    kv = tok(LANES)
    kv_shp = shp(LANES, BF16)
    cm = pl.BlockSpec((None, TM_IN // CMP_STRIDE, CMP_STRIDE * LANES), lambda i, j: (i, j, 0))
    cm_shp = jax.ShapeDtypeStruct((b, t // CMP_STRIDE, CMP_STRIDE * LANES), BF16)
    return pl.pallas_call(
        _inproj0_kernel,
        grid=(b, t // TM_IN),
        in_specs=[tok(d), pl.BlockSpec((None, 3, d), lambda i, j: (i, 0, 0)), full(g), tab, tab,
                  full(wq), full(wkv), full(wz), full(wg)],
        out_specs=[tok(nq), tok(nq), cm, cm, kt, kv, kt, kv, tok(nz), tok(LANES)],
        out_shape=[shp(nq, BF16), shp(nq, BF16), cm_shp, cm_shp, kt_shp, kv_shp, kt_shp, kv_shp,
                   shp(nz, BF16), shp(LANES, F32)],
        scratch_shapes=[pltpu.VMEM((2, TM_IN, LANES), F32)],
        compiler_params=_cparams("parallel", "parallel"),
        name="inproj_nsa",
    )(x, mod, g, cos, sin, wq, wkv, wz, wg)


def _compress_kernel(ak_ref, av_ref, pekt_ref, pekb_ref, pevt_ref, pevb_ref,
                     w1kt_ref, w1kb_ref, w1vt_ref, w1vb_ref, w2k_ref, w2v_ref, kcmp_ref, vcmp_ref):
    def mlp(a_ref, pet_ref, peb_ref, w1t_ref, w1b_ref, w2_ref):
        a = a_ref[...].astype(F32)
        top = _dot((a + pet_ref[...]).astype(BF16), w1t_ref[...])
        bot = _dot((a + peb_ref[...]).astype(BF16), w1b_ref[...])
        n = bot.shape[0]
        hid = top + pltpu.roll(bot, n - 1, axis=0)
        return _dot(_silu(hid).astype(BF16), w2_ref[...])

    kcmp_ref[...] = mlp(ak_ref, pekt_ref, pekb_ref, w1kt_ref, w1kb_ref, w2k_ref).astype(BF16)
    vcmp_ref[...] = mlp(av_ref, pevt_ref, pevb_ref, w1vt_ref, w1vb_ref, w2v_ref).astype(BF16)


def _compress(ak, av, consts):
    b, n, w = ak.shape
    full = lambda a: pl.BlockSpec(a.shape, lambda i: (0,) * a.ndim)
    blk = pl.BlockSpec((None, n, w), lambda i: (i, 0, 0))
    out = pl.BlockSpec((None, n, LANES), lambda i: (i, 0, 0))
    return pl.pallas_call(
        _compress_kernel,
        grid=(b,),
        in_specs=[blk, blk] + [full(a) for a in consts],
        out_specs=[out, out],
        out_shape=[jax.ShapeDtypeStruct((b, n, LANES), BF16)] * 2,
        compiler_params=_cparams("parallel"),
        name="compress",
    )(ak, av, *consts)


def _head_gate_rows(gates, g, branches, lane):
    blocks = []
    for j in range(NSA_GROUP):
        base = (g * NSA_GROUP + j) * NSA_BRANCHES
        kept = jnp.zeros_like(gates)
        for br in branches:
            kept = jnp.where(lane == base + br, gates, kept)
        hi = kept.astype(BF16)
        blocks.append(jnp.concatenate([hi, (kept - hi.astype(F32)).astype(BF16)], axis=1))
    return jnp.concatenate(blocks, axis=0)


def _gate_spread(branches):
    col = np.arange(2 * LANES) % LANES
    spread = np.zeros((2 * LANES, LANES * len(branches)), np.float32)
    for i, br in enumerate(branches):
        spread[col % NSA_BRANCHES == br, i * LANES:(i + 1) * LANES] = 1.0
    return jnp.asarray(spread, dtype=BF16)


def _cmp_attn_kernel(qp_ref, kcmp_ref, vcmp_ref, ovl_ref, gl_ref, ocmp_ref, msel_ref, *, n_slc, tq):
    n_cmp = vcmp_ref.shape[0]
    hg = qp_ref.shape[1] // LANES
    n_sub = qp_ref.shape[0] // tq
    assert n_cmp == LANES and tq == LANES
    row = lax.broadcasted_iota(jnp.int32, (LANES, tq), 0)
    col = lax.broadcasted_iota(jnp.int32, (LANES, tq), 1)
    jb = lax.broadcasted_iota(jnp.int32, (n_slc, tq), 0)
    tok_rows = lambda sub: slice(sub * tq, (sub + 1) * tq)
    t0 = lambda sub: pl.program_id(1) * qp_ref.shape[0] + sub * tq
    units = [(sub, g) for sub in range(n_sub) for g in range(NSA_KV_HEADS)]
    eye = jnp.where(row == col, 1.0, 0.0).astype(BF16)
    k_sel = jnp.concatenate([kcmp_ref[...], eye], axis=1)
    v_t = vcmp_ref[...].T

    q_t = [[qp_ref[tok_rows(sub), j * LANES:(j + 1) * LANES].T for j in range(hg)] for sub in range(n_sub)]
    gates_t = [jax.nn.sigmoid(gl_ref[tok_rows(sub), :]).T for sub in range(n_sub)]
    seen = [jnp.concatenate([jnp.where(CMP_STRIDE * row + (CMP_LEN - 1) <= t0(sub) + col, 0.0, NEG).astype(BF16)] * hg,
                            axis=1) for sub in range(n_sub)]
    weights = {}
    for sub, g in units:
        in_group = ((row % HEAD_DIM) // HALF) == g
        q_cols = jnp.concatenate([jnp.where(in_group, v, jnp.zeros_like(v)) for v in q_t[sub]], axis=1)
        s = _dot(k_sel, jnp.concatenate([q_cols, seen[sub]], axis=0))
        m = jnp.max(s, axis=0, keepdims=True)
        m = jnp.where(m > 0.5 * NEG, m, 0.0)
        e = jnp.exp(s - m)
        den = jnp.sum(e, axis=0, keepdims=True)
        weights[sub, g] = e * (1.0 / jnp.where(den > 0, den, 1.0))
    outs, imps = {}, {}
    for sub, g in units:
        p = weights[sub, g]
        gate = jnp.concatenate([gates_t[sub][(g * NSA_GROUP + j) * NSA_BRANCHES:(g * NSA_GROUP + j) * NSA_BRANCHES + 1, :]
                                for j in range(hg)], axis=1)
        outs[sub, g] = gate * _dot(v_t, p.astype(BF16))
        p_sum = p[:, 0:tq]
        for j in range(1, hg):
            p_sum = p_sum + p[:, j * tq:(j + 1) * tq]
        imps[sub, g] = jnp.dot(ovl_ref[...], p_sum, precision=HIGHEST, preferred_element_type=F32)
    picked = {}
    for sub, g in units:
        cur = (t0(sub) + lax.broadcasted_iota(jnp.int32, (n_slc, tq), 1)) // SLC_LEN
        visible = jb <= cur
        forced = (jb == 0) | (jb == cur) | (jb == cur - 1)
        rank = jnp.where(forced, -NEG, jnp.where(visible, imps[sub, g], NEG))
        cnt = jnp.zeros((n_slc, tq), F32)
        for jp in range(n_slc):
            rj = rank[jp:jp + 1, :]
            tie = jnp.where(jb > jp, 1.0, 0.0)
            cnt = cnt + jnp.where(rj > rank, 1.0, jnp.where(rj == rank, tie, 0.0))
        picked[sub, g] = jnp.where(visible, jnp.where(cnt < SLC_TOP, 1.0, 0.0), 0.0)
    pad = jnp.zeros((LANES - NSA_KV_HEADS * n_slc, tq), F32)
    for sub in range(n_sub):
        for j in range(hg):
            cols = slice(j * tq, (j + 1) * tq)
            o_t = jnp.where(row < HEAD_DIM, outs[sub, 0][:, cols], outs[sub, 1][:, cols])
            ocmp_ref[tok_rows(sub), j * LANES:(j + 1) * LANES] = o_t.T.astype(BF16)
        sel_t = jnp.concatenate([picked[sub, g] for g in range(NSA_KV_HEADS)] + [pad], axis=0)
        msel_ref[tok_rows(sub), :] = sel_t.T.astype(BF16)


def _cmp_attn(qp, kcmp, vcmp, ovl_t, gl):
    b, t, nq = qp.shape
    n_cmp = kcmp.shape[1]
    n_slc = t // SLC_LEN
    tok = lambda n: pl.BlockSpec((None, CMP_STEP_TOKENS, n), lambda i, j: (i, j, 0))
    per_b = pl.BlockSpec((None, n_cmp, LANES), lambda i, j: (i, 0, 0))
    full = lambda a: pl.BlockSpec(a.shape, lambda i, j: (0,) * a.ndim)
    return pl.pallas_call(
        functools.partial(_cmp_attn_kernel, n_slc=n_slc, tq=TQ),
        grid=(b, t // CMP_STEP_TOKENS),
        in_specs=[tok(nq), per_b, per_b, full(ovl_t), tok(LANES)],
        out_specs=[tok(nq), tok(LANES)],
        out_shape=[jax.ShapeDtypeStruct((b, t, nq), BF16), jax.ShapeDtypeStruct((b, t, LANES), BF16)],
        compiler_params=_cparams("parallel", "parallel"),
        name="cmp_attn_select",
    )(qp, kcmp, vcmp, ovl_t, gl)


def _values_with_ones(v, g):
    in_own_half = (lax.broadcasted_iota(jnp.int32, v.shape, 1) // HEAD_DIM) == g
    return jnp.where(in_own_half, v, jnp.ones_like(v))


def _lane_tiles(v):
    return [v[:, i * LANES:(i + 1) * LANES] for i in range(v.shape[1] // LANES)]


def _tile_max(tiles):
    mx = tiles[0]
    for v in tiles[1:]:
        mx = jnp.maximum(mx, v)
    return jnp.max(mx, axis=-1, keepdims=True)


def _probs(tiles, m):
    return jnp.concatenate([jnp.exp(v - m) for v in tiles], axis=1).astype(BF16)


def _attend_chunks(q_sel, ks_ref, e_ref, vs_ref, m_ref, acc_ref, c0, n):
    k0 = pl.multiple_of(c0 * KEY_CHUNK, KEY_CHUNK)
    scores = []
    for g in range(NSA_KV_HEADS):
        kt = jnp.concatenate([ks_ref[g, c0 + i] for i in range(n)], axis=1)
        eb = jnp.concatenate([e_ref[g, c0 + i] for i in range(n)], axis=1)
        scores.append(_dot(q_sel, jnp.concatenate([kt, eb], axis=0)))
    groups = range(NSA_KV_HEADS)
    m_old = [m_ref[g] for g in groups]
    m_new = [jnp.maximum(m_old[g], _tile_max(_lane_tiles(scores[g]))) for g in groups]
    p = [_probs(_lane_tiles(scores[g]), m_new[g]) for g in groups]
    pv = [_dot(p[g], _values_with_ones(vs_ref[pl.ds(k0, n * KEY_CHUNK), :], g)) for g in groups]
    for g in groups:
        acc_ref[g] = jnp.exp(m_old[g] - m_new[g]) * acc_ref[g] + pv[g]
        m_ref[g] = m_new[g]


def _nsa_attn_kernel(*refs, tq, n_slc, ck):
    *io_refs, m_scr, acc_scr = refs
    qr_ref, ks_ref, vs_ref, _, _, _, e_ref = io_refs[:7]
    n_sub = qr_ref.shape[0] // tq
    per_ck = ck // KEY_CHUNK
    rows = (qr_ref.shape[1] // LANES) * tq
    assert per_ck % n_sub == 0
    first = pl.program_id(1) * n_sub
    tiles = [_nsa_attn_tile(*io_refs, m_scr.at[:, sub * rows:(sub + 1) * rows, :], acc_scr.at[:, sub * rows:(sub + 1) * rows, :],
                            tok=slice(sub * tq, (sub + 1) * tq), qi=first + sub, n_slc=n_slc, ck=ck) for sub in range(n_sub)]
    q_sel = jnp.concatenate([next(tile) for tile in tiles], axis=0)

    def whole_steps(kc, carry):
        _attend_chunks(q_sel, ks_ref, e_ref, vs_ref, m_scr, acc_scr, kc * per_ck, per_ck)
        return carry

    lax.fori_loop(0, first // per_ck, whole_steps, 0)
    _run_in_lockstep(tiles)


def _nsa_attn_tile(qr_ref, ks_ref, vs_ref, kw_ref, vw_ref, msel_ref, e_ref, gsp_ref, ocmp_ref, z_ref, gl_ref,
                   og_ref, m_scr, acc_scr, *, tok, qi, n_slc, ck):
    tq = tok.stop - tok.start
    hg = qr_ref.shape[1] // LANES
    rows = hg * tq
    n_wc = WIN_LEN // tq + 1
    per_ck = ck // KEY_CHUNK
    t0 = qi * tq
    lane = lax.broadcasted_iota(jnp.int32, (tq, LANES), 1)
    a_idx = lax.broadcasted_iota(jnp.int32, (tq, LANES), 0)

    eye = jnp.where(lane == a_idx, 1.0, 0.0).astype(BF16)
    earlier = jnp.where((lane % n_slc) < t0 // SLC_LEN, 0.0, NEG)
    mneg = jnp.where(msel_ref[tok, :].astype(F32) > 0.5, earlier, NEG).astype(BF16)
    mneg_rows = jnp.concatenate([mneg] * hg, axis=0)
    eye_rows = jnp.concatenate([eye] * hg, axis=0)

    def with_selector(sel_rows):
        q_rows = jnp.concatenate([qr_ref[tok, j * LANES:(j + 1) * LANES] for j in range(hg)], axis=0)
        return jnp.concatenate([q_rows, sel_rows], axis=1)

    zero_b = jnp.zeros((tq, LANES), BF16)
    neg_b = jnp.full((tq, LANES), NEG, BF16)
    causal_b = jnp.where(lane <= a_idx, 0.0, NEG).astype(BF16)
    far_b = jnp.where(lane > a_idx, 0.0, NEG).astype(BF16)
    own_half = [lane < HEAD_DIM, lane >= HEAD_DIM]
    values_with_ones, lane_tiles, tile_max, probs = _values_with_ones, _lane_tiles, _tile_max, _probs

    m_scr[...] = jnp.full(m_scr.shape, NEG, F32)
    acc_scr[...] = jnp.zeros(acc_scr.shape, F32)
    yield with_selector(mneg_rows)

    for rem in range(1, per_ck):
        @pl.when(qi % per_ck == rem)
        def _():
            _attend_chunks(with_selector(mneg_rows), ks_ref, e_ref, vs_ref, m_scr, acc_scr, (qi // per_ck) * per_ck, rem)
    yield

    kd = pl.multiple_of(t0, tq)
    zeros_v = jnp.zeros((tq, LANES), BF16)
    win_chunks = [qi - (n_wc - 1 - c) for c in range(n_wc - 1)]
    win_bias = [jnp.where(cidx >= 0, far_b if c == 0 else zero_b, neg_b) for c, cidx in enumerate(win_chunks)]
    pairs = [(c, c + 1) for c in range(0, n_wc - 1, 2)]
    q_eye = with_selector(eye_rows)
    s_own, scores = [], []
    for g in range(NSA_KV_HEADS):
        own_k = jnp.concatenate([jnp.concatenate([kw_ref[g, qi], causal_b], axis=0),
                                 jnp.concatenate([ks_ref[g, qi], causal_b], axis=0)], axis=1)
        kbs = [jnp.concatenate([kw_ref[g, jnp.maximum(cidx, 0)], win_bias[c]], axis=0)
               for c, cidx in enumerate(win_chunks)]
        s_own.append(_dot(q_eye, own_k))
        scores.append([_dot(q_eye, jnp.concatenate([kbs[c] for c in pr], axis=1)) for pr in pairs])

    yield
    groups = range(NSA_KV_HEADS)
    s_win_own = [s_own[g][:, 0:tq] for g in groups]
    s_slc_own = [s_own[g][:, tq:2 * tq] for g in groups]
    m_win = [tile_max([v for s in scores[g] for v in lane_tiles(s)] + [s_win_own[g]]) for g in groups]
    m_old = [m_scr[g] for g in groups]
    m_slc = [jnp.maximum(m_old[g], jnp.max(s_slc_own[g], axis=-1, keepdims=True)) for g in groups]
    p_all = [jnp.concatenate([probs(lane_tiles(s), m_win[g]) for s in scores[g]]
                             + [probs([s_win_own[g]], m_win[g]), probs([s_slc_own[g]], m_slc[g])], axis=1) for g in groups]
    yield
    acc_slc, acc_win = [], []
    for g in groups:
        vbs = [values_with_ones(vw_ref[pl.ds(pl.multiple_of(jnp.maximum(cidx, 0) * tq, tq), tq), :], g)
               for cidx in win_chunks] + [values_with_ones(vw_ref[pl.ds(kd, tq), :], g)]
        win_v = [jnp.concatenate([v, zeros_v], axis=1) for v in vbs]
        slc_v = jnp.concatenate([zeros_v, values_with_ones(vs_ref[pl.ds(kd, tq), :], g)], axis=1)
        pv = _dot(p_all[g], jnp.concatenate(win_v + [slc_v], axis=0))
        acc_win.append(pv[:, 0:LANES])
        acc_slc.append(jnp.exp(m_old[g] - m_slc[g]) * acc_scr[g] + pv[:, LANES:2 * LANES])
    yield

    gates = jax.nn.sigmoid(gl_ref[tok, :])
    row_gates = [_dot(_head_gate_rows(gates, g, (1, 2), lane), gsp_ref[...]) for g in range(NSA_KV_HEADS)]

    def gated(accs, rs, branch_lanes):
        num = [accs[g][rs] * row_gates[g][rs, branch_lanes] for g in range(NSA_KV_HEADS)]
        sums = pltpu.roll(jnp.where(own_half[0], accs[1][rs], accs[0][rs]), HEAD_DIM, axis=1)
        return jnp.where(own_half[0], num[0], num[1]) / sums

    for j in range(hg):
        rs = slice(j * tq, (j + 1) * tq)
        cols = slice(j * LANES, (j + 1) * LANES)
        o = (ocmp_ref[tok, cols].astype(F32) + gated(acc_slc, rs, slice(0, LANES))
             + gated(acc_win, rs, slice(LANES, 2 * LANES)))
        og_ref[tok, cols] = (o * z_ref[tok, cols].astype(F32)).astype(BF16)


def _nsa_attn(qr, ks, vs, kw, vw, msel, e, ocmp, z, gl):
    b, t, nq = qr.shape
    gsp = _gate_spread((1, 2))
    hg = nq // LANES
    rows = hg * TQ
    assert TQ == KEY_CHUNK and WIN_LEN % TQ == 0
    tok = lambda n: pl.BlockSpec((None, NSA_STEP_TOKENS, n), lambda i, j: (i, j, 0))
    per_b = pl.BlockSpec((None, t, LANES), lambda i, j: (i, 0, 0))
    per_b_t = pl.BlockSpec((None, NSA_KV_HEADS, t // KEY_CHUNK, LANES, KEY_CHUNK), lambda i, j: (i, 0, 0, 0, 0))
    full = lambda a: pl.BlockSpec(a.shape, lambda i, j: (0,) * a.ndim)
    return pl.pallas_call(
        functools.partial(_nsa_attn_kernel, tq=TQ, n_slc=t // SLC_LEN, ck=SLC_CK),
        grid=(b, t // NSA_STEP_TOKENS),
        in_specs=[tok(nq), per_b_t, per_b, per_b_t, per_b, tok(LANES), full(e), full(gsp), tok(nq), tok(nq), tok(LANES)],
        out_specs=tok(nq),
        out_shape=jax.ShapeDtypeStruct((b, t, nq), BF16),
        scratch_shapes=[
            pltpu.VMEM((NSA_KV_HEADS, NSA_STEP_TOKENS // TQ * rows, LANES), F32),
            pltpu.VMEM((NSA_KV_HEADS, NSA_STEP_TOKENS // TQ * rows, LANES), F32),
        ],
        compiler_params=_cparams("parallel", "arbitrary"),
        name="nsa_slc_win_attn",
    )(qr, ks, vs, kw, vw, msel, e, gsp, ocmp, z, gl)


def _inproj1_kernel(og_ref, wo_ref, x_ref, mod0_ref, mod_ref, g_ref, cos_ref, sin_ref, wq_ref, wk_ref, wv_ref, wz_ref,
                    x1_ref, *refs):
    n_grp = len(DIL_PATTERNS)
    outs = [refs[a * n_grp:(a + 1) * n_grp] for a in range(3)]
    z_ref, stage = refs[3 * n_grp], refs[3 * n_grp + 1]
    tm = PART_ROWS
    gw = DIL_TILES_PER_GROUP * LANES

    def row_part(part):
        tok = slice(part * tm, (part + 1) * tm)
        x1 = x_ref[tok, :] + mod0_ref[2:3, :] * _dot(og_ref[tok, :], wo_ref[...])
        x1_ref[tok, :] = x1
        yield
        h = _modulated_norm(x1, mod_ref, g_ref).astype(BF16)
        yield
        projected = [_dot(h, w_ref[...]) for w_ref in (wq_ref, wk_ref, wv_ref, wz_ref)]
        yield
        cos = cos_ref[tok, :]
        sin = sin_ref[tok, :]
        for a, u in enumerate(projected[0:3]):
            tiles = [u[:, pt * LANES:(pt + 1) * LANES] for pt in range(len(DIL_HEAD_PAIRS))]
            if a < 2:
                tiles = [_rope_tile(v, cos, sin) for v in tiles]
            for gi, (_, dil) in enumerate(DIL_PATTERNS):
                for jt, (pt, _) in enumerate(DIL_GROUP_TILES[gi]):
                    tile = tiles[pt]
                    if dil == 1:
                        outs[a][gi][tok, jt * LANES:(jt + 1) * LANES] = tile.astype(BF16)
                        continue
                    slab = (a * n_grp + gi) * DIL_TILES_PER_GROUP + jt
                    stage[slab, tok, :] = tile
                    short = slice(part * tm // dil, (part + 1) * tm // dil)
                    for r in range(dil):
                        rows = stage[slab, pl.ds(part * tm + r, tm // dil, stride=dil), :]
                        outs[a][gi][short, r * gw + jt * LANES:r * gw + (jt + 1) * LANES] = rows.astype(BF16)
        z_ref[tok, :] = _silu(projected[3]).astype(BF16)

    _run_in_lockstep([row_part(part) for part in range(x_ref.shape[0] // PART_ROWS)])


def _outproj0_inproj1(og, wo, x, mod0, mod, g, cos, sin, wq, wk, wv, wz):
    b, t, d = x.shape
    n = wq.shape[1]
    gw = DIL_TILES_PER_GROUP * LANES
    tok = lambda k: pl.BlockSpec((None, TM, k), lambda i, j: (i, j, 0))
    full = lambda a: pl.BlockSpec(a.shape, lambda i, j: (0,) * a.ndim)
    tab = pl.BlockSpec((TM, LANES), lambda i, j: (j, 0))
    mods = pl.BlockSpec((None, 3, d), lambda i, j: (i, 0, 0))
    grp_specs = [pl.BlockSpec((None, TM // dil, dil * gw), lambda i, j: (i, j, 0)) for _, dil in DIL_PATTERNS]
    grp_shapes = [jax.ShapeDtypeStruct((b, t // dil, dil * gw), BF16) for _, dil in DIL_PATTERNS]
    n_slabs = 3 * len(DIL_PATTERNS) * DIL_TILES_PER_GROUP
    return pl.pallas_call(
        _inproj1_kernel,
        grid=(b, t // TM),
        in_specs=[tok(og.shape[2]), full(wo), tok(d), mods, mods, full(g), tab, tab,
                  full(wq), full(wk), full(wv), full(wz)],
        out_specs=[tok(d)] + grp_specs * 3 + [tok(n)],
        out_shape=[jax.ShapeDtypeStruct((b, t, d), F32)] + grp_shapes * 3 + [jax.ShapeDtypeStruct((b, t, n), BF16)],
        scratch_shapes=[pltpu.VMEM((n_slabs, TM, LANES), F32)],
        compiler_params=_cparams("parallel", "parallel"),
        name="outproj_nsa_inproj_dil",
    )(og, wo, x, mod0, mod, g, cos, sin, wq, wk, wv, wz)


def _dil_attn_kernel(*refs, plans):
    n_grp = len(plans)
    step = pl.program_id(1)
    groups = []
    for gi, (members, tq, n_seq, blocks_per_seq) in enumerate(plans):
        q_ref, k_ref, v_ref = refs[3 * gi:3 * gi + 3]
        o_ref, lse_ref = refs[3 * n_grp + 2 * gi:3 * n_grp + 2 * gi + 2]
        groups.append(_dil_attn_group(q_ref, k_ref, v_ref, o_ref, lse_ref, step % blocks_per_seq, members, tq, n_seq))
    _run_in_lockstep(groups)


def _dil_attn_group(q_ref, k_ref, v_ref, o_ref, lse_ref, i, members, tq, n_seq):
    n_heads = sum(len(m) for m in members)
    tq_blk = q_ref.shape[0]
    gw = q_ref.shape[1] // n_seq
    n_tiles = gw // LANES
    n_sub = tq_blk // tq
    assert tq == LANES
    row = lax.broadcasted_iota(jnp.int32, (LANES, tq), 0)
    col = lax.broadcasted_iota(jnp.int32, (LANES, tq), 1)
    head_a = ((row % HEAD_DIM) // HALF) == 0
    diag_b = jnp.where(row <= col, 0.0, NEG)
    prev_b = jnp.where(row >= col, 0.0, NEG)
    neg_b = jnp.full((LANES, tq), NEG, F32)
    two = lambda v: jnp.concatenate([v, v], axis=1)
    units = [(sq, sub, jt) for sq in range(n_seq) for sub in range(n_sub) for jt in range(n_tiles)]
    rows_of = lambda sub: slice(sub * tq, (sub + 1) * tq)
    cols_of = lambda sq, jt: slice(sq * gw + jt * LANES, sq * gw + (jt + 1) * LANES)
    key_rows = lambda c: pl.ds(pl.multiple_of(jnp.maximum(i * n_sub + c, 0) * tq, tq), tq)
    q_cols, v_t = {}, {}
    for sq, sub, jt in units:
        q_t = q_ref[rows_of(sub), cols_of(sq, jt)].T
        zero = jnp.zeros_like(q_t)
        q_cols[sq, sub, jt] = jnp.concatenate([jnp.where(head_a, q_t, zero), jnp.where(head_a, zero, q_t)], axis=1)
        v_t[sq, sub, jt] = v_ref[key_rows(sub), cols_of(sq, jt)].T
        if sub == 0:
            v_t[sq, -1, jt] = v_ref[key_rows(-1), cols_of(sq, jt)].T
    yield
    probs_t, stats = {}, {}
    for sq, sub, jt in units:
        prev_bias = prev_b if sub > 0 else jnp.where(i > 0, prev_b, neg_b)
        bias = jnp.concatenate([two(prev_bias), two(diag_b)], axis=0)
        keys = jnp.concatenate([k_ref[key_rows(sub - 1), cols_of(sq, jt)], k_ref[key_rows(sub), cols_of(sq, jt)]], axis=0)
        s = _dot(keys, q_cols[sq, sub, jt]) + bias
        m = jnp.max(s, axis=0, keepdims=True)
        p = jnp.exp2(s - m)
        l = jnp.sum(p, axis=0, keepdims=True)
        probs_t[sq, sub, jt] = p.astype(BF16)
        stats[sq, sub, jt] = (m, l)
    yield
    for sq, sub, jt in units:
        m, l = stats[sq, sub, jt]
        o_t = _dot(jnp.concatenate([v_t[sq, sub - 1, jt], v_t[sq, sub, jt]], axis=1), probs_t[sq, sub, jt])
        inv = 1.0 / l
        o_t = jnp.where(row < HEAD_DIM, o_t[:, 0:tq] * inv[:, 0:tq], o_t[:, tq:2 * tq] * inv[:, tq:2 * tq])
        o_ref[rows_of(sub), cols_of(sq, jt)] = o_t.T.astype(BF16)
    yield
    for sq in range(n_seq):
        for sub in range(n_sub):
            lses = []
            for jt in range(n_tiles):
                m, l = stats[sq, sub, jt]
                lse = m * LN2 + jnp.log(l)
                lses += [lse[:, mb * tq:(mb + 1) * tq] for mb in members[jt]]
            top = lses[0]
            for v in lses[1:]:
                top = jnp.maximum(top, v)
            tot = jnp.zeros_like(top)
            for v in lses:
                tot = tot + jnp.exp(v - top)
            group_lse = top + jnp.log(tot) - math.log(n_heads)
            lse_ref[rows_of(sub), sq * LANES:(sq + 1) * LANES] = jnp.broadcast_to(group_lse, (LANES, tq)).T


def _dil_attn(qs, ks, vs, t):
    b = qs[0].shape[0]
    steps = t // DIL_STEP_TOKENS
    plans, in_specs, out_specs, out_shapes, operands = [], [], [], [], []
    for gi, (_, dil) in enumerate(DIL_PATTERNS):
        length = t // dil
        gw = qs[gi].shape[2] // dil
        tq_blk = min(DIL_STEP_TOKENS, length)
        blocks_per_seq = length // tq_blk
        n_seq = DIL_STEP_TOKENS // tq_blk
        assert dil * blocks_per_seq == steps * n_seq
        tok = lambda w, bps=blocks_per_seq, ns=n_seq, tb=tq_blk: pl.BlockSpec(
            (None, tb, ns * w), lambda bi, s: (bi, s % bps, s // bps))
        seq = pl.BlockSpec((None, length, n_seq * gw), lambda bi, s, bps=blocks_per_seq: (bi, 0, s // bps))
        plans.append((tuple(m for _, m in DIL_GROUP_TILES[gi]), min(TQ, length), n_seq, blocks_per_seq))
        in_specs += [tok(gw), seq, seq]
        operands += [qs[gi], ks[gi], vs[gi]]
        out_specs += [tok(gw), tok(LANES)]
        out_shapes += [jax.ShapeDtypeStruct(qs[gi].shape, BF16), jax.ShapeDtypeStruct((b, length, dil * LANES), F32)]
    outs = pl.pallas_call(
        functools.partial(_dil_attn_kernel, plans=tuple(plans)),
        grid=(b, steps),
        in_specs=in_specs,
        out_specs=out_specs,
        out_shape=out_shapes,
        compiler_params=_cparams("parallel", "parallel"),
        name="dil_attn",
    )(*operands)
    return outs[0::2], outs[1::2]


def _outproj1_kernel(*refs):
    n_grp = len(DIL_PATTERNS)
    o_refs, l_refs = refs[0:n_grp], refs[n_grp:2 * n_grp]
    z_ref, w_ref, x_ref, mod_ref, fg_ref, out_ref, stage = refs[2 * n_grp:]
    tm = x_ref.shape[0]
    gw = DIL_TILES_PER_GROUP * LANES

    def token_order(ref, gi, width, col0, slab):
        dil = DIL_PATTERNS[gi][1]
        if dil == 1:
            return ref[:, col0:col0 + LANES].astype(F32)
        for r in range(dil):
            stage[slab, pl.ds(r, tm // dil, stride=dil), :] = ref[:, r * width + col0:r * width + col0 + LANES].astype(F32)
        return stage[slab]

    ls = [token_order(l_refs[gi], gi, LANES, 0, gi) for gi in range(n_grp)]
    top = jnp.maximum(jnp.maximum(ls[0], ls[1]), ls[2])
    es = [jnp.exp(v - top) for v in ls]
    den = es[0] + es[1] + es[2]
    alphas = [e / den * float(n_grp) for e in es]
    lane = lax.broadcasted_iota(jnp.int32, (tm, LANES), 1)
    weighted = [None] * len(DIL_HEAD_PAIRS)
    for gi in range(n_grp):
        for jt, (pt, members) in enumerate(DIL_GROUP_TILES[gi]):
            ot = token_order(o_refs[gi], gi, gw, jt * LANES, n_grp + gi * DIL_TILES_PER_GROUP + jt) * alphas[gi]
            if len(members) == 2:
                weighted[pt] = ot
            else:
                mine = (lane // HEAD_DIM) == members[0]
                weighted[pt] = jnp.where(mine, ot, 0.0 if weighted[pt] is None else weighted[pt])
    parts = [(weighted[pt] * z_ref[:, pt * LANES:(pt + 1) * LANES].astype(F32)).astype(BF16)
             for pt in range(len(DIL_HEAD_PAIRS))]
    og = jnp.concatenate(parts, axis=1)
    y = _dot(og, w_ref[...])
    x2 = x_ref[...] + mod_ref[2:3, :] * y
    ms = jnp.mean(x2 * x2, axis=-1, keepdims=True)
    out_ref[...] = x2 * lax.rsqrt(ms + NORM_EPS) * fg_ref[...]


def _outproj1(o_groups, lse_groups, z, w, x, mod, final_g):
    b, t, d = x.shape
    tm = TM_OUT
    tok = lambda k: pl.BlockSpec((None, tm, k), lambda i, j: (i, j, 0))
    full = lambda a: pl.BlockSpec(a.shape, lambda i, j: (0,) * a.ndim)
    grp = lambda a, dil: pl.BlockSpec((None, tm // dil, a.shape[2]), lambda i, j: (i, j, 0))
    dils = [dil for _, dil in DIL_PATTERNS]
    n_slabs = len(dils) * (1 + DIL_TILES_PER_GROUP)
    return pl.pallas_call(
        _outproj1_kernel,
        grid=(b, t // tm),
        in_specs=[grp(a, dl) for a, dl in zip(o_groups, dils)] + [grp(a, dl) for a, dl in zip(lse_groups, dils)]
                 + [tok(z.shape[2]), full(w), tok(d), pl.BlockSpec((None, 3, d), lambda i, j: (i, 0, 0)), full(final_g)],
        out_specs=tok(d),
        out_shape=jax.ShapeDtypeStruct((b, t, d), F32),
        scratch_shapes=[pltpu.VMEM((n_slabs, tm, LANES), F32)],
        compiler_params=_cparams("parallel", "parallel"),
        name="outproj_dil_final",
    )(*o_groups, *lse_groups, z, w, x, mod, final_g)


def _rope_tables(t):
    inv = ROPE_THETA ** (-jnp.arange(HALF, dtype=F32) / HALF)
    ang = jnp.arange(t, dtype=F32)[:, None] * inv[None, :]
    cos = jnp.tile(jnp.cos(ang), (1, 4))
    sin = jnp.tile(jnp.sin(ang), (1, 4))
    sign = jnp.where(jnp.arange(LANES) < 2 * HALF, -1.0, 1.0).astype(F32)
    return cos, sin * sign[None, :]


def _pair_rope_layout(w, n_pairs):
    d = w.shape[0]
    return w.reshape(d, n_pairs, 2, 2, HALF).transpose(0, 1, 3, 2, 4).reshape(d, n_pairs * LANES)


def _nsa_weights(w_in, pe_k, pe_v, ck_w1, ck_w2, cv_w1, cv_w2, w_out):
    d = w_in.shape[0]
    aw = N_HEADS * HEAD_DIM
    kvw = NSA_KV_HEADS * HEAD_DIM
    cuts = np.cumsum([aw] + [kvw] * 6 + [aw]).tolist()
    q, k_c, v_c, k_s, v_s, k_w, v_w, z, gl = jnp.split(w_in, cuts, axis=1)
    scale = HEAD_DIM ** -0.5
    wq = (q * scale).reshape(d, NSA_KV_HEADS, NSA_GROUP, 2, HALF).transpose(0, 2, 3, 1, 4).reshape(d, aw)
    k_lay = lambda w: _pair_rope_layout(w, 1)
    wkv = jnp.concatenate([k_lay(k_c), v_c, k_lay(k_s), v_s, k_lay(k_w), v_w], axis=1)
    wz = z.reshape(d, NSA_KV_HEADS, NSA_GROUP, HEAD_DIM).transpose(0, 2, 1, 3).reshape(d, aw)
    wo = w_out.reshape(NSA_KV_HEADS, NSA_GROUP, HEAD_DIM, -1).transpose(1, 0, 2, 3).reshape(aw, -1)
    wg = jnp.pad(gl, ((0, 0), (0, LANES - gl.shape[1])))

    def grouped(w1_half, rope_lanes):
        out = []
        for g in range(NSA_KV_HEADS):
            if rope_lanes:
                src = w1_half.reshape(CMP_STRIDE, 2, 1, HALF, CMP_HIDDEN)
                pads = ((0, 0), (0, 0), (g, NSA_KV_HEADS - 1 - g), (0, 0), (0, 0))
            else:
                src = w1_half.reshape(CMP_STRIDE, 1, HEAD_DIM, CMP_HIDDEN)
                pads = ((0, 0), (g, NSA_KV_HEADS - 1 - g), (0, 0), (0, 0))
            out.append(jnp.pad(src, pads).reshape(CMP_STRIDE * LANES, CMP_HIDDEN))
        return jnp.concatenate(out, axis=1)

    def pe_rows(pe_half, rope_lanes):
        if rope_lanes:
            v = jnp.broadcast_to(pe_half.reshape(CMP_STRIDE, 2, 1, HALF), (CMP_STRIDE, 2, NSA_KV_HEADS, HALF))
        else:
            v = jnp.broadcast_to(pe_half.reshape(CMP_STRIDE, 1, HEAD_DIM), (CMP_STRIDE, NSA_KV_HEADS, HEAD_DIM))
        return v.reshape(1, CMP_STRIDE * LANES)

    def w2_padded(w2, rope_lanes):
        out = []
        for g in range(NSA_KV_HEADS):
            if rope_lanes:
                src = w2.reshape(CMP_HIDDEN, 2, 1, HALF)
                pads = ((0, 0), (0, 0), (g, NSA_KV_HEADS - 1 - g), (0, 0))
            else:
                src = w2.reshape(CMP_HIDDEN, 1, HEAD_DIM)
                pads = ((0, 0), (g, NSA_KV_HEADS - 1 - g), (0, 0))
            out.append(jnp.pad(src, pads).reshape(CMP_HIDDEN, LANES))
        return jnp.concatenate(out, axis=0)

    k1 = ck_w1.reshape(2, CMP_STRIDE, HEAD_DIM, CMP_HIDDEN)
    v1 = cv_w1.reshape(2, CMP_STRIDE, HEAD_DIM, CMP_HIDDEN)
    compress_consts = (
        pe_rows(pe_k[:CMP_STRIDE], True), pe_rows(pe_k[CMP_STRIDE:], True),
        pe_rows(pe_v[:CMP_STRIDE], False), pe_rows(pe_v[CMP_STRIDE:], False),
        grouped(k1[0], True).astype(BF16), grouped(k1[1], True).astype(BF16),
        grouped(v1[0], False).astype(BF16), grouped(v1[1], False).astype(BF16),
        w2_padded(ck_w2, True).astype(BF16), w2_padded(cv_w2, False).astype(BF16),
    )
    return wq.astype(BF16), wkv.astype(BF16), wz.astype(BF16), wg.astype(BF16), wo.astype(BF16), compress_consts


def _dil_weights(w_in, w_out):
    d = w_in.shape[0]
    q, k, v, z = jnp.split(w_in, 4, axis=1)
    n_pairs = len(DIL_HEAD_PAIRS)
    order = [hd for pair in DIL_HEAD_PAIRS for hd in pair]

    def tile_order(w, axis):
        parts = []
        for hd in order:
            sl = [slice(None)] * w.ndim
            sl[axis] = slice(hd * HEAD_DIM, (hd + 1) * HEAD_DIM)
            parts.append(w[tuple(sl)])
        return jnp.concatenate(parts, axis=axis)

    scale = HEAD_DIM ** -0.5 * LOG2E
    wq = _pair_rope_layout(tile_order(q * scale, 1), n_pairs)
    wk = _pair_rope_layout(tile_order(k, 1), n_pairs)
    return (wq.astype(BF16), wk.astype(BF16), tile_order(v, 1).astype(BF16), tile_order(z, 1).astype(BF16),
            tile_order(w_out, 0).astype(BF16))


def _selection_constants(t):
    n_cmp_rows = t // CMP_STRIDE
    n_slc = t // SLC_LEN
    c_start = CMP_STRIDE * np.arange(n_cmp_rows)
    s_start = SLC_LEN * np.arange(n_slc)
    ovl_t = ((c_start[None, :] < s_start[:, None] + SLC_LEN)
             & (c_start[None, :] + CMP_LEN > s_start[:, None])).astype(np.float32)
    ovl_t[:, (t - CMP_LEN) // CMP_STRIDE + 1:] = 0.0
    e = np.zeros((NSA_KV_HEADS, t // KEY_CHUNK, LANES, KEY_CHUNK), np.float32)
    keys = np.arange(t)
    for g in range(NSA_KV_HEADS):
        e[g, keys // KEY_CHUNK, g * n_slc + keys // SLC_LEN, keys % KEY_CHUNK] = 1.0
    return jnp.asarray(ovl_t), jnp.asarray(e, dtype=BF16)


@jax.jit
def kernel(x, c, norm_g, ada_w, ada_b, nsa_w_in, nsa_pe_k, nsa_pe_v, nsa_ck_w1, nsa_ck_w2,
           nsa_cv_w1, nsa_cv_w2, nsa_w_out, dil_w_in, dil_w_out, final_g):
    b, t, d = x.shape
    assert t % SLC_CK == 0 and t % TM == 0 and d % LANES == 0
    assert NSA_KV_HEADS * (t // SLC_LEN) <= LANES and t // CMP_STRIDE == LANES
    mod = _adaln_mod(c, ada_w, ada_b).reshape(ada_w.shape[0], b, 3, d)
    cos, sin = _rope_tables(t)

    wq, wkv, wz, wg, wo, compress_consts = _nsa_weights(
        nsa_w_in[0], nsa_pe_k[0], nsa_pe_v[0], nsa_ck_w1[0], nsa_ck_w2[0], nsa_cv_w1[0], nsa_cv_w2[0], nsa_w_out[0])
    qp, qr, kc, vc, ks, vs, kw, vw, z, gl = _inproj0(x, mod[0], norm_g[0:1], cos, sin, wq, wkv, wz, wg)
    kcmp, vcmp = _compress(kc, vc, compress_consts)
    ovl_t, e_sel = _selection_constants(t)
    ocmp, msel = _cmp_attn(qp, kcmp, vcmp, ovl_t, gl)
    og = _nsa_attn(qr, ks, vs, kw, vw, msel, e_sel, ocmp, z, gl)

    dq, dk, dv, dz, dwo = _dil_weights(dil_w_in[0], dil_w_out[0])
    n_grp = len(DIL_PATTERNS)
    x1, *qkv, z1 = _outproj0_inproj1(og, wo, x, mod[0], mod[1], norm_g[1:2], cos, sin, dq, dk, dv, dz)
    assert all(win // dil == DIL_WIN and TM % (dil * 16) == 0 for win, dil in DIL_PATTERNS)
    o_groups, lse_groups = _dil_attn(qkv[0:n_grp], qkv[n_grp:2 * n_grp], qkv[2 * n_grp:3 * n_grp], t)
    return _outproj1(o_groups, lse_groups, z1, dwo, x1, mod[1], final_g.reshape(1, d))
```

```python
import functools
import math

import numpy as np
import jax
import jax.numpy as jnp
from jax import lax
from jax.experimental import pallas as pl
from jax.experimental.pallas import tpu as pltpu

F32 = jnp.float32
BF16 = jnp.bfloat16
HIGHEST = lax.Precision.HIGHEST

HEAD_DIM = 64
HALF = HEAD_DIM // 2
N_HEADS = 16
ROPE_THETA = 10000.0
NORM_EPS = 1e-6
NSA_KV_HEADS = 2
NSA_GROUP = N_HEADS // NSA_KV_HEADS
NSA_BRANCHES = 3
CMP_LEN = 32
CMP_STRIDE = 16
CMP_HIDDEN = 256
SLC_LEN = 64
SLC_TOP = 16
WIN_LEN = 512
DIL_PATTERNS = ((128, 1), (512, 4), (2048, 16))
DIL_GROUP_HEADS = (6, 5, 5)
DIL_WIN = 128


def _dil_tile_plan():
    offs = np.cumsum((0,) + DIL_GROUP_HEADS)
    pairs, group_tiles, leftovers = [], [], []
    for gi, hn in enumerate(DIL_GROUP_HEADS):
        heads = list(range(offs[gi], offs[gi + 1]))
        group_tiles.append([])
        for i in range(0, hn - hn % 2, 2):
            group_tiles[gi].append((len(pairs), (0, 1)))
            pairs.append((heads[i], heads[i + 1]))
        if hn % 2:
            leftovers.append((gi, heads[-1]))
    assert len(leftovers) % 2 == 0
    for (ga, ha), (gb, hb) in zip(leftovers[0::2], leftovers[1::2]):
        group_tiles[ga].append((len(pairs), (0,)))
        group_tiles[gb].append((len(pairs), (1,)))
        pairs.append((ha, hb))
    return tuple(pairs), tuple(tuple(tiles) for tiles in group_tiles)


DIL_HEAD_PAIRS, DIL_GROUP_TILES = _dil_tile_plan()
DIL_TILES_PER_GROUP = len(DIL_GROUP_TILES[0])
assert all(len(tiles) == DIL_TILES_PER_GROUP for tiles in DIL_GROUP_TILES)

LANES = 128
VMEM_LIMIT_BYTES = 48 * 1024 * 1024

NEG = -1e30
LOG2E = math.log2(math.e)
LN2 = math.log(2.0)
TQ = 128
KEY_CHUNK = 128
SLC_CK = 512
MOD_COLS = 1024
TM = 512
TM_IN = 1024
PART_ROWS = 256
TM_OUT = 1024
DIL_STEP_TOKENS = 1024
NSA_STEP_TOKENS = 256
CMP_STEP_TOKENS = 2048


def _cparams(*sem):
    return pltpu.CompilerParams(dimension_semantics=sem, vmem_limit_bytes=VMEM_LIMIT_BYTES)


def _dot(a, b):
    return jnp.dot(a, b, preferred_element_type=F32)


def _silu(v):
    return v * jax.nn.sigmoid(v)


def _run_in_lockstep(stage_generators):
    pending = list(stage_generators)
    done = object()
    while pending:
        pending = [gen for gen in pending if next(gen, done) is not done]


def _mod_kernel(c_ref, w_ref, b_ref, o_ref):
    s = _silu(c_ref[...])
    o_ref[...] = jnp.dot(s, w_ref[...], precision=HIGHEST, preferred_element_type=F32) + b_ref[...]


def _adaln_mod(c, ada_w, ada_b):
    depth, d, n3 = ada_w.shape
    b = c.shape[0]
    tn = MOD_COLS
    return pl.pallas_call(
        _mod_kernel,
        grid=(depth, n3 // tn),
        in_specs=[
            pl.BlockSpec((b, d), lambda i, j: (0, 0)),
            pl.BlockSpec((None, d, tn), lambda i, j: (i, 0, j)),
            pl.BlockSpec((None, 1, tn), lambda i, j: (i, 0, j)),
        ],
        out_specs=pl.BlockSpec((None, b, tn), lambda i, j: (i, 0, j)),
        out_shape=jax.ShapeDtypeStruct((depth, b, n3), F32),
        compiler_params=_cparams("arbitrary", "arbitrary"),
        name="adaln_mod",
    )(c, ada_w, ada_b.reshape(depth, 1, n3))


def _modulated_norm(x, mod_ref, g_ref):
    ms = jnp.mean(x * x, axis=-1, keepdims=True)
    y = x * lax.rsqrt(ms + NORM_EPS) * g_ref[...]
    return y * (1.0 + mod_ref[1:2, :]) + mod_ref[0:1, :]


def _rope_tile(v, cos, sin):
    return v * cos + pltpu.roll(v, 2 * HALF, axis=1) * sin


def _store_key_chunks_t(kt_ref, first_chunk, k):
    dim = lax.broadcasted_iota(jnp.int32, (LANES, KEY_CHUNK), 0)
    chunks_t = [k[c * KEY_CHUNK:(c + 1) * KEY_CHUNK, :].T for c in range(k.shape[0] // KEY_CHUNK)]
    for c, kt in enumerate(chunks_t):
        for g in range(kt_ref.shape[0]):
            kt_ref[g, first_chunk + c] = jnp.where((dim % HEAD_DIM) // HALF == g, kt, 0.0).astype(BF16)


def _inproj0_kernel(x_ref, mod_ref, g_ref, cos_ref, sin_ref, wq_ref, wkv_ref, wz_ref, wg_ref,
                    qp_ref, qr_ref, kc_ref, vc_ref, ks_ref, vs_ref, kw_ref, vw_ref, z_ref, gl_ref, stage):
    tm = PART_ROWS

    def row_part(part):
        tok = slice(part * tm, (part + 1) * tm)
        h = _modulated_norm(x_ref[tok, :], mod_ref, g_ref).astype(BF16)
        yield
        cos = cos_ref[tok, :]
        sin = sin_ref[tok, :]
        q = _dot(h, wq_ref[...])
        yield
        for j in range(q.shape[1] // LANES):
            sl = slice(j * LANES, (j + 1) * LANES)
            qp_ref[tok, sl] = q[:, sl].astype(BF16)
            qr_ref[tok, sl] = _rope_tile(q[:, sl], cos, sin).astype(BF16)
        kv = _dot(h, wkv_ref[...])
        yield
        tiles = [kv[:, i * LANES:(i + 1) * LANES] for i in range(6)]
        short = slice(part * tm // CMP_STRIDE, (part + 1) * tm // CMP_STRIDE)
        for slab, (tile, o_ref) in enumerate(((tiles[0], kc_ref), (tiles[1], vc_ref))):
            stage[slab, tok, :] = tile
            for r in range(CMP_STRIDE):
                rows = stage[slab, pl.ds(part * tm + r, tm // CMP_STRIDE, stride=CMP_STRIDE), :]
                o_ref[short, r * LANES:(r + 1) * LANES] = rows.astype(BF16)
        first_chunk = part * tm // KEY_CHUNK
        _store_key_chunks_t(ks_ref, first_chunk, _rope_tile(tiles[2], cos, sin))
        vs_ref[tok, :] = tiles[3].astype(BF16)
        _store_key_chunks_t(kw_ref, first_chunk, _rope_tile(tiles[4], cos, sin))
        vw_ref[tok, :] = tiles[5].astype(BF16)
        z_ref[tok, :] = _silu(_dot(h, wz_ref[...])).astype(BF16)
        gl_ref[tok, :] = _dot(h, wg_ref[...])

    _run_in_lockstep([row_part(part) for part in range(x_ref.shape[0] // PART_ROWS)])


def _inproj0(x, mod, g, cos, sin, wq, wkv, wz, wg):
    b, t, d = x.shape
    nq, nz = wq.shape[1], wz.shape[1]
    tok = lambda n: pl.BlockSpec((None, TM_IN, n), lambda i, j: (i, j, 0))
    full = lambda a: pl.BlockSpec(a.shape, lambda i, j: (0,) * a.ndim)
    tab = pl.BlockSpec((TM_IN, LANES), lambda i, j: (j, 0))
    shp = lambda n, dt: jax.ShapeDtypeStruct((b, t, n), dt)
    kt = pl.BlockSpec((None, NSA_KV_HEADS, TM_IN // KEY_CHUNK, LANES, KEY_CHUNK), lambda i, j: (i, 0, j, 0, 0))
    kt_shp = jax.ShapeDtypeStruct((b, NSA_KV_HEADS, t // KEY_CHUNK, LANES, KEY_CHUNK), BF16)
    kv = tok(LANES)
    kv_shp = shp(LANES, BF16)
    cm = pl.BlockSpec((None, TM_IN // CMP_STRIDE, CMP_STRIDE * LANES), lambda i, j: (i, j, 0))
    cm_shp = jax.ShapeDtypeStruct((b, t // CMP_STRIDE, CMP_STRIDE * LANES), BF16)
    return pl.pallas_call(
        _inproj0_kernel,
        grid=(b, t // TM_IN),
        in_specs=[tok(d), pl.BlockSpec((None, 3, d), lambda i, j: (i, 0, 0)), full(g), tab, tab,
                  full(wq), full(wkv), full(wz), full(wg)],
        out_specs=[tok(nq), tok(nq), cm, cm, kt, kv, kt, kv, tok(nz), tok(LANES)],
        out_shape=[shp(nq, BF16), shp(nq, BF16), cm_shp, cm_shp, kt_shp, kv_shp, kt_shp, kv_shp,
                   shp(nz, BF16), shp(LANES, F32)],
        scratch_shapes=[pltpu.VMEM((2, TM_IN, LANES), F32)],
        compiler_params=_cparams("parallel", "parallel"),
        name="inproj_nsa",
    )(x, mod, g, cos, sin, wq, wkv, wz, wg)


def _compress_kernel(ak_ref, av_ref, pekt_ref, pekb_ref, pevt_ref, pevb_ref,
                     w1kt_ref, w1kb_ref, w1vt_ref, w1vb_ref, w2k_ref, w2v_ref, kcmp_ref, vcmp_ref):
    def mlp(a_ref, pet_ref, peb_ref, w1t_ref, w1b_ref, w2_ref):
        a = a_ref[...].astype(F32)
        top = _dot((a + pet_ref[...]).astype(BF16), w1t_ref[...])
        bot = _dot((a + peb_ref[...]).astype(BF16), w1b_ref[...])
        n = bot.shape[0]
        hid = top + pltpu.roll(bot, n - 1, axis=0)
        return _dot(_silu(hid).astype(BF16), w2_ref[...])

    kcmp_ref[...] = mlp(ak_ref, pekt_ref, pekb_ref, w1kt_ref, w1kb_ref, w2k_ref).astype(BF16)
    vcmp_ref[...] = mlp(av_ref, pevt_ref, pevb_ref, w1vt_ref, w1vb_ref, w2v_ref).astype(BF16)


def _compress(ak, av, consts):
    b, n, w = ak.shape
    full = lambda a: pl.BlockSpec(a.shape, lambda i: (0,) * a.ndim)
    blk = pl.BlockSpec((None, n, w), lambda i: (i, 0, 0))
    out = pl.BlockSpec((None, n, LANES), lambda i: (i, 0, 0))
    return pl.pallas_call(
        _compress_kernel,
        grid=(b,),
        in_specs=[blk, blk] + [full(a) for a in consts],
        out_specs=[out, out],
        out_shape=[jax.ShapeDtypeStruct((b, n, LANES), BF16)] * 2,
        compiler_params=_cparams("parallel"),
        name="compress",
    )(ak, av, *consts)


def _head_gate_rows(gates, g, branches, lane):
    blocks = []
    for j in range(NSA_GROUP):
        base = (g * NSA_GROUP + j) * NSA_BRANCHES
        kept = jnp.zeros_like(gates)
        for br in branches:
            kept = jnp.where(lane == base + br, gates, kept)
        hi = kept.astype(BF16)
        blocks.append(jnp.concatenate([hi, (kept - hi.astype(F32)).astype(BF16)], axis=1))
    return jnp.concatenate(blocks, axis=0)


def _gate_spread(branches):
    col = np.arange(2 * LANES) % LANES
    spread = np.zeros((2 * LANES, LANES * len(branches)), np.float32)
    for i, br in enumerate(branches):
        spread[col % NSA_BRANCHES == br, i * LANES:(i + 1) * LANES] = 1.0
    return jnp.asarray(spread, dtype=BF16)


def _cmp_attn_kernel(qp_ref, kcmp_ref, vcmp_ref, ovl_ref, gl_ref, ocmp_ref, msel_ref, *, n_slc, tq):
    n_cmp = vcmp_ref.shape[0]
    hg = qp_ref.shape[1] // LANES
    n_sub = qp_ref.shape[0] // tq
    assert n_cmp == LANES and tq == LANES
    row = lax.broadcasted_iota(jnp.int32, (LANES, tq), 0)
    col = lax.broadcasted_iota(jnp.int32, (LANES, tq), 1)
    jb = lax.broadcasted_iota(jnp.int32, (n_slc, tq), 0)
    tok_rows = lambda sub: slice(sub * tq, (sub + 1) * tq)
    t0 = lambda sub: pl.program_id(1) * qp_ref.shape[0] + sub * tq
    units = [(sub, g) for sub in range(n_sub) for g in range(NSA_KV_HEADS)]
    eye = jnp.where(row == col, 1.0, 0.0).astype(BF16)
    k_sel = jnp.concatenate([kcmp_ref[...], eye], axis=1)
    v_t = vcmp_ref[...].T

    q_t = [[qp_ref[tok_rows(sub), j * LANES:(j + 1) * LANES].T for j in range(hg)] for sub in range(n_sub)]
    gates_t = [jax.nn.sigmoid(gl_ref[tok_rows(sub), :]).T for sub in range(n_sub)]
    seen = [jnp.concatenate([jnp.where(CMP_STRIDE * row + (CMP_LEN - 1) <= t0(sub) + col, 0.0, NEG).astype(BF16)] * hg,
                            axis=1) for sub in range(n_sub)]
    weights = {}
    for sub, g in units:
        in_group = ((row % HEAD_DIM) // HALF) == g
        q_cols = jnp.concatenate([jnp.where(in_group, v, jnp.zeros_like(v)) for v in q_t[sub]], axis=1)
        s = _dot(k_sel, jnp.concatenate([q_cols, seen[sub]], axis=0))
        m = jnp.max(s, axis=0, keepdims=True)
        m = jnp.where(m > 0.5 * NEG, m, 0.0)
        e = jnp.exp(s - m)
        den = jnp.sum(e, axis=0, keepdims=True)
        weights[sub, g] = e * (1.0 / jnp.where(den > 0, den, 1.0))
    outs, imps = {}, {}
    for sub, g in units:
        p = weights[sub, g]
        gate = jnp.concatenate([gates_t[sub][(g * NSA_GROUP + j) * NSA_BRANCHES:(g * NSA_GROUP + j) * NSA_BRANCHES + 1, :]
                                for j in range(hg)], axis=1)
        outs[sub, g] = gate * _dot(v_t, p.astype(BF16))
        p_sum = p[:, 0:tq]
        for j in range(1, hg):
            p_sum = p_sum + p[:, j * tq:(j + 1) * tq]
        imps[sub, g] = jnp.dot(ovl_ref[...], p_sum, precision=HIGHEST, preferred_element_type=F32)
    picked = {}
    for sub, g in units:
        cur = (t0(sub) + lax.broadcasted_iota(jnp.int32, (n_slc, tq), 1)) // SLC_LEN
        visible = jb <= cur
        forced = (jb == 0) | (jb == cur) | (jb == cur - 1)
        rank = jnp.where(forced, -NEG, jnp.where(visible, imps[sub, g], NEG))
        cnt = jnp.zeros((n_slc, tq), F32)
        for jp in range(n_slc):
            rj = rank[jp:jp + 1, :]
            tie = jnp.where(jb > jp, 1.0, 0.0)
            cnt = cnt + jnp.where(rj > rank, 1.0, jnp.where(rj == rank, tie, 0.0))
        picked[sub, g] = jnp.where(visible, jnp.where(cnt < SLC_TOP, 1.0, 0.0), 0.0)
    pad = jnp.zeros((LANES - NSA_KV_HEADS * n_slc, tq), F32)
    for sub in range(n_sub):
        for j in range(hg):
            cols = slice(j * tq, (j + 1) * tq)
            o_t = jnp.where(row < HEAD_DIM, outs[sub, 0][:, cols], outs[sub, 1][:, cols])
            ocmp_ref[tok_rows(sub), j * LANES:(j + 1) * LANES] = o_t.T.astype(BF16)
        sel_t = jnp.concatenate([picked[sub, g] for g in range(NSA_KV_HEADS)] + [pad], axis=0)
        msel_ref[tok_rows(sub), :] = sel_t.T.astype(BF16)


def _cmp_attn(qp, kcmp, vcmp, ovl_t, gl):
    b, t, nq = qp.shape
    n_cmp = kcmp.shape[1]
    n_slc = t // SLC_LEN
    tok = lambda n: pl.BlockSpec((None, CMP_STEP_TOKENS, n), lambda i, j: (i, j, 0))
    per_b = pl.BlockSpec((None, n_cmp, LANES), lambda i, j: (i, 0, 0))
    full = lambda a: pl.BlockSpec(a.shape, lambda i, j: (0,) * a.ndim)
    return pl.pallas_call(
        functools.partial(_cmp_attn_kernel, n_slc=n_slc, tq=TQ),
        grid=(b, t // CMP_STEP_TOKENS),
        in_specs=[tok(nq), per_b, per_b, full(ovl_t), tok(LANES)],
        out_specs=[tok(nq), tok(LANES)],
        out_shape=[jax.ShapeDtypeStruct((b, t, nq), BF16), jax.ShapeDtypeStruct((b, t, LANES), BF16)],
        compiler_params=_cparams("parallel", "parallel"),
        name="cmp_attn_select",
    )(qp, kcmp, vcmp, ovl_t, gl)


def _values_with_ones(v, g):
    in_own_half = (lax.broadcasted_iota(jnp.int32, v.shape, 1) // HEAD_DIM) == g
    return jnp.where(in_own_half, v, jnp.ones_like(v))


def _lane_tiles(v):
    return [v[:, i * LANES:(i + 1) * LANES] for i in range(v.shape[1] // LANES)]


def _tile_max(tiles):
    mx = tiles[0]
    for v in tiles[1:]:
        mx = jnp.maximum(mx, v)
    return jnp.max(mx, axis=-1, keepdims=True)


def _probs(tiles, m):
    return jnp.concatenate([jnp.exp(v - m) for v in tiles], axis=1).astype(BF16)


def _attend_chunks(q_sel, ks_ref, e_ref, vs_ref, m_ref, acc_ref, c0, n):
    k0 = pl.multiple_of(c0 * KEY_CHUNK, KEY_CHUNK)
    scores = []
    for g in range(NSA_KV_HEADS):
        kt = jnp.concatenate([ks_ref[g, c0 + i] for i in range(n)], axis=1)
        eb = jnp.concatenate([e_ref[g, c0 + i] for i in range(n)], axis=1)
        scores.append(_dot(q_sel, jnp.concatenate([kt, eb], axis=0)))
    groups = range(NSA_KV_HEADS)
    m_old = [m_ref[g] for g in groups]
    m_new = [jnp.maximum(m_old[g], _tile_max(_lane_tiles(scores[g]))) for g in groups]
    p = [_probs(_lane_tiles(scores[g]), m_new[g]) for g in groups]
    pv = [_dot(p[g], _values_with_ones(vs_ref[pl.ds(k0, n * KEY_CHUNK), :], g)) for g in groups]
    for g in groups:
        acc_ref[g] = jnp.exp(m_old[g] - m_new[g]) * acc_ref[g] + pv[g]
        m_ref[g] = m_new[g]


def _nsa_attn_kernel(*refs, tq, n_slc, ck):
    *io_refs, m_scr, acc_scr = refs
    qr_ref, ks_ref, vs_ref, _, _, _, e_ref = io_refs[:7]
    n_sub = qr_ref.shape[0] // tq
    per_ck = ck // KEY_CHUNK
    rows = (qr_ref.shape[1] // LANES) * tq
    assert per_ck % n_sub == 0
    first = pl.program_id(1) * n_sub
    tiles = [_nsa_attn_tile(*io_refs, m_scr.at[:, sub * rows:(sub + 1) * rows, :], acc_scr.at[:, sub * rows:(sub + 1) * rows, :],
                            tok=slice(sub * tq, (sub + 1) * tq), qi=first + sub, n_slc=n_slc, ck=ck) for sub in range(n_sub)]
    q_sel = jnp.concatenate([next(tile) for tile in tiles], axis=0)

    def whole_steps(kc, carry):
        _attend_chunks(q_sel, ks_ref, e_ref, vs_ref, m_scr, acc_scr, kc * per_ck, per_ck)
        return carry

    lax.fori_loop(0, first // per_ck, whole_steps, 0)
    _run_in_lockstep(tiles)


def _nsa_attn_tile(qr_ref, ks_ref, vs_ref, kw_ref, vw_ref, msel_ref, e_ref, gsp_ref, ocmp_ref, z_ref, gl_ref,
                   og_ref, m_scr, acc_scr, *, tok, qi, n_slc, ck):
    tq = tok.stop - tok.start
    hg = qr_ref.shape[1] // LANES
    rows = hg * tq
    n_wc = WIN_LEN // tq + 1
    per_ck = ck // KEY_CHUNK
    t0 = qi * tq
    lane = lax.broadcasted_iota(jnp.int32, (tq, LANES), 1)
    a_idx = lax.broadcasted_iota(jnp.int32, (tq, LANES), 0)

    eye = jnp.where(lane == a_idx, 1.0, 0.0).astype(BF16)
    earlier = jnp.where((lane % n_slc) < t0 // SLC_LEN, 0.0, NEG)
    mneg = jnp.where(msel_ref[tok, :].astype(F32) > 0.5, earlier, NEG).astype(BF16)
    mneg_rows = jnp.concatenate([mneg] * hg, axis=0)
    eye_rows = jnp.concatenate([eye] * hg, axis=0)

    def with_selector(sel_rows):
        q_rows = jnp.concatenate([qr_ref[tok, j * LANES:(j + 1) * LANES] for j in range(hg)], axis=0)
        return jnp.concatenate([q_rows, sel_rows], axis=1)

    zero_b = jnp.zeros((tq, LANES), BF16)
    neg_b = jnp.full((tq, LANES), NEG, BF16)
    causal_b = jnp.where(lane <= a_idx, 0.0, NEG).astype(BF16)
    far_b = jnp.where(lane > a_idx, 0.0, NEG).astype(BF16)
    own_half = [lane < HEAD_DIM, lane >= HEAD_DIM]
    values_with_ones, lane_tiles, tile_max, probs = _values_with_ones, _lane_tiles, _tile_max, _probs

    m_scr[...] = jnp.full(m_scr.shape, NEG, F32)
    acc_scr[...] = jnp.zeros(acc_scr.shape, F32)
    yield with_selector(mneg_rows)

    for rem in range(1, per_ck):
        @pl.when(qi % per_ck == rem)
        def _():
            _attend_chunks(with_selector(mneg_rows), ks_ref, e_ref, vs_ref, m_scr, acc_scr, (qi // per_ck) * per_ck, rem)
    yield

    kd = pl.multiple_of(t0, tq)
    zeros_v = jnp.zeros((tq, LANES), BF16)
    win_chunks = [qi - (n_wc - 1 - c) for c in range(n_wc - 1)]
    win_bias = [jnp.where(cidx >= 0, far_b if c == 0 else zero_b, neg_b) for c, cidx in enumerate(win_chunks)]
    pairs = [(c, c + 1) for c in range(0, n_wc - 1, 2)]
    q_eye = with_selector(eye_rows)
    s_own, scores = [], []
    for g in range(NSA_KV_HEADS):
        own_k = jnp.concatenate([jnp.concatenate([kw_ref[g, qi], causal_b], axis=0),
                                 jnp.concatenate([ks_ref[g, qi], causal_b], axis=0)], axis=1)
        kbs = [jnp.concatenate([kw_ref[g, jnp.maximum(cidx, 0)], win_bias[c]], axis=0)
               for c, cidx in enumerate(win_chunks)]
        s_own.append(_dot(q_eye, own_k))
        scores.append([_dot(q_eye, jnp.concatenate([kbs[c] for c in pr], axis=1)) for pr in pairs])

    yield
    groups = range(NSA_KV_HEADS)
    s_win_own = [s_own[g][:, 0:tq] for g in groups]
    s_slc_own = [s_own[g][:, tq:2 * tq] for g in groups]
    m_win = [tile_max([v for s in scores[g] for v in lane_tiles(s)] + [s_win_own[g]]) for g in groups]
    m_old = [m_scr[g] for g in groups]
    m_slc = [jnp.maximum(m_old[g], jnp.max(s_slc_own[g], axis=-1, keepdims=True)) for g in groups]
    p_all = [jnp.concatenate([probs(lane_tiles(s), m_win[g]) for s in scores[g]]
                             + [probs([s_win_own[g]], m_win[g]), probs([s_slc_own[g]], m_slc[g])], axis=1) for g in groups]
    yield
    acc_slc, acc_win = [], []
    for g in groups:
        vbs = [values_with_ones(vw_ref[pl.ds(pl.multiple_of(jnp.maximum(cidx, 0) * tq, tq), tq), :], g)
               for cidx in win_chunks] + [values_with_ones(vw_ref[pl.ds(kd, tq), :], g)]
        win_v = [jnp.concatenate([v, zeros_v], axis=1) for v in vbs]
        slc_v = jnp.concatenate([zeros_v, values_with_ones(vs_ref[pl.ds(kd, tq), :], g)], axis=1)
        pv = _dot(p_all[g], jnp.concatenate(win_v + [slc_v], axis=0))
        acc_win.append(pv[:, 0:LANES])
        acc_slc.append(jnp.exp(m_old[g] - m_slc[g]) * acc_scr[g] + pv[:, LANES:2 * LANES])
    yield

    gates = jax.nn.sigmoid(gl_ref[tok, :])
    row_gates = [_dot(_head_gate_rows(gates, g, (1, 2), lane), gsp_ref[...]) for g in range(NSA_KV_HEADS)]

    def gated(accs, rs, branch_lanes):
        num = [accs[g][rs] * row_gates[g][rs, branch_lanes] for g in range(NSA_KV_HEADS)]
        sums = pltpu.roll(jnp.where(own_half[0], accs[1][rs], accs[0][rs]), HEAD_DIM, axis=1)
        return jnp.where(own_half[0], num[0], num[1]) / sums

    for j in range(hg):
        rs = slice(j * tq, (j + 1) * tq)
        cols = slice(j * LANES, (j + 1) * LANES)
        o = (ocmp_ref[tok, cols].astype(F32) + gated(acc_slc, rs, slice(0, LANES))
             + gated(acc_win, rs, slice(LANES, 2 * LANES)))
        og_ref[tok, cols] = (o * z_ref[tok, cols].astype(F32)).astype(BF16)


def _nsa_attn(qr, ks, vs, kw, vw, msel, e, ocmp, z, gl):
    b, t, nq = qr.shape
    gsp = _gate_spread((1, 2))
    hg = nq // LANES
    rows = hg * TQ
    assert TQ == KEY_CHUNK and WIN_LEN % TQ == 0
    tok = lambda n: pl.BlockSpec((None, NSA_STEP_TOKENS, n), lambda i, j: (i, j, 0))
    per_b = pl.BlockSpec((None, t, LANES), lambda i, j: (i, 0, 0))
    per_b_t = pl.BlockSpec((None, NSA_KV_HEADS, t // KEY_CHUNK, LANES, KEY_CHUNK), lambda i, j: (i, 0, 0, 0, 0))
    full = lambda a: pl.BlockSpec(a.shape, lambda i, j: (0,) * a.ndim)
    return pl.pallas_call(
        functools.partial(_nsa_attn_kernel, tq=TQ, n_slc=t // SLC_LEN, ck=SLC_CK),
        grid=(b, t // NSA_STEP_TOKENS),
        in_specs=[tok(nq), per_b_t, per_b, per_b_t, per_b, tok(LANES), full(e), full(gsp), tok(nq), tok(nq), tok(LANES)],
        out_specs=tok(nq),
        out_shape=jax.ShapeDtypeStruct((b, t, nq), BF16),
        scratch_shapes=[
            pltpu.VMEM((NSA_KV_HEADS, NSA_STEP_TOKENS // TQ * rows, LANES), F32),
            pltpu.VMEM((NSA_KV_HEADS, NSA_STEP_TOKENS // TQ * rows, LANES), F32),
        ],
        compiler_params=_cparams("parallel", "arbitrary"),
        name="nsa_slc_win_attn",
    )(qr, ks, vs, kw, vw, msel, e, gsp, ocmp, z, gl)


def _inproj1_kernel(og_ref, wo_ref, x_ref, mod0_ref, mod_ref, g_ref, cos_ref, sin_ref, wq_ref, wk_ref, wv_ref, wz_ref,
                    x1_ref, *refs):
    n_grp = len(DIL_PATTERNS)
    outs = [refs[a * n_grp:(a + 1) * n_grp] for a in range(3)]
    z_ref, stage = refs[3 * n_grp], refs[3 * n_grp + 1]
    tm = PART_ROWS
    gw = DIL_TILES_PER_GROUP * LANES

    def row_part(part):
        tok = slice(part * tm, (part + 1) * tm)
        x1 = x_ref[tok, :] + mod0_ref[2:3, :] * _dot(og_ref[tok, :], wo_ref[...])
        x1_ref[tok, :] = x1
        yield
        h = _modulated_norm(x1, mod_ref, g_ref).astype(BF16)
        yield
        projected = [_dot(h, w_ref[...]) for w_ref in (wq_ref, wk_ref, wv_ref, wz_ref)]
        yield
        cos = cos_ref[tok, :]
        sin = sin_ref[tok, :]
        for a, u in enumerate(projected[0:3]):
            tiles = [u[:, pt * LANES:(pt + 1) * LANES] for pt in range(len(DIL_HEAD_PAIRS))]
            if a < 2:
                tiles = [_rope_tile(v, cos, sin) for v in tiles]
            for gi, (_, dil) in enumerate(DIL_PATTERNS):
                for jt, (pt, _) in enumerate(DIL_GROUP_TILES[gi]):
                    tile = tiles[pt]
                    if dil == 1:
                        outs[a][gi][tok, jt * LANES:(jt + 1) * LANES] = tile.astype(BF16)
                        continue
                    slab = (a * n_grp + gi) * DIL_TILES_PER_GROUP + jt
                    stage[slab, tok, :] = tile
                    short = slice(part * tm // dil, (part + 1) * tm // dil)
                    for r in range(dil):
                        rows = stage[slab, pl.ds(part * tm + r, tm // dil, stride=dil), :]
                        outs[a][gi][short, r * gw + jt * LANES:r * gw + (jt + 1) * LANES] = rows.astype(BF16)
        z_ref[tok, :] = _silu(projected[3]).astype(BF16)

    _run_in_lockstep([row_part(part) for part in range(x_ref.shape[0] // PART_ROWS)])


def _outproj0_inproj1(og, wo, x, mod0, mod, g, cos, sin, wq, wk, wv, wz):
    b, t, d = x.shape
    n = wq.shape[1]
    gw = DIL_TILES_PER_GROUP * LANES
    tok = lambda k: pl.BlockSpec((None, TM, k), lambda i, j: (i, j, 0))
    full = lambda a: pl.BlockSpec(a.shape, lambda i, j: (0,) * a.ndim)
    tab = pl.BlockSpec((TM, LANES), lambda i, j: (j, 0))
    mods = pl.BlockSpec((None, 3, d), lambda i, j: (i, 0, 0))
    grp_specs = [pl.BlockSpec((None, TM // dil, dil * gw), lambda i, j: (i, j, 0)) for _, dil in DIL_PATTERNS]
    grp_shapes = [jax.ShapeDtypeStruct((b, t // dil, dil * gw), BF16) for _, dil in DIL_PATTERNS]
    n_slabs = 3 * len(DIL_PATTERNS) * DIL_TILES_PER_GROUP
    return pl.pallas_call(
        _inproj1_kernel,
        grid=(b, t // TM),
        in_specs=[tok(og.shape[2]), full(wo), tok(d), mods, mods, full(g), tab, tab,
                  full(wq), full(wk), full(wv), full(wz)],
        out_specs=[tok(d)] + grp_specs * 3 + [tok(n)],
        out_shape=[jax.ShapeDtypeStruct((b, t, d), F32)] + grp_shapes * 3 + [jax.ShapeDtypeStruct((b, t, n), BF16)],
        scratch_shapes=[pltpu.VMEM((n_slabs, TM, LANES), F32)],
        compiler_params=_cparams("parallel", "parallel"),
        name="outproj_nsa_inproj_dil",
    )(og, wo, x, mod0, mod, g, cos, sin, wq, wk, wv, wz)


def _dil_attn_kernel(*refs, plans):
    n_grp = len(plans)
    step = pl.program_id(1)
    groups = []
    for gi, (members, tq, n_seq, blocks_per_seq) in enumerate(plans):
        q_ref, k_ref, v_ref = refs[3 * gi:3 * gi + 3]
        o_ref, lse_ref = refs[3 * n_grp + 2 * gi:3 * n_grp + 2 * gi + 2]
        groups.append(_dil_attn_group(q_ref, k_ref, v_ref, o_ref, lse_ref, step % blocks_per_seq, members, tq, n_seq))
    _run_in_lockstep(groups)


def _dil_attn_group(q_ref, k_ref, v_ref, o_ref, lse_ref, i, members, tq, n_seq):
    n_heads = sum(len(m) for m in members)
    tq_blk = q_ref.shape[0]
    gw = q_ref.shape[1] // n_seq
    n_tiles = gw // LANES
    n_sub = tq_blk // tq
    assert tq == LANES
    row = lax.broadcasted_iota(jnp.int32, (LANES, tq), 0)
    col = lax.broadcasted_iota(jnp.int32, (LANES, tq), 1)
    head_a = ((row % HEAD_DIM) // HALF) == 0
    diag_b = jnp.where(row <= col, 0.0, NEG)
    prev_b = jnp.where(row >= col, 0.0, NEG)
    neg_b = jnp.full((LANES, tq), NEG, F32)
    two = lambda v: jnp.concatenate([v, v], axis=1)
    units = [(sq, sub, jt) for sq in range(n_seq) for sub in range(n_sub) for jt in range(n_tiles)]
    rows_of = lambda sub: slice(sub * tq, (sub + 1) * tq)
    cols_of = lambda sq, jt: slice(sq * gw + jt * LANES, sq * gw + (jt + 1) * LANES)
    key_rows = lambda c: pl.ds(pl.multiple_of(jnp.maximum(i * n_sub + c, 0) * tq, tq), tq)
    q_cols, v_t = {}, {}
    for sq, sub, jt in units:
        q_t = q_ref[rows_of(sub), cols_of(sq, jt)].T
        zero = jnp.zeros_like(q_t)
        q_cols[sq, sub, jt] = jnp.concatenate([jnp.where(head_a, q_t, zero), jnp.where(head_a, zero, q_t)], axis=1)
        v_t[sq, sub, jt] = v_ref[key_rows(sub), cols_of(sq, jt)].T
        if sub == 0:
            v_t[sq, -1, jt] = v_ref[key_rows(-1), cols_of(sq, jt)].T
    yield
    probs_t, stats = {}, {}
    for sq, sub, jt in units:
        prev_bias = prev_b if sub > 0 else jnp.where(i > 0, prev_b, neg_b)
        bias = jnp.concatenate([two(prev_bias), two(diag_b)], axis=0)
        keys = jnp.concatenate([k_ref[key_rows(sub - 1), cols_of(sq, jt)], k_ref[key_rows(sub), cols_of(sq, jt)]], axis=0)
        s = _dot(keys, q_cols[sq, sub, jt]) + bias
        m = jnp.max(s, axis=0, keepdims=True)
        p = jnp.exp2(s - m)
        l = jnp.sum(p, axis=0, keepdims=True)
        probs_t[sq, sub, jt] = p.astype(BF16)
        stats[sq, sub, jt] = (m, l)
    yield
    for sq, sub, jt in units:
        m, l = stats[sq, sub, jt]
        o_t = _dot(jnp.concatenate([v_t[sq, sub - 1, jt], v_t[sq, sub, jt]], axis=1), probs_t[sq, sub, jt])
        inv = 1.0 / l
        o_t = jnp.where(row < HEAD_DIM, o_t[:, 0:tq] * inv[:, 0:tq], o_t[:, tq:2 * tq] * inv[:, tq:2 * tq])
        o_ref[rows_of(sub), cols_of(sq, jt)] = o_t.T.astype(BF16)
    yield
    for sq in range(n_seq):
        for sub in range(n_sub):
            lses = []
            for jt in range(n_tiles):
                m, l = stats[sq, sub, jt]
                lse = m * LN2 + jnp.log(l)
                lses += [lse[:, mb * tq:(mb + 1) * tq] for mb in members[jt]]
            top = lses[0]
            for v in lses[1:]:
                top = jnp.maximum(top, v)
            tot = jnp.zeros_like(top)
            for v in lses:
                tot = tot + jnp.exp(v - top)
            group_lse = top + jnp.log(tot) - math.log(n_heads)
            lse_ref[rows_of(sub), sq * LANES:(sq + 1) * LANES] = jnp.broadcast_to(group_lse, (LANES, tq)).T


def _dil_attn(qs, ks, vs, t):
    b = qs[0].shape[0]
    steps = t // DIL_STEP_TOKENS
    plans, in_specs, out_specs, out_shapes, operands = [], [], [], [], []
    for gi, (_, dil) in enumerate(DIL_PATTERNS):
        length = t // dil
        gw = qs[gi].shape[2] // dil
        tq_blk = min(DIL_STEP_TOKENS, length)
        blocks_per_seq = length // tq_blk
        n_seq = DIL_STEP_TOKENS // tq_blk
        assert dil * blocks_per_seq == steps * n_seq
        tok = lambda w, bps=blocks_per_seq, ns=n_seq, tb=tq_blk: pl.BlockSpec(
            (None, tb, ns * w), lambda bi, s: (bi, s % bps, s // bps))
        seq = pl.BlockSpec((None, length, n_seq * gw), lambda bi, s, bps=blocks_per_seq: (bi, 0, s // bps))
        plans.append((tuple(m for _, m in DIL_GROUP_TILES[gi]), min(TQ, length), n_seq, blocks_per_seq))
        in_specs += [tok(gw), seq, seq]
        operands += [qs[gi], ks[gi], vs[gi]]
        out_specs += [tok(gw), tok(LANES)]
        out_shapes += [jax.ShapeDtypeStruct(qs[gi].shape, BF16), jax.ShapeDtypeStruct((b, length, dil * LANES), F32)]
    outs = pl.pallas_call(
        functools.partial(_dil_attn_kernel, plans=tuple(plans)),
        grid=(b, steps),
        in_specs=in_specs,
        out_specs=out_specs,
        out_shape=out_shapes,
        compiler_params=_cparams("parallel", "parallel"),
        name="dil_attn",
    )(*operands)
    return outs[0::2], outs[1::2]


def _outproj1_kernel(*refs):
    n_grp = len(DIL_PATTERNS)
    o_refs, l_refs = refs[0:n_grp], refs[n_grp:2 * n_grp]
    z_ref, w_ref, x_ref, mod_ref, fg_ref, out_ref, stage = refs[2 * n_grp:]
    tm = x_ref.shape[0]
    gw = DIL_TILES_PER_GROUP * LANES

    def token_order(ref, gi, width, col0, slab):
        dil = DIL_PATTERNS[gi][1]
        if dil == 1:
            return ref[:, col0:col0 + LANES].astype(F32)
        for r in range(dil):
            stage[slab, pl.ds(r, tm // dil, stride=dil), :] = ref[:, r * width + col0:r * width + col0 + LANES].astype(F32)
        return stage[slab]

    ls = [token_order(l_refs[gi], gi, LANES, 0, gi) for gi in range(n_grp)]
    top = jnp.maximum(jnp.maximum(ls[0], ls[1]), ls[2])
    es = [jnp.exp(v - top) for v in ls]
    den = es[0] + es[1] + es[2]
    alphas = [e / den * float(n_grp) for e in es]
    lane = lax.broadcasted_iota(jnp.int32, (tm, LANES), 1)
    weighted = [None] * len(DIL_HEAD_PAIRS)
    for gi in range(n_grp):
        for jt, (pt, members) in enumerate(DIL_GROUP_TILES[gi]):
            ot = token_order(o_refs[gi], gi, gw, jt * LANES, n_grp + gi * DIL_TILES_PER_GROUP + jt) * alphas[gi]
            if len(members) == 2:
                weighted[pt] = ot
            else:
                mine = (lane // HEAD_DIM) == members[0]
                weighted[pt] = jnp.where(mine, ot, 0.0 if weighted[pt] is None else weighted[pt])
    parts = [(weighted[pt] * z_ref[:, pt * LANES:(pt + 1) * LANES].astype(F32)).astype(BF16)
             for pt in range(len(DIL_HEAD_PAIRS))]
    og = jnp.concatenate(parts, axis=1)
    y = _dot(og, w_ref[...])
    x2 = x_ref[...] + mod_ref[2:3, :] * y
    ms = jnp.mean(x2 * x2, axis=-1, keepdims=True)
    out_ref[...] = x2 * lax.rsqrt(ms + NORM_EPS) * fg_ref[...]


def _outproj1(o_groups, lse_groups, z, w, x, mod, final_g):
    b, t, d = x.shape
    tm = TM_OUT
    tok = lambda k: pl.BlockSpec((None, tm, k), lambda i, j: (i, j, 0))
    full = lambda a: pl.BlockSpec(a.shape, lambda i, j: (0,) * a.ndim)
    grp = lambda a, dil: pl.BlockSpec((None, tm // dil, a.shape[2]), lambda i, j: (i, j, 0))
    dils = [dil for _, dil in DIL_PATTERNS]
    n_slabs = len(dils) * (1 + DIL_TILES_PER_GROUP)
    return pl.pallas_call(
        _outproj1_kernel,
        grid=(b, t // tm),
        in_specs=[grp(a, dl) for a, dl in zip(o_groups, dils)] + [grp(a, dl) for a, dl in zip(lse_groups, dils)]
                 + [tok(z.shape[2]), full(w), tok(d), pl.BlockSpec((None, 3, d), lambda i, j: (i, 0, 0)), full(final_g)],
        out_specs=tok(d),
        out_shape=jax.ShapeDtypeStruct((b, t, d), F32),
        scratch_shapes=[pltpu.VMEM((n_slabs, tm, LANES), F32)],
        compiler_params=_cparams("parallel", "parallel"),
        name="outproj_dil_final",
    )(*o_groups, *lse_groups, z, w, x, mod, final_g)


def _rope_tables(t):
    inv = ROPE_THETA ** (-jnp.arange(HALF, dtype=F32) / HALF)
    ang = jnp.arange(t, dtype=F32)[:, None] * inv[None, :]
    cos = jnp.tile(jnp.cos(ang), (1, 4))
    sin = jnp.tile(jnp.sin(ang), (1, 4))
    sign = jnp.where(jnp.arange(LANES) < 2 * HALF, -1.0, 1.0).astype(F32)
    return cos, sin * sign[None, :]


def _pair_rope_layout(w, n_pairs):
    d = w.shape[0]
    return w.reshape(d, n_pairs, 2, 2, HALF).transpose(0, 1, 3, 2, 4).reshape(d, n_pairs * LANES)


def _nsa_weights(w_in, pe_k, pe_v, ck_w1, ck_w2, cv_w1, cv_w2, w_out):
    d = w_in.shape[0]
    aw = N_HEADS * HEAD_DIM
    kvw = NSA_KV_HEADS * HEAD_DIM
    cuts = np.cumsum([aw] + [kvw] * 6 + [aw]).tolist()
    q, k_c, v_c, k_s, v_s, k_w, v_w, z, gl = jnp.split(w_in, cuts, axis=1)
    scale = HEAD_DIM ** -0.5
    wq = (q * scale).reshape(d, NSA_KV_HEADS, NSA_GROUP, 2, HALF).transpose(0, 2, 3, 1, 4).reshape(d, aw)
    k_lay = lambda w: _pair_rope_layout(w, 1)
    wkv = jnp.concatenate([k_lay(k_c), v_c, k_lay(k_s), v_s, k_lay(k_w), v_w], axis=1)
    wz = z.reshape(d, NSA_KV_HEADS, NSA_GROUP, HEAD_DIM).transpose(0, 2, 1, 3).reshape(d, aw)
    wo = w_out.reshape(NSA_KV_HEADS, NSA_GROUP, HEAD_DIM, -1).transpose(1, 0, 2, 3).reshape(aw, -1)
    wg = jnp.pad(gl, ((0, 0), (0, LANES - gl.shape[1])))

    def grouped(w1_half, rope_lanes):
        out = []
        for g in range(NSA_KV_HEADS):
            if rope_lanes:
                src = w1_half.reshape(CMP_STRIDE, 2, 1, HALF, CMP_HIDDEN)
                pads = ((0, 0), (0, 0), (g, NSA_KV_HEADS - 1 - g), (0, 0), (0, 0))
            else:
                src = w1_half.reshape(CMP_STRIDE, 1, HEAD_DIM, CMP_HIDDEN)
                pads = ((0, 0), (g, NSA_KV_HEADS - 1 - g), (0, 0), (0, 0))
            out.append(jnp.pad(src, pads).reshape(CMP_STRIDE * LANES, CMP_HIDDEN))
        return jnp.concatenate(out, axis=1)

    def pe_rows(pe_half, rope_lanes):
        if rope_lanes:
            v = jnp.broadcast_to(pe_half.reshape(CMP_STRIDE, 2, 1, HALF), (CMP_STRIDE, 2, NSA_KV_HEADS, HALF))
        else:
            v = jnp.broadcast_to(pe_half.reshape(CMP_STRIDE, 1, HEAD_DIM), (CMP_STRIDE, NSA_KV_HEADS, HEAD_DIM))
        return v.reshape(1, CMP_STRIDE * LANES)

    def w2_padded(w2, rope_lanes):
        out = []
        for g in range(NSA_KV_HEADS):
            if rope_lanes:
                src = w2.reshape(CMP_HIDDEN, 2, 1, HALF)
                pads = ((0, 0), (0, 0), (g, NSA_KV_HEADS - 1 - g), (0, 0))
            else:
                src = w2.reshape(CMP_HIDDEN, 1, HEAD_DIM)
                pads = ((0, 0), (g, NSA_KV_HEADS - 1 - g), (0, 0))
            out.append(jnp.pad(src, pads).reshape(CMP_HIDDEN, LANES))
        return jnp.concatenate(out, axis=0)

    k1 = ck_w1.reshape(2, CMP_STRIDE, HEAD_DIM, CMP_HIDDEN)
    v1 = cv_w1.reshape(2, CMP_STRIDE, HEAD_DIM, CMP_HIDDEN)
    compress_consts = (
        pe_rows(pe_k[:CMP_STRIDE], True), pe_rows(pe_k[CMP_STRIDE:], True),
        pe_rows(pe_v[:CMP_STRIDE], False), pe_rows(pe_v[CMP_STRIDE:], False),
        grouped(k1[0], True).astype(BF16), grouped(k1[1], True).astype(BF16),
        grouped(v1[0], False).astype(BF16), grouped(v1[1], False).astype(BF16),
        w2_padded(ck_w2, True).astype(BF16), w2_padded(cv_w2, False).astype(BF16),
    )
    return wq.astype(BF16), wkv.astype(BF16), wz.astype(BF16), wg.astype(BF16), wo.astype(BF16), compress_consts


def _dil_weights(w_in, w_out):
    d = w_in.shape[0]
    q, k, v, z = jnp.split(w_in, 4, axis=1)
    n_pairs = len(DIL_HEAD_PAIRS)
    order = [hd for pair in DIL_HEAD_PAIRS for hd in pair]

    def tile_order(w, axis):
        parts = []
        for hd in order:
            sl = [slice(None)] * w.ndim
            sl[axis] = slice(hd * HEAD_DIM, (hd + 1) * HEAD_DIM)
            parts.append(w[tuple(sl)])
        return jnp.concatenate(parts, axis=axis)

    scale = HEAD_DIM ** -0.5 * LOG2E
    wq = _pair_rope_layout(tile_order(q * scale, 1), n_pairs)
    wk = _pair_rope_layout(tile_order(k, 1), n_pairs)
    return (wq.astype(BF16), wk.astype(BF16), tile_order(v, 1).astype(BF16), tile_order(z, 1).astype(BF16),
            tile_order(w_out, 0).astype(BF16))


def _selection_constants(t):
    n_cmp_rows = t // CMP_STRIDE
    n_slc = t // SLC_LEN
    c_start = CMP_STRIDE * np.arange(n_cmp_rows)
    s_start = SLC_LEN * np.arange(n_slc)
    ovl_t = ((c_start[None, :] < s_start[:, None] + SLC_LEN)
             & (c_start[None, :] + CMP_LEN > s_start[:, None])).astype(np.float32)
    ovl_t[:, (t - CMP_LEN) // CMP_STRIDE + 1:] = 0.0
    e = np.zeros((NSA_KV_HEADS, t // KEY_CHUNK, LANES, KEY_CHUNK), np.float32)
    keys = np.arange(t)
    for g in range(NSA_KV_HEADS):
        e[g, keys // KEY_CHUNK, g * n_slc + keys // SLC_LEN, keys % KEY_CHUNK] = 1.0
    return jnp.asarray(ovl_t), jnp.asarray(e, dtype=BF16)


@jax.jit
def kernel(x, c, norm_g, ada_w, ada_b, nsa_w_in, nsa_pe_k, nsa_pe_v, nsa_ck_w1, nsa_ck_w2,
           nsa_cv_w1, nsa_cv_w2, nsa_w_out, dil_w_in, dil_w_out, final_g):
    b, t, d = x.shape
    assert t % SLC_CK == 0 and t % TM == 0 and t % TM_IN == 0 and d % LANES == 0
    assert TM % PART_ROWS == 0 and TM_IN % PART_ROWS == 0 and PART_ROWS % (CMP_STRIDE * 16) == 0
    assert NSA_KV_HEADS * (t // SLC_LEN) <= LANES and t // CMP_STRIDE == LANES
    mod = _adaln_mod(c, ada_w, ada_b).reshape(ada_w.shape[0], b, 3, d)
    cos, sin = _rope_tables(t)

    wq, wkv, wz, wg, wo, compress_consts = _nsa_weights(
        nsa_w_in[0], nsa_pe_k[0], nsa_pe_v[0], nsa_ck_w1[0], nsa_ck_w2[0], nsa_cv_w1[0], nsa_cv_w2[0], nsa_w_out[0])
    qp, qr, kc, vc, ks, vs, kw, vw, z, gl = _inproj0(x, mod[0], norm_g[0:1], cos, sin, wq, wkv, wz, wg)
    kcmp, vcmp = _compress(kc, vc, compress_consts)
    ovl_t, e_sel = _selection_constants(t)
    ocmp, msel = _cmp_attn(qp, kcmp, vcmp, ovl_t, gl)
    og = _nsa_attn(qr, ks, vs, kw, vw, msel, e_sel, ocmp, z, gl)

    dq, dk, dv, dz, dwo = _dil_weights(dil_w_in[0], dil_w_out[0])
    n_grp = len(DIL_PATTERNS)
    x1, *qkv, z1 = _outproj0_inproj1(og, wo, x, mod[0], mod[1], norm_g[1:2], cos, sin, dq, dk, dv, dz)
    assert all(win // dil == DIL_WIN and PART_ROWS % (dil * 16) == 0 for win, dil in DIL_PATTERNS)
    o_groups, lse_groups = _dil_attn(qkv[0:n_grp], qkv[n_grp:2 * n_grp], qkv[2 * n_grp:3 * n_grp], t)
    return _outproj1(o_groups, lse_groups, z1, dwo, x1, mod[1], final_g.reshape(1, d))
```

```python
import functools
import math

import numpy as np
import jax
import jax.numpy as jnp
from jax import lax
from jax.experimental import pallas as pl
from jax.experimental.pallas import tpu as pltpu

F32 = jnp.float32
BF16 = jnp.bfloat16
HIGHEST = lax.Precision.HIGHEST

HEAD_DIM = 64
HALF = HEAD_DIM // 2
N_HEADS = 16
ROPE_THETA = 10000.0
NORM_EPS = 1e-6
NSA_KV_HEADS = 2
NSA_GROUP = N_HEADS // NSA_KV_HEADS
NSA_BRANCHES = 3
CMP_LEN = 32
CMP_STRIDE = 16
CMP_HIDDEN = 256
SLC_LEN = 64
SLC_TOP = 16
WIN_LEN = 512
DIL_PATTERNS = ((128, 1), (512, 4), (2048, 16))
DIL_GROUP_HEADS = (6, 5, 5)
DIL_WIN = 128


def _dil_tile_plan():
    offs = np.cumsum((0,) + DIL_GROUP_HEADS)
    pairs, group_tiles, leftovers = [], [], []
    for gi, hn in enumerate(DIL_GROUP_HEADS):
        heads = list(range(offs[gi], offs[gi + 1]))
        group_tiles.append([])
        for i in range(0, hn - hn % 2, 2):
            group_tiles[gi].append((len(pairs), (0, 1)))
            pairs.append((heads[i], heads[i + 1]))
        if hn % 2:
            leftovers.append((gi, heads[-1]))
    assert len(leftovers) % 2 == 0
    for (ga, ha), (gb, hb) in zip(leftovers[0::2], leftovers[1::2]):
        group_tiles[ga].append((len(pairs), (0,)))
        group_tiles[gb].append((len(pairs), (1,)))
        pairs.append((ha, hb))
    return tuple(pairs), tuple(tuple(tiles) for tiles in group_tiles)


DIL_HEAD_PAIRS, DIL_GROUP_TILES = _dil_tile_plan()
DIL_TILES_PER_GROUP = len(DIL_GROUP_TILES[0])
assert all(len(tiles) == DIL_TILES_PER_GROUP for tiles in DIL_GROUP_TILES)

LANES = 128
VMEM_LIMIT_BYTES = 48 * 1024 * 1024

NEG = -1e30
LOG2E = math.log2(math.e)
LN2 = math.log(2.0)
TQ = 128
KEY_CHUNK = 128
SLC_CK = 512
MOD_COLS = 1024
TM = 512
TM_IN = 1024
PART_ROWS = 256
TM_OUT = 1024
RESID_SLOTS = 3
DIL_STEP_TOKENS = 1024
NSA_STEP_TOKENS = 256
CMP_STEP_TOKENS = 2048


def _cparams(*sem):
    return pltpu.CompilerParams(dimension_semantics=sem, vmem_limit_bytes=VMEM_LIMIT_BYTES)


def _dot(a, b):
    return jnp.dot(a, b, preferred_element_type=F32)


def _silu(v):
    return v * jax.nn.sigmoid(v)


def _run_in_lockstep(stage_generators):
    pending = list(stage_generators)
    done = object()
    while pending:
        pending = [gen for gen in pending if next(gen, done) is not done]


def _mod_kernel(c_ref, w_ref, b_ref, o_ref):
    s = _silu(c_ref[...])
    o_ref[...] = jnp.dot(s, w_ref[...], precision=HIGHEST, preferred_element_type=F32) + b_ref[...]


def _adaln_mod(c, ada_w, ada_b):
    depth, d, n3 = ada_w.shape
    b = c.shape[0]
    tn = MOD_COLS
    return pl.pallas_call(
        _mod_kernel,
        grid=(depth, n3 // tn),
        in_specs=[
            pl.BlockSpec((b, d), lambda i, j: (0, 0)),
            pl.BlockSpec((None, d, tn), lambda i, j: (i, 0, j)),
            pl.BlockSpec((None, 1, tn), lambda i, j: (i, 0, j)),
        ],
        out_specs=pl.BlockSpec((None, b, tn), lambda i, j: (i, 0, j)),
        out_shape=jax.ShapeDtypeStruct((depth, b, n3), F32),
        compiler_params=_cparams("arbitrary", "arbitrary"),
        name="adaln_mod",
    )(c, ada_w, ada_b.reshape(depth, 1, n3))


def _modulated_norm(x, mod_ref, g_ref):
    ms = jnp.mean(x * x, axis=-1, keepdims=True)
    y = x * lax.rsqrt(ms + NORM_EPS) * g_ref[...]
    return y * (1.0 + mod_ref[1:2, :]) + mod_ref[0:1, :]


def _rope_tile(v, cos, sin):
    return v * cos + pltpu.roll(v, 2 * HALF, axis=1) * sin


def _store_key_chunks_t(kt_ref, first_chunk, k):
    dim = lax.broadcasted_iota(jnp.int32, (LANES, KEY_CHUNK), 0)
    chunks_t = [k[c * KEY_CHUNK:(c + 1) * KEY_CHUNK, :].T for c in range(k.shape[0] // KEY_CHUNK)]
    for c, kt in enumerate(chunks_t):
        for g in range(kt_ref.shape[0]):
            kt_ref[g, first_chunk + c] = jnp.where((dim % HEAD_DIM) // HALF == g, kt, 0.0).astype(BF16)


def _inproj0_kernel(x_ref, mod_ref, g_ref, cos_ref, sin_ref, wq_ref, wkv_ref, wz_ref, wg_ref,
                    qp_ref, qr_ref, kc_ref, vc_ref, ks_ref, vs_ref, kw_ref, vw_ref, z_ref, gl_ref, stage):
    tm = PART_ROWS

    def row_part(part):
        tok = slice(part * tm, (part + 1) * tm)
        h = _modulated_norm(x_ref[tok, :], mod_ref, g_ref).astype(BF16)
        yield
        cos = cos_ref[tok, :]
        sin = sin_ref[tok, :]
        q = _dot(h, wq_ref[...])
        yield
        for j in range(q.shape[1] // LANES):
            sl = slice(j * LANES, (j + 1) * LANES)
            qp_ref[tok, sl] = q[:, sl].astype(BF16)
            qr_ref[tok, sl] = _rope_tile(q[:, sl], cos, sin).astype(BF16)
        kv = _dot(h, wkv_ref[...])
        yield
        tiles = [kv[:, i * LANES:(i + 1) * LANES] for i in range(6)]
        short = slice(part * tm // CMP_STRIDE, (part + 1) * tm // CMP_STRIDE)
        for slab, (tile, o_ref) in enumerate(((tiles[0], kc_ref), (tiles[1], vc_ref))):
            stage[slab, tok, :] = tile
            for r in range(CMP_STRIDE):
                rows = stage[slab, pl.ds(part * tm + r, tm // CMP_STRIDE, stride=CMP_STRIDE), :]
                o_ref[short, r * LANES:(r + 1) * LANES] = rows.astype(BF16)
        first_chunk = part * tm // KEY_CHUNK
        _store_key_chunks_t(ks_ref, first_chunk, _rope_tile(tiles[2], cos, sin))
        vs_ref[tok, :] = tiles[3].astype(BF16)
        _store_key_chunks_t(kw_ref, first_chunk, _rope_tile(tiles[4], cos, sin))
        vw_ref[tok, :] = tiles[5].astype(BF16)
        z_ref[tok, :] = _silu(_dot(h, wz_ref[...])).astype(BF16)
        gl_ref[tok, :] = _dot(h, wg_ref[...])

    _run_in_lockstep([row_part(part) for part in range(x_ref.shape[0] // PART_ROWS)])


def _inproj0(x, mod, g, cos, sin, wq, wkv, wz, wg):
    b, t, d = x.shape
    nq, nz = wq.shape[1], wz.shape[1]
    tok = lambda n: pl.BlockSpec((None, TM_IN, n), lambda i, j: (i, j, 0))
    full = lambda a: pl.BlockSpec(a.shape, lambda i, j: (0,) * a.ndim)
    tab = pl.BlockSpec((TM_IN, LANES), lambda i, j: (j, 0))
    shp = lambda n, dt: jax.ShapeDtypeStruct((b, t, n), dt)
    kt = pl.BlockSpec((None, NSA_KV_HEADS, TM_IN // KEY_CHUNK, LANES, KEY_CHUNK), lambda i, j: (i, 0, j, 0, 0))
    kt_shp = jax.ShapeDtypeStruct((b, NSA_KV_HEADS, t // KEY_CHUNK, LANES, KEY_CHUNK), BF16)
    kv = tok(LANES)
    kv_shp = shp(LANES, BF16)
    cm = pl.BlockSpec((None, TM_IN // CMP_STRIDE, CMP_STRIDE * LANES), lambda i, j: (i, j, 0))
    cm_shp = jax.ShapeDtypeStruct((b, t // CMP_STRIDE, CMP_STRIDE * LANES), BF16)
    return pl.pallas_call(
        _inproj0_kernel,
        grid=(b, t // TM_IN),
        in_specs=[tok(d), pl.BlockSpec((None, 3, d), lambda i, j: (i, 0, 0)), full(g), tab, tab,
                  full(wq), full(wkv), full(wz), full(wg)],
        out_specs=[tok(nq), tok(nq), cm, cm, kt, kv, kt, kv, tok(nz), tok(LANES)],
        out_shape=[shp(nq, BF16), shp(nq, BF16), cm_shp, cm_shp, kt_shp, kv_shp, kt_shp, kv_shp,
                   shp(nz, BF16), shp(LANES, F32)],
        scratch_shapes=[pltpu.VMEM((2, TM_IN, LANES), F32)],
        compiler_params=_cparams("parallel", "parallel"),
        name="inproj_nsa",
    )(x, mod, g, cos, sin, wq, wkv, wz, wg)


def _compress_kernel(ak_ref, av_ref, pekt_ref, pekb_ref, pevt_ref, pevb_ref,
                     w1kt_ref, w1kb_ref, w1vt_ref, w1vb_ref, w2k_ref, w2v_ref, kcmp_ref, vcmp_ref):
    def mlp(a_ref, pet_ref, peb_ref, w1t_ref, w1b_ref, w2_ref):
        a = a_ref[...].astype(F32)
        top = _dot((a + pet_ref[...]).astype(BF16), w1t_ref[...])
        bot = _dot((a + peb_ref[...]).astype(BF16), w1b_ref[...])
        n = bot.shape[0]
        hid = top + pltpu.roll(bot, n - 1, axis=0)
        return _dot(_silu(hid).astype(BF16), w2_ref[...])

    kcmp_ref[...] = mlp(ak_ref, pekt_ref, pekb_ref, w1kt_ref, w1kb_ref, w2k_ref).astype(BF16)
    vcmp_ref[...] = mlp(av_ref, pevt_ref, pevb_ref, w1vt_ref, w1vb_ref, w2v_ref).astype(BF16)


def _compress(ak, av, consts):
    b, n, w = ak.shape
    full = lambda a: pl.BlockSpec(a.shape, lambda i: (0,) * a.ndim)
    blk = pl.BlockSpec((None, n, w), lambda i: (i, 0, 0))
    out = pl.BlockSpec((None, n, LANES), lambda i: (i, 0, 0))
    return pl.pallas_call(
        _compress_kernel,
        grid=(b,),
        in_specs=[blk, blk] + [full(a) for a in consts],
        out_specs=[out, out],
        out_shape=[jax.ShapeDtypeStruct((b, n, LANES), BF16)] * 2,
        compiler_params=_cparams("parallel"),
        name="compress",
    )(ak, av, *consts)


def _head_gate_rows(gates, g, branches, lane):
    blocks = []
    for j in range(NSA_GROUP):
        base = (g * NSA_GROUP + j) * NSA_BRANCHES
        kept = jnp.zeros_like(gates)
        for br in branches:
            kept = jnp.where(lane == base + br, gates, kept)
        hi = kept.astype(BF16)
        blocks.append(jnp.concatenate([hi, (kept - hi.astype(F32)).astype(BF16)], axis=1))
    return jnp.concatenate(blocks, axis=0)


def _gate_spread(branches):
    col = np.arange(2 * LANES) % LANES
    spread = np.zeros((2 * LANES, LANES * len(branches)), np.float32)
    for i, br in enumerate(branches):
        spread[col % NSA_BRANCHES == br, i * LANES:(i + 1) * LANES] = 1.0
    return jnp.asarray(spread, dtype=BF16)


def _cmp_attn_kernel(qp_ref, kcmp_ref, vcmp_ref, ovl_ref, gl_ref, ocmp_ref, msel_ref, *, n_slc, tq):
    n_cmp = vcmp_ref.shape[0]
    hg = qp_ref.shape[1] // LANES
    n_sub = qp_ref.shape[0] // tq
    assert n_cmp == LANES and tq == LANES
    row = lax.broadcasted_iota(jnp.int32, (LANES, tq), 0)
    col = lax.broadcasted_iota(jnp.int32, (LANES, tq), 1)
    jb = lax.broadcasted_iota(jnp.int32, (n_slc, tq), 0)
    tok_rows = lambda sub: slice(sub * tq, (sub + 1) * tq)
    t0 = lambda sub: pl.program_id(1) * qp_ref.shape[0] + sub * tq
    units = [(sub, g) for sub in range(n_sub) for g in range(NSA_KV_HEADS)]
    eye = jnp.where(row == col, 1.0, 0.0).astype(BF16)
    k_sel = jnp.concatenate([kcmp_ref[...], eye], axis=1)
    v_t = vcmp_ref[...].T

    q_t = [[qp_ref[tok_rows(sub), j * LANES:(j + 1) * LANES].T for j in range(hg)] for sub in range(n_sub)]
    gates_t = [jax.nn.sigmoid(gl_ref[tok_rows(sub), :]).T for sub in range(n_sub)]
    seen = [jnp.concatenate([jnp.where(CMP_STRIDE * row + (CMP_LEN - 1) <= t0(sub) + col, 0.0, NEG).astype(BF16)] * hg,
                            axis=1) for sub in range(n_sub)]
    weights = {}
    for sub, g in units:
        in_group = ((row % HEAD_DIM) // HALF) == g
        q_cols = jnp.concatenate([jnp.where(in_group, v, jnp.zeros_like(v)) for v in q_t[sub]], axis=1)
        s = _dot(k_sel, jnp.concatenate([q_cols, seen[sub]], axis=0))
        m = jnp.max(s, axis=0, keepdims=True)
        m = jnp.where(m > 0.5 * NEG, m, 0.0)
        e = jnp.exp(s - m)
        den = jnp.sum(e, axis=0, keepdims=True)
        weights[sub, g] = e * (1.0 / jnp.where(den > 0, den, 1.0))
    outs, imps = {}, {}
    for sub, g in units:
        p = weights[sub, g]
        gate = jnp.concatenate([gates_t[sub][(g * NSA_GROUP + j) * NSA_BRANCHES:(g * NSA_GROUP + j) * NSA_BRANCHES + 1, :]
                                for j in range(hg)], axis=1)
        outs[sub, g] = gate * _dot(v_t, p.astype(BF16))
        p_sum = p[:, 0:tq]
        for j in range(1, hg):
            p_sum = p_sum + p[:, j * tq:(j + 1) * tq]
        imps[sub, g] = jnp.dot(ovl_ref[...], p_sum, precision=HIGHEST, preferred_element_type=F32)
    picked = {}
    for sub, g in units:
        cur = (t0(sub) + lax.broadcasted_iota(jnp.int32, (n_slc, tq), 1)) // SLC_LEN
        visible = jb <= cur
        forced = (jb == 0) | (jb == cur) | (jb == cur - 1)
        rank = jnp.where(forced, -NEG, jnp.where(visible, imps[sub, g], NEG))
        cnt = jnp.zeros((n_slc, tq), F32)
        for jp in range(n_slc):
            rj = rank[jp:jp + 1, :]
            tie = jnp.where(jb > jp, 1.0, 0.0)
            cnt = cnt + jnp.where(rj > rank, 1.0, jnp.where(rj == rank, tie, 0.0))
        picked[sub, g] = jnp.where(visible, jnp.where(cnt < SLC_TOP, 1.0, 0.0), 0.0)
    pad = jnp.zeros((LANES - NSA_KV_HEADS * n_slc, tq), F32)
    for sub in range(n_sub):
        for j in range(hg):
            cols = slice(j * tq, (j + 1) * tq)
            o_t = jnp.where(row < HEAD_DIM, outs[sub, 0][:, cols], outs[sub, 1][:, cols])
            ocmp_ref[tok_rows(sub), j * LANES:(j + 1) * LANES] = o_t.T.astype(BF16)
        sel_t = jnp.concatenate([picked[sub, g] for g in range(NSA_KV_HEADS)] + [pad], axis=0)
        msel_ref[tok_rows(sub), :] = sel_t.T.astype(BF16)


def _cmp_attn(qp, kcmp, vcmp, ovl_t, gl):
    b, t, nq = qp.shape
    n_cmp = kcmp.shape[1]
    n_slc = t // SLC_LEN
    tok = lambda n: pl.BlockSpec((None, CMP_STEP_TOKENS, n), lambda i, j: (i, j, 0))
    per_b = pl.BlockSpec((None, n_cmp, LANES), lambda i, j: (i, 0, 0))
    full = lambda a: pl.BlockSpec(a.shape, lambda i, j: (0,) * a.ndim)
    return pl.pallas_call(
        functools.partial(_cmp_attn_kernel, n_slc=n_slc, tq=TQ),
        grid=(b, t // CMP_STEP_TOKENS),
        in_specs=[tok(nq), per_b, per_b, full(ovl_t), tok(LANES)],
        out_specs=[tok(nq), tok(LANES)],
        out_shape=[jax.ShapeDtypeStruct((b, t, nq), BF16), jax.ShapeDtypeStruct((b, t, LANES), BF16)],
        compiler_params=_cparams("parallel", "parallel"),
        name="cmp_attn_select",
    )(qp, kcmp, vcmp, ovl_t, gl)


def _values_with_ones(v, g):
    in_own_half = (lax.broadcasted_iota(jnp.int32, v.shape, 1) // HEAD_DIM) == g
    return jnp.where(in_own_half, v, jnp.ones_like(v))


def _lane_tiles(v):
    return [v[:, i * LANES:(i + 1) * LANES] for i in range(v.shape[1] // LANES)]


def _tile_max(tiles):
    mx = tiles[0]
    for v in tiles[1:]:
        mx = jnp.maximum(mx, v)
    return jnp.max(mx, axis=-1, keepdims=True)


def _probs(tiles, m):
    return jnp.concatenate([jnp.exp(v - m) for v in tiles], axis=1).astype(BF16)


def _attend_chunks(q_sel, ks_ref, e_ref, vs_ref, m_ref, acc_ref, c0, n):
    k0 = pl.multiple_of(c0 * KEY_CHUNK, KEY_CHUNK)
    scores = []
    for g in range(NSA_KV_HEADS):
        kt = jnp.concatenate([ks_ref[g, c0 + i] for i in range(n)], axis=1)
        eb = jnp.concatenate([e_ref[g, c0 + i] for i in range(n)], axis=1)
        scores.append(_dot(q_sel, jnp.concatenate([kt, eb], axis=0)))
    groups = range(NSA_KV_HEADS)
    m_old = [m_ref[g] for g in groups]
    m_new = [jnp.maximum(m_old[g], _tile_max(_lane_tiles(scores[g]))) for g in groups]
    p = [_probs(_lane_tiles(scores[g]), m_new[g]) for g in groups]
    pv = [_dot(p[g], _values_with_ones(vs_ref[pl.ds(k0, n * KEY_CHUNK), :], g)) for g in groups]
    for g in groups:
        acc_ref[g] = jnp.exp(m_old[g] - m_new[g]) * acc_ref[g] + pv[g]
        m_ref[g] = m_new[g]


def _nsa_attn_kernel(*refs, tq, n_slc, ck):
    *io_refs, m_scr, acc_scr = refs
    qr_ref, ks_ref, vs_ref, _, _, _, e_ref = io_refs[:7]
    n_sub = qr_ref.shape[0] // tq
    per_ck = ck // KEY_CHUNK
    rows = (qr_ref.shape[1] // LANES) * tq
    assert per_ck % n_sub == 0
    first = pl.program_id(1) * n_sub
    tiles = [_nsa_attn_tile(*io_refs, m_scr.at[:, sub * rows:(sub + 1) * rows, :], acc_scr.at[:, sub * rows:(sub + 1) * rows, :],
                            tok=slice(sub * tq, (sub + 1) * tq), qi=first + sub, n_slc=n_slc, ck=ck) for sub in range(n_sub)]
    q_sel = jnp.concatenate([next(tile) for tile in tiles], axis=0)

    def whole_steps(kc, carry):
        _attend_chunks(q_sel, ks_ref, e_ref, vs_ref, m_scr, acc_scr, kc * per_ck, per_ck)
        return carry

    lax.fori_loop(0, first // per_ck, whole_steps, 0)
    _run_in_lockstep(tiles)


def _nsa_attn_tile(qr_ref, ks_ref, vs_ref, kw_ref, vw_ref, msel_ref, e_ref, gsp_ref, ocmp_ref, z_ref, gl_ref,
                   og_ref, m_scr, acc_scr, *, tok, qi, n_slc, ck):
    tq = tok.stop - tok.start
    hg = qr_ref.shape[1] // LANES
    rows = hg * tq
    n_wc = WIN_LEN // tq + 1
    per_ck = ck // KEY_CHUNK
    t0 = qi * tq
    lane = lax.broadcasted_iota(jnp.int32, (tq, LANES), 1)
    a_idx = lax.broadcasted_iota(jnp.int32, (tq, LANES), 0)

    eye = jnp.where(lane == a_idx, 1.0, 0.0).astype(BF16)
    earlier = jnp.where((lane % n_slc) < t0 // SLC_LEN, 0.0, NEG)
    mneg = jnp.where(msel_ref[tok, :].astype(F32) > 0.5, earlier, NEG).astype(BF16)
    mneg_rows = jnp.concatenate([mneg] * hg, axis=0)
    eye_rows = jnp.concatenate([eye] * hg, axis=0)

    def with_selector(sel_rows):
        q_rows = jnp.concatenate([qr_ref[tok, j * LANES:(j + 1) * LANES] for j in range(hg)], axis=0)
        return jnp.concatenate([q_rows, sel_rows], axis=1)

    zero_b = jnp.zeros((tq, LANES), BF16)
    neg_b = jnp.full((tq, LANES), NEG, BF16)
    causal_b = jnp.where(lane <= a_idx, 0.0, NEG).astype(BF16)
    far_b = jnp.where(lane > a_idx, 0.0, NEG).astype(BF16)
    own_half = [lane < HEAD_DIM, lane >= HEAD_DIM]
    values_with_ones, lane_tiles, tile_max, probs = _values_with_ones, _lane_tiles, _tile_max, _probs

    m_scr[...] = jnp.full(m_scr.shape, NEG, F32)
    acc_scr[...] = jnp.zeros(acc_scr.shape, F32)
    yield with_selector(mneg_rows)

    for rem in range(1, per_ck):
        @pl.when(qi % per_ck == rem)
        def _():
            _attend_chunks(with_selector(mneg_rows), ks_ref, e_ref, vs_ref, m_scr, acc_scr, (qi // per_ck) * per_ck, rem)
    yield

    kd = pl.multiple_of(t0, tq)
    zeros_v = jnp.zeros((tq, LANES), BF16)
    win_chunks = [qi - (n_wc - 1 - c) for c in range(n_wc - 1)]
    win_bias = [jnp.where(cidx >= 0, far_b if c == 0 else zero_b, neg_b) for c, cidx in enumerate(win_chunks)]
    pairs = [(c, c + 1) for c in range(0, n_wc - 1, 2)]
    q_eye = with_selector(eye_rows)
    s_own, scores = [], []
    for g in range(NSA_KV_HEADS):
        own_k = jnp.concatenate([jnp.concatenate([kw_ref[g, qi], causal_b], axis=0),
                                 jnp.concatenate([ks_ref[g, qi], causal_b], axis=0)], axis=1)
        kbs = [jnp.concatenate([kw_ref[g, jnp.maximum(cidx, 0)], win_bias[c]], axis=0)
               for c, cidx in enumerate(win_chunks)]
        s_own.append(_dot(q_eye, own_k))
        scores.append([_dot(q_eye, jnp.concatenate([kbs[c] for c in pr], axis=1)) for pr in pairs])

    yield
    groups = range(NSA_KV_HEADS)
    s_win_own = [s_own[g][:, 0:tq] for g in groups]
    s_slc_own = [s_own[g][:, tq:2 * tq] for g in groups]
    m_win = [tile_max([v for s in scores[g] for v in lane_tiles(s)] + [s_win_own[g]]) for g in groups]
    m_old = [m_scr[g] for g in groups]
    m_slc = [jnp.maximum(m_old[g], jnp.max(s_slc_own[g], axis=-1, keepdims=True)) for g in groups]
    p_all = [jnp.concatenate([probs(lane_tiles(s), m_win[g]) for s in scores[g]]
                             + [probs([s_win_own[g]], m_win[g]), probs([s_slc_own[g]], m_slc[g])], axis=1) for g in groups]
    yield
    acc_slc, acc_win = [], []
    for g in groups:
        vbs = [values_with_ones(vw_ref[pl.ds(pl.multiple_of(jnp.maximum(cidx, 0) * tq, tq), tq), :], g)
               for cidx in win_chunks] + [values_with_ones(vw_ref[pl.ds(kd, tq), :], g)]
        win_v = [jnp.concatenate([v, zeros_v], axis=1) for v in vbs]
        slc_v = jnp.concatenate([zeros_v, values_with_ones(vs_ref[pl.ds(kd, tq), :], g)], axis=1)
        pv = _dot(p_all[g], jnp.concatenate(win_v + [slc_v], axis=0))
        acc_win.append(pv[:, 0:LANES])
        acc_slc.append(jnp.exp(m_old[g] - m_slc[g]) * acc_scr[g] + pv[:, LANES:2 * LANES])
    yield

    gates = jax.nn.sigmoid(gl_ref[tok, :])
    row_gates = [_dot(_head_gate_rows(gates, g, (1, 2), lane), gsp_ref[...]) for g in range(NSA_KV_HEADS)]

    def gated(accs, rs, branch_lanes):
        num = [accs[g][rs] * row_gates[g][rs, branch_lanes] for g in range(NSA_KV_HEADS)]
        sums = pltpu.roll(jnp.where(own_half[0], accs[1][rs], accs[0][rs]), HEAD_DIM, axis=1)
        return jnp.where(own_half[0], num[0], num[1]) / sums

    for j in range(hg):
        rs = slice(j * tq, (j + 1) * tq)
        cols = slice(j * LANES, (j + 1) * LANES)
        o = (ocmp_ref[tok, cols].astype(F32) + gated(acc_slc, rs, slice(0, LANES))
             + gated(acc_win, rs, slice(LANES, 2 * LANES)))
        og_ref[tok, cols] = (o * z_ref[tok, cols].astype(F32)).astype(BF16)


def _nsa_attn(qr, ks, vs, kw, vw, msel, e, ocmp, z, gl):
    b, t, nq = qr.shape
    gsp = _gate_spread((1, 2))
    hg = nq // LANES
    rows = hg * TQ
    assert TQ == KEY_CHUNK and WIN_LEN % TQ == 0
    tok = lambda n: pl.BlockSpec((None, NSA_STEP_TOKENS, n), lambda i, j: (i, j, 0))
    per_b = pl.BlockSpec((None, t, LANES), lambda i, j: (i, 0, 0))
    per_b_t = pl.BlockSpec((None, NSA_KV_HEADS, t // KEY_CHUNK, LANES, KEY_CHUNK), lambda i, j: (i, 0, 0, 0, 0))
    full = lambda a: pl.BlockSpec(a.shape, lambda i, j: (0,) * a.ndim)
    return pl.pallas_call(
        functools.partial(_nsa_attn_kernel, tq=TQ, n_slc=t // SLC_LEN, ck=SLC_CK),
        grid=(b, t // NSA_STEP_TOKENS),
        in_specs=[tok(nq), per_b_t, per_b, per_b_t, per_b, tok(LANES), full(e), full(gsp), tok(nq), tok(nq), tok(LANES)],
        out_specs=tok(nq),
        out_shape=jax.ShapeDtypeStruct((b, t, nq), BF16),
        scratch_shapes=[
            pltpu.VMEM((NSA_KV_HEADS, NSA_STEP_TOKENS // TQ * rows, LANES), F32),
            pltpu.VMEM((NSA_KV_HEADS, NSA_STEP_TOKENS // TQ * rows, LANES), F32),
        ],
        compiler_params=_cparams("parallel", "arbitrary"),
        name="nsa_slc_win_attn",
    )(qr, ks, vs, kw, vw, msel, e, gsp, ocmp, z, gl)


def _inproj1_kernel(og_ref, wo_ref, x_ref, mod0_ref, mod_ref, g_ref, cos_ref, sin_ref, wq_ref, wk_ref, wv_ref, wz_ref,
                    x1_ref, *refs):
    n_grp = len(DIL_PATTERNS)
    outs = [refs[a * n_grp:(a + 1) * n_grp] for a in range(3)]
    z_ref, stage = refs[3 * n_grp], refs[3 * n_grp + 1]
    tm = PART_ROWS
    gw = DIL_TILES_PER_GROUP * LANES

    def row_part(part):
        tok = slice(part * tm, (part + 1) * tm)
        x1 = x_ref[tok, :] + mod0_ref[2:3, :] * _dot(og_ref[tok, :], wo_ref[...])
        x1_ref[tok, :] = x1
        yield
        h = _modulated_norm(x1, mod_ref, g_ref).astype(BF16)
        yield
        projected = [_dot(h, w_ref[...]) for w_ref in (wq_ref, wk_ref, wv_ref, wz_ref)]
        yield
        cos = cos_ref[tok, :]
        sin = sin_ref[tok, :]
        for a, u in enumerate(projected[0:3]):
            tiles = [u[:, pt * LANES:(pt + 1) * LANES] for pt in range(len(DIL_HEAD_PAIRS))]
            if a < 2:
                tiles = [_rope_tile(v, cos, sin) for v in tiles]
            for gi, (_, dil) in enumerate(DIL_PATTERNS):
                for jt, (pt, _) in enumerate(DIL_GROUP_TILES[gi]):
                    tile = tiles[pt]
                    if dil == 1:
                        outs[a][gi][tok, jt * LANES:(jt + 1) * LANES] = tile.astype(BF16)
                        continue
                    slab = (a * n_grp + gi) * DIL_TILES_PER_GROUP + jt
                    stage[slab, tok, :] = tile
                    short = slice(part * tm // dil, (part + 1) * tm // dil)
                    for r in range(dil):
                        rows = stage[slab, pl.ds(part * tm + r, tm // dil, stride=dil), :]
                        outs[a][gi][short, r * gw + jt * LANES:r * gw + (jt + 1) * LANES] = rows.astype(BF16)
        z_ref[tok, :] = _silu(projected[3]).astype(BF16)

    _run_in_lockstep([row_part(part) for part in range(x_ref.shape[0] // PART_ROWS)])


def _outproj0_inproj1(og, wo, x, mod0, mod, g, cos, sin, wq, wk, wv, wz):
    b, t, d = x.shape
    n = wq.shape[1]
    gw = DIL_TILES_PER_GROUP * LANES
    tok = lambda k: pl.BlockSpec((None, TM, k), lambda i, j: (i, j, 0))
    full = lambda a: pl.BlockSpec(a.shape, lambda i, j: (0,) * a.ndim)
    tab = pl.BlockSpec((TM, LANES), lambda i, j: (j, 0))
    mods = pl.BlockSpec((None, 3, d), lambda i, j: (i, 0, 0))
    grp_specs = [pl.BlockSpec((None, TM // dil, dil * gw), lambda i, j: (i, j, 0)) for _, dil in DIL_PATTERNS]
    grp_shapes = [jax.ShapeDtypeStruct((b, t // dil, dil * gw), BF16) for _, dil in DIL_PATTERNS]
    n_slabs = 3 * len(DIL_PATTERNS) * DIL_TILES_PER_GROUP
    return pl.pallas_call(
        _inproj1_kernel,
        grid=(b, t // TM),
        in_specs=[tok(og.shape[2]), full(wo), tok(d), mods, mods, full(g), tab, tab,
                  full(wq), full(wk), full(wv), full(wz)],
        out_specs=[tok(d)] + grp_specs * 3 + [tok(n)],
        out_shape=[jax.ShapeDtypeStruct((b, t, d), F32)] + grp_shapes * 3 + [jax.ShapeDtypeStruct((b, t, n), BF16)],
        scratch_shapes=[pltpu.VMEM((n_slabs, TM, LANES), F32)],
        compiler_params=_cparams("parallel", "parallel"),
        name="outproj_nsa_inproj_dil",
    )(og, wo, x, mod0, mod, g, cos, sin, wq, wk, wv, wz)


def _dil_attn_kernel(*refs, plans):
    n_grp = len(plans)
    step = pl.program_id(1)
    groups = []
    for gi, (members, tq, n_seq, blocks_per_seq) in enumerate(plans):
        q_ref, k_ref, v_ref = refs[3 * gi:3 * gi + 3]
        o_ref, lse_ref = refs[3 * n_grp + 2 * gi:3 * n_grp + 2 * gi + 2]
        groups.append(_dil_attn_group(q_ref, k_ref, v_ref, o_ref, lse_ref, step % blocks_per_seq, members, tq, n_seq))
    _run_in_lockstep(groups)


def _dil_attn_group(q_ref, k_ref, v_ref, o_ref, lse_ref, i, members, tq, n_seq):
    n_heads = sum(len(m) for m in members)
    tq_blk = q_ref.shape[0]
    gw = q_ref.shape[1] // n_seq
    n_tiles = gw // LANES
    n_sub = tq_blk // tq
    assert tq == LANES
    row = lax.broadcasted_iota(jnp.int32, (LANES, tq), 0)
    col = lax.broadcasted_iota(jnp.int32, (LANES, tq), 1)
    head_a = ((row % HEAD_DIM) // HALF) == 0
    diag_b = jnp.where(row <= col, 0.0, NEG)
    prev_b = jnp.where(row >= col, 0.0, NEG)
    neg_b = jnp.full((LANES, tq), NEG, F32)
    two = lambda v: jnp.concatenate([v, v], axis=1)
    units = [(sq, sub, jt) for sq in range(n_seq) for sub in range(n_sub) for jt in range(n_tiles)]
    rows_of = lambda sub: slice(sub * tq, (sub + 1) * tq)
    cols_of = lambda sq, jt: slice(sq * gw + jt * LANES, sq * gw + (jt + 1) * LANES)
    key_rows = lambda c: pl.ds(pl.multiple_of(jnp.maximum(i * n_sub + c, 0) * tq, tq), tq)
    q_cols, v_t = {}, {}
    for sq, sub, jt in units:
        q_t = q_ref[rows_of(sub), cols_of(sq, jt)].T
        zero = jnp.zeros_like(q_t)
        q_cols[sq, sub, jt] = jnp.concatenate([jnp.where(head_a, q_t, zero), jnp.where(head_a, zero, q_t)], axis=1)
        v_t[sq, sub, jt] = v_ref[key_rows(sub), cols_of(sq, jt)].T
        if sub == 0:
            v_t[sq, -1, jt] = v_ref[key_rows(-1), cols_of(sq, jt)].T
    yield
    probs_t, stats = {}, {}
    for sq, sub, jt in units:
        prev_bias = prev_b if sub > 0 else jnp.where(i > 0, prev_b, neg_b)
        bias = jnp.concatenate([two(prev_bias), two(diag_b)], axis=0)
        keys = jnp.concatenate([k_ref[key_rows(sub - 1), cols_of(sq, jt)], k_ref[key_rows(sub), cols_of(sq, jt)]], axis=0)
        s = _dot(keys, q_cols[sq, sub, jt]) + bias
        m = jnp.max(s, axis=0, keepdims=True)
        p = jnp.exp2(s - m)
        l = jnp.sum(p, axis=0, keepdims=True)
        probs_t[sq, sub, jt] = p.astype(BF16)
        stats[sq, sub, jt] = (m, l)
    yield
    for sq, sub, jt in units:
        m, l = stats[sq, sub, jt]
        o_t = _dot(jnp.concatenate([v_t[sq, sub - 1, jt], v_t[sq, sub, jt]], axis=1), probs_t[sq, sub, jt])
        inv = 1.0 / l
        o_t = jnp.where(row < HEAD_DIM, o_t[:, 0:tq] * inv[:, 0:tq], o_t[:, tq:2 * tq] * inv[:, tq:2 * tq])
        o_ref[rows_of(sub), cols_of(sq, jt)] = o_t.T.astype(BF16)
    yield
    for sq in range(n_seq):
        for sub in range(n_sub):
            lses = []
            for jt in range(n_tiles):
                m, l = stats[sq, sub, jt]
                lse = m * LN2 + jnp.log(l)
                lses += [lse[:, mb * tq:(mb + 1) * tq] for mb in members[jt]]
            top = lses[0]
            for v in lses[1:]:
                top = jnp.maximum(top, v)
            tot = jnp.zeros_like(top)
            for v in lses:
                tot = tot + jnp.exp(v - top)
            group_lse = top + jnp.log(tot) - math.log(n_heads)
            lse_ref[rows_of(sub), sq * LANES:(sq + 1) * LANES] = jnp.broadcast_to(group_lse, (LANES, tq)).T


def _dil_attn(qs, ks, vs, t):
    b = qs[0].shape[0]
    steps = t // DIL_STEP_TOKENS
    plans, in_specs, out_specs, out_shapes, operands = [], [], [], [], []
    for gi, (_, dil) in enumerate(DIL_PATTERNS):
        length = t // dil
        gw = qs[gi].shape[2] // dil
        tq_blk = min(DIL_STEP_TOKENS, length)
        blocks_per_seq = length // tq_blk
        n_seq = DIL_STEP_TOKENS // tq_blk
        assert dil * blocks_per_seq == steps * n_seq
        tok = lambda w, bps=blocks_per_seq, ns=n_seq, tb=tq_blk: pl.BlockSpec(
            (None, tb, ns * w), lambda bi, s: (bi, s % bps, s // bps))
        seq = pl.BlockSpec((None, length, n_seq * gw), lambda bi, s, bps=blocks_per_seq: (bi, 0, s // bps))
        plans.append((tuple(m for _, m in DIL_GROUP_TILES[gi]), min(TQ, length), n_seq, blocks_per_seq))
        in_specs += [tok(gw), seq, seq]
        operands += [qs[gi], ks[gi], vs[gi]]
        out_specs += [tok(gw), tok(LANES)]
        out_shapes += [jax.ShapeDtypeStruct(qs[gi].shape, BF16), jax.ShapeDtypeStruct((b, length, dil * LANES), F32)]
    outs = pl.pallas_call(
        functools.partial(_dil_attn_kernel, plans=tuple(plans)),
        grid=(b, steps),
        in_specs=in_specs,
        out_specs=out_specs,
        out_shape=out_shapes,
        compiler_params=_cparams("parallel", "parallel"),
        name="dil_attn",
    )(*operands)
    return outs[0::2], outs[1::2]


def _outproj1_kernel(*refs, n_steps):
    n_grp = len(DIL_PATTERNS)
    o_refs, l_refs = refs[0:n_grp], refs[n_grp:2 * n_grp]
    z_ref, w_ref, x_hbm, mod_ref, fg_ref, out_ref, stage, x_ring, x_sem = refs[2 * n_grp:]
    tm = out_ref.shape[0]
    gw = DIL_TILES_PER_GROUP * LANES
    steps_per_row = pl.num_programs(1)
    step = pl.program_id(0) * steps_per_row + pl.program_id(1)

    def resid_copy(s):
        src = x_hbm.at[s // steps_per_row, pl.ds((s % steps_per_row) * tm, tm), :]
        return pltpu.make_async_copy(src, x_ring.at[s % RESID_SLOTS], x_sem.at[s % RESID_SLOTS])

    @pl.when(step == 0)
    def _():
        for s in range(min(RESID_SLOTS - 1, n_steps)):
            resid_copy(s).start()

    @pl.when(step + (RESID_SLOTS - 1) < n_steps)
    def _():
        resid_copy(step + (RESID_SLOTS - 1)).start()

    def token_order(ref, gi, width, col0, slab):
        dil = DIL_PATTERNS[gi][1]
        if dil == 1:
            return ref[:, col0:col0 + LANES].astype(F32)
        for r in range(dil):
            stage[slab, pl.ds(r, tm // dil, stride=dil), :] = ref[:, r * width + col0:r * width + col0 + LANES].astype(F32)
        return stage[slab]

    ls = [token_order(l_refs[gi], gi, LANES, 0, gi) for gi in range(n_grp)]
    top = jnp.maximum(jnp.maximum(ls[0], ls[1]), ls[2])
    es = [jnp.exp(v - top) for v in ls]
    den = es[0] + es[1] + es[2]
    alphas = [e / den * float(n_grp) for e in es]
    lane = lax.broadcasted_iota(jnp.int32, (tm, LANES), 1)
    weighted = [None] * len(DIL_HEAD_PAIRS)
    for gi in range(n_grp):
        for jt, (pt, members) in enumerate(DIL_GROUP_TILES[gi]):
            ot = token_order(o_refs[gi], gi, gw, jt * LANES, n_grp + gi * DIL_TILES_PER_GROUP + jt) * alphas[gi]
            if len(members) == 2:
                weighted[pt] = ot
            else:
                mine = (lane // HEAD_DIM) == members[0]
                weighted[pt] = jnp.where(mine, ot, 0.0 if weighted[pt] is None else weighted[pt])
    parts = [(weighted[pt] * z_ref[:, pt * LANES:(pt + 1) * LANES].astype(F32)).astype(BF16)
             for pt in range(len(DIL_HEAD_PAIRS))]
    og = jnp.concatenate(parts, axis=1)
    y = _dot(og, w_ref[...])
    resid_copy(step).wait()
    x2 = x_ring[step % RESID_SLOTS] + mod_ref[2:3, :] * y
    ms = jnp.mean(x2 * x2, axis=-1, keepdims=True)
    out_ref[...] = x2 * lax.rsqrt(ms + NORM_EPS) * fg_ref[...]


def _outproj1(o_groups, lse_groups, z, w, x, mod, final_g):
    b, t, d = x.shape
    tm = TM_OUT
    tok = lambda k: pl.BlockSpec((None, tm, k), lambda i, j: (i, j, 0))
    full = lambda a: pl.BlockSpec(a.shape, lambda i, j: (0,) * a.ndim)
    grp = lambda a, dil: pl.BlockSpec((None, tm // dil, a.shape[2]), lambda i, j: (i, j, 0))
    resid = pl.BlockSpec(memory_space=pl.ANY)
    dils = [dil for _, dil in DIL_PATTERNS]
    n_slabs = len(dils) * (1 + DIL_TILES_PER_GROUP)
    return pl.pallas_call(
        functools.partial(_outproj1_kernel, n_steps=b * (t // tm)),
        grid=(b, t // tm),
        in_specs=[grp(a, dl) for a, dl in zip(o_groups, dils)] + [grp(a, dl) for a, dl in zip(lse_groups, dils)]
                 + [tok(z.shape[2]), full(w), resid, pl.BlockSpec((None, 3, d), lambda i, j: (i, 0, 0)), full(final_g)],
        out_specs=tok(d),
        out_shape=jax.ShapeDtypeStruct((b, t, d), F32),
        scratch_shapes=[pltpu.VMEM((n_slabs, tm, LANES), F32), pltpu.VMEM((RESID_SLOTS, tm, d), F32),
                        pltpu.SemaphoreType.DMA((RESID_SLOTS,))],
        compiler_params=_cparams("arbitrary", "arbitrary"),
        name="outproj_dil_final",
    )(*o_groups, *lse_groups, z, w, x, mod, final_g)


def _rope_tables(t):
    inv = ROPE_THETA ** (-jnp.arange(HALF, dtype=F32) / HALF)
    ang = jnp.arange(t, dtype=F32)[:, None] * inv[None, :]
    cos = jnp.tile(jnp.cos(ang), (1, 4))
    sin = jnp.tile(jnp.sin(ang), (1, 4))
    sign = jnp.where(jnp.arange(LANES) < 2 * HALF, -1.0, 1.0).astype(F32)
    return cos, sin * sign[None, :]


def _pair_rope_layout(w, n_pairs):
    d = w.shape[0]
    return w.reshape(d, n_pairs, 2, 2, HALF).transpose(0, 1, 3, 2, 4).reshape(d, n_pairs * LANES)


def _nsa_weights(w_in, pe_k, pe_v, ck_w1, ck_w2, cv_w1, cv_w2, w_out):
    d = w_in.shape[0]
    aw = N_HEADS * HEAD_DIM
    kvw = NSA_KV_HEADS * HEAD_DIM
    cuts = np.cumsum([aw] + [kvw] * 6 + [aw]).tolist()
    q, k_c, v_c, k_s, v_s, k_w, v_w, z, gl = jnp.split(w_in, cuts, axis=1)
    scale = HEAD_DIM ** -0.5
    wq = (q * scale).reshape(d, NSA_KV_HEADS, NSA_GROUP, 2, HALF).transpose(0, 2, 3, 1, 4).reshape(d, aw)
    k_lay = lambda w: _pair_rope_layout(w, 1)
    wkv = jnp.concatenate([k_lay(k_c), v_c, k_lay(k_s), v_s, k_lay(k_w), v_w], axis=1)
    wz = z.reshape(d, NSA_KV_HEADS, NSA_GROUP, HEAD_DIM).transpose(0, 2, 1, 3).reshape(d, aw)
    wo = w_out.reshape(NSA_KV_HEADS, NSA_GROUP, HEAD_DIM, -1).transpose(1, 0, 2, 3).reshape(aw, -1)
    wg = jnp.pad(gl, ((0, 0), (0, LANES - gl.shape[1])))

    def grouped(w1_half, rope_lanes):
        out = []
        for g in range(NSA_KV_HEADS):
            if rope_lanes:
                src = w1_half.reshape(CMP_STRIDE, 2, 1, HALF, CMP_HIDDEN)
                pads = ((0, 0), (0, 0), (g, NSA_KV_HEADS - 1 - g), (0, 0), (0, 0))
            else:
                src = w1_half.reshape(CMP_STRIDE, 1, HEAD_DIM, CMP_HIDDEN)
                pads = ((0, 0), (g, NSA_KV_HEADS - 1 - g), (0, 0), (0, 0))
            out.append(jnp.pad(src, pads).reshape(CMP_STRIDE * LANES, CMP_HIDDEN))
        return jnp.concatenate(out, axis=1)

    def pe_rows(pe_half, rope_lanes):
        if rope_lanes:
            v = jnp.broadcast_to(pe_half.reshape(CMP_STRIDE, 2, 1, HALF), (CMP_STRIDE, 2, NSA_KV_HEADS, HALF))
        else:
            v = jnp.broadcast_to(pe_half.reshape(CMP_STRIDE, 1, HEAD_DIM), (CMP_STRIDE, NSA_KV_HEADS, HEAD_DIM))
        return v.reshape(1, CMP_STRIDE * LANES)

    def w2_padded(w2, rope_lanes):
        out = []
        for g in range(NSA_KV_HEADS):
            if rope_lanes:
                src = w2.reshape(CMP_HIDDEN, 2, 1, HALF)
                pads = ((0, 0), (0, 0), (g, NSA_KV_HEADS - 1 - g), (0, 0))
            else:
                src = w2.reshape(CMP_HIDDEN, 1, HEAD_DIM)
                pads = ((0, 0), (g, NSA_KV_HEADS - 1 - g), (0, 0))
            out.append(jnp.pad(src, pads).reshape(CMP_HIDDEN, LANES))
        return jnp.concatenate(out, axis=0)

    k1 = ck_w1.reshape(2, CMP_STRIDE, HEAD_DIM, CMP_HIDDEN)
    v1 = cv_w1.reshape(2, CMP_STRIDE, HEAD_DIM, CMP_HIDDEN)
    compress_consts = (
        pe_rows(pe_k[:CMP_STRIDE], True), pe_rows(pe_k[CMP_STRIDE:], True),
        pe_rows(pe_v[:CMP_STRIDE], False), pe_rows(pe_v[CMP_STRIDE:], False),
        grouped(k1[0], True).astype(BF16), grouped(k1[1], True).astype(BF16),
        grouped(v1[0], False).astype(BF16), grouped(v1[1], False).astype(BF16),
        w2_padded(ck_w2, True).astype(BF16), w2_padded(cv_w2, False).astype(BF16),
    )
    return wq.astype(BF16), wkv.astype(BF16), wz.astype(BF16), wg.astype(BF16), wo.astype(BF16), compress_consts


def _dil_weights(w_in, w_out):
    d = w_in.shape[0]
    q, k, v, z = jnp.split(w_in, 4, axis=1)
    n_pairs = len(DIL_HEAD_PAIRS)
    order = [hd for pair in DIL_HEAD_PAIRS for hd in pair]

    def tile_order(w, axis):
        parts = []
        for hd in order:
            sl = [slice(None)] * w.ndim
            sl[axis] = slice(hd * HEAD_DIM, (hd + 1) * HEAD_DIM)
            parts.append(w[tuple(sl)])
        return jnp.concatenate(parts, axis=axis)

    scale = HEAD_DIM ** -0.5 * LOG2E
    wq = _pair_rope_layout(tile_order(q * scale, 1), n_pairs)
    wk = _pair_rope_layout(tile_order(k, 1), n_pairs)
    return (wq.astype(BF16), wk.astype(BF16), tile_order(v, 1).astype(BF16), tile_order(z, 1).astype(BF16),
            tile_order(w_out, 0).astype(BF16))


def _selection_constants(t):
    n_cmp_rows = t // CMP_STRIDE
    n_slc = t // SLC_LEN
    c_start = CMP_STRIDE * np.arange(n_cmp_rows)
    s_start = SLC_LEN * np.arange(n_slc)
    ovl_t = ((c_start[None, :] < s_start[:, None] + SLC_LEN)
             & (c_start[None, :] + CMP_LEN > s_start[:, None])).astype(np.float32)
    ovl_t[:, (t - CMP_LEN) // CMP_STRIDE + 1:] = 0.0
    e = np.zeros((NSA_KV_HEADS, t // KEY_CHUNK, LANES, KEY_CHUNK), np.float32)
    keys = np.arange(t)
    for g in range(NSA_KV_HEADS):
        e[g, keys // KEY_CHUNK, g * n_slc + keys // SLC_LEN, keys % KEY_CHUNK] = 1.0
    return jnp.asarray(ovl_t), jnp.asarray(e, dtype=BF16)


@jax.jit
def kernel(x, c, norm_g, ada_w, ada_b, nsa_w_in, nsa_pe_k, nsa_pe_v, nsa_ck_w1, nsa_ck_w2,
           nsa_cv_w1, nsa_cv_w2, nsa_w_out, dil_w_in, dil_w_out, final_g):
    b, t, d = x.shape
    assert t % SLC_CK == 0 and t % TM == 0 and t % TM_IN == 0 and d % LANES == 0
    assert TM % PART_ROWS == 0 and TM_IN % PART_ROWS == 0 and PART_ROWS % (CMP_STRIDE * 16) == 0
    assert NSA_KV_HEADS * (t // SLC_LEN) <= LANES and t // CMP_STRIDE == LANES
    mod = _adaln_mod(c, ada_w, ada_b).reshape(ada_w.shape[0], b, 3, d)
    cos, sin = _rope_tables(t)

    wq, wkv, wz, wg, wo, compress_consts = _nsa_weights(
        nsa_w_in[0], nsa_pe_k[0], nsa_pe_v[0], nsa_ck_w1[0], nsa_ck_w2[0], nsa_cv_w1[0], nsa_cv_w2[0], nsa_w_out[0])
    qp, qr, kc, vc, ks, vs, kw, vw, z, gl = _inproj0(x, mod[0], norm_g[0:1], cos, sin, wq, wkv, wz, wg)
    kcmp, vcmp = _compress(kc, vc, compress_consts)
    ovl_t, e_sel = _selection_constants(t)
    ocmp, msel = _cmp_attn(qp, kcmp, vcmp, ovl_t, gl)
    og = _nsa_attn(qr, ks, vs, kw, vw, msel, e_sel, ocmp, z, gl)

    dq, dk, dv, dz, dwo = _dil_weights(dil_w_in[0], dil_w_out[0])
    n_grp = len(DIL_PATTERNS)
    x1, *qkv, z1 = _outproj0_inproj1(og, wo, x, mod[0], mod[1], norm_g[1:2], cos, sin, dq, dk, dv, dz)
    assert all(win // dil == DIL_WIN and PART_ROWS % (dil * 16) == 0 for win, dil in DIL_PATTERNS)
    o_groups, lse_groups = _dil_attn(qkv[0:n_grp], qkv[n_grp:2 * n_grp], qkv[2 * n_grp:3 * n_grp], t)
    return _outproj1(o_groups, lse_groups, z1, dwo, x1, mod[1], final_g.reshape(1, d))
```
